```python
import math
import jax
import jax.numpy as jnp
from jax import lax
import numpy as np

D_MODEL = 1024
BATCH = 8
SEQ = 2048
DEPTH = 4

GRID_W = 64
CTX_LEN = 256
CHUNK = 64
NORM_EPS = 1e-6
N_MOD = 9
D_FF = 2816
SHORT_CONV = 5

A_HEADS = 4
A_DK = 128
A_DV = 128
A_QKV = A_HEADS * (2 * A_DK + A_DV)

B_CH = 512
B_SHORT = 3
B_EMB = 33
B_FFN = 64
B_INNER_MLPS = 2
B_WINDOW_SHIFT = 0.05
B_DECAY_SHORT_PCT = 0.3
B_DECAY_LONG_PCT = 1.5
B_DECAY_TARGET = 1e-2

C_HEADS = 8
C_HEADDIM = 64
C_GROUPS = 2
C_STATE = 64
C_INNER = C_HEADS * C_HEADDIM
C_XBC = C_INNER + 2 * C_GROUPS * C_STATE

D_HEADS = 4
D_DK = 64
D_DV = 128
D_RANK = 16
D_GATE_NORM = 16.0

N_BRANCH = 4
BRANCH_W = 512

IN_SIZES = (A_QKV, A_HEADS * A_DV, 2 * A_HEADS, 2 * A_HEADS,
            3 * B_CH,
            C_INNER, C_XBC, 2 * C_HEADS,
            D_HEADS * (2 * D_DK + D_DV), D_HEADS * D_DV, 2 * D_RANK,
            N_BRANCH * D_MODEL)
N_IN = sum(IN_SIZES)

kernel_name = 'hybrid_quad_mixer_prefix_dit'


def rms_norm(x, w):
    xf = x.astype(jnp.float32)
    y = xf * lax.rsqrt(jnp.mean(xf * xf, axis=-1, keepdims=True) + NORM_EPS)
    return (y * w.astype(jnp.float32)).astype(x.dtype)


def l2_normalize(x):
    return x * lax.rsqrt(jnp.sum(x * x, axis=-1, keepdims=True) + NORM_EPS)


def modulate(h, shift, scale):
    return h * (1.0 + scale) + shift


def pre_norm(t, gain, m, i):
    return modulate(rms_norm(t, gain), m[..., 3 * i, :], m[..., 3 * i + 1, :])


def swiglu(h, w_up, w_down):
    a, g = jnp.split(h @ w_up, 2, axis=-1)
    return (jax.nn.silu(a) * g) @ w_down


def split_cols(z, sizes):
    idx, acc = [], 0
    for s in sizes[:-1]:
        acc += s
        idx.append(acc)
    return jnp.split(z, idx, axis=-1)


def dwconv_centred(x, w, b=None):
    k_w, l = w.shape[0], x.shape[1]
    left = (k_w - 1) // 2
    xp = jnp.pad(x, ((0, 0), (left, k_w - 1 - left), (0, 0)))
    y = xp[:, 0:l] * w[0]
    for i in range(1, k_w):
        y = y + xp[:, i:i + l] * w[i]
    return y if b is None else y + b


def lower_masks():
    i = jnp.arange(CHUNK)[:, None]
    j = jnp.arange(CHUNK)[None, :]
    return i >= j, i > j


def to_chunks(t):
    b, l, h = t.shape[:3]
    t = t.reshape((b, l // CHUNK, CHUNK, h) + t.shape[3:])
    return jnp.moveaxis(t, 3, 1)


def from_chunks(t):
    t = jnp.moveaxis(t, 1, 3)
    b, n, cl, h = t.shape[:4]
    return t.reshape((b, n * cl, h) + t.shape[4:])


def scan_chunks(step, s0, xs):
    xs = tuple(jnp.moveaxis(a, 2, 0) for a in xs)
    s_final, ys = lax.scan(step, s0, xs)
    return jnp.moveaxis(ys, 0, 2), s_final


def gated_delta_chunked(q, k, v, beta, g, s0):
    q, k, v, beta, g = map(to_chunks, (q, k, v, beta, g))
    incl, strict = lower_masks()
    gc = jnp.cumsum(g, axis=-1)
    diff = gc[..., :, None] - gc[..., None, :]
    dmask = jnp.where(incl, jnp.exp(jnp.where(incl, diff, 0.0)), 0.0)
    kb = k * beta[..., None]
    low = jnp.where(strict, jnp.einsum('bhncd,bhnsd->bhncs', kb, k) * dmask, 0.0)
    eye = jnp.eye(CHUNK, dtype=low.dtype)
    tinv = lax.linalg.triangular_solve(eye + low, jnp.broadcast_to(eye, low.shape),
                                       left_side=True, lower=True, unit_diagonal=True)
    u = tinv @ (v * beta[..., None])
    w = tinv @ (kb * jnp.exp(gc)[..., None])
    aqk = jnp.einsum('bhncd,bhnsd->bhncs', q, k) * dmask
    qg = q * jnp.exp(gc)[..., None]
    glast = gc[..., -1:]
    kg = k * jnp.exp(glast - gc)[..., None]
    dlast = jnp.exp(glast[..., 0])

    def step(s, inp):
        u_c, w_c, aqk_c, qg_c, kg_c, d_c = inp
        v_new = u_c - jnp.einsum('bhcd,bhde->bhce', w_c, s)
        o_c = jnp.einsum('bhcd,bhde->bhce', qg_c, s) + jnp.einsum('bhcs,bhse->bhce', aqk_c, v_new)
        s = d_c[..., None, None] * s + jnp.einsum('bhcd,bhce->bhde', kg_c, v_new)
        return s, o_c

    o, s_final = scan_chunks(step, s0, (u, w, aqk, qg, kg, dlast))
    return from_chunks(o), s_final


def gla_chunked(q, k, v, g, s0):
    q, k, v, g = map(to_chunks, (q, k, v, g))
    incl, _ = lower_masks()
    gc = jnp.cumsum(g, axis=3)
    glast = gc[..., -1:, :]
    kd = k * jnp.exp(glast - gc)
    qd = q * jnp.exp(gc)
    qr = q * jnp.exp(gc - glast)
    aqk = jnp.where(incl, jnp.einsum('bhncd,bhnsd->bhncs', qr, kd), 0.0)
    o_intra = aqk @ v
    dlast = jnp.exp(glast[..., 0, :])

    def step(s, inp):
        qd_c, kd_c, v_c, d_c = inp
        o_c = jnp.einsum('bhcd,bhde->bhce', qd_c, s)
        s = d_c[..., None] * s + jnp.einsum('bhcd,bhce->bhde', kd_c, v_c)
        return s, o_c

    o_inter, s_final = scan_chunks(step, s0, (qd, kd, v, dlast))
    return from_chunks(o_intra + o_inter), s_final


def ssd_chunked(x, a, bm, cm, s0):
    x, a, bm, cm = map(to_chunks, (x, a, bm, cm))
    incl, _ = lower_masks()
    ac = jnp.cumsum(a, axis=-1)
    diff = ac[..., :, None] - ac[..., None, :]
    seg = jnp.where(incl, jnp.exp(jnp.where(incl, diff, 0.0)), 0.0)
    scores = jnp.einsum('bhkcn,bhksn->bhkcs', cm, bm) * seg
    y_diag = jnp.einsum('bhkcs,bhksp->bhkcp', scores, x)
    alast = ac[..., -1:]
    cd = cm * jnp.exp(ac)[..., None]
    bd = bm * jnp.exp(alast - ac)[..., None]
    dl = jnp.exp(alast[..., 0])

    def step(s, inp):
        cd_c, bd_c, x_c, d_c = inp
        y_c = jnp.einsum('bhcn,bhpn->bhcp', cd_c, s)
        s = d_c[..., None, None] * s + jnp.einsum('bhcp,bhcn->bhpn', x_c, bd_c)
        return s, y_c

    y_off, s_final = scan_chunks(step, s0, (cd, bd, x, dl))
    return from_chunks(y_diag + y_off), s_final


def bidirectional_scan(scan_fn, ctx_dirs, lat_dirs, s0):
    out_c, out_x = None, None
    for d in range(2):
        args_c, args_x = ctx_dirs[d], lat_dirs[d]
        if d == 1:
            args_c = tuple(jnp.flip(t, axis=1) for t in args_c)
            args_x = tuple(jnp.flip(t, axis=1) for t in args_x)
        oc, s_ctx = scan_fn(*args_c, s0)
        ox, _ = scan_fn(*args_x, s_ctx)
        if d == 1:
            oc, ox = jnp.flip(oc, axis=1), jnp.flip(ox, axis=1)
        out_c = oc if out_c is None else out_c + oc
        out_x = ox if out_x is None else out_x + ox
    return out_c, out_x


def head_gate_norm(o, gate_raw, norm_w):
    b, l, h, dv = o.shape
    y = rms_norm(o, norm_w) * jax.nn.silu(gate_raw.astype(jnp.float32)).reshape(b, l, h, dv)
    return y.reshape(b, l, h * dv)


def gdn_prep(qkv_raw, beta_raw, decay_raw, conv_w, a_log, dt_bias):
    b, l, _ = qkv_raw.shape
    f32 = jnp.float32
    qkv = jax.nn.silu(dwconv_centred(qkv_raw, conv_w)).astype(f32)
    q, k, v = jnp.split(qkv, [A_HEADS * A_DK, 2 * A_HEADS * A_DK], axis=-1)
    q = l2_normalize(q.reshape(b, l, A_HEADS, A_DK)) * (A_DK ** -0.5)
    k = l2_normalize(k.reshape(b, l, A_HEADS, A_DK))
    v = v.reshape(b, l, A_HEADS, A_DV)
    beta = jax.nn.sigmoid(beta_raw.astype(f32)).reshape(b, l, 2, A_HEADS)
    g = -jnp.exp(a_log.astype(f32)) * jax.nn.softplus(
        decay_raw.astype(f32).reshape(b, l, 2, A_HEADS) + dt_bias.astype(f32))
    return [(q, k, v, beta[:, :, d], g[:, :, d]) for d in range(2)]


def gdn_branch(pc, px, conv_w, a_log, dt_bias, norm_w, need_ctx):
    dc = gdn_prep(pc[0], pc[2], pc[3], conv_w, a_log, dt_bias)
    dx = gdn_prep(px[0], px[2], px[3], conv_w, a_log, dt_bias)
    s0 = jnp.zeros((px[0].shape[0], A_HEADS, A_DK, A_DV), jnp.float32)
    oc, ox = bidirectional_scan(gated_delta_chunked, dc, dx, s0)
    yx = head_gate_norm(ox, px[1], norm_w)
    yc = head_gate_norm(oc, pc[1], norm_w) if need_ctx else None
    return yc, yx


def hyena_filters(l, w1, b1, w2, b2, w_out, freq):
    f32 = jnp.float32
    t = jnp.linspace(0.0, 1.0, l, dtype=f32)[:, None]
    bands = (B_EMB - 1) // 2
    ang = 2.0 * math.pi * jnp.arange(l, dtype=f32)[:, None] / l
    fr = jnp.linspace(1e-4, bands - 1, bands, dtype=f32)[None, :]
    z = jnp.concatenate([t, jnp.cos(fr * ang), -jnp.sin(fr * ang)], axis=-1)
    sfreq = freq.astype(f32)
    h = jnp.sin(sfreq * (z @ w1.astype(f32) + b1.astype(f32)))
    for i in range(B_INNER_MLPS):
        h = jnp.sin(sfreq * (h @ w2[i].astype(f32) + b2[i].astype(f32)))
    h = (h @ w_out.astype(f32)).reshape(l, 2, 2, B_CH)
    max_decay = math.log(B_DECAY_TARGET) / B_DECAY_SHORT_PCT
    min_decay = math.log(B_DECAY_TARGET) / B_DECAY_LONG_PCT
    deltas = jnp.abs(jnp.linspace(min_decay, max_decay, B_CH, dtype=f32))
    h = h * (jnp.exp(-t * deltas) + B_WINDOW_SHIFT)[:, None, None, :]
    hf, hb = h[:, :, 0], h[:, :, 1]
    g = jnp.concatenate([hf, jnp.zeros_like(hf[:1]), hb[:0:-1]], axis=0)
    return jnp.fft.rfft(jnp.moveaxis(g, 1, 0), axis=1)


def fft_conv(u, g_freq):
    l = u.shape[1]
    u_f = jnp.fft.rfft(u, n=2 * l, axis=1)
    return jnp.fft.irfft(u_f * g_freq[None], n=2 * l, axis=1)[:, :l]


def hyena_op(z, conv_w, conv_b, g_freq, bias):
    b, l, _, ch = z.shape
    u = dwconv_centred(z.reshape(b, l, 3 * ch), conv_w.reshape(B_SHORT, 3 * ch), conv_b.reshape(3 * ch))
    u = u.astype(jnp.float32).reshape(b, l, 3, ch)
    y = u[:, :, 0]
    for order in range(2):
        y = u[:, :, 1 + order] * (fft_conv(y, g_freq[order]) + y * bias[order].astype(jnp.float32))
    return y


def raster_to_columns(t, rows):
    b = t.shape[0]
    t = t.reshape((b, rows, GRID_W) + t.shape[2:])
    return jnp.swapaxes(t, 1, 2).reshape((b, rows * GRID_W) + t.shape[3:])


def columns_to_raster(t, rows):
    b = t.shape[0]
    t = t.reshape((b, GRID_W, rows) + t.shape[2:])
    return jnp.swapaxes(t, 1, 2).reshape((b, rows * GRID_W) + t.shape[3:])


def hyena_branch(zc, zx, conv_w, conv_b, w1, b1, w2, b2, w_out, freq, bias, need_ctx):
    b, l, _ = zx.shape
    rows = l // GRID_W
    half = B_CH // 2
    zx = zx.reshape(b, l, 3, B_CH)
    gx = hyena_filters(l, w1, b1, w2, b2, w_out, freq)
    y_row = hyena_op(zx[..., :half], conv_w[..., :half], conv_b[..., :half], gx[..., :half], bias[..., :half])
    y_col = columns_to_raster(
        hyena_op(raster_to_columns(zx[..., half:], rows), conv_w[..., half:], conv_b[..., half:],
                 gx[..., half:], bias[..., half:]), rows)
    yx = jnp.concatenate([y_row, y_col], axis=-1)
    yc = None
    if need_ctx:
        lc = zc.shape[1]
        gcf = hyena_filters(lc, w1, b1, w2, b2, w_out, freq)
        yc = hyena_op(zc.reshape(b, lc, 3, B_CH), conv_w, conv_b, gcf, bias)
    return yc, yx


def ssd_prep(xbc_raw, dt_raw, conv_w, conv_b, a_log, dt_bias):
    b, l, _ = xbc_raw.shape
    f32 = jnp.float32
    xbc = jax.nn.silu(dwconv_centred(xbc_raw, conv_w, conv_b)).astype(f32)
    xs, bm, cm = jnp.split(xbc, [C_INNER, C_INNER + C_GROUPS * C_STATE], axis=-1)
    xs = xs.reshape(b, l, C_HEADS, C_HEADDIM)
    rep = C_HEADS // C_GROUPS
    bm = jnp.repeat(bm.reshape(b, l, C_GROUPS, C_STATE), rep, axis=2)
    cm = jnp.repeat(cm.reshape(b, l, C_GROUPS, C_STATE), rep, axis=2)
    dt = jax.nn.softplus(dt_raw.astype(f32).reshape(b, l, 2, C_HEADS) + dt_bias.astype(f32))
    a = -jnp.exp(a_log.astype(f32)) * dt
    dirs = [(xs * dt[:, :, d, :, None], a[:, :, d], bm, cm) for d in range(2)]
    return xs, dirs


def ssd_branch(pc, px, conv_w, conv_b, a_log, dt_bias, d_skip, norm_w, need_ctx):
    f32 = jnp.float32
    xs_c, dc = ssd_prep(pc[1], pc[2], conv_w, conv_b, a_log, dt_bias)
    xs_x, dx = ssd_prep(px[1], px[2], conv_w, conv_b, a_log, dt_bias)
    s0 = jnp.zeros((px[0].shape[0], C_HEADS, C_HEADDIM, C_STATE), f32)
    oc, ox = bidirectional_scan(ssd_chunked, dc, dx, s0)

    def ssd_out(y, xs, z):
        b, l = y.shape[:2]
        y = (y + d_skip.astype(f32)[:, None] * xs).reshape(b, l, C_INNER) * jax.nn.silu(z.astype(f32))
        y = rms_norm(y.reshape(b, l, C_GROUPS, C_INNER // C_GROUPS),
                     norm_w.reshape(C_GROUPS, C_INNER // C_GROUPS))
        return y.reshape(b, l, C_INNER)

    yx = ssd_out(ox, xs_x, px[0])
    yc = ssd_out(oc, xs_c, pc[0]) if need_ctx else None
    return yc, yx


def gla_prep(qkv_raw, lr_raw, gk_w, gk_b):
    b, l, _ = qkv_raw.shape
    f32 = jnp.float32
    q, k, v = jnp.split(qkv_raw.astype(f32), [D_HEADS * D_DK, 2 * D_HEADS * D_DK], axis=-1)
    q = q.reshape(b, l, D_HEADS, D_DK) * (D_DK ** -0.5)
    k = k.reshape(b, l, D_HEADS, D_DK)
    v = v.reshape(b, l, D_HEADS, D_DV)
    lr = lr_raw.astype(f32).reshape(b, l, 2, D_RANK)
    gk = jnp.einsum('bldr,drk->bldk', lr, gk_w.astype(f32)) + gk_b.astype(f32)
    g = (jax.nn.log_sigmoid(gk) / D_GATE_NORM).reshape(b, l, 2, D_HEADS, D_DK)
    return [(q, k, v, g[:, :, d]) for d in range(2)]


def gla_branch(pc, px, gk_w, gk_b, norm_w, need_ctx):
    dc = gla_prep(pc[0], pc[2], gk_w, gk_b)
    dx = gla_prep(px[0], px[2], gk_w, gk_b)
    s0 = jnp.zeros((px[0].shape[0], D_HEADS, D_DK, D_DV), jnp.float32)
    oc, ox = bidirectional_scan(gla_chunked, dc, dx, s0)
    yx = head_gate_norm(ox, px[1], norm_w)
    yc = head_gate_norm(oc, pc[1], norm_w) if need_ctx else None
    return yc, yx


def merge_branches(ys, gate_raw, w_branch, w_out, dtype):
    b, l = gate_raw.shape[:2]
    yb = jnp.stack([y.astype(dtype) for y in ys], axis=2)
    proj = jnp.einsum('blkw,kwd->blkd', yb, w_branch)
    gates = jax.nn.sigmoid(gate_raw.astype(jnp.float32)).astype(dtype).reshape(b, l, N_BRANCH, D_MODEL)
    return jnp.sum(gates * proj, axis=2) @ w_out


def token_mix(hc, hx, w_in, gdn_conv, gdn_a_log, gdn_dt_bias, gdn_norm,
              hy_conv_w, hy_conv_b, hy_w1, hy_b1, hy_w2, hy_b2, hy_wout, hy_freq, hy_bias,
              ssd_conv_w, ssd_conv_b, ssd_a_log, ssd_dt_bias, ssd_d, ssd_norm,
              gla_gk_w, gla_gk_b, gla_norm, w_branch, w_out, need_ctx):
    dtype = hx.dtype
    pc = split_cols(hc @ w_in, IN_SIZES)
    px = split_cols(hx @ w_in, IN_SIZES)
    ya_c, ya_x = gdn_branch(pc[0:4], px[0:4], gdn_conv, gdn_a_log, gdn_dt_bias, gdn_norm, need_ctx)
    yb_c, yb_x = hyena_branch(pc[4], px[4], hy_conv_w, hy_conv_b, hy_w1, hy_b1, hy_w2, hy_b2,
                              hy_wout, hy_freq, hy_bias, need_ctx)
    yc_c, yc_x = ssd_branch(pc[5:8], px[5:8], ssd_conv_w, ssd_conv_b, ssd_a_log, ssd_dt_bias,
                            ssd_d, ssd_norm, need_ctx)
    yd_c, yd_x = gla_branch(pc[8:11], px[8:11], gla_gk_w, gla_gk_b, gla_norm, need_ctx)
    out_x = merge_branches([ya_x, yb_x, yc_x, yd_x], px[11], w_branch, w_out, dtype)
    out_c = merge_branches([ya_c, yb_c, yc_c, yd_c], pc[11], w_branch, w_out, dtype) if need_ctx else None
    return out_c, out_x


def setup_inputs(seed: int = 0) -> dict:
    key = jax.random.key(seed)
    keys = jax.random.split(key, 40)
    counter = iter(range(40))
    f32 = jnp.float32

    def nk():
        return keys[next(counter)]

    def normal(shape, scale):
        return jax.random.normal(nk(), shape, f32) * scale

    def gain(shape):
        return 1.0 + normal(shape, 0.1)

    def a_log_init(shape):
        return jnp.log(jax.random.uniform(nk(), shape, f32, 1.0, 16.0))

    def dt_bias_init(shape):
        dt = jnp.exp(jax.random.uniform(nk(), shape, f32, math.log(1e-3), math.log(1e-1)))
        return dt + jnp.log(-jnp.expm1(-dt))

    L = DEPTH
    return {
        'x': normal((BATCH, SEQ, D_MODEL), 1.0),
        'c': normal((BATCH, D_MODEL), 1.0),
        'ctx': normal((BATCH, CTX_LEN, D_MODEL), 1.0),
        'c_ctx': normal((D_MODEL,), 0.5),
        'w_ada': normal((L, D_MODEL, N_MOD * D_MODEL), 0.5 * D_MODEL ** -0.5),
        'b_ada': normal((L, N_MOD * D_MODEL), 0.02),
        'norm_w': gain((L, 3, D_MODEL)),
        'ffn_up': normal((L, 2, D_MODEL, 2 * D_FF), D_MODEL ** -0.5),
        'ffn_down': normal((L, 2, D_FF, D_MODEL), D_FF ** -0.5),
        'w_in': normal((L, D_MODEL, N_IN), D_MODEL ** -0.5),
        'gdn_conv': normal((L, SHORT_CONV, A_QKV), SHORT_CONV ** -0.5),
        'gdn_a_log': a_log_init((L, 2, A_HEADS)),
        'gdn_dt_bias': dt_bias_init((L, 2, A_HEADS)),
        'gdn_norm': gain((L, A_DV)),
        'hy_conv_w': normal((L, B_SHORT, 3, B_CH), B_SHORT ** -0.5),
        'hy_conv_b': normal((L, 3, B_CH), 0.02),
        'hy_w1': normal((L, B_EMB, B_FFN), B_EMB ** -0.5),
        'hy_b1': normal((L, B_FFN), 0.1),
        'hy_w2': normal((L, B_INNER_MLPS, B_FFN, B_FFN), B_FFN ** -0.5),
        'hy_b2': normal((L, B_INNER_MLPS, B_FFN), 0.1),
        'hy_wout': normal((L, B_FFN, 4 * B_CH), 0.05 * B_FFN ** -0.5),
        'hy_freq': gain((L, B_FFN)),
        'hy_bias': normal((L, 2, B_CH), 0.5),
        'ssd_conv_w': normal((L, SHORT_CONV, C_XBC), SHORT_CONV ** -0.5),
        'ssd_conv_b': normal((L, C_XBC), 0.02),
        'ssd_a_log': a_log_init((L, 2, C_HEADS)),
        'ssd_dt_bias': dt_bias_init((L, 2, C_HEADS)),
        'ssd_d': gain((L, C_HEADS)),
        'ssd_norm': gain((L, C_INNER)),
        'gla_gk_w': normal((L, 2, D_RANK, D_HEADS * D_DK), D_RANK ** -0.5),
        'gla_gk_b': normal((L, 2, D_HEADS * D_DK), 0.1),
        'gla_norm': gain((L, D_DV)),
        'w_branch': normal((L, N_BRANCH, BRANCH_W, D_MODEL), BRANCH_W ** -0.5),
        'w_out': normal((L, D_MODEL, D_MODEL), D_MODEL ** -0.5),
        'final_norm': gain((D_MODEL,)),
    }


def reference(x, c, ctx, c_ctx, w_ada, b_ada, norm_w, ffn_up, ffn_down, w_in,
              gdn_conv, gdn_a_log, gdn_dt_bias, gdn_norm,
              hy_conv_w, hy_conv_b, hy_w1, hy_b1, hy_w2, hy_b2, hy_wout, hy_freq, hy_bias,
              ssd_conv_w, ssd_conv_b, ssd_a_log, ssd_dt_bias, ssd_d, ssd_norm,
              gla_gk_w, gla_gk_b, gla_norm, w_branch, w_out, final_norm):
    b = x.shape[0]
    xc = ctx
    cond_x = jax.nn.silu(c)
    cond_c = jax.nn.silu(c_ctx)
    for layer in range(DEPTH):
        last = layer == DEPTH - 1
        mx = (cond_x @ w_ada[layer] + b_ada[layer]).reshape(b, 1, N_MOD, D_MODEL)
        mc = (cond_c @ w_ada[layer] + b_ada[layer]).reshape(N_MOD, D_MODEL)
        nw = norm_w[layer]
        x = x + 0.5 * mx[..., 2, :] * swiglu(pre_norm(x, nw[0], mx, 0), ffn_up[layer, 0], ffn_down[layer, 0])
        xc = xc + 0.5 * mc[..., 2, :] * swiglu(pre_norm(xc, nw[0], mc, 0), ffn_up[layer, 0], ffn_down[layer, 0])
        yc, yx = token_mix(pre_norm(xc, nw[1], mc, 1), pre_norm(x, nw[1], mx, 1), w_in[layer],
                           gdn_conv[layer], gdn_a_log[layer], gdn_dt_bias[layer], gdn_norm[layer],
                           hy_conv_w[layer], hy_conv_b[layer], hy_w1[layer], hy_b1[layer], hy_w2[layer],
                           hy_b2[layer], hy_wout[layer], hy_freq[layer], hy_bias[layer],
                           ssd_conv_w[layer], ssd_conv_b[layer], ssd_a_log[layer], ssd_dt_bias[layer],
                           ssd_d[layer], ssd_norm[layer],
                           gla_gk_w[layer], gla_gk_b[layer], gla_norm[layer],
                           w_branch[layer], w_out[layer], not last)
        x = x + mx[..., 5, :] * yx
        x = x + 0.5 * mx[..., 8, :] * swiglu(pre_norm(x, nw[2], mx, 2), ffn_up[layer, 1], ffn_down[layer, 1])
        if not last:
            xc = xc + mc[..., 5, :] * yc
            xc = xc + 0.5 * mc[..., 8, :] * swiglu(pre_norm(xc, nw[2], mc, 2), ffn_up[layer, 1], ffn_down[layer, 1])
    return rms_norm(x, final_norm)
```

```python
import functools
import math

import numpy as np
import jax
import jax.numpy as jnp
from jax import lax
from jax.experimental import pallas as pl
from jax.experimental.pallas import tpu as pltpu

F32 = jnp.float32
BF16 = jnp.bfloat16
HI = lax.Precision.HIGHEST

D_MODEL = 1024
GRID_W = 64
CHUNK = 64
NORM_EPS = 1e-6
N_MOD = 9
D_FF = 2816
SHORT_CONV = 5

A_HEADS, A_DK, A_DV = 4, 128, 128
A_QKV = A_HEADS * (2 * A_DK + A_DV)
B_CH, B_SHORT, B_EMB, B_FFN, B_INNER_MLPS = 512, 3, 33, 64, 2
B_WINDOW_SHIFT, B_DECAY_SHORT_PCT, B_DECAY_LONG_PCT, B_DECAY_TARGET = 0.05, 0.3, 1.5, 1e-2
C_HEADS, C_HEADDIM, C_GROUPS, C_STATE = 8, 64, 2, 64
C_INNER = C_HEADS * C_HEADDIM
C_XBC = C_INNER + 2 * C_GROUPS * C_STATE
D_HEADS, D_DK, D_DV, D_RANK = 4, 64, 128, 16
D_GATE_NORM = 16.0
N_BRANCH, BRANCH_W = 4, 512

IN_SIZES = (A_QKV, A_HEADS * A_DV, 2 * A_HEADS, 2 * A_HEADS, 3 * B_CH, C_INNER, C_XBC, 2 * C_HEADS,
            D_HEADS * (2 * D_DK + D_DV), D_HEADS * D_DV, 2 * D_RANK, N_BRANCH * D_MODEL)
IN_OFFS = tuple(int(v) for v in np.cumsum((0,) + IN_SIZES))

ROW_TILE = 256
SMALL_W = 128
VMEM_LIMIT = 56 * 1024 * 1024


def _cparams(sem):
    return pltpu.CompilerParams(dimension_semantics=sem, vmem_limit_bytes=VMEM_LIMIT)


def _mm(a, b):
    return jnp.dot(a.astype(BF16), b.astype(BF16), preferred_element_type=F32)


def _mm_nt(a, b):
    return lax.dot_general(a.astype(BF16), b.astype(BF16), (((1,), (1,)), ((), ())),
                           preferred_element_type=F32)


def _mm_hi(a, b):
    return jnp.dot(a, b, precision=HI, preferred_element_type=F32)


def _silu(v):
    return v * jax.nn.sigmoid(v)


def _softplus(v):
    return jnp.maximum(v, 0.0) + jnp.log1p(jnp.exp(-jnp.abs(v)))


def _log_sigmoid(v):
    return jnp.minimum(v, 0.0) - jnp.log1p(jnp.exp(-jnp.abs(v)))


def _prenorm(xv, gain, shift, scale):
    ms = jnp.mean(xv * xv, axis=-1, keepdims=True)
    return (xv * lax.rsqrt(ms + NORM_EPS) * gain) * (1.0 + scale) + shift


def _full(shape):
    nd = len(shape)
    return pl.BlockSpec(shape, lambda *_: (0,) * nd)


class _Layout:
    def __init__(self, batch, ctx_len, seq):
        assert ctx_len % ROW_TILE == 0 and seq % ROW_TILE == 0
        self.b, self.ctx, self.seq = batch, ctx_len, seq
        self.n_ctx = batch * ctx_len
        self.n = batch * (ctx_len + seq)
        self.tc, self.tl = ctx_len // ROW_TILE, seq // ROW_TILE
        self.nct = batch * self.tc
        self.tiles = self.n // ROW_TILE

    def mod_index(self, i):
        return jnp.where(i < self.nct, 0, 1 + (i - self.nct) // self.tl)

    def seg_first(self, i):
        return jnp.where(i < self.nct, lax.rem(i, self.tc) == 0, lax.rem(i - self.nct, self.tl) == 0)

    def seg_last(self, i):
        return jnp.where(i < self.nct, lax.rem(i, self.tc) == self.tc - 1,
                         lax.rem(i - self.nct, self.tl) == self.tl - 1)

    def fwd_tile(self, b, s):
        return jnp.where(s < self.tc, b * self.tc + s, self.nct + b * self.tl + (s - self.tc))

    def rev_tile(self, b, s):
        return jnp.where(s < self.tc, b * self.tc + (self.tc - 1 - s),
                         self.nct + b * self.tl + (self.tl - 1 - (s - self.tc)))


def _ada_kernel(c_ref, w_ref, b_ref, o_ref):
    o_ref[...] = _mm_hi(_silu(c_ref[...]), w_ref[...]) + b_ref[...]


def _ada(cond, w_ada, b_ada):
    depth, d, nm = w_ada.shape
    rp = cond.shape[0]
    tn = 1152
    return pl.pallas_call(
        _ada_kernel,
        grid=(depth, nm // tn),
        in_specs=[_full((rp, d)),
                  pl.BlockSpec((None, d, tn), lambda l, j: (l, 0, j)),
                  pl.BlockSpec((None, 1, tn), lambda l, j: (l, 0, j))],
        out_specs=pl.BlockSpec((None, rp, tn), lambda l, j: (l, 0, j)),
        out_shape=jax.ShapeDtypeStruct((depth, rp, nm), F32),
        compiler_params=_cparams(("parallel", "parallel")),
    )(cond, w_ada, b_ada.reshape(depth, 1, nm))


FF_CHUNK = 1408


def _ffn_kernel(x_ref, mod_ref, nw_ref, wup_ref, wdn_ref, o_ref, *, sub):
    xv = x_ref[...]
    m = mod_ref[...]
    h = _prenorm(xv, nw_ref[sub:sub + 1, :], m[3 * sub:3 * sub + 1, :], m[3 * sub + 1:3 * sub + 2, :])
    hb = h.astype(BF16)
    acc = jnp.zeros(xv.shape, F32)
    for c in range(D_FF // FF_CHUNK):
        lo = c * FF_CHUNK
        a = jnp.dot(hb, wup_ref[:, lo:lo + FF_CHUNK], preferred_element_type=F32)
        g = jnp.dot(hb, wup_ref[:, D_FF + lo:D_FF + lo + FF_CHUNK], preferred_element_type=F32)
        acc = acc + jnp.dot((_silu(a) * g).astype(BF16), wdn_ref[lo:lo + FF_CHUNK, :],
                            preferred_element_type=F32)
    o_ref[...] = xv + 0.5 * m[3 * sub + 2:3 * sub + 3, :] * acc


def _ffn(lay, x, mod, nw, wup, wdn, sub):
    n, d = x.shape
    return pl.pallas_call(
        functools.partial(_ffn_kernel, sub=sub),
        grid=(lay.tiles,),
        in_specs=[pl.BlockSpec((ROW_TILE, d), lambda i: (i, 0)),
                  pl.BlockSpec((None, N_MOD, d), lambda i: (lay.mod_index(i), 0, 0)),
                  _full(nw.shape),
                  pl.BlockSpec(wup.shape, lambda i: (0, 0), pipeline_mode=pl.Buffered(1)),
                  pl.BlockSpec(wdn.shape, lambda i: (0, 0), pipeline_mode=pl.Buffered(1))],
        out_specs=pl.BlockSpec((ROW_TILE, d), lambda i: (i, 0)),
        out_shape=jax.ShapeDtypeStruct((n, d), F32),
        compiler_params=_cparams(("parallel",)),
    )(x, mod, nw, wup, wdn)


PROJ_W = (A_QKV, A_HEADS * A_DV, 3 * B_CH, C_INNER, C_XBC, D_HEADS * (2 * D_DK + D_DV), D_HEADS * D_DV, SMALL_W)
PROJ_O = tuple(int(v) for v in np.cumsum((0,) + PROJ_W))


def _rearrange_w_in(w_in):
    o = IN_OFFS
    small = jnp.concatenate([w_in[:, o[2]:o[4]], w_in[:, o[7]:o[8]], w_in[:, o[10]:o[11]],
                             jnp.zeros((w_in.shape[0], SMALL_W - 64), w_in.dtype)], axis=1)
    w_r = jnp.concatenate([w_in[:, o[0]:o[1]], w_in[:, o[1]:o[2]], w_in[:, o[4]:o[5]], w_in[:, o[5]:o[6]],
                           w_in[:, o[6]:o[7]], w_in[:, o[8]:o[9]], w_in[:, o[9]:o[10]], small], axis=1)
    return w_r.astype(BF16), w_in[:, o[11]:o[12]].astype(BF16)


def _inproj_kernel(x_ref, mod_ref, nw_ref, w_ref, *o_refs):
    m = mod_ref[...]
    hb = _prenorm(x_ref[...], nw_ref[1:2, :], m[3:4, :], m[4:5, :]).astype(BF16)
    for k, o_ref in enumerate(o_refs):
        o_ref[...] = jnp.dot(hb, w_ref[:, PROJ_O[k]:PROJ_O[k + 1]], preferred_element_type=F32)


def _inproj(lay, x, mod, nw, w_r):
    n, d = x.shape
    return pl.pallas_call(
        _inproj_kernel,
        grid=(lay.tiles,),
        in_specs=[pl.BlockSpec((ROW_TILE, d), lambda i: (i, 0)),
                  pl.BlockSpec((None, N_MOD, d), lambda i: (lay.mod_index(i), 0, 0)),
                  _full(nw.shape),
                  pl.BlockSpec(w_r.shape, lambda i: (0, 0), pipeline_mode=pl.Buffered(1))],
        out_specs=[pl.BlockSpec((ROW_TILE, w), lambda i: (i, 0)) for w in PROJ_W],
        out_shape=[jax.ShapeDtypeStruct((n, w), F32) for w in PROJ_W],
        compiler_params=_cparams(("parallel",)),
    )(x, mod, nw, w_r)


HALO = 8


def _dwconv_kernel(xp_ref, x_ref, xn_ref, w_ref, b_ref, o_ref, buf, *, lay, taps, act):
    i = pl.program_id(0)
    keep_p = jnp.where(lay.seg_first(i), 0.0, 1.0)
    keep_n = jnp.where(lay.seg_last(i), 0.0, 1.0)
    buf[0:HALO, :] = xp_ref[...] * keep_p
    buf[HALO:HALO + ROW_TILE, :] = x_ref[...]
    buf[HALO + ROW_TILE:2 * HALO + ROW_TILE, :] = xn_ref[...] * keep_n
    left = (taps - 1) // 2
    acc = b_ref[...] + buf[HALO - left:HALO - left + ROW_TILE, :] * w_ref[0:1, :]
    for k in range(1, taps):
        acc = acc + buf[HALO - left + k:HALO - left + k + ROW_TILE, :] * w_ref[k:k + 1, :]
    o_ref[...] = _silu(acc) if act else acc


def _dwconv(lay, x, w, bias, act, wb, split=False):
    n, width = x.shape
    taps = w.shape[0]
    nwb = width // wb
    r8 = ROW_TILE // HALO
    last8 = n // HALO - 1
    if split:
        out_spec = pl.BlockSpec((None, ROW_TILE, wb), lambda i, j: (j, i, 0))
        out_shape = jax.ShapeDtypeStruct((nwb, n, wb), F32)
    else:
        out_spec = pl.BlockSpec((ROW_TILE, wb), lambda i, j: (i, j))
        out_shape = jax.ShapeDtypeStruct((n, width), F32)
    return pl.pallas_call(
        functools.partial(_dwconv_kernel, lay=lay, taps=taps, act=act),
        grid=(lay.tiles, nwb),
        in_specs=[pl.BlockSpec((HALO, wb), lambda i, j: (jnp.maximum(i * r8 - 1, 0), j)),
                  pl.BlockSpec((ROW_TILE, wb), lambda i, j: (i, j)),
                  pl.BlockSpec((HALO, wb), lambda i, j: (jnp.minimum((i + 1) * r8, last8), j)),
                  pl.BlockSpec((taps, wb), lambda i, j: (0, j)),
                  pl.BlockSpec((1, wb), lambda i, j: (0, j))],
        out_specs=out_spec,
        out_shape=out_shape,
        scratch_shapes=[pltpu.VMEM((ROW_TILE + 2 * HALO, wb), F32)],
        compiler_params=_cparams(("parallel", "parallel")),
    )(x, x, x, w, bias)


def _tri_masks(rev):
    ii = lax.broadcasted_iota(jnp.int32, (CHUNK, CHUNK), 0)
    jj = lax.broadcasted_iota(jnp.int32, (CHUNK, CHUNK), 1)
    return (ii <= jj, ii < jj) if rev else (ii >= jj, ii > jj)


def _masked_decay(col, row, incl):
    return jnp.where(incl, jnp.exp(jnp.where(incl, col - row, 0.0)), 0.0)


def _chunk_order(rev):
    n = ROW_TILE // CHUNK
    return range(n - 1, -1, -1) if rev else range(n)


def _scan_call(kernel, lay, ins_tiled, ins_full, out_w, scratch):
    steps = lay.tc + lay.tl
    specs = []
    args = []
    for walk in (lay.fwd_tile, lay.rev_tile):
        for a in ins_tiled:
            specs.append(pl.BlockSpec((ROW_TILE, a.shape[1]), lambda b, s, walk=walk: (walk(b, s), 0)))
            args.append(a)
    for a in ins_full:
        specs.append(_full(a.shape))
        args.append(a)
    return pl.pallas_call(
        kernel,
        grid=(lay.b, steps),
        in_specs=specs,
        out_specs=[pl.BlockSpec((ROW_TILE, out_w), lambda b, s: (lay.fwd_tile(b, s), 0)),
                   pl.BlockSpec((ROW_TILE, out_w), lambda b, s: (lay.rev_tile(b, s), 0))],
        out_shape=[jax.ShapeDtypeStruct((lay.n, out_w), F32)] * 2,
        scratch_shapes=scratch,
        compiler_params=_cparams(("arbitrary", "arbitrary")),
    )(*args)


def _ssd_kernel(xf_ref, sf_ref, xr_ref, sr_ref, alog_ref, dtb_ref, alog_t_ref, dtb_t_ref, exp_ref,
                of_ref, or_ref, st_ref):
    @pl.when(pl.program_id(1) == 0)
    def _():
        st_ref[...] = jnp.zeros(st_ref.shape, F32)

    hpg = C_HEADS // C_GROUPS
    gw = hpg * C_HEADDIM
    for d, (x_ref, s_ref, o_ref) in enumerate(((xf_ref, sf_ref, of_ref), (xr_ref, sr_ref, or_ref))):
        incl, _ = _tri_masks(d == 1)
        m_col = incl.astype(F32)
        m_row = m_col.T
        sm = s_ref[...]
        sm_t = sm.T
        dt_all = _softplus(sm[:, 16:32] + dtb_ref[...])
        a_all = -jnp.exp(alog_ref[...]) * dt_all
        a_t_all = -jnp.exp(alog_t_ref[...]) * _softplus(sm_t[16:32, :] + dtb_t_ref[...])
        bm_t = x_ref[:, C_INNER:C_INNER + C_GROUPS * C_STATE].T
        e_d = exp_ref[d]
        for c in _chunk_order(d == 1):
            r0 = c * CHUNK
            rows = slice(r0, r0 + CHUNK)
            a_ch = a_all[rows]
            ac = _mm_hi(m_col, a_ch)
            ac_t = _mm_hi(a_t_all[:, rows], m_row)
            alast = jnp.sum(a_ch, axis=0, keepdims=True)
            xs = x_ref[rows, 0:C_INNER]
            bm = x_ref[rows, C_INNER:C_INNER + C_GROUPS * C_STATE]
            cm = x_ref[rows, C_INNER + C_GROUPS * C_STATE:C_XBC]
            xdt = xs * _mm_hi(dt_all[rows], e_d)
            eac_e = _mm_hi(jnp.exp(ac), e_d)
            xd = xdt * _mm_hi(jnp.exp(alast - ac), e_d)
            dle = _mm_hi(jnp.broadcast_to(jnp.exp(alast), (8, 2 * C_HEADS)), e_d)[0:1]
            y_diag, y_off = [], []
            for g in range(C_GROUPS):
                cm_g = cm[:, g * C_STATE:(g + 1) * C_STATE]
                bm_g = bm[:, g * C_STATE:(g + 1) * C_STATE]
                cb = _mm_nt(cm_g, bm_g)
                for hh in range(hpg):
                    h = g * hpg + hh
                    col = d * C_HEADS + h
                    seg = _masked_decay(ac[:, col:col + 1], ac_t[col:col + 1, :], incl)
                    y_diag.append(_mm(cb * seg, xdt[:, h * C_HEADDIM:(h + 1) * C_HEADDIM]))
                st = st_ref[d, g]
                y_off.append(_mm(cm_g, st))
                st_ref[d, g] = dle[:, g * gw:(g + 1) * gw] * st + _mm(
                    bm_t[g * C_STATE:(g + 1) * C_STATE, rows], xd[:, g * gw:(g + 1) * gw])
            o_ref[rows, :] = jnp.concatenate(y_diag, axis=1) + eac_e * jnp.concatenate(y_off, axis=1)


def _ssd_scan(lay, xbc, small, a_log, dt_bias):
    expand = np.zeros((2, 2 * C_HEADS, C_INNER), np.float32)
    for d in range(2):
        for h in range(C_HEADS):
            expand[d, d * C_HEADS + h, h * C_HEADDIM:(h + 1) * C_HEADDIM] = 1.0
    al = a_log.reshape(1, 2 * C_HEADS)
    db = dt_bias.reshape(1, 2 * C_HEADS)
    return _scan_call(_ssd_kernel, lay, (xbc, small), (al, db, al.T, db.T, jnp.asarray(expand)), C_INNER,
                      [pltpu.VMEM((2, C_GROUPS, C_STATE, C_INNER // C_GROUPS), F32)])


def _gla_kernel(xf_ref, sf_ref, xr_ref, sr_ref, gkw_ref, gkb_ref, of_ref, or_ref, st_ref):
    @pl.when(pl.program_id(1) == 0)
    def _():
        st_ref[...] = jnp.zeros(st_ref.shape, F32)

    nk = D_HEADS * D_DK
    for d, (x_ref, s_ref, o_ref) in enumerate(((xf_ref, sf_ref, of_ref), (xr_ref, sr_ref, or_ref))):
        incl, _ = _tri_masks(d == 1)
        m_col = incl.astype(F32)
        lr = s_ref[:, 32 + D_RANK * d:32 + D_RANK * (d + 1)]
        g_all = _log_sigmoid(_mm_hi(lr, gkw_ref[d]) + gkb_ref[d:d + 1, :]) / D_GATE_NORM
        v_t = x_ref[:, 2 * nk:].T
        for c in _chunk_order(d == 1):
            r0 = c * CHUNK
            rows = slice(r0, r0 + CHUNK)
            g_ch = g_all[rows]
            gc = _mm_hi(m_col, g_ch)
            glast = jnp.sum(g_ch, axis=0, keepdims=True)
            q = x_ref[rows, 0:nk] * (D_DK ** -0.5)
            k = x_ref[rows, nk:2 * nk]
            v = x_ref[rows, 2 * nk:]
            kd = k * jnp.exp(glast - gc)
            qd = q * jnp.exp(gc)
            qr = q * jnp.exp(gc - glast)
            dlast = jnp.exp(glast)
            outs = []
            for h in range(D_HEADS):
                ks = slice(h * D_DK, (h + 1) * D_DK)
                vs = slice(h * D_DV, (h + 1) * D_DV)
                aqk = jnp.where(incl, _mm_nt(qr[:, ks], kd[:, ks]), 0.0)
                st = st_ref[d, h]
                outs.append(_mm(aqk, v[:, vs]) + _mm_nt(qd[:, ks], st))
                st_ref[d, h] = st * dlast[:, ks] + _mm(v_t[vs, rows], kd[:, ks])
            o_ref[rows, :] = jnp.concatenate(outs, axis=1)


def _gla_scan(lay, qkv, small, gk_w, gk_b):
    return _scan_call(_gla_kernel, lay, (qkv, small), (gk_w, gk_b), D_HEADS * D_DV,
                      [pltpu.VMEM((2, D_HEADS, D_DV, D_DK), F32)])


TRI_BLOCK = 16


def _unit_lower_inverse(nm, eye, blk):
    dg = jnp.where(blk, nm, 0.0)
    off = nm - dg
    t0 = eye - dg
    p = _mm(dg, dg)
    t0 = t0 + _mm(t0, p)
    p = _mm(p, p)
    t0 = t0 + _mm(t0, p)
    p = _mm(p, p)
    t0 = t0 + _mm(t0, p)
    m = _mm(t0, off)
    r = eye - m
    r = r + _mm(r, _mm(m, m))
    return _mm(r, t0)


def _gdn_kernel(xf_ref, sf_ref, xr_ref, sr_ref, alog_ref, dtb_ref, alog_t_ref, dtb_t_ref,
                of_ref, or_ref, st_ref):
    @pl.when(pl.program_id(1) == 0)
    def _():
        st_ref[...] = jnp.zeros(st_ref.shape, F32)

    nk = A_HEADS * A_DK
    ii = lax.broadcasted_iota(jnp.int32, (CHUNK, CHUNK), 0)
    jj = lax.broadcasted_iota(jnp.int32, (CHUNK, CHUNK), 1)
    eye = (ii == jj).astype(F32)
    blk = (ii // TRI_BLOCK) == (jj // TRI_BLOCK)
    for d, (x_ref, s_ref, o_ref) in enumerate(((xf_ref, sf_ref, of_ref), (xr_ref, sr_ref, or_ref))):
        incl, strict = _tri_masks(d == 1)
        m_col = incl.astype(F32)
        m_row = m_col.T
        sm = s_ref[...]
        sm_t = sm.T
        beta_all = jax.nn.sigmoid(sm[:, 0:8])
        g_all = -jnp.exp(alog_ref[...]) * _softplus(sm[:, 8:16] + dtb_ref[...])
        g_t_all = -jnp.exp(alog_t_ref[...]) * _softplus(sm_t[8:16, :] + dtb_t_ref[...])
        qn, kn = [], []
        for h in range(A_HEADS):
            qh = x_ref[:, h * A_DK:(h + 1) * A_DK]
            kh = x_ref[:, nk + h * A_DK:nk + (h + 1) * A_DK]
            qn.append(qh * lax.rsqrt(jnp.sum(qh * qh, axis=-1, keepdims=True) + NORM_EPS) * (A_DK ** -0.5))
            kn.append(kh * lax.rsqrt(jnp.sum(kh * kh, axis=-1, keepdims=True) + NORM_EPS))
        kn_t = [kh.T for kh in kn]
        for c in _chunk_order(d == 1):
            r0 = c * CHUNK
            rows = slice(r0, r0 + CHUNK)
            g_ch = g_all[rows]
            gc = _mm_hi(m_col, g_ch)
            gc_t = _mm_hi(g_t_all[:, rows], m_row)
            glast = jnp.sum(g_ch, axis=0, keepdims=True)
            outs = []
            for h in range(A_HEADS):
                col = d * A_HEADS + h
                gcc = gc[:, col:col + 1]
                gcr = gc_t[col:col + 1, :]
                gl = glast[:, col:col + 1]
                dmask = _masked_decay(gcc, gcr, incl)
                qh = qn[h][rows]
                kh = kn[h][rows]
                vh = x_ref[rows, 2 * nk + h * A_DV:2 * nk + (h + 1) * A_DV]
                bh = beta_all[rows, col:col + 1]
                kb = kh * bh
                low = jnp.where(strict, _mm_nt(kb, kh) * dmask, 0.0)
                tinv = _unit_lower_inverse(low, eye, blk)
                uw = _mm(tinv, jnp.concatenate([vh * bh, kb * jnp.exp(gcc)], axis=1))
                aqk = _mm_nt(qh, kh) * dmask
                st = st_ref[d, h]
                ws = _mm(jnp.concatenate([uw[:, A_DV:], qh * jnp.exp(gcc)], axis=0), st)
                v_new = uw[:, :A_DV] - ws[:CHUNK]
                outs.append(ws[CHUNK:] + _mm(aqk, v_new))
                kg_t = kn_t[h][:, rows] * jnp.exp(gl - gcr)
                st_ref[d, h] = jnp.exp(gl) * st + _mm(kg_t, v_new)
            o_ref[rows, :] = jnp.concatenate(outs, axis=1)


def _gdn_scan(lay, qkv, small, a_log, dt_bias):
    al = a_log.reshape(1, 2 * A_HEADS)
    db = dt_bias.reshape(1, 2 * A_HEADS)
    return _scan_call(_gdn_kernel, lay, (qkv, small), (al, db, al.T, db.T), A_HEADS * A_DV,
                      [pltpu.VMEM((2, A_HEADS, A_DK, A_DV), F32)])


FFT_N1 = 64
FFT_KB = 8


def _dft_consts(n2):
    n1 = FFT_N1
    m = n1 * n2
    k1 = np.arange(n1, dtype=np.float64)
    ang1 = 2.0 * np.pi * np.outer(k1, k1) / n1
    f1_full = np.concatenate([np.cos(ang1), -np.sin(ang1)], axis=0)
    f1 = f1_full[:, :n1 // 2]
    f3 = np.concatenate([np.cos(ang1[:, :n1 // 2]).T, -np.sin(ang1[:, :n1 // 2]).T], axis=1) / m
    j = np.arange(n2, dtype=np.float64)
    theta = 2.0 * np.pi * (np.outer(j, j)[None] / n2 + (k1[:, None, None] * j[None, None, :]) / m)
    wr, wi = np.cos(theta), -np.sin(theta)
    w2 = np.concatenate([np.concatenate([wr, -wi], axis=2), np.concatenate([wi, wr], axis=2)], axis=1)
    phi = 2.0 * np.pi * (np.outer(j, j)[None] / n2 + (k1[:, None, None] * j[None, :, None]) / m)
    vr, vi = np.cos(phi), np.sin(phi)
    v2 = np.concatenate([np.concatenate([vr, -vi], axis=2), np.concatenate([vi, vr], axis=2)], axis=1)
    return f1_full, f1, w2, v2, f3


def _left_mm_kernel(w_ref, x_ref, o_ref, *, hi):
    if hi:
        o_ref[...] = _mm_hi(w_ref[...], x_ref[...]).astype(o_ref.dtype)
    else:
        o_ref[...] = _mm(w_ref[...], x_ref[...]).astype(o_ref.dtype)


def _left_mm_epi_kernel(w_ref, x_ref, xn_ref, yp_ref, bias_ref, o_ref):
    conv = _mm(w_ref[...], x_ref[...])
    yp = yp_ref[...]
    o_ref[...] = xn_ref[...] * (conv + yp * bias_ref[...])


def _left_mm(wm, x, out_dtype, hi=False, epi=None):
    bs, kdim, ncols = x.shape
    mdim = wm.shape[0]
    tn = min(ncols, 4096)
    xspec = lambda rows: pl.BlockSpec((None, rows, tn), lambda b, j: (b, 0, j))
    if epi is None:
        kern, ins, specs = functools.partial(_left_mm_kernel, hi=hi), (wm, x), [_full(wm.shape), xspec(kdim)]
    else:
        kern, ins = _left_mm_epi_kernel, (wm, x) + tuple(epi)
        specs = [_full(wm.shape), xspec(kdim), xspec(mdim), xspec(mdim), pl.BlockSpec((1, tn), lambda b, j: (0, j))]
    return pl.pallas_call(
        kern, grid=(bs, ncols // tn), in_specs=specs, out_specs=xspec(mdim),
        out_shape=jax.ShapeDtypeStruct((bs, mdim, ncols), out_dtype),
        compiler_params=_cparams(("parallel", "parallel")),
    )(*ins)


def _mid_kernel(a_ref, g_ref, w_ref, v_ref, o_ref, *, n2):
    for kk in range(FFT_KB):
        a = jnp.concatenate([a_ref[0, kk], a_ref[1, kk]], axis=0)
        xf = _mm(w_ref[kk], a)
        xr, xi = xf[:n2], xf[n2:]
        gr, gi = g_ref[kk, 0], g_ref[kk, 1]
        y = jnp.concatenate([xr * gr - xi * gi, xr * gi + xi * gr], axis=0)
        bm = _mm(v_ref[kk], y)
        o_ref[0, kk] = bm[:n2].astype(o_ref.dtype)
        o_ref[1, kk] = bm[n2:].astype(o_ref.dtype)


def _mid_stage(a, g, w2, v2):
    bs, _, n1, n2, ch = a.shape
    return pl.pallas_call(
        functools.partial(_mid_kernel, n2=n2),
        grid=(bs, n1 // FFT_KB),
        in_specs=[pl.BlockSpec((None, 2, FFT_KB, n2, ch), lambda b, j: (b, 0, j, 0, 0)),
                  pl.BlockSpec((FFT_KB, 2, n2, ch), lambda b, j: (j, 0, 0, 0)),
                  pl.BlockSpec((FFT_KB, 2 * n2, 2 * n2), lambda b, j: (j, 0, 0)),
                  pl.BlockSpec((FFT_KB, 2 * n2, 2 * n2), lambda b, j: (j, 0, 0))],
        out_specs=pl.BlockSpec((None, 2, FFT_KB, n2, ch), lambda b, j: (b, 0, j, 0, 0)),
        out_shape=jax.ShapeDtypeStruct(a.shape, a.dtype),
        compiler_params=_cparams(("parallel", "parallel")),
    )(a, g, w2, v2)


def _filt_mid_kernel(a_ref, w_ref, o_ref, *, n2):
    for kk in range(FFT_KB):
        a = jnp.concatenate([a_ref[0, kk], a_ref[1, kk]], axis=0)
        xf = _mm_hi(w_ref[kk], a)
        o_ref[kk, 0] = xf[:n2]
        o_ref[kk, 1] = xf[n2:]


def _filt_mid_stage(a, w2):
    bs, _, n1, n2, ch = a.shape
    return pl.pallas_call(
        functools.partial(_filt_mid_kernel, n2=n2),
        grid=(bs, n1 // FFT_KB),
        in_specs=[pl.BlockSpec((None, 2, FFT_KB, n2, ch), lambda b, j: (b, 0, j, 0, 0)),
                  pl.BlockSpec((FFT_KB, 2 * n2, 2 * n2), lambda b, j: (j, 0, 0))],
        out_specs=pl.BlockSpec((None, FFT_KB, 2, n2, ch), lambda b, j: (b, j, 0, 0, 0)),
        out_shape=jax.ShapeDtypeStruct((bs, n1, 2, n2, ch), F32),
        compiler_params=_cparams(("parallel", "parallel")),
    )(a, w2)


FILT_ROWS = 512


def _filter_kernel(z_ref, win_ref, sel_ref, w1_ref, b1_ref, w2_ref, b2_ref, wo_ref, fr_ref, o_ref):
    fr = fr_ref[...]
    h = jnp.sin(fr * (_mm_hi(z_ref[...], w1_ref[...]) + b1_ref[...]))
    for i in range(B_INNER_MLPS):
        h = jnp.sin(fr * (_mm_hi(h, w2_ref[i]) + b2_ref[i]))
    ho = _mm_hi(h, wo_ref[...])
    fwd = sel_ref[...] > 0.5
    win = win_ref[...]
    for o in range(2):
        base = o * 2 * B_CH
        o_ref[o] = jnp.where(fwd, ho[:, base:base + B_CH], ho[:, base + B_CH:base + 2 * B_CH]) * win


def _hyena_filter_time(l, w1, b1, w2, b2, w_out, freq):
    bands = (B_EMB - 1) // 2
    circ = jnp.arange(2 * l)
    pos = jnp.where(circ < l, circ, 2 * l - circ)
    pos = jnp.where(circ == l, 0, pos)
    t = (jnp.linspace(0.0, 1.0, l, dtype=F32)[pos])[:, None]
    ang = 2.0 * math.pi * pos.astype(F32)[:, None] / l
    fr = jnp.linspace(1e-4, bands - 1, bands, dtype=F32)[None, :]
    z = jnp.concatenate([t, jnp.cos(fr * ang), -jnp.sin(fr * ang)], axis=-1)
    z = jnp.pad(z, ((0, 0), (0, 128 - B_EMB)))
    max_decay = math.log(B_DECAY_TARGET) / B_DECAY_SHORT_PCT
    min_decay = math.log(B_DECAY_TARGET) / B_DECAY_LONG_PCT
    deltas = jnp.abs(jnp.linspace(min_decay, max_decay, B_CH, dtype=F32))
    win = (jnp.exp(-t * deltas) + B_WINDOW_SHIFT) * (circ != l).astype(F32)[:, None]
    sel = (circ < l).astype(F32)[:, None]
    w1p = jnp.pad(w1, ((0, 128 - B_EMB), (0, 0)))
    rows = min(FILT_ROWS, 2 * l)
    return pl.pallas_call(
        _filter_kernel,
        grid=(2 * l // rows,),
        in_specs=[pl.BlockSpec((rows, 128), lambda i: (i, 0)),
                  pl.BlockSpec((rows, B_CH), lambda i: (i, 0)),
                  pl.BlockSpec((rows, 1), lambda i: (i, 0)),
                  _full(w1p.shape), _full((1, B_FFN)), _full(w2.shape), _full((B_INNER_MLPS, 1, B_FFN)),
                  _full(w_out.shape), _full((1, B_FFN))],
        out_specs=pl.BlockSpec((2, rows, B_CH), lambda i: (0, i, 0)),
        out_shape=jax.ShapeDtypeStruct((2, 2 * l, B_CH), F32),
        compiler_params=_cparams(("parallel",)),
    )(z, win, sel, w1p, b1.reshape(1, B_FFN), w2, b2.reshape(B_INNER_MLPS, 1, B_FFN), w_out,
      freq.reshape(1, B_FFN))


def _hyena_segment(u3, l, bs, consts, g_spec, bias):
    f1_full, f1, w2, v2, f3 = consts
    n2 = 2 * l // FFT_N1
    ch = B_CH
    mid_dtype = BF16 if n2 % 16 == 0 else F32
    view = lambda a: a.reshape(bs, FFT_N1 // 2, n2 * ch)
    y = view(u3[0])
    for order in range(2):
        a = _left_mm(jnp.asarray(f1, BF16), y, mid_dtype)
        bm = _mid_stage(a.reshape(bs, 2, FFT_N1, n2, ch), g_spec[order],
                        jnp.asarray(w2, BF16), jnp.asarray(v2, BF16))
        bias_row = jnp.tile(bias[order], n2).reshape(1, n2 * ch)
        y = _left_mm(jnp.asarray(f3, BF16), bm.reshape(bs, 2 * FFT_N1, n2 * ch), F32,
                     epi=(view(u3[1 + order]), y, bias_row))
    return y.reshape(bs * l, ch)


def _hyena_spectrum(l, consts, w1, b1, w2m, b2, w_out, freq):
    f1_full, _, w2, _, _ = consts
    n2 = 2 * l // FFT_N1
    g_time = _hyena_filter_time(l, w1, b1, w2m, b2, w_out, freq)
    a = _left_mm(jnp.asarray(f1_full, F32), g_time.reshape(2, FFT_N1, n2 * B_CH), F32, hi=True)
    return _filt_mid_stage(a.reshape(2, 2, FFT_N1, n2, B_CH), jnp.asarray(w2, F32))


def _to_columns(t, b, rows):
    t = t.reshape((b, rows, GRID_W) + t.shape[1:])
    return jnp.swapaxes(t, 1, 2).reshape((b * rows * GRID_W,) + t.shape[3:])


def _to_raster(t, b, rows):
    t = t.reshape((b, GRID_W, rows) + t.shape[1:])
    return jnp.swapaxes(t, 1, 2).reshape((b * rows * GRID_W,) + t.shape[3:])


def _hyena(lay, z, conv_w, conv_b, spec_c, spec_x, bias, consts_c, consts_x):
    half = B_CH // 2
    rows = lay.seq // GRID_W
    z4 = z.reshape(lay.n, 3, B_CH)
    zl = z4[lay.n_ctx:]
    zl = jnp.concatenate([zl[..., :half], _to_columns(zl[..., half:], lay.b, rows)], axis=-1)
    zp = jnp.concatenate([z4[:lay.n_ctx], zl], axis=0).reshape(lay.n, 3 * B_CH)
    u3 = _dwconv(lay, zp, conv_w.reshape(B_SHORT, 3 * B_CH), conv_b.reshape(1, 3 * B_CH), False, B_CH, split=True)
    yc = _hyena_segment(u3[:, :lay.n_ctx], lay.ctx, lay.b, consts_c, spec_c, bias)
    yl = _hyena_segment(u3[:, lay.n_ctx:], lay.seq, lay.b, consts_x, spec_x, bias)
    yl = jnp.concatenate([yl[:, :half], _to_raster(yl[:, half:], lay.b, rows)], axis=-1)
    return jnp.concatenate([yc, yl], axis=0)


def _head_norm_gate(o, gate, norm_w, heads, width):
    outs = []
    for h in range(heads):
        oh = o[:, h * width:(h + 1) * width]
        y = oh * lax.rsqrt(jnp.mean(oh * oh, axis=-1, keepdims=True) + NORM_EPS) * norm_w
        outs.append(y * _silu(gate[:, h * width:(h + 1) * width]))
    return jnp.concatenate(outs, axis=1)


def _merge_kernel(x_ref, mod_ref, nw_ref, a0_ref, a1_ref, ag_ref, yb_ref, c0_ref, c1_ref, cx_ref, cz_ref,
                  d0_ref, d1_ref, dg_ref, an_ref, cd_ref, cn_ref, dn_ref, wg_ref, wb_ref, wo_ref, o_ref):
    xv = x_ref[...]
    m = mod_ref[...]
    hb = _prenorm(xv, nw_ref[1:2, :], m[3:4, :], m[4:5, :]).astype(BF16)
    ya = _head_norm_gate(a0_ref[...] + a1_ref[...], ag_ref[...], an_ref[...], A_HEADS, A_DV)
    yd = _head_norm_gate(d0_ref[...] + d1_ref[...], dg_ref[...], dn_ref[...], D_HEADS, D_DV)
    yc = (c0_ref[...] + c1_ref[...] + cd_ref[...] * cx_ref[...]) * _silu(cz_ref[...])
    gw = C_INNER // C_GROUPS
    cn = cn_ref[...]
    yc = jnp.concatenate(
        [yc[:, g * gw:(g + 1) * gw]
         * lax.rsqrt(jnp.mean(yc[:, g * gw:(g + 1) * gw] ** 2, axis=-1, keepdims=True) + NORM_EPS)
         * cn[:, g * gw:(g + 1) * gw] for g in range(C_GROUPS)], axis=1)
    acc = jnp.zeros(xv.shape, F32)
    for k, y in enumerate((ya, yb_ref[...], yc, yd)):
        gate = jax.nn.sigmoid(jnp.dot(hb, wg_ref[:, k * D_MODEL:(k + 1) * D_MODEL], preferred_element_type=F32))
        acc = acc + gate * jnp.dot(y.astype(BF16), wb_ref[k], preferred_element_type=F32)
    o_ref[...] = xv + m[5:6, :] * jnp.dot(acc.astype(BF16), wo_ref[...], preferred_element_type=F32)


def _merge(lay, x, mod, nw, a0, a1, ag, yb, c0, c1, cx, cz, d0, d1, dg, an, cd, cn, dn, wg, wb, wo):
    n, d = x.shape
    row = lambda w: pl.BlockSpec((ROW_TILE, w), lambda i: (i, 0))
    once = lambda a: pl.BlockSpec(a.shape, lambda i: (0,) * a.ndim, pipeline_mode=pl.Buffered(1))
    return pl.pallas_call(
        _merge_kernel,
        grid=(lay.tiles,),
        in_specs=[row(d), pl.BlockSpec((None, N_MOD, d), lambda i: (lay.mod_index(i), 0, 0)), _full(nw.shape)]
                 + [row(BRANCH_W)] * 11 + [_full(an.shape), _full(cd.shape), _full(cn.shape), _full(dn.shape),
                                           once(wg), once(wb), once(wo)],
        out_specs=row(d),
        out_shape=jax.ShapeDtypeStruct((n, d), F32),
        compiler_params=_cparams(("parallel",)),
    )(x, mod, nw, a0, a1, ag, yb, c0, c1, cx, cz, d0, d1, dg, an, cd, cn, dn, wg, wb, wo)


def _final_norm_kernel(x_ref, w_ref, o_ref):
    xv = x_ref[...]
    o_ref[...] = xv * lax.rsqrt(jnp.mean(xv * xv, axis=-1, keepdims=True) + NORM_EPS) * w_ref[...]


def _final_norm(lay, x, w):
    d = x.shape[1]
    return pl.pallas_call(
        _final_norm_kernel,
        grid=(lay.b * lay.tl,),
        in_specs=[pl.BlockSpec((ROW_TILE, d), lambda i: (lay.nct + i, 0)), _full((1, d))],
        out_specs=pl.BlockSpec((ROW_TILE, d), lambda i: (i, 0)),
        out_shape=jax.ShapeDtypeStruct((lay.b * lay.seq, d), F32),
        compiler_params=_cparams(("parallel",)),
    )(x, w.reshape(1, d))


def kernel(x, c, ctx, c_ctx, w_ada, b_ada, norm_w, ffn_up, ffn_down, w_in, gdn_conv, gdn_a_log, gdn_dt_bias, gdn_norm, hy_conv_w, hy_conv_b, hy_w1, hy_b1, hy_w2, hy_b2, hy_wout, hy_freq, hy_bias, ssd_conv_w, ssd_conv_b, ssd_a_log, ssd_dt_bias, ssd_d, ssd_norm, gla_gk_w, gla_gk_b, gla_norm, w_branch, w_out, final_norm):
    b, seq, d = x.shape
    ctx_len = ctx.shape[1]
    depth = w_ada.shape[0]
    lay = _Layout(b, ctx_len, seq)
    consts_c = _dft_consts(2 * ctx_len // FFT_N1)
    consts_x = _dft_consts(2 * seq // FFT_N1)

    rp = -(-(1 + b) // 8) * 8
    cond = jnp.concatenate([c_ctx[None, :], c, jnp.zeros((rp - 1 - b, d), F32)], axis=0)
    mods = _ada(cond, w_ada, b_ada).reshape(depth, rp, N_MOD, d)

    xf = jnp.concatenate([ctx.reshape(b * ctx_len, d), x.reshape(b * seq, d)], axis=0)
    for l in range(depth):
        mod = mods[l]
        nw = norm_w[l]
        w_r, w_gate = _rearrange_w_in(w_in[l])
        xf = _ffn(lay, xf, mod, nw, ffn_up[l, 0].astype(BF16), ffn_down[l, 0].astype(BF16), 0)

        a_qkv, a_gate, zb, c_z, c_xbc, d_qkv, d_gate, small = _inproj(lay, xf, mod, nw, w_r)
        qkv = _dwconv(lay, a_qkv, gdn_conv[l], jnp.zeros((1, A_QKV), F32), True, 512)
        a0, a1 = _gdn_scan(lay, qkv, small, gdn_a_log[l], gdn_dt_bias[l])
        xbc = _dwconv(lay, c_xbc, ssd_conv_w[l], ssd_conv_b[l].reshape(1, C_XBC), True, 256)
        c0, c1 = _ssd_scan(lay, xbc, small, ssd_a_log[l], ssd_dt_bias[l])
        d0, d1 = _gla_scan(lay, d_qkv, small, gla_gk_w[l], gla_gk_b[l])
        fargs = (hy_w1[l], hy_b1[l], hy_w2[l], hy_b2[l], hy_wout[l], hy_freq[l])
        spec_c = _hyena_spectrum(ctx_len, consts_c, *fargs)
        spec_x = _hyena_spectrum(seq, consts_x, *fargs)
        yb = _hyena(lay, zb, hy_conv_w[l], hy_conv_b[l], spec_c, spec_x, hy_bias[l], consts_c, consts_x)

        xf = _merge(lay, xf, mod, nw, a0, a1, a_gate, yb, c0, c1, xbc, c_z, d0, d1, d_gate,
                    gdn_norm[l].reshape(1, A_DV), jnp.repeat(ssd_d[l], C_HEADDIM).reshape(1, C_INNER),
                    ssd_norm[l].reshape(1, C_INNER), gla_norm[l].reshape(1, D_DV),
                    w_gate, w_branch[l].astype(BF16), w_out[l].astype(BF16))
        xf = _ffn(lay, xf, mod, nw, ffn_up[l, 1].astype(BF16), ffn_down[l, 1].astype(BF16), 2)
    return _final_norm(lay, xf, final_norm).reshape(b, seq, d)
```

```python
import functools
import math

import numpy as np
import jax
import jax.numpy as jnp
from jax import lax
from jax.experimental import pallas as pl
from jax.experimental.pallas import tpu as pltpu

F32 = jnp.float32
BF16 = jnp.bfloat16
HI = lax.Precision.HIGHEST

D_MODEL = 1024
GRID_W = 64
CHUNK = 64
NORM_EPS = 1e-6
N_MOD = 9
D_FF = 2816
SHORT_CONV = 5

A_HEADS, A_DK, A_DV = 4, 128, 128
A_QKV = A_HEADS * (2 * A_DK + A_DV)
B_CH, B_SHORT, B_EMB, B_FFN, B_INNER_MLPS = 512, 3, 33, 64, 2
B_WINDOW_SHIFT, B_DECAY_SHORT_PCT, B_DECAY_LONG_PCT, B_DECAY_TARGET = 0.05, 0.3, 1.5, 1e-2
C_HEADS, C_HEADDIM, C_GROUPS, C_STATE = 8, 64, 2, 64
C_INNER = C_HEADS * C_HEADDIM
C_XBC = C_INNER + 2 * C_GROUPS * C_STATE
D_HEADS, D_DK, D_DV, D_RANK = 4, 64, 128, 16
D_GATE_NORM = 16.0
N_BRANCH, BRANCH_W = 4, 512

IN_SIZES = (A_QKV, A_HEADS * A_DV, 2 * A_HEADS, 2 * A_HEADS, 3 * B_CH, C_INNER, C_XBC, 2 * C_HEADS,
            D_HEADS * (2 * D_DK + D_DV), D_HEADS * D_DV, 2 * D_RANK, N_BRANCH * D_MODEL)
IN_OFFS = tuple(int(v) for v in np.cumsum((0,) + IN_SIZES))

ROW_TILE = 256
SMALL_W = 128
VMEM_LIMIT = 56 * 1024 * 1024


def _cparams(sem):
    return pltpu.CompilerParams(dimension_semantics=sem, vmem_limit_bytes=VMEM_LIMIT)


def _mm(a, b):
    return jnp.dot(a.astype(BF16), b.astype(BF16), preferred_element_type=F32)


def _mm_nt(a, b):
    return lax.dot_general(a.astype(BF16), b.astype(BF16), (((1,), (1,)), ((), ())),
                           preferred_element_type=F32)


def _mm_hi(a, b):
    return jnp.dot(a, b, precision=HI, preferred_element_type=F32)


def _silu(v):
    return v * jax.nn.sigmoid(v)


def _softplus(v):
    return jnp.maximum(v, 0.0) + jnp.log1p(jnp.exp(-jnp.abs(v)))


def _log_sigmoid(v):
    return jnp.minimum(v, 0.0) - jnp.log1p(jnp.exp(-jnp.abs(v)))


def _prenorm(xv, gain, shift, scale):
    ms = jnp.mean(xv * xv, axis=-1, keepdims=True)
    return (xv * lax.rsqrt(ms + NORM_EPS) * gain) * (1.0 + scale) + shift


def _full(shape):
    nd = len(shape)
    return pl.BlockSpec(shape, lambda *_: (0,) * nd)


class _Layout:
    def __init__(self, batch, ctx_len, seq):
        assert ctx_len % ROW_TILE == 0 and seq % ROW_TILE == 0
        self.b, self.ctx, self.seq = batch, ctx_len, seq
        self.n_ctx = batch * ctx_len
        self.n = batch * (ctx_len + seq)
        self.tc, self.tl = ctx_len // ROW_TILE, seq // ROW_TILE
        self.nct = batch * self.tc
        self.tiles = self.n // ROW_TILE

    def mod_index(self, i):
        return jnp.where(i < self.nct, 0, 1 + (i - self.nct) // self.tl)

    def seg_first(self, i):
        return jnp.where(i < self.nct, lax.rem(i, self.tc) == 0, lax.rem(i - self.nct, self.tl) == 0)

    def seg_last(self, i):
        return jnp.where(i < self.nct, lax.rem(i, self.tc) == self.tc - 1,
                         lax.rem(i - self.nct, self.tl) == self.tl - 1)

    def fwd_tile(self, b, s):
        return jnp.where(s < self.tc, b * self.tc + s, self.nct + b * self.tl + (s - self.tc))

    def rev_tile(self, b, s):
        return jnp.where(s < self.tc, b * self.tc + (self.tc - 1 - s),
                         self.nct + b * self.tl + (self.tl - 1 - (s - self.tc)))


def _ada_kernel(c_ref, w_ref, b_ref, o_ref):
    o_ref[...] = _mm_hi(_silu(c_ref[...]), w_ref[...]) + b_ref[...]


def _ada(cond, w_ada, b_ada):
    depth, d, nm = w_ada.shape
    rp = cond.shape[0]
    tn = 1152
    return pl.pallas_call(
        _ada_kernel,
        grid=(depth, nm // tn),
        in_specs=[_full((rp, d)),
                  pl.BlockSpec((None, d, tn), lambda l, j: (l, 0, j)),
                  pl.BlockSpec((None, 1, tn), lambda l, j: (l, 0, j))],
        out_specs=pl.BlockSpec((None, rp, tn), lambda l, j: (l, 0, j)),
        out_shape=jax.ShapeDtypeStruct((depth, rp, nm), F32),
        compiler_params=_cparams(("parallel", "parallel")), name="ada",
    )(cond, w_ada, b_ada.reshape(depth, 1, nm))


FF_CHUNK = 1408


def _ffn_kernel(x_ref, mod_ref, nw_ref, wup_ref, wdn_ref, o_ref, *, sub):
    xv = x_ref[...]
    m = mod_ref[...]
    h = _prenorm(xv, nw_ref[sub:sub + 1, :], m[3 * sub:3 * sub + 1, :], m[3 * sub + 1:3 * sub + 2, :])
    hb = h.astype(BF16)
    acc = jnp.zeros(xv.shape, F32)
    for c in range(D_FF // FF_CHUNK):
        lo = c * FF_CHUNK
        a = jnp.dot(hb, wup_ref[:, lo:lo + FF_CHUNK], preferred_element_type=F32)
        g = jnp.dot(hb, wup_ref[:, D_FF + lo:D_FF + lo + FF_CHUNK], preferred_element_type=F32)
        acc = acc + jnp.dot((_silu(a) * g).astype(BF16), wdn_ref[lo:lo + FF_CHUNK, :],
                            preferred_element_type=F32)
    o_ref[...] = xv + 0.5 * m[3 * sub + 2:3 * sub + 3, :] * acc


def _ffn(lay, x, mod, nw, wup, wdn, sub):
    n, d = x.shape
    return pl.pallas_call(
        functools.partial(_ffn_kernel, sub=sub),
        grid=(lay.tiles,),
        in_specs=[pl.BlockSpec((ROW_TILE, d), lambda i: (i, 0)),
                  pl.BlockSpec((None, N_MOD, d), lambda i: (lay.mod_index(i), 0, 0)),
                  _full(nw.shape),
                  pl.BlockSpec(wup.shape, lambda i: (0, 0), pipeline_mode=pl.Buffered(1)),
                  pl.BlockSpec(wdn.shape, lambda i: (0, 0), pipeline_mode=pl.Buffered(1))],
        out_specs=pl.BlockSpec((ROW_TILE, d), lambda i: (i, 0)),
        out_shape=jax.ShapeDtypeStruct((n, d), F32),
        compiler_params=_cparams(("parallel",)), name="ffn",
    )(x, mod, nw, wup, wdn)


PROJ_W = (A_QKV, A_HEADS * A_DV, 3 * B_CH, C_INNER, C_XBC, D_HEADS * (2 * D_DK + D_DV), D_HEADS * D_DV, SMALL_W)
PROJ_O = tuple(int(v) for v in np.cumsum((0,) + PROJ_W))


def _rearrange_w_in(w_in):
    o = IN_OFFS
    small = jnp.concatenate([w_in[:, o[2]:o[4]], w_in[:, o[7]:o[8]], w_in[:, o[10]:o[11]],
                             jnp.zeros((w_in.shape[0], SMALL_W - 64), w_in.dtype)], axis=1)
    w_r = jnp.concatenate([w_in[:, o[0]:o[1]], w_in[:, o[1]:o[2]], w_in[:, o[4]:o[5]], w_in[:, o[5]:o[6]],
                           w_in[:, o[6]:o[7]], w_in[:, o[8]:o[9]], w_in[:, o[9]:o[10]], small], axis=1)
    return w_r.astype(BF16), w_in[:, o[11]:o[12]].astype(BF16)


def _inproj_kernel(x_ref, mod_ref, nw_ref, w_ref, *o_refs):
    m = mod_ref[...]
    hb = _prenorm(x_ref[...], nw_ref[1:2, :], m[3:4, :], m[4:5, :]).astype(BF16)
    for k, o_ref in enumerate(o_refs):
        o_ref[...] = jnp.dot(hb, w_ref[:, PROJ_O[k]:PROJ_O[k + 1]], preferred_element_type=F32)


def _inproj(lay, x, mod, nw, w_r):
    n, d = x.shape
    return pl.pallas_call(
        _inproj_kernel,
        grid=(lay.tiles,),
        in_specs=[pl.BlockSpec((ROW_TILE, d), lambda i: (i, 0)),
                  pl.BlockSpec((None, N_MOD, d), lambda i: (lay.mod_index(i), 0, 0)),
                  _full(nw.shape),
                  pl.BlockSpec(w_r.shape, lambda i: (0, 0), pipeline_mode=pl.Buffered(1))],
        out_specs=[pl.BlockSpec((ROW_TILE, w), lambda i: (i, 0)) for w in PROJ_W],
        out_shape=[jax.ShapeDtypeStruct((n, w), F32) for w in PROJ_W],
        compiler_params=_cparams(("parallel",)), name="inproj",
    )(x, mod, nw, w_r)


HALO = 8


def _dwconv_kernel(xp_ref, x_ref, xn_ref, w_ref, b_ref, o_ref, buf, *, lay, taps, act):
    i = pl.program_id(0)
    keep_p = jnp.where(lay.seg_first(i), 0.0, 1.0)
    keep_n = jnp.where(lay.seg_last(i), 0.0, 1.0)
    buf[0:HALO, :] = xp_ref[...] * keep_p
    buf[HALO:HALO + ROW_TILE, :] = x_ref[...]
    buf[HALO + ROW_TILE:2 * HALO + ROW_TILE, :] = xn_ref[...] * keep_n
    left = (taps - 1) // 2
    acc = b_ref[...] + buf[HALO - left:HALO - left + ROW_TILE, :] * w_ref[0:1, :]
    for k in range(1, taps):
        acc = acc + buf[HALO - left + k:HALO - left + k + ROW_TILE, :] * w_ref[k:k + 1, :]
    o_ref[...] = _silu(acc) if act else acc


def _dwconv(lay, x, w, bias, act, wb, split=False):
    n, width = x.shape
    taps = w.shape[0]
    nwb = width // wb
    r8 = ROW_TILE // HALO
    last8 = n // HALO - 1
    if split:
        out_spec = pl.BlockSpec((None, ROW_TILE, wb), lambda i, j: (j, i, 0))
        out_shape = jax.ShapeDtypeStruct((nwb, n, wb), F32)
    else:
        out_spec = pl.BlockSpec((ROW_TILE, wb), lambda i, j: (i, j))
        out_shape = jax.ShapeDtypeStruct((n, width), F32)
    return pl.pallas_call(
        functools.partial(_dwconv_kernel, lay=lay, taps=taps, act=act),
        grid=(lay.tiles, nwb),
        in_specs=[pl.BlockSpec((HALO, wb), lambda i, j: (jnp.maximum(i * r8 - 1, 0), j)),
                  pl.BlockSpec((ROW_TILE, wb), lambda i, j: (i, j)),
                  pl.BlockSpec((HALO, wb), lambda i, j: (jnp.minimum((i + 1) * r8, last8), j)),
                  pl.BlockSpec((taps, wb), lambda i, j: (0, j)),
                  pl.BlockSpec((1, wb), lambda i, j: (0, j))],
        out_specs=out_spec,
        out_shape=out_shape,
        scratch_shapes=[pltpu.VMEM((ROW_TILE + 2 * HALO, wb), F32)],
        compiler_params=_cparams(("parallel", "parallel")), name="dwconv",
    )(x, x, x, w, bias)


def _tri_masks(rev):
    ii = lax.broadcasted_iota(jnp.int32, (CHUNK, CHUNK), 0)
    jj = lax.broadcasted_iota(jnp.int32, (CHUNK, CHUNK), 1)
    return (ii <= jj, ii < jj) if rev else (ii >= jj, ii > jj)


def _masked_decay(col, row, incl):
    return jnp.where(incl, jnp.exp(jnp.where(incl, col - row, 0.0)), 0.0)


def _chunk_order(rev):
    n = ROW_TILE // CHUNK
    return range(n - 1, -1, -1) if rev else range(n)


def _scan_call(kernel, lay, ins_tiled, ins_full, out_w, scratch):
    steps = lay.tc + lay.tl
    specs = []
    args = []
    for walk in (lay.fwd_tile, lay.rev_tile):
        for a in ins_tiled:
            specs.append(pl.BlockSpec((ROW_TILE, a.shape[1]), lambda b, s, walk=walk: (walk(b, s), 0)))
            args.append(a)
    for a in ins_full:
        specs.append(_full(a.shape))
        args.append(a)
    return pl.pallas_call(
        kernel,
        grid=(lay.b, steps),
        in_specs=specs,
        out_specs=[pl.BlockSpec((ROW_TILE, out_w), lambda b, s: (lay.fwd_tile(b, s), 0)),
                   pl.BlockSpec((ROW_TILE, out_w), lambda b, s: (lay.rev_tile(b, s), 0))],
        out_shape=[jax.ShapeDtypeStruct((lay.n, out_w), F32)] * 2,
        scratch_shapes=scratch,
        compiler_params=_cparams(("arbitrary", "arbitrary")), name=kernel.__name__.strip("_"),
    )(*args)


def _ssd_kernel(xf_ref, sf_ref, xr_ref, sr_ref, alog_ref, dtb_ref, alog_t_ref, dtb_t_ref, exp_ref,
                of_ref, or_ref, st_ref):
    @pl.when(pl.program_id(1) == 0)
    def _():
        st_ref[...] = jnp.zeros(st_ref.shape, F32)

    hpg = C_HEADS // C_GROUPS
    gw = hpg * C_HEADDIM
    nch = ROW_TILE // CHUNK
    refs = ((xf_ref, sf_ref, of_ref), (xr_ref, sr_ref, or_ref))
    units, pre = [], {}
    for d, (x_ref, s_ref, _) in enumerate(refs):
        incl, _ = _tri_masks(d == 1)
        m_col = incl.astype(F32)
        m_row = m_col.T
        sm = s_ref[...]
        sm_t = sm.T
        dt_all = _softplus(sm[:, 16:32] + dtb_ref[...])
        a_all = -jnp.exp(alog_ref[...]) * dt_all
        a_t_all = -jnp.exp(alog_t_ref[...]) * _softplus(sm_t[16:32, :] + dtb_t_ref[...])
        bm_t = x_ref[:, C_INNER:C_INNER + C_GROUPS * C_STATE].T
        for c in range(nch):
            rows = slice(c * CHUNK, (c + 1) * CHUNK)
            units.append((d, c))
            pre[d, c] = dict(incl=incl, m_col=m_col, m_row=m_row, a_ch=a_all[rows], a_t=a_t_all[:, rows],
                             dt=dt_all[rows], bm_t=bm_t[:, rows], e=exp_ref[d], xs=x_ref[rows, 0:C_INNER],
                             bm=x_ref[rows, C_INNER:C_INNER + C_GROUPS * C_STATE],
                             cm=x_ref[rows, C_INNER + C_GROUPS * C_STATE:C_XBC])
    for u in units:
        w = pre[u]
        w["ac"] = _mm_hi(w["m_col"], w["a_ch"])
        w["ac_t"] = _mm_hi(w["a_t"], w["m_row"])
        w["alast"] = jnp.sum(w["a_ch"], axis=0, keepdims=True)
    for u in units:
        w = pre[u]
        w["xdt"] = w["xs"] * _mm_hi(w["dt"], w["e"])
        w["eac_e"] = _mm_hi(jnp.exp(w["ac"]), w["e"])
        w["xd"] = w["xdt"] * _mm_hi(jnp.exp(w["alast"] - w["ac"]), w["e"])
        w["dle"] = _mm_hi(jnp.broadcast_to(jnp.exp(w["alast"]), (8, 2 * C_HEADS)), w["e"])[0:1]
    grp = lambda a, g: a[:, g * C_STATE:(g + 1) * C_STATE]
    cb = {(u, g): _mm_nt(grp(pre[u]["cm"], g), grp(pre[u]["bm"], g)) for u in units for g in range(C_GROUPS)}
    upd = {(u, g): _mm(pre[u]["bm_t"][g * C_STATE:(g + 1) * C_STATE], pre[u]["xd"][:, g * gw:(g + 1) * gw])
           for u in units for g in range(C_GROUPS)}
    y_diag = {}
    for u in units:
        w = pre[u]
        for h in range(C_HEADS):
            col = u[0] * C_HEADS + h
            seg = _masked_decay(w["ac"][:, col:col + 1], w["ac_t"][col:col + 1, :], w["incl"])
            y_diag[u, h] = _mm(cb[u, h // hpg] * seg, w["xdt"][:, h * C_HEADDIM:(h + 1) * C_HEADDIM])
    st = {(d, g): st_ref[d, g] for d in range(2) for g in range(C_GROUPS)}
    entry = {}
    for p in range(nch):
        for d in range(2):
            u = (d, (nch - 1 - p) if d == 1 else p)
            for g in range(C_GROUPS):
                entry[u, g] = st[d, g]
                st[d, g] = pre[u]["dle"][:, g * gw:(g + 1) * gw] * st[d, g] + upd[u, g]
    for d in range(2):
        for g in range(C_GROUPS):
            st_ref[d, g] = st[d, g]
    for u in units:
        y_off = [_mm(grp(pre[u]["cm"], g), entry[u, g]) for g in range(C_GROUPS)]
        refs[u[0]][2][u[1] * CHUNK:(u[1] + 1) * CHUNK, :] = (
            jnp.concatenate([y_diag[u, h] for h in range(C_HEADS)], axis=1)
            + pre[u]["eac_e"] * jnp.concatenate(y_off, axis=1))


def _ssd_scan(lay, xbc, small, a_log, dt_bias):
    expand = np.zeros((2, 2 * C_HEADS, C_INNER), np.float32)
    for d in range(2):
        for h in range(C_HEADS):
            expand[d, d * C_HEADS + h, h * C_HEADDIM:(h + 1) * C_HEADDIM] = 1.0
    al = a_log.reshape(1, 2 * C_HEADS)
    db = dt_bias.reshape(1, 2 * C_HEADS)
    return _scan_call(_ssd_kernel, lay, (xbc, small), (al, db, al.T, db.T, jnp.asarray(expand)), C_INNER,
                      [pltpu.VMEM((2, C_GROUPS, C_STATE, C_INNER // C_GROUPS), F32)])


def _gla_kernel(xf_ref, sf_ref, xr_ref, sr_ref, gkw_ref, gkb_ref, of_ref, or_ref, st_ref):
    @pl.when(pl.program_id(1) == 0)
    def _():
        st_ref[...] = jnp.zeros(st_ref.shape, F32)

    nk = D_HEADS * D_DK
    nch = ROW_TILE // CHUNK
    refs = ((xf_ref, sf_ref, of_ref), (xr_ref, sr_ref, or_ref))
    units, pre = [], {}
    for d, (x_ref, s_ref, _) in enumerate(refs):
        incl, _ = _tri_masks(d == 1)
        m_col = incl.astype(F32)
        lr = s_ref[:, 32 + D_RANK * d:32 + D_RANK * (d + 1)]
        g_all = _log_sigmoid(_mm_hi(lr, gkw_ref[d]) + gkb_ref[d:d + 1, :]) / D_GATE_NORM
        v_t = x_ref[:, 2 * nk:].T
        for c in range(nch):
            rows = slice(c * CHUNK, (c + 1) * CHUNK)
            g_ch = g_all[rows]
            gc = _mm_hi(m_col, g_ch)
            glast = jnp.sum(g_ch, axis=0, keepdims=True)
            q = x_ref[rows, 0:nk] * (D_DK ** -0.5)
            k = x_ref[rows, nk:2 * nk]
            v = x_ref[rows, 2 * nk:]
            kd = k * jnp.exp(glast - gc)
            qd = q * jnp.exp(gc)
            qr = q * jnp.exp(gc - glast)
            dlast = jnp.exp(glast)
            for h in range(D_HEADS):
                ks = slice(h * D_DK, (h + 1) * D_DK)
                vs = slice(h * D_DV, (h + 1) * D_DV)
                units.append((d, c, h))
                pre[d, c, h] = dict(incl=incl, qr=qr[:, ks], kd=kd[:, ks], qd=qd[:, ks], v=v[:, vs],
                                    v_t=v_t[vs, rows], dl=dlast[:, ks])
    aqk = [jnp.where(pre[u]["incl"], _mm_nt(pre[u]["qr"], pre[u]["kd"]), 0.0) for u in units]
    upd = {u: _mm(pre[u]["v_t"], pre[u]["kd"]) for u in units}
    intra = {u: _mm(a, pre[u]["v"]) for u, a in zip(units, aqk)}
    heads = [(d, h) for d in range(2) for h in range(D_HEADS)]
    st = {dh: st_ref[dh[0], dh[1]] for dh in heads}
    entry = {}
    for p in range(nch):
        for d, h in heads:
            u = (d, (nch - 1 - p) if d == 1 else p, h)
            entry[u] = st[d, h]
            st[d, h] = st[d, h] * pre[u]["dl"] + upd[u]
    for dh in heads:
        st_ref[dh[0], dh[1]] = st[dh]
    inter = {u: _mm_nt(pre[u]["qd"], entry[u]) for u in units}
    for d, (_, _, o_ref) in enumerate(refs):
        for c in range(nch):
            o_ref[c * CHUNK:(c + 1) * CHUNK, :] = jnp.concatenate(
                [intra[d, c, h] + inter[d, c, h] for h in range(D_HEADS)], axis=1)


def _gla_scan(lay, qkv, small, gk_w, gk_b):
    return _scan_call(_gla_kernel, lay, (qkv, small), (gk_w, gk_b), D_HEADS * D_DV,
                      [pltpu.VMEM((2, D_HEADS, D_DV, D_DK), F32)])


TRI_BLOCK = 16


def _unit_lower_inverse(nms, eye, blk):
    dg = [jnp.where(blk, nm, 0.0) for nm in nms]
    off = [nm - d for nm, d in zip(nms, dg)]
    t0 = [eye - d for d in dg]
    p = [_mm(d, d) for d in dg]
    for it in range(3):
        t0 = [t + _mm(t, q) for t, q in zip(t0, p)]
        if it < 2:
            p = [_mm(q, q) for q in p]
    m = [_mm(t, o) for t, o in zip(t0, off)]
    m2 = [_mm(a, a) for a in m]
    r = [eye - a for a in m]
    r = [a + _mm(a, b) for a, b in zip(r, m2)]
    return [_mm(a, t) for a, t in zip(r, t0)]


def _gdn_kernel(xf_ref, sf_ref, xr_ref, sr_ref, alog_ref, dtb_ref, alog_t_ref, dtb_t_ref,
                of_ref, or_ref, st_ref):
    @pl.when(pl.program_id(1) == 0)
    def _():
        st_ref[...] = jnp.zeros(st_ref.shape, F32)

    nk = A_HEADS * A_DK
    nch = ROW_TILE // CHUNK
    ii = lax.broadcasted_iota(jnp.int32, (CHUNK, CHUNK), 0)
    jj = lax.broadcasted_iota(jnp.int32, (CHUNK, CHUNK), 1)
    eye = (ii == jj).astype(F32)
    blk = (ii // TRI_BLOCK) == (jj // TRI_BLOCK)
    refs = ((xf_ref, sf_ref, of_ref), (xr_ref, sr_ref, or_ref))

    units, pre = [], {}
    for d, (x_ref, s_ref, _) in enumerate(refs):
        incl, strict = _tri_masks(d == 1)
        m_col = incl.astype(F32)
        m_row = m_col.T
        sm = s_ref[...]
        sm_t = sm.T
        beta_all = jax.nn.sigmoid(sm[:, 0:8])
        g_all = -jnp.exp(alog_ref[...]) * _softplus(sm[:, 8:16] + dtb_ref[...])
        g_t_all = -jnp.exp(alog_t_ref[...]) * _softplus(sm_t[8:16, :] + dtb_t_ref[...])
        qn, kn = [], []
        for h in range(A_HEADS):
            qh = x_ref[:, h * A_DK:(h + 1) * A_DK]
            kh = x_ref[:, nk + h * A_DK:nk + (h + 1) * A_DK]
            qn.append(qh * lax.rsqrt(jnp.sum(qh * qh, axis=-1, keepdims=True) + NORM_EPS) * (A_DK ** -0.5))
            kn.append(kh * lax.rsqrt(jnp.sum(kh * kh, axis=-1, keepdims=True) + NORM_EPS))
        kn_t = [kh.T for kh in kn]
        for c in range(nch):
            rows = slice(c * CHUNK, (c + 1) * CHUNK)
            g_ch = g_all[rows]
            gc = _mm_hi(m_col, g_ch)
            gc_t = _mm_hi(g_t_all[:, rows], m_row)
            glast = jnp.sum(g_ch, axis=0, keepdims=True)
            for h in range(A_HEADS):
                col = d * A_HEADS + h
                gcc, gcr, gl = gc[:, col:col + 1], gc_t[col:col + 1, :], glast[:, col:col + 1]
                bh = beta_all[rows, col:col + 1]
                kh = kn[h][rows]
                units.append((d, c, h))
                pre[d, c, h] = dict(
                    dmask=_masked_decay(gcc, gcr, incl), strict=strict, egc=jnp.exp(gcc), kh=kh, kb=kh * bh,
                    qh=qn[h][rows], vb=x_ref[rows, 2 * nk + h * A_DV:2 * nk + (h + 1) * A_DV] * bh,
                    kg_t=kn_t[h][:, rows] * jnp.exp(gl - gcr), dl=jnp.exp(gl))
    kk = [_mm_nt(pre[u]["kb"], pre[u]["kh"]) for u in units]
    qk = [_mm_nt(pre[u]["qh"], pre[u]["kh"]) for u in units]
    tinv = _unit_lower_inverse(
        [jnp.where(pre[u]["strict"], a * pre[u]["dmask"], 0.0) for u, a in zip(units, kk)], eye, blk)
    uw = [_mm(t, jnp.concatenate([pre[u]["vb"], pre[u]["kb"] * pre[u]["egc"]], axis=1))
          for u, t in zip(units, tinv)]
    for u, a, b in zip(units, uw, qk):
        pre[u]["u"] = a[:, :A_DV]
        pre[u]["wq"] = jnp.concatenate([a[:, A_DV:], pre[u]["qh"] * pre[u]["egc"]], axis=0)
        pre[u]["aqk"] = b * pre[u]["dmask"]

    heads = [(d, h) for d in range(2) for h in range(A_HEADS)]
    st = {dh: st_ref[dh[0], dh[1]] for dh in heads}
    for p in range(nch):
        us = [(d, (nch - 1 - p) if d == 1 else p, h) for d, h in heads]
        ws = [_mm(pre[u]["wq"], st[u[0], u[2]]) for u in us]
        v_new = [pre[u]["u"] - w[:CHUNK] for u, w in zip(us, ws)]
        outs = [w[CHUNK:] + _mm(pre[u]["aqk"], v) for u, w, v in zip(us, ws, v_new)]
        for u, v in zip(us, v_new):
            st[u[0], u[2]] = pre[u]["dl"] * st[u[0], u[2]] + _mm(pre[u]["kg_t"], v)
        for d, (_, _, o_ref) in enumerate(refs):
            c = us[d * A_HEADS][1]
            o_ref[c * CHUNK:(c + 1) * CHUNK, :] = jnp.concatenate(outs[d * A_HEADS:(d + 1) * A_HEADS], axis=1)
    for dh in heads:
        st_ref[dh[0], dh[1]] = st[dh]


def _gdn_scan(lay, qkv, small, a_log, dt_bias):
    al = a_log.reshape(1, 2 * A_HEADS)
    db = dt_bias.reshape(1, 2 * A_HEADS)
    return _scan_call(_gdn_kernel, lay, (qkv, small), (al, db, al.T, db.T), A_HEADS * A_DV,
                      [pltpu.VMEM((2, A_HEADS, A_DK, A_DV), F32)])


FFT_N1 = 64
FFT_KB = 8


def _dft_consts(n2):
    n1 = FFT_N1
    m = n1 * n2
    k1 = np.arange(n1, dtype=np.float64)
    ang1 = 2.0 * np.pi * np.outer(k1, k1) / n1
    f1_full = np.concatenate([np.cos(ang1), -np.sin(ang1)], axis=0)
    f1 = f1_full[:, :n1 // 2]
    f3 = np.concatenate([np.cos(ang1[:, :n1 // 2]).T, -np.sin(ang1[:, :n1 // 2]).T], axis=1) / m
    j = np.arange(n2, dtype=np.float64)
    theta = 2.0 * np.pi * (np.outer(j, j)[None] / n2 + (k1[:, None, None] * j[None, None, :]) / m)
    wr, wi = np.cos(theta), -np.sin(theta)
    w2 = np.concatenate([np.concatenate([wr, -wi], axis=2), np.concatenate([wi, wr], axis=2)], axis=1)
    phi = 2.0 * np.pi * (np.outer(j, j)[None] / n2 + (k1[:, None, None] * j[None, :, None]) / m)
    vr, vi = np.cos(phi), np.sin(phi)
    v2 = np.concatenate([np.concatenate([vr, -vi], axis=2), np.concatenate([vi, vr], axis=2)], axis=1)
    return f1_full, f1, w2, v2, f3


def _left_mm_kernel(w_ref, x_ref, o_ref, *, hi):
    if hi:
        o_ref[...] = _mm_hi(w_ref[...], x_ref[...]).astype(o_ref.dtype)
    else:
        o_ref[...] = _mm(w_ref[...], x_ref[...]).astype(o_ref.dtype)


def _left_mm_epi_kernel(w_ref, x_ref, xn_ref, yp_ref, bias_ref, o_ref):
    conv = _mm(w_ref[...], x_ref[...])
    yp = yp_ref[...]
    o_ref[...] = xn_ref[...] * (conv + yp * bias_ref[...])


def _left_mm(wm, x, out_dtype, hi=False, epi=None):
    bs, kdim, ncols = x.shape
    mdim = wm.shape[0]
    tn = min(ncols, 4096)
    xspec = lambda rows: pl.BlockSpec((None, rows, tn), lambda b, j: (b, 0, j))
    if epi is None:
        kern, ins, specs = functools.partial(_left_mm_kernel, hi=hi), (wm, x), [_full(wm.shape), xspec(kdim)]
    else:
        kern, ins = _left_mm_epi_kernel, (wm, x) + tuple(epi)
        specs = [_full(wm.shape), xspec(kdim), xspec(mdim), xspec(mdim), pl.BlockSpec((1, tn), lambda b, j: (0, j))]
    return pl.pallas_call(
        kern, grid=(bs, ncols // tn), in_specs=specs, out_specs=xspec(mdim),
        out_shape=jax.ShapeDtypeStruct((bs, mdim, ncols), out_dtype),
        compiler_params=_cparams(("parallel", "parallel")),
        name="dft_outer" + ("_gate" if epi is not None else "_hi" if hi else ""),
    )(*ins)


def _mid_kernel(a_ref, g_ref, w_ref, v_ref, o_ref, *, n2):
    for kk in range(FFT_KB):
        a = jnp.concatenate([a_ref[0, kk], a_ref[1, kk]], axis=0)
        xf = _mm(w_ref[kk], a)
        xr, xi = xf[:n2], xf[n2:]
        gr, gi = g_ref[kk, 0], g_ref[kk, 1]
        y = jnp.concatenate([xr * gr - xi * gi, xr * gi + xi * gr], axis=0)
        bm = _mm(v_ref[kk], y)
        o_ref[0, kk] = bm[:n2].astype(o_ref.dtype)
        o_ref[1, kk] = bm[n2:].astype(o_ref.dtype)


def _mid_stage(a, g, w2, v2):
    bs, _, n1, n2, ch = a.shape
    return pl.pallas_call(
        functools.partial(_mid_kernel, n2=n2),
        grid=(bs, n1 // FFT_KB),
        in_specs=[pl.BlockSpec((None, 2, FFT_KB, n2, ch), lambda b, j: (b, 0, j, 0, 0)),
                  pl.BlockSpec((FFT_KB, 2, n2, ch), lambda b, j: (j, 0, 0, 0)),
                  pl.BlockSpec((FFT_KB, 2 * n2, 2 * n2), lambda b, j: (j, 0, 0)),
                  pl.BlockSpec((FFT_KB, 2 * n2, 2 * n2), lambda b, j: (j, 0, 0))],
        out_specs=pl.BlockSpec((None, 2, FFT_KB, n2, ch), lambda b, j: (b, 0, j, 0, 0)),
        out_shape=jax.ShapeDtypeStruct(a.shape, a.dtype),
        compiler_params=_cparams(("parallel", "parallel")), name="dft_mid",
    )(a, g, w2, v2)


def _filt_mid_kernel(a_ref, w_ref, o_ref, *, n2):
    for kk in range(FFT_KB):
        a = jnp.concatenate([a_ref[0, kk], a_ref[1, kk]], axis=0)
        xf = _mm_hi(w_ref[kk], a)
        o_ref[kk, 0] = xf[:n2]
        o_ref[kk, 1] = xf[n2:]


def _filt_mid_stage(a, w2):
    bs, _, n1, n2, ch = a.shape
    return pl.pallas_call(
        functools.partial(_filt_mid_kernel, n2=n2),
        grid=(bs, n1 // FFT_KB),
        in_specs=[pl.BlockSpec((None, 2, FFT_KB, n2, ch), lambda b, j: (b, 0, j, 0, 0)),
                  pl.BlockSpec((FFT_KB, 2 * n2, 2 * n2), lambda b, j: (j, 0, 0))],
        out_specs=pl.BlockSpec((None, FFT_KB, 2, n2, ch), lambda b, j: (b, j, 0, 0, 0)),
        out_shape=jax.ShapeDtypeStruct((bs, n1, 2, n2, ch), F32),
        compiler_params=_cparams(("parallel", "parallel")), name="filter_dft_mid",
    )(a, w2)


FILT_ROWS = 512


def _filter_kernel(z_ref, win_ref, sel_ref, w1_ref, b1_ref, w2_ref, b2_ref, wo_ref, fr_ref, o_ref):
    fr = fr_ref[...]
    h = jnp.sin(fr * (_mm_hi(z_ref[...], w1_ref[...]) + b1_ref[...]))
    for i in range(B_INNER_MLPS):
        h = jnp.sin(fr * (_mm_hi(h, w2_ref[i]) + b2_ref[i]))
    ho = _mm_hi(h, wo_ref[...])
    fwd = sel_ref[...] > 0.5
    win = win_ref[...]
    for o in range(2):
        base = o * 2 * B_CH
        o_ref[o] = jnp.where(fwd, ho[:, base:base + B_CH], ho[:, base + B_CH:base + 2 * B_CH]) * win


def _hyena_filter_time(l, w1, b1, w2, b2, w_out, freq):
    bands = (B_EMB - 1) // 2
    circ = jnp.arange(2 * l)
    pos = jnp.where(circ < l, circ, 2 * l - circ)
    pos = jnp.where(circ == l, 0, pos)
    t = (jnp.linspace(0.0, 1.0, l, dtype=F32)[pos])[:, None]
    ang = 2.0 * math.pi * pos.astype(F32)[:, None] / l
    fr = jnp.linspace(1e-4, bands - 1, bands, dtype=F32)[None, :]
    z = jnp.concatenate([t, jnp.cos(fr * ang), -jnp.sin(fr * ang)], axis=-1)
    z = jnp.pad(z, ((0, 0), (0, 128 - B_EMB)))
    max_decay = math.log(B_DECAY_TARGET) / B_DECAY_SHORT_PCT
    min_decay = math.log(B_DECAY_TARGET) / B_DECAY_LONG_PCT
    deltas = jnp.abs(jnp.linspace(min_decay, max_decay, B_CH, dtype=F32))
    win = (jnp.exp(-t * deltas) + B_WINDOW_SHIFT) * (circ != l).astype(F32)[:, None]
    sel = (circ < l).astype(F32)[:, None]
    w1p = jnp.pad(w1, ((0, 128 - B_EMB), (0, 0)))
    rows = min(FILT_ROWS, 2 * l)
    return pl.pallas_call(
        _filter_kernel,
        grid=(2 * l // rows,),
        in_specs=[pl.BlockSpec((rows, 128), lambda i: (i, 0)),
                  pl.BlockSpec((rows, B_CH), lambda i: (i, 0)),
                  pl.BlockSpec((rows, 1), lambda i: (i, 0)),
                  _full(w1p.shape), _full((1, B_FFN)), _full(w2.shape), _full((B_INNER_MLPS, 1, B_FFN)),
                  _full(w_out.shape), _full((1, B_FFN))],
        out_specs=pl.BlockSpec((2, rows, B_CH), lambda i: (0, i, 0)),
        out_shape=jax.ShapeDtypeStruct((2, 2 * l, B_CH), F32),
        compiler_params=_cparams(("parallel",)), name="hyena_filter",
    )(z, win, sel, w1p, b1.reshape(1, B_FFN), w2, b2.reshape(B_INNER_MLPS, 1, B_FFN), w_out,
      freq.reshape(1, B_FFN))


def _hyena_segment(u3, l, bs, consts, g_spec, bias):
    f1_full, f1, w2, v2, f3 = consts
    n2 = 2 * l // FFT_N1
    ch = B_CH
    mid_dtype = BF16 if n2 % 16 == 0 else F32
    view = lambda a: a.reshape(bs, FFT_N1 // 2, n2 * ch)
    y = view(u3[0])
    for order in range(2):
        a = _left_mm(jnp.asarray(f1, BF16), y, mid_dtype)
        bm = _mid_stage(a.reshape(bs, 2, FFT_N1, n2, ch), g_spec[order],
                        jnp.asarray(w2, BF16), jnp.asarray(v2, BF16))
        bias_row = jnp.tile(bias[order], n2).reshape(1, n2 * ch)
        y = _left_mm(jnp.asarray(f3, BF16), bm.reshape(bs, 2 * FFT_N1, n2 * ch), F32,
                     epi=(view(u3[1 + order]), y, bias_row))
    return y.reshape(bs * l, ch)


def _hyena_spectrum(l, consts, w1, b1, w2m, b2, w_out, freq):
    f1_full, _, w2, _, _ = consts
    n2 = 2 * l // FFT_N1
    g_time = _hyena_filter_time(l, w1, b1, w2m, b2, w_out, freq)
    a = _left_mm(jnp.asarray(f1_full, F32), g_time.reshape(2, FFT_N1, n2 * B_CH), F32, hi=True)
    return _filt_mid_stage(a.reshape(2, 2, FFT_N1, n2, B_CH), jnp.asarray(w2, F32))


def _to_columns(t, b, rows):
    t = t.reshape((b, rows, GRID_W) + t.shape[1:])
    return jnp.swapaxes(t, 1, 2).reshape((b * rows * GRID_W,) + t.shape[3:])


def _to_raster(t, b, rows):
    t = t.reshape((b, GRID_W, rows) + t.shape[1:])
    return jnp.swapaxes(t, 1, 2).reshape((b * rows * GRID_W,) + t.shape[3:])


def _hyena(lay, z, conv_w, conv_b, spec_c, spec_x, bias, consts_c, consts_x):
    half = B_CH // 2
    rows = lay.seq // GRID_W
    z4 = z.reshape(lay.n, 3, B_CH)
    zl = z4[lay.n_ctx:]
    zl = jnp.concatenate([zl[..., :half], _to_columns(zl[..., half:], lay.b, rows)], axis=-1)
    zp = jnp.concatenate([z4[:lay.n_ctx], zl], axis=0).reshape(lay.n, 3 * B_CH)
    u3 = _dwconv(lay, zp, conv_w.reshape(B_SHORT, 3 * B_CH), conv_b.reshape(1, 3 * B_CH), False, B_CH, split=True)
    yc = _hyena_segment(u3[:, :lay.n_ctx], lay.ctx, lay.b, consts_c, spec_c, bias)
    yl = _hyena_segment(u3[:, lay.n_ctx:], lay.seq, lay.b, consts_x, spec_x, bias)
    yl = jnp.concatenate([yl[:, :half], _to_raster(yl[:, half:], lay.b, rows)], axis=-1)
    return jnp.concatenate([yc, yl], axis=0)


def _head_norm_gate(o, gate, norm_w, heads, width):
    outs = []
    for h in range(heads):
        oh = o[:, h * width:(h + 1) * width]
        y = oh * lax.rsqrt(jnp.mean(oh * oh, axis=-1, keepdims=True) + NORM_EPS) * norm_w
        outs.append(y * _silu(gate[:, h * width:(h + 1) * width]))
    return jnp.concatenate(outs, axis=1)


def _merge_kernel(x_ref, mod_ref, nw_ref, a0_ref, a1_ref, ag_ref, yb_ref, c0_ref, c1_ref, cx_ref, cz_ref,
                  d0_ref, d1_ref, dg_ref, an_ref, cd_ref, cn_ref, dn_ref, wg_ref, wb_ref, wo_ref, o_ref):
    xv = x_ref[...]
    m = mod_ref[...]
    hb = _prenorm(xv, nw_ref[1:2, :], m[3:4, :], m[4:5, :]).astype(BF16)
    ya = _head_norm_gate(a0_ref[...] + a1_ref[...], ag_ref[...], an_ref[...], A_HEADS, A_DV)
    yd = _head_norm_gate(d0_ref[...] + d1_ref[...], dg_ref[...], dn_ref[...], D_HEADS, D_DV)
    yc = (c0_ref[...] + c1_ref[...] + cd_ref[...] * cx_ref[...]) * _silu(cz_ref[...])
    gw = C_INNER // C_GROUPS
    cn = cn_ref[...]
    yc = jnp.concatenate(
        [yc[:, g * gw:(g + 1) * gw]
         * lax.rsqrt(jnp.mean(yc[:, g * gw:(g + 1) * gw] ** 2, axis=-1, keepdims=True) + NORM_EPS)
         * cn[:, g * gw:(g + 1) * gw] for g in range(C_GROUPS)], axis=1)
    acc = jnp.zeros(xv.shape, F32)
    for k, y in enumerate((ya, yb_ref[...], yc, yd)):
        gate = jax.nn.sigmoid(jnp.dot(hb, wg_ref[:, k * D_MODEL:(k + 1) * D_MODEL], preferred_element_type=F32))
        acc = acc + gate * jnp.dot(y.astype(BF16), wb_ref[k], preferred_element_type=F32)
    o_ref[...] = xv + m[5:6, :] * jnp.dot(acc.astype(BF16), wo_ref[...], preferred_element_type=F32)


def _merge(lay, x, mod, nw, a0, a1, ag, yb, c0, c1, cx, cz, d0, d1, dg, an, cd, cn, dn, wg, wb, wo):
    n, d = x.shape
    row = lambda w: pl.BlockSpec((ROW_TILE, w), lambda i: (i, 0))
    once = lambda a: pl.BlockSpec(a.shape, lambda i: (0,) * a.ndim, pipeline_mode=pl.Buffered(1))
    return pl.pallas_call(
        _merge_kernel,
        grid=(lay.tiles,),
        in_specs=[row(d), pl.BlockSpec((None, N_MOD, d), lambda i: (lay.mod_index(i), 0, 0)), _full(nw.shape)]
                 + [row(BRANCH_W)] * 11 + [_full(an.shape), _full(cd.shape), _full(cn.shape), _full(dn.shape),
                                           once(wg), once(wb), once(wo)],
        out_specs=row(d),
        out_shape=jax.ShapeDtypeStruct((n, d), F32),
        compiler_params=_cparams(("parallel",)), name="merge",
    )(x, mod, nw, a0, a1, ag, yb, c0, c1, cx, cz, d0, d1, dg, an, cd, cn, dn, wg, wb, wo)


def _final_norm_kernel(x_ref, w_ref, o_ref):
    xv = x_ref[...]
    o_ref[...] = xv * lax.rsqrt(jnp.mean(xv * xv, axis=-1, keepdims=True) + NORM_EPS) * w_ref[...]


def _final_norm(lay, x, w):
    d = x.shape[1]
    return pl.pallas_call(
        _final_norm_kernel,
        grid=(lay.b * lay.tl,),
        in_specs=[pl.BlockSpec((ROW_TILE, d), lambda i: (lay.nct + i, 0)), _full((1, d))],
        out_specs=pl.BlockSpec((ROW_TILE, d), lambda i: (i, 0)),
        out_shape=jax.ShapeDtypeStruct((lay.b * lay.seq, d), F32),
        compiler_params=_cparams(("parallel",)), name="final_norm",
    )(x, w.reshape(1, d))


def kernel(x, c, ctx, c_ctx, w_ada, b_ada, norm_w, ffn_up, ffn_down, w_in, gdn_conv, gdn_a_log, gdn_dt_bias, gdn_norm, hy_conv_w, hy_conv_b, hy_w1, hy_b1, hy_w2, hy_b2, hy_wout, hy_freq, hy_bias, ssd_conv_w, ssd_conv_b, ssd_a_log, ssd_dt_bias, ssd_d, ssd_norm, gla_gk_w, gla_gk_b, gla_norm, w_branch, w_out, final_norm):
    b, seq, d = x.shape
    ctx_len = ctx.shape[1]
    depth = w_ada.shape[0]
    lay = _Layout(b, ctx_len, seq)
    consts_c = _dft_consts(2 * ctx_len // FFT_N1)
    consts_x = _dft_consts(2 * seq // FFT_N1)

    rp = -(-(1 + b) // 8) * 8
    cond = jnp.concatenate([c_ctx[None, :], c, jnp.zeros((rp - 1 - b, d), F32)], axis=0)
    mods = _ada(cond, w_ada, b_ada).reshape(depth, rp, N_MOD, d)

    xf = jnp.concatenate([ctx.reshape(b * ctx_len, d), x.reshape(b * seq, d)], axis=0)
    for l in range(depth):
        mod = mods[l]
        nw = norm_w[l]
        w_r, w_gate = _rearrange_w_in(w_in[l])
        xf = _ffn(lay, xf, mod, nw, ffn_up[l, 0].astype(BF16), ffn_down[l, 0].astype(BF16), 0)

        a_qkv, a_gate, zb, c_z, c_xbc, d_qkv, d_gate, small = _inproj(lay, xf, mod, nw, w_r)
        qkv = _dwconv(lay, a_qkv, gdn_conv[l], jnp.zeros((1, A_QKV), F32), True, 512)
        a0, a1 = _gdn_scan(lay, qkv, small, gdn_a_log[l], gdn_dt_bias[l])
        xbc = _dwconv(lay, c_xbc, ssd_conv_w[l], ssd_conv_b[l].reshape(1, C_XBC), True, 256)
        c0, c1 = _ssd_scan(lay, xbc, small, ssd_a_log[l], ssd_dt_bias[l])
        d0, d1 = _gla_scan(lay, d_qkv, small, gla_gk_w[l], gla_gk_b[l])
        fargs = (hy_w1[l], hy_b1[l], hy_w2[l], hy_b2[l], hy_wout[l], hy_freq[l])
        spec_c = _hyena_spectrum(ctx_len, consts_c, *fargs)
        spec_x = _hyena_spectrum(seq, consts_x, *fargs)
        yb = _hyena(lay, zb, hy_conv_w[l], hy_conv_b[l], spec_c, spec_x, hy_bias[l], consts_c, consts_x)

        xf = _merge(lay, xf, mod, nw, a0, a1, a_gate, yb, c0, c1, xbc, c_z, d0, d1, d_gate,
                    gdn_norm[l].reshape(1, A_DV), jnp.repeat(ssd_d[l], C_HEADDIM).reshape(1, C_INNER),
                    ssd_norm[l].reshape(1, C_INNER), gla_norm[l].reshape(1, D_DV),
                    w_gate, w_branch[l].astype(BF16), w_out[l].astype(BF16))
        xf = _ffn(lay, xf, mod, nw, ffn_up[l, 1].astype(BF16), ffn_down[l, 1].astype(BF16), 2)
    return _final_norm(lay, xf, final_norm).reshape(b, seq, d)
```

```python
import functools
import math

import numpy as np
import jax
import jax.numpy as jnp
from jax import lax
from jax.experimental import pallas as pl
from jax.experimental.pallas import tpu as pltpu

F32 = jnp.float32
BF16 = jnp.bfloat16
HI = lax.Precision.HIGHEST

D_MODEL = 1024
GRID_W = 64
CHUNK = 64
NORM_EPS = 1e-6
N_MOD = 9
D_FF = 2816
SHORT_CONV = 5

A_HEADS, A_DK, A_DV = 4, 128, 128
A_QKV = A_HEADS * (2 * A_DK + A_DV)
B_CH, B_SHORT, B_EMB, B_FFN, B_INNER_MLPS = 512, 3, 33, 64, 2
B_WINDOW_SHIFT, B_DECAY_SHORT_PCT, B_DECAY_LONG_PCT, B_DECAY_TARGET = 0.05, 0.3, 1.5, 1e-2
C_HEADS, C_HEADDIM, C_GROUPS, C_STATE = 8, 64, 2, 64
C_INNER = C_HEADS * C_HEADDIM
C_XBC = C_INNER + 2 * C_GROUPS * C_STATE
D_HEADS, D_DK, D_DV, D_RANK = 4, 64, 128, 16
D_GATE_NORM = 16.0
N_BRANCH, BRANCH_W = 4, 512

IN_SIZES = (A_QKV, A_HEADS * A_DV, 2 * A_HEADS, 2 * A_HEADS, 3 * B_CH, C_INNER, C_XBC, 2 * C_HEADS,
            D_HEADS * (2 * D_DK + D_DV), D_HEADS * D_DV, 2 * D_RANK, N_BRANCH * D_MODEL)
IN_OFFS = tuple(int(v) for v in np.cumsum((0,) + IN_SIZES))

ROW_TILE = 256
SMALL_W = 128
VMEM_LIMIT = 56 * 1024 * 1024


def _cparams(sem):
    return pltpu.CompilerParams(dimension_semantics=sem, vmem_limit_bytes=VMEM_LIMIT)


def _mm(a, b):
    return jnp.dot(a.astype(BF16), b.astype(BF16), preferred_element_type=F32)


def _mm_nt(a, b):
    return lax.dot_general(a.astype(BF16), b.astype(BF16), (((1,), (1,)), ((), ())),
                           preferred_element_type=F32)


def _mm_hi(a, b):
    return jnp.dot(a, b, precision=HI, preferred_element_type=F32)


def _silu(v):
    return v * jax.nn.sigmoid(v)


def _softplus(v):
    return jnp.maximum(v, 0.0) + jnp.log1p(jnp.exp(-jnp.abs(v)))


def _log_sigmoid(v):
    return jnp.minimum(v, 0.0) - jnp.log1p(jnp.exp(-jnp.abs(v)))


def _prenorm(xv, gain, shift, scale):
    ms = jnp.mean(xv * xv, axis=-1, keepdims=True)
    return (xv * lax.rsqrt(ms + NORM_EPS) * gain) * (1.0 + scale) + shift


def _full(shape):
    nd = len(shape)
    return pl.BlockSpec(shape, lambda *_: (0,) * nd)


class _Layout:
    def __init__(self, batch, ctx_len, seq):
        assert ctx_len % ROW_TILE == 0 and seq % ROW_TILE == 0
        self.b, self.ctx, self.seq = batch, ctx_len, seq
        self.n_ctx = batch * ctx_len
        self.n = batch * (ctx_len + seq)
        self.tc, self.tl = ctx_len // ROW_TILE, seq // ROW_TILE
        self.nct = batch * self.tc
        self.tiles = self.n // ROW_TILE

    def mod_index(self, i):
        return jnp.where(i < self.nct, 0, 1 + (i - self.nct) // self.tl)

    def seg_first(self, i):
        if self.tl == 0:
            return lax.rem(i, self.tc) == 0
        return jnp.where(i < self.nct, lax.rem(i, self.tc) == 0, lax.rem(i - self.nct, self.tl) == 0)

    def seg_last(self, i):
        if self.tl == 0:
            return lax.rem(i, self.tc) == self.tc - 1
        return jnp.where(i < self.nct, lax.rem(i, self.tc) == self.tc - 1,
                         lax.rem(i - self.nct, self.tl) == self.tl - 1)

    def fwd_tile(self, b, s):
        return jnp.where(s < self.tc, b * self.tc + s, self.nct + b * self.tl + (s - self.tc))

    def rev_tile(self, b, s):
        return jnp.where(s < self.tc, b * self.tc + (self.tc - 1 - s),
                         self.nct + b * self.tl + (self.tl - 1 - (s - self.tc)))


def _ada_kernel(c_ref, w_ref, b_ref, o_ref):
    o_ref[...] = _mm_hi(_silu(c_ref[...]), w_ref[...]) + b_ref[...]


def _ada(cond, w_ada, b_ada):
    depth, d, nm = w_ada.shape
    rp = cond.shape[0]
    tn = 1152
    return pl.pallas_call(
        _ada_kernel,
        grid=(depth, nm // tn),
        in_specs=[_full((rp, d)),
                  pl.BlockSpec((None, d, tn), lambda l, j: (l, 0, j)),
                  pl.BlockSpec((None, 1, tn), lambda l, j: (l, 0, j))],
        out_specs=pl.BlockSpec((None, rp, tn), lambda l, j: (l, 0, j)),
        out_shape=jax.ShapeDtypeStruct((depth, rp, nm), F32),
        compiler_params=_cparams(("parallel", "parallel")), name="ada",
    )(cond, w_ada, b_ada.reshape(depth, 1, nm))


FF_CHUNK = 1408


def _ffn_kernel(x_ref, mod_ref, nw_ref, wup_ref, wdn_ref, o_ref, *, sub):
    xv = x_ref[...]
    m = mod_ref[...]
    h = _prenorm(xv, nw_ref[sub:sub + 1, :], m[3 * sub:3 * sub + 1, :], m[3 * sub + 1:3 * sub + 2, :])
    hb = h.astype(BF16)
    acc = jnp.zeros(xv.shape, F32)
    for c in range(D_FF // FF_CHUNK):
        lo = c * FF_CHUNK
        a = jnp.dot(hb, wup_ref[:, lo:lo + FF_CHUNK], preferred_element_type=F32)
        g = jnp.dot(hb, wup_ref[:, D_FF + lo:D_FF + lo + FF_CHUNK], preferred_element_type=F32)
        acc = acc + jnp.dot((_silu(a) * g).astype(BF16), wdn_ref[lo:lo + FF_CHUNK, :],
                            preferred_element_type=F32)
    o_ref[...] = xv + 0.5 * m[3 * sub + 2:3 * sub + 3, :] * acc


def _ffn(lay, x, mod, nw, wup, wdn, sub):
    n, d = x.shape
    return pl.pallas_call(
        functools.partial(_ffn_kernel, sub=sub),
        grid=(lay.tiles,),
        in_specs=[pl.BlockSpec((ROW_TILE, d), lambda i: (i, 0)),
                  pl.BlockSpec((None, N_MOD, d), lambda i: (lay.mod_index(i), 0, 0)),
                  _full(nw.shape),
                  pl.BlockSpec(wup.shape, lambda i: (0, 0), pipeline_mode=pl.Buffered(1)),
                  pl.BlockSpec(wdn.shape, lambda i: (0, 0), pipeline_mode=pl.Buffered(1))],
        out_specs=pl.BlockSpec((ROW_TILE, d), lambda i: (i, 0)),
        out_shape=jax.ShapeDtypeStruct((n, d), F32),
        compiler_params=_cparams(("parallel",)), name="ffn",
    )(x, mod, nw, wup, wdn)


PROJ_W = (A_QKV, A_HEADS * A_DV, 3 * B_CH, C_INNER, C_XBC, D_HEADS * (2 * D_DK + D_DV), D_HEADS * D_DV, SMALL_W)
PROJ_O = tuple(int(v) for v in np.cumsum((0,) + PROJ_W))


def _rearrange_w_in(w_in):
    o = IN_OFFS
    small = jnp.concatenate([w_in[:, o[2]:o[4]], w_in[:, o[7]:o[8]], w_in[:, o[10]:o[11]],
                             jnp.zeros((w_in.shape[0], SMALL_W - 64), w_in.dtype)], axis=1)
    w_r = jnp.concatenate([w_in[:, o[0]:o[1]], w_in[:, o[1]:o[2]], w_in[:, o[4]:o[5]], w_in[:, o[5]:o[6]],
                           w_in[:, o[6]:o[7]], w_in[:, o[8]:o[9]], w_in[:, o[9]:o[10]], small], axis=1)
    return w_r.astype(BF16), w_in[:, o[11]:o[12]].astype(BF16)


def _inproj_kernel(x_ref, mod_ref, nw_ref, w_ref, *o_refs):
    m = mod_ref[...]
    hb = _prenorm(x_ref[...], nw_ref[1:2, :], m[3:4, :], m[4:5, :]).astype(BF16)
    for k, o_ref in enumerate(o_refs):
        o_ref[...] = jnp.dot(hb, w_ref[:, PROJ_O[k]:PROJ_O[k + 1]], preferred_element_type=F32)


def _inproj(lay, x, mod, nw, w_r):
    n, d = x.shape
    return pl.pallas_call(
        _inproj_kernel,
        grid=(lay.tiles,),
        in_specs=[pl.BlockSpec((ROW_TILE, d), lambda i: (i, 0)),
                  pl.BlockSpec((None, N_MOD, d), lambda i: (lay.mod_index(i), 0, 0)),
                  _full(nw.shape),
                  pl.BlockSpec(w_r.shape, lambda i: (0, 0), pipeline_mode=pl.Buffered(1))],
        out_specs=[pl.BlockSpec((ROW_TILE, w), lambda i: (i, 0)) for w in PROJ_W],
        out_shape=[jax.ShapeDtypeStruct((n, w), F32) for w in PROJ_W],
        compiler_params=_cparams(("parallel",)), name="inproj",
    )(x, mod, nw, w_r)


HALO = 8


def _dwconv_kernel(xp_ref, x_ref, xn_ref, w_ref, b_ref, o_ref, buf, *, lay, taps, act):
    i = pl.program_id(0)
    keep_p = jnp.where(lay.seg_first(i), 0.0, 1.0)
    keep_n = jnp.where(lay.seg_last(i), 0.0, 1.0)
    buf[0:HALO, :] = xp_ref[...] * keep_p
    buf[HALO:HALO + ROW_TILE, :] = x_ref[...]
    buf[HALO + ROW_TILE:2 * HALO + ROW_TILE, :] = xn_ref[...] * keep_n
    left = (taps - 1) // 2
    acc = b_ref[...] + buf[HALO - left:HALO - left + ROW_TILE, :] * w_ref[0:1, :]
    for k in range(1, taps):
        acc = acc + buf[HALO - left + k:HALO - left + k + ROW_TILE, :] * w_ref[k:k + 1, :]
    o_ref[...] = _silu(acc) if act else acc


def _dwconv(lay, x, w, bias, act, wb, split=False):
    n, width = x.shape
    taps = w.shape[0]
    nwb = width // wb
    r8 = ROW_TILE // HALO
    last8 = n // HALO - 1
    if split:
        out_spec = pl.BlockSpec((None, ROW_TILE, wb), lambda i, j: (j, i, 0))
        out_shape = jax.ShapeDtypeStruct((nwb, n, wb), F32)
    else:
        out_spec = pl.BlockSpec((ROW_TILE, wb), lambda i, j: (i, j))
        out_shape = jax.ShapeDtypeStruct((n, width), F32)
    return pl.pallas_call(
        functools.partial(_dwconv_kernel, lay=lay, taps=taps, act=act),
        grid=(lay.tiles, nwb),
        in_specs=[pl.BlockSpec((HALO, wb), lambda i, j: (jnp.maximum(i * r8 - 1, 0), j)),
                  pl.BlockSpec((ROW_TILE, wb), lambda i, j: (i, j)),
                  pl.BlockSpec((HALO, wb), lambda i, j: (jnp.minimum((i + 1) * r8, last8), j)),
                  pl.BlockSpec((taps, wb), lambda i, j: (0, j)),
                  pl.BlockSpec((1, wb), lambda i, j: (0, j))],
        out_specs=out_spec,
        out_shape=out_shape,
        scratch_shapes=[pltpu.VMEM((ROW_TILE + 2 * HALO, wb), F32)],
        compiler_params=_cparams(("parallel", "parallel")), name="dwconv",
    )(x, x, x, w, bias)


def _tri_masks(rev):
    ii = lax.broadcasted_iota(jnp.int32, (CHUNK, CHUNK), 0)
    jj = lax.broadcasted_iota(jnp.int32, (CHUNK, CHUNK), 1)
    return (ii <= jj, ii < jj) if rev else (ii >= jj, ii > jj)


def _masked_decay(col, row, incl):
    return jnp.where(incl, jnp.exp(jnp.where(incl, col - row, 0.0)), 0.0)


def _chunk_order(rev):
    n = ROW_TILE // CHUNK
    return range(n - 1, -1, -1) if rev else range(n)


def _scan_call(kernel, lay, ins_tiled, ins_full, out_w, scratch):
    steps = lay.tc + lay.tl
    specs = []
    args = []
    for walk in (lay.fwd_tile, lay.rev_tile):
        for a in ins_tiled:
            specs.append(pl.BlockSpec((ROW_TILE, a.shape[1]), lambda b, s, walk=walk: (walk(b, s), 0)))
            args.append(a)
    for a in ins_full:
        specs.append(_full(a.shape))
        args.append(a)
    return pl.pallas_call(
        kernel,
        grid=(lay.b, steps),
        in_specs=specs,
        out_specs=[pl.BlockSpec((ROW_TILE, out_w), lambda b, s: (lay.fwd_tile(b, s), 0)),
                   pl.BlockSpec((ROW_TILE, out_w), lambda b, s: (lay.rev_tile(b, s), 0))],
        out_shape=[jax.ShapeDtypeStruct((lay.n, out_w), F32)] * 2,
        scratch_shapes=scratch,
        compiler_params=_cparams(("arbitrary", "arbitrary")), name=kernel.__name__.strip("_"),
    )(*args)


def _ssd_kernel(xf_ref, sf_ref, xr_ref, sr_ref, alog_ref, dtb_ref, alog_t_ref, dtb_t_ref, exp_ref,
                of_ref, or_ref, st_ref):
    @pl.when(pl.program_id(1) == 0)
    def _():
        st_ref[...] = jnp.zeros(st_ref.shape, F32)

    hpg = C_HEADS // C_GROUPS
    gw = hpg * C_HEADDIM
    nch = ROW_TILE // CHUNK
    refs = ((xf_ref, sf_ref, of_ref), (xr_ref, sr_ref, or_ref))
    units, pre = [], {}
    for d, (x_ref, s_ref, _) in enumerate(refs):
        incl, _ = _tri_masks(d == 1)
        m_col = incl.astype(F32)
        m_row = m_col.T
        sm = s_ref[...]
        sm_t = sm.T
        dt_all = _softplus(sm[:, 16:32] + dtb_ref[...])
        a_all = -jnp.exp(alog_ref[...]) * dt_all
        a_t_all = -jnp.exp(alog_t_ref[...]) * _softplus(sm_t[16:32, :] + dtb_t_ref[...])
        bm_t = x_ref[:, C_INNER:C_INNER + C_GROUPS * C_STATE].T
        for c in range(nch):
            rows = slice(c * CHUNK, (c + 1) * CHUNK)
            units.append((d, c))
            pre[d, c] = dict(incl=incl, m_col=m_col, m_row=m_row, a_ch=a_all[rows], a_t=a_t_all[:, rows],
                             dt=dt_all[rows], bm_t=bm_t[:, rows], e=exp_ref[d], xs=x_ref[rows, 0:C_INNER],
                             bm=x_ref[rows, C_INNER:C_INNER + C_GROUPS * C_STATE],
                             cm=x_ref[rows, C_INNER + C_GROUPS * C_STATE:C_XBC])
    for u in units:
        w = pre[u]
        w["ac"] = _mm_hi(w["m_col"], w["a_ch"])
        w["ac_t"] = _mm_hi(w["a_t"], w["m_row"])
        w["alast"] = jnp.sum(w["a_ch"], axis=0, keepdims=True)
    for u in units:
        w = pre[u]
        w["xdt"] = w["xs"] * _mm_hi(w["dt"], w["e"])
        w["eac_e"] = _mm_hi(jnp.exp(w["ac"]), w["e"])
        w["xd"] = w["xdt"] * _mm_hi(jnp.exp(w["alast"] - w["ac"]), w["e"])
        w["dle"] = _mm_hi(jnp.broadcast_to(jnp.exp(w["alast"]), (8, 2 * C_HEADS)), w["e"])[0:1]
    grp = lambda a, g: a[:, g * C_STATE:(g + 1) * C_STATE]
    cb = {(u, g): _mm_nt(grp(pre[u]["cm"], g), grp(pre[u]["bm"], g)) for u in units for g in range(C_GROUPS)}
    upd = {(u, g): _mm(pre[u]["bm_t"][g * C_STATE:(g + 1) * C_STATE], pre[u]["xd"][:, g * gw:(g + 1) * gw])
           for u in units for g in range(C_GROUPS)}
    y_diag = {}
    for u in units:
        w = pre[u]
        for h in range(C_HEADS):
            col = u[0] * C_HEADS + h
            seg = _masked_decay(w["ac"][:, col:col + 1], w["ac_t"][col:col + 1, :], w["incl"])
            y_diag[u, h] = _mm(cb[u, h // hpg] * seg, w["xdt"][:, h * C_HEADDIM:(h + 1) * C_HEADDIM])
    st = {(d, g): st_ref[d, g] for d in range(2) for g in range(C_GROUPS)}
    entry = {}
    for p in range(nch):
        for d in range(2):
            u = (d, (nch - 1 - p) if d == 1 else p)
            for g in range(C_GROUPS):
                entry[u, g] = st[d, g]
                st[d, g] = pre[u]["dle"][:, g * gw:(g + 1) * gw] * st[d, g] + upd[u, g]
    for d in range(2):
        for g in range(C_GROUPS):
            st_ref[d, g] = st[d, g]
    for u in units:
        y_off = [_mm(grp(pre[u]["cm"], g), entry[u, g]) for g in range(C_GROUPS)]
        refs[u[0]][2][u[1] * CHUNK:(u[1] + 1) * CHUNK, :] = (
            jnp.concatenate([y_diag[u, h] for h in range(C_HEADS)], axis=1)
            + pre[u]["eac_e"] * jnp.concatenate(y_off, axis=1))


def _ssd_scan(lay, xbc, small, a_log, dt_bias):
    expand = np.zeros((2, 2 * C_HEADS, C_INNER), np.float32)
    for d in range(2):
        for h in range(C_HEADS):
            expand[d, d * C_HEADS + h, h * C_HEADDIM:(h + 1) * C_HEADDIM] = 1.0
    al = a_log.reshape(1, 2 * C_HEADS)
    db = dt_bias.reshape(1, 2 * C_HEADS)
    return _scan_call(_ssd_kernel, lay, (xbc, small), (al, db, al.T, db.T, jnp.asarray(expand)), C_INNER,
                      [pltpu.VMEM((2, C_GROUPS, C_STATE, C_INNER // C_GROUPS), F32)])


def _gla_kernel(xf_ref, sf_ref, xr_ref, sr_ref, gkw_ref, gkb_ref, of_ref, or_ref, st_ref):
    @pl.when(pl.program_id(1) == 0)
    def _():
        st_ref[...] = jnp.zeros(st_ref.shape, F32)

    nk = D_HEADS * D_DK
    nch = ROW_TILE // CHUNK
    refs = ((xf_ref, sf_ref, of_ref), (xr_ref, sr_ref, or_ref))
    units, pre = [], {}
    for d, (x_ref, s_ref, _) in enumerate(refs):
        incl, _ = _tri_masks(d == 1)
        m_col = incl.astype(F32)
        lr = s_ref[:, 32 + D_RANK * d:32 + D_RANK * (d + 1)]
        g_all = _log_sigmoid(_mm_hi(lr, gkw_ref[d]) + gkb_ref[d:d + 1, :]) / D_GATE_NORM
        v_t = x_ref[:, 2 * nk:].T
        for c in range(nch):
            rows = slice(c * CHUNK, (c + 1) * CHUNK)
            g_ch = g_all[rows]
            gc = _mm_hi(m_col, g_ch)
            glast = jnp.sum(g_ch, axis=0, keepdims=True)
            q = x_ref[rows, 0:nk] * (D_DK ** -0.5)
            k = x_ref[rows, nk:2 * nk]
            v = x_ref[rows, 2 * nk:]
            kd = k * jnp.exp(glast - gc)
            qd = q * jnp.exp(gc)
            qr = q * jnp.exp(gc - glast)
            dlast = jnp.exp(glast)
            for h in range(D_HEADS):
                ks = slice(h * D_DK, (h + 1) * D_DK)
                vs = slice(h * D_DV, (h + 1) * D_DV)
                units.append((d, c, h))
                pre[d, c, h] = dict(incl=incl, qr=qr[:, ks], kd=kd[:, ks], qd=qd[:, ks], v=v[:, vs],
                                    v_t=v_t[vs, rows], dl=dlast[:, ks])
    aqk = [jnp.where(pre[u]["incl"], _mm_nt(pre[u]["qr"], pre[u]["kd"]), 0.0) for u in units]
    upd = {u: _mm(pre[u]["v_t"], pre[u]["kd"]) for u in units}
    intra = {u: _mm(a, pre[u]["v"]) for u, a in zip(units, aqk)}
    heads = [(d, h) for d in range(2) for h in range(D_HEADS)]
    st = {dh: st_ref[dh[0], dh[1]] for dh in heads}
    entry = {}
    for p in range(nch):
        for d, h in heads:
            u = (d, (nch - 1 - p) if d == 1 else p, h)
            entry[u] = st[d, h]
            st[d, h] = st[d, h] * pre[u]["dl"] + upd[u]
    for dh in heads:
        st_ref[dh[0], dh[1]] = st[dh]
    inter = {u: _mm_nt(pre[u]["qd"], entry[u]) for u in units}
    for d, (_, _, o_ref) in enumerate(refs):
        for c in range(nch):
            o_ref[c * CHUNK:(c + 1) * CHUNK, :] = jnp.concatenate(
                [intra[d, c, h] + inter[d, c, h] for h in range(D_HEADS)], axis=1)


def _gla_scan(lay, qkv, small, gk_w, gk_b):
    return _scan_call(_gla_kernel, lay, (qkv, small), (gk_w, gk_b), D_HEADS * D_DV,
                      [pltpu.VMEM((2, D_HEADS, D_DV, D_DK), F32)])


TRI_BLOCK = 16


def _unit_lower_inverse(nms, eye, blk):
    dg = [jnp.where(blk, nm, 0.0) for nm in nms]
    off = [nm - d for nm, d in zip(nms, dg)]
    t0 = [eye - d for d in dg]
    p = [_mm(d, d) for d in dg]
    for it in range(3):
        t0 = [t + _mm(t, q) for t, q in zip(t0, p)]
        if it < 2:
            p = [_mm(q, q) for q in p]
    m = [_mm(t, o) for t, o in zip(t0, off)]
    m2 = [_mm(a, a) for a in m]
    r = [eye - a for a in m]
    r = [a + _mm(a, b) for a, b in zip(r, m2)]
    return [_mm(a, t) for a, t in zip(r, t0)]


def _gdn_kernel(xf_ref, sf_ref, xr_ref, sr_ref, alog_ref, dtb_ref, alog_t_ref, dtb_t_ref,
                of_ref, or_ref, st_ref):
    @pl.when(pl.program_id(1) == 0)
    def _():
        st_ref[...] = jnp.zeros(st_ref.shape, F32)

    nk = A_HEADS * A_DK
    nch = ROW_TILE // CHUNK
    ii = lax.broadcasted_iota(jnp.int32, (CHUNK, CHUNK), 0)
    jj = lax.broadcasted_iota(jnp.int32, (CHUNK, CHUNK), 1)
    eye = (ii == jj).astype(F32)
    blk = (ii // TRI_BLOCK) == (jj // TRI_BLOCK)
    refs = ((xf_ref, sf_ref, of_ref), (xr_ref, sr_ref, or_ref))

    units, pre = [], {}
    for d, (x_ref, s_ref, _) in enumerate(refs):
        incl, strict = _tri_masks(d == 1)
        m_col = incl.astype(F32)
        m_row = m_col.T
        sm = s_ref[...]
        sm_t = sm.T
        beta_all = jax.nn.sigmoid(sm[:, 0:8])
        g_all = -jnp.exp(alog_ref[...]) * _softplus(sm[:, 8:16] + dtb_ref[...])
        g_t_all = -jnp.exp(alog_t_ref[...]) * _softplus(sm_t[8:16, :] + dtb_t_ref[...])
        qn, kn = [], []
        for h in range(A_HEADS):
            qh = x_ref[:, h * A_DK:(h + 1) * A_DK]
            kh = x_ref[:, nk + h * A_DK:nk + (h + 1) * A_DK]
            qn.append(qh * lax.rsqrt(jnp.sum(qh * qh, axis=-1, keepdims=True) + NORM_EPS) * (A_DK ** -0.5))
            kn.append(kh * lax.rsqrt(jnp.sum(kh * kh, axis=-1, keepdims=True) + NORM_EPS))
        kn_t = [kh.T for kh in kn]
        for c in range(nch):
            rows = slice(c * CHUNK, (c + 1) * CHUNK)
            g_ch = g_all[rows]
            gc = _mm_hi(m_col, g_ch)
            gc_t = _mm_hi(g_t_all[:, rows], m_row)
            glast = jnp.sum(g_ch, axis=0, keepdims=True)
            for h in range(A_HEADS):
                col = d * A_HEADS + h
                gcc, gcr, gl = gc[:, col:col + 1], gc_t[col:col + 1, :], glast[:, col:col + 1]
                bh = beta_all[rows, col:col + 1]
                kh = kn[h][rows]
                units.append((d, c, h))
                pre[d, c, h] = dict(
                    dmask=_masked_decay(gcc, gcr, incl), strict=strict, egc=jnp.exp(gcc), kh=kh, kb=kh * bh,
                    qh=qn[h][rows], vb=x_ref[rows, 2 * nk + h * A_DV:2 * nk + (h + 1) * A_DV] * bh,
                    kg_t=kn_t[h][:, rows] * jnp.exp(gl - gcr), dl=jnp.exp(gl))
    kk = [_mm_nt(pre[u]["kb"], pre[u]["kh"]) for u in units]
    qk = [_mm_nt(pre[u]["qh"], pre[u]["kh"]) for u in units]
    tinv = _unit_lower_inverse(
        [jnp.where(pre[u]["strict"], a * pre[u]["dmask"], 0.0) for u, a in zip(units, kk)], eye, blk)
    uw = [_mm(t, jnp.concatenate([pre[u]["vb"], pre[u]["kb"] * pre[u]["egc"]], axis=1))
          for u, t in zip(units, tinv)]
    for u, a, b in zip(units, uw, qk):
        pre[u]["u"] = a[:, :A_DV]
        pre[u]["wq"] = jnp.concatenate([a[:, A_DV:], pre[u]["qh"] * pre[u]["egc"]], axis=0)
        pre[u]["aqk"] = b * pre[u]["dmask"]

    heads = [(d, h) for d in range(2) for h in range(A_HEADS)]
    st = {dh: st_ref[dh[0], dh[1]] for dh in heads}
    for p in range(nch):
        us = [(d, (nch - 1 - p) if d == 1 else p, h) for d, h in heads]
        ws = [_mm(pre[u]["wq"], st[u[0], u[2]]) for u in us]
        v_new = [pre[u]["u"] - w[:CHUNK] for u, w in zip(us, ws)]
        outs = [w[CHUNK:] + _mm(pre[u]["aqk"], v) for u, w, v in zip(us, ws, v_new)]
        for u, v in zip(us, v_new):
            st[u[0], u[2]] = pre[u]["dl"] * st[u[0], u[2]] + _mm(pre[u]["kg_t"], v)
        for d, (_, _, o_ref) in enumerate(refs):
            c = us[d * A_HEADS][1]
            o_ref[c * CHUNK:(c + 1) * CHUNK, :] = jnp.concatenate(outs[d * A_HEADS:(d + 1) * A_HEADS], axis=1)
    for dh in heads:
        st_ref[dh[0], dh[1]] = st[dh]


def _gdn_scan(lay, qkv, small, a_log, dt_bias):
    al = a_log.reshape(1, 2 * A_HEADS)
    db = dt_bias.reshape(1, 2 * A_HEADS)
    return _scan_call(_gdn_kernel, lay, (qkv, small), (al, db, al.T, db.T), A_HEADS * A_DV,
                      [pltpu.VMEM((2, A_HEADS, A_DK, A_DV), F32)])


FFT_N1 = 64
FFT_KB = 8


def _dft_consts(n2):
    n1 = FFT_N1
    m = n1 * n2
    k1 = np.arange(n1, dtype=np.float64)
    ang1 = 2.0 * np.pi * np.outer(k1, k1) / n1
    f1_full = np.concatenate([np.cos(ang1), -np.sin(ang1)], axis=0)
    f1 = f1_full[:, :n1 // 2]
    f3 = np.concatenate([np.cos(ang1[:, :n1 // 2]).T, -np.sin(ang1[:, :n1 // 2]).T], axis=1) / m
    j = np.arange(n2, dtype=np.float64)
    theta = 2.0 * np.pi * (np.outer(j, j)[None] / n2 + (k1[:, None, None] * j[None, None, :]) / m)
    wr, wi = np.cos(theta), -np.sin(theta)
    w2 = np.concatenate([np.concatenate([wr, -wi], axis=2), np.concatenate([wi, wr], axis=2)], axis=1)
    phi = 2.0 * np.pi * (np.outer(j, j)[None] / n2 + (k1[:, None, None] * j[None, :, None]) / m)
    vr, vi = np.cos(phi), np.sin(phi)
    v2 = np.concatenate([np.concatenate([vr, -vi], axis=2), np.concatenate([vi, vr], axis=2)], axis=1)
    return f1_full, f1, w2, v2, f3


def _left_mm_kernel(w_ref, x_ref, o_ref, *, hi):
    if hi:
        o_ref[...] = _mm_hi(w_ref[...], x_ref[...]).astype(o_ref.dtype)
    else:
        o_ref[...] = _mm(w_ref[...], x_ref[...]).astype(o_ref.dtype)


def _left_mm_epi_kernel(w_ref, x_ref, xn_ref, yp_ref, bias_ref, o_ref):
    conv = _mm(w_ref[...], x_ref[...])
    yp = yp_ref[...]
    o_ref[...] = xn_ref[...] * (conv + yp * bias_ref[...])


def _left_mm(wm, x, out_dtype, hi=False, epi=None):
    bs, kdim, ncols = x.shape
    mdim = wm.shape[0]
    tn = min(ncols, 4096)
    xspec = lambda rows: pl.BlockSpec((None, rows, tn), lambda b, j: (b, 0, j))
    if epi is None:
        kern, ins, specs = functools.partial(_left_mm_kernel, hi=hi), (wm, x), [_full(wm.shape), xspec(kdim)]
    else:
        kern, ins = _left_mm_epi_kernel, (wm, x) + tuple(epi)
        specs = [_full(wm.shape), xspec(kdim), xspec(mdim), xspec(mdim), pl.BlockSpec((1, tn), lambda b, j: (0, j))]
    return pl.pallas_call(
        kern, grid=(bs, ncols // tn), in_specs=specs, out_specs=xspec(mdim),
        out_shape=jax.ShapeDtypeStruct((bs, mdim, ncols), out_dtype),
        compiler_params=_cparams(("parallel", "parallel")),
        name="dft_outer" + ("_gate" if epi is not None else "_hi" if hi else ""),
    )(*ins)


def _mid_kernel(a_ref, g_ref, w_ref, v_ref, o_ref, *, n2):
    for kk in range(FFT_KB):
        a = jnp.concatenate([a_ref[0, kk], a_ref[1, kk]], axis=0)
        xf = _mm(w_ref[kk], a)
        xr, xi = xf[:n2], xf[n2:]
        gr, gi = g_ref[kk, 0], g_ref[kk, 1]
        y = jnp.concatenate([xr * gr - xi * gi, xr * gi + xi * gr], axis=0)
        bm = _mm(v_ref[kk], y)
        o_ref[0, kk] = bm[:n2].astype(o_ref.dtype)
        o_ref[1, kk] = bm[n2:].astype(o_ref.dtype)


def _mid_stage(a, g, w2, v2):
    bs, _, n1, n2, ch = a.shape
    return pl.pallas_call(
        functools.partial(_mid_kernel, n2=n2),
        grid=(bs, n1 // FFT_KB),
        in_specs=[pl.BlockSpec((None, 2, FFT_KB, n2, ch), lambda b, j: (b, 0, j, 0, 0)),
                  pl.BlockSpec((FFT_KB, 2, n2, ch), lambda b, j: (j, 0, 0, 0)),
                  pl.BlockSpec((FFT_KB, 2 * n2, 2 * n2), lambda b, j: (j, 0, 0)),
                  pl.BlockSpec((FFT_KB, 2 * n2, 2 * n2), lambda b, j: (j, 0, 0))],
        out_specs=pl.BlockSpec((None, 2, FFT_KB, n2, ch), lambda b, j: (b, 0, j, 0, 0)),
        out_shape=jax.ShapeDtypeStruct(a.shape, a.dtype),
        compiler_params=_cparams(("parallel", "parallel")), name="dft_mid",
    )(a, g, w2, v2)


def _filt_mid_kernel(a_ref, w_ref, o_ref, *, n2):
    for kk in range(FFT_KB):
        a = jnp.concatenate([a_ref[0, kk], a_ref[1, kk]], axis=0)
        xf = _mm_hi(w_ref[kk], a)
        o_ref[kk, 0] = xf[:n2]
        o_ref[kk, 1] = xf[n2:]


def _filt_mid_stage(a, w2):
    bs, _, n1, n2, ch = a.shape
    return pl.pallas_call(
        functools.partial(_filt_mid_kernel, n2=n2),
        grid=(bs, n1 // FFT_KB),
        in_specs=[pl.BlockSpec((None, 2, FFT_KB, n2, ch), lambda b, j: (b, 0, j, 0, 0)),
                  pl.BlockSpec((FFT_KB, 2 * n2, 2 * n2), lambda b, j: (j, 0, 0))],
        out_specs=pl.BlockSpec((None, FFT_KB, 2, n2, ch), lambda b, j: (b, j, 0, 0, 0)),
        out_shape=jax.ShapeDtypeStruct((bs, n1, 2, n2, ch), F32),
        compiler_params=_cparams(("parallel", "parallel")), name="filter_dft_mid",
    )(a, w2)


FILT_ROWS = 512


def _filter_kernel(z_ref, win_ref, sel_ref, w1_ref, b1_ref, w2_ref, b2_ref, wo_ref, fr_ref, o_ref):
    fr = fr_ref[...]
    h = jnp.sin(fr * (_mm_hi(z_ref[...], w1_ref[...]) + b1_ref[...]))
    for i in range(B_INNER_MLPS):
        h = jnp.sin(fr * (_mm_hi(h, w2_ref[i]) + b2_ref[i]))
    ho = _mm_hi(h, wo_ref[...])
    fwd = sel_ref[...] > 0.5
    win = win_ref[...]
    for o in range(2):
        base = o * 2 * B_CH
        o_ref[o] = jnp.where(fwd, ho[:, base:base + B_CH], ho[:, base + B_CH:base + 2 * B_CH]) * win


def _hyena_filter_time(l, w1, b1, w2, b2, w_out, freq):
    bands = (B_EMB - 1) // 2
    circ = jnp.arange(2 * l)
    pos = jnp.where(circ < l, circ, 2 * l - circ)
    pos = jnp.where(circ == l, 0, pos)
    t = (jnp.linspace(0.0, 1.0, l, dtype=F32)[pos])[:, None]
    ang = 2.0 * math.pi * pos.astype(F32)[:, None] / l
    fr = jnp.linspace(1e-4, bands - 1, bands, dtype=F32)[None, :]
    z = jnp.concatenate([t, jnp.cos(fr * ang), -jnp.sin(fr * ang)], axis=-1)
    z = jnp.pad(z, ((0, 0), (0, 128 - B_EMB)))
    max_decay = math.log(B_DECAY_TARGET) / B_DECAY_SHORT_PCT
    min_decay = math.log(B_DECAY_TARGET) / B_DECAY_LONG_PCT
    deltas = jnp.abs(jnp.linspace(min_decay, max_decay, B_CH, dtype=F32))
    win = (jnp.exp(-t * deltas) + B_WINDOW_SHIFT) * (circ != l).astype(F32)[:, None]
    sel = (circ < l).astype(F32)[:, None]
    w1p = jnp.pad(w1, ((0, 128 - B_EMB), (0, 0)))
    rows = min(FILT_ROWS, 2 * l)
    return pl.pallas_call(
        _filter_kernel,
        grid=(2 * l // rows,),
        in_specs=[pl.BlockSpec((rows, 128), lambda i: (i, 0)),
                  pl.BlockSpec((rows, B_CH), lambda i: (i, 0)),
                  pl.BlockSpec((rows, 1), lambda i: (i, 0)),
                  _full(w1p.shape), _full((1, B_FFN)), _full(w2.shape), _full((B_INNER_MLPS, 1, B_FFN)),
                  _full(w_out.shape), _full((1, B_FFN))],
        out_specs=pl.BlockSpec((2, rows, B_CH), lambda i: (0, i, 0)),
        out_shape=jax.ShapeDtypeStruct((2, 2 * l, B_CH), F32),
        compiler_params=_cparams(("parallel",)), name="hyena_filter",
    )(z, win, sel, w1p, b1.reshape(1, B_FFN), w2, b2.reshape(B_INNER_MLPS, 1, B_FFN), w_out,
      freq.reshape(1, B_FFN))


def _hyena_segment(u3, l, bs, consts, g_spec, bias):
    f1_full, f1, w2, v2, f3 = consts
    n2 = 2 * l // FFT_N1
    ch = B_CH
    mid_dtype = BF16 if n2 % 16 == 0 else F32
    view = lambda a: a.reshape(bs, FFT_N1 // 2, n2 * ch)
    y = view(u3[0])
    for order in range(2):
        a = _left_mm(jnp.asarray(f1, BF16), y, mid_dtype)
        bm = _mid_stage(a.reshape(bs, 2, FFT_N1, n2, ch), g_spec[order],
                        jnp.asarray(w2, BF16), jnp.asarray(v2, BF16))
        bias_row = jnp.tile(bias[order], n2).reshape(1, n2 * ch)
        y = _left_mm(jnp.asarray(f3, BF16), bm.reshape(bs, 2 * FFT_N1, n2 * ch), F32,
                     epi=(view(u3[1 + order]), y, bias_row))
    return y.reshape(bs * l, ch)


def _hyena_spectrum(l, consts, w1, b1, w2m, b2, w_out, freq):
    f1_full, _, w2, _, _ = consts
    n2 = 2 * l // FFT_N1
    g_time = _hyena_filter_time(l, w1, b1, w2m, b2, w_out, freq)
    a = _left_mm(jnp.asarray(f1_full, F32), g_time.reshape(2, FFT_N1, n2 * B_CH), F32, hi=True)
    return _filt_mid_stage(a.reshape(2, 2, FFT_N1, n2, B_CH), jnp.asarray(w2, F32))


HY_SLAB = 128
HY_PAD = 8
HY_KG = 8


def _hyena_lat_kernel(zv_ref, z1_ref, z2_ref, cw_ref, cb_ref, g_ref, bias_ref, f1_ref, w2_ref, v2_ref, f3_ref,
                      o_ref, useq, ur, a_re, a_im, cbuf, *, col_mode, seq):
    n1h = FFT_N1 // 2
    n2c = seq // n1h
    pitch = n2c + HY_PAD
    grows = seq // GRID_W
    row_id = lax.broadcasted_iota(jnp.int32, (seq, HY_SLAB), 0)

    def short_conv(z_ref, k):
        z = z_ref[...]
        w = cw_ref[k]
        if col_mode:
            g0, gl = z[:GRID_W], z[seq - GRID_W:]
            cc = row_id[:GRID_W]
            wrap_p = jnp.where(cc == 0, 0.0, pltpu.roll(gl, 1, axis=0))
            wrap_n = jnp.where(cc == GRID_W - 1, 0.0, pltpu.roll(g0, GRID_W - 1, axis=0))
            prev = jnp.concatenate([wrap_p, z[:seq - GRID_W]], axis=0)
            nxt = jnp.concatenate([z[GRID_W:], wrap_n], axis=0)
        else:
            prev = jnp.where(row_id == 0, 0.0, pltpu.roll(z, 1, axis=0))
            nxt = jnp.where(row_id == seq - 1, 0.0, pltpu.roll(z, seq - 1, axis=0))
        return prev * w[0:1] + z * w[1:2] + nxt * w[2:3] + cb_ref[k]

    def seq_start(j):
        return (GRID_W * lax.rem(j, grows) + j // grows) if col_mode else j

    seq_stride = 2 if col_mode else pitch
    seq_ref = ur if col_mode else useq

    def put_seq(val):
        if col_mode:
            ur[...] = val
        else:
            for n1 in range(n1h):
                useq[n1 * pitch:n1 * pitch + n2c, :] = val[n1 * n2c:(n1 + 1) * n2c]

    def conv_out():
        if col_mode:
            return cbuf[...]
        return jnp.concatenate([cbuf[n1 * pitch:n1 * pitch + n2c, :] for n1 in range(n1h)], axis=0)

    y_prev = short_conv(zv_ref, 0)
    x_next = (short_conv(z1_ref, 1), short_conv(z2_ref, 2))
    put_seq(y_prev)
    f1 = f1_ref[...]
    f3 = f3_ref[...]
    for order in range(2):
        def stage1(j, carry):
            x = seq_ref[pl.ds(seq_start(j), n1h, stride=seq_stride), :]
            a = _mm(f1, x)
            a_re[pl.ds(j, FFT_N1, stride=pitch), :] = a[:FFT_N1]
            a_im[pl.ds(j, FFT_N1, stride=pitch), :] = a[FFT_N1:]
            return carry
        lax.fori_loop(0, n2c, stage1, 0, unroll=4)

        def mid(kg, carry):
            k1s = [kg * HY_KG + kk for kk in range(HY_KG)]
            offs = [pl.multiple_of(k1 * pitch, 8) for k1 in k1s]
            xin = [jnp.concatenate([a_re[pl.ds(o, n2c), :], a_im[pl.ds(o, n2c), :]], axis=0) for o in offs]
            xf = [_mm(w2_ref[k1], a) for k1, a in zip(k1s, xin)]
            ys = []
            for k1, x in zip(k1s, xf):
                xr, xi = x[:n2c], x[n2c:]
                gr, gi = g_ref[order, k1, 0], g_ref[order, k1, 1]
                ys.append(jnp.concatenate([xr * gr - xi * gi, xr * gi + xi * gr], axis=0))
            bm = [_mm(v2_ref[k1], y) for k1, y in zip(k1s, ys)]
            for o, b in zip(offs, bm):
                a_re[pl.ds(o, n2c), :] = b[:n2c]
                a_im[pl.ds(o, n2c), :] = b[n2c:]
            return carry
        lax.fori_loop(0, FFT_N1 // HY_KG, mid, 0)

        def stage3(j, carry):
            b = jnp.concatenate([a_re[pl.ds(j, FFT_N1, stride=pitch), :],
                                 a_im[pl.ds(j, FFT_N1, stride=pitch), :]], axis=0)
            cbuf[pl.ds(seq_start(j), n1h, stride=seq_stride), :] = _mm(f3, b)
            return carry
        lax.fori_loop(0, n2c, stage3, 0, unroll=4)

        y_prev = x_next[order] * (conv_out() + y_prev * bias_ref[order:order + 1, :])
        if order == 0:
            put_seq(y_prev)
    o_ref[...] = y_prev


def _hyena_latent(lay, z, conv_w, conv_b, spec_x, bias, consts_x, col_mode):
    _, f1, w2, v2, f3 = consts_x
    seq = lay.seq
    assert lay.n_ctx % seq == 0
    row0 = lay.n_ctx // seq
    n2c = seq // (FFT_N1 // 2)
    pitch = n2c + HY_PAD
    nslab = B_CH // 2 // HY_SLAB
    s0 = nslab if col_mode else 0
    cps = B_CH // HY_SLAB
    zspec = lambda k: pl.BlockSpec((seq, HY_SLAB), lambda j, b, k=k: (row0 + b, k * cps + s0 + j))
    once = lambda a: pl.BlockSpec(a.shape, lambda j, b: (0,) * a.ndim, pipeline_mode=pl.Buffered(1))
    mats = (jnp.asarray(f1, BF16), jnp.asarray(w2, BF16), jnp.asarray(v2, BF16), jnp.asarray(f3, BF16))
    seq_rows = (FFT_N1 // 2) * pitch
    return pl.pallas_call(
        functools.partial(_hyena_lat_kernel, col_mode=col_mode, seq=seq),
        grid=(nslab, lay.b),
        in_specs=[zspec(0), zspec(1), zspec(2),
                  pl.BlockSpec((3, B_SHORT, HY_SLAB), lambda j, b: (0, 0, s0 + j)),
                  pl.BlockSpec((3, 1, HY_SLAB), lambda j, b: (0, 0, s0 + j)),
                  pl.BlockSpec((2, FFT_N1, 2, n2c, HY_SLAB), lambda j, b: (0, 0, 0, 0, s0 + j),
                               pipeline_mode=pl.Buffered(1)),
                  pl.BlockSpec((2, HY_SLAB), lambda j, b: (0, s0 + j))] + [once(m) for m in mats],
        out_specs=pl.BlockSpec((seq, HY_SLAB), lambda j, b: (b, j)),
        out_shape=jax.ShapeDtypeStruct((lay.b * seq, nslab * HY_SLAB), F32),
        scratch_shapes=[pltpu.VMEM((seq_rows, HY_SLAB), F32), pltpu.VMEM((seq, HY_SLAB), F32),
                        pltpu.VMEM((FFT_N1 * pitch, HY_SLAB), F32), pltpu.VMEM((FFT_N1 * pitch, HY_SLAB), F32),
                        pltpu.VMEM((seq if col_mode else seq_rows, HY_SLAB), F32)],
        compiler_params=_cparams(("parallel", "parallel")),
        name="hyena_latent_col" if col_mode else "hyena_latent_row",
    )(z, z, z, jnp.transpose(conv_w, (1, 0, 2)), conv_b.reshape(3, 1, B_CH), spec_x, bias, *mats)


def _hyena(lay, z, conv_w, conv_b, spec_c, spec_x, bias, consts_c, consts_x):
    lay_c = _Layout(lay.b, lay.ctx, 0)
    u3 = _dwconv(lay_c, z[:lay.n_ctx], conv_w.reshape(B_SHORT, 3 * B_CH), conv_b.reshape(1, 3 * B_CH), False,
                 B_CH, split=True)
    yc = _hyena_segment(u3, lay.ctx, lay.b, consts_c, spec_c, bias)
    yl = jnp.concatenate([_hyena_latent(lay, z, conv_w, conv_b, spec_x, bias, consts_x, cm) for cm in (False, True)],
                         axis=1)
    return jnp.concatenate([yc, yl], axis=0)


def _head_norm_gate(o, gate, norm_w, heads, width):
    outs = []
    for h in range(heads):
        oh = o[:, h * width:(h + 1) * width]
        y = oh * lax.rsqrt(jnp.mean(oh * oh, axis=-1, keepdims=True) + NORM_EPS) * norm_w
        outs.append(y * _silu(gate[:, h * width:(h + 1) * width]))
    return jnp.concatenate(outs, axis=1)


def _merge_kernel(x_ref, mod_ref, nw_ref, a0_ref, a1_ref, ag_ref, yb_ref, c0_ref, c1_ref, cx_ref, cz_ref,
                  d0_ref, d1_ref, dg_ref, an_ref, cd_ref, cn_ref, dn_ref, wg_ref, wb_ref, wo_ref, o_ref):
    xv = x_ref[...]
    m = mod_ref[...]
    hb = _prenorm(xv, nw_ref[1:2, :], m[3:4, :], m[4:5, :]).astype(BF16)
    ya = _head_norm_gate(a0_ref[...] + a1_ref[...], ag_ref[...], an_ref[...], A_HEADS, A_DV)
    yd = _head_norm_gate(d0_ref[...] + d1_ref[...], dg_ref[...], dn_ref[...], D_HEADS, D_DV)
    yc = (c0_ref[...] + c1_ref[...] + cd_ref[...] * cx_ref[...]) * _silu(cz_ref[...])
    gw = C_INNER // C_GROUPS
    cn = cn_ref[...]
    yc = jnp.concatenate(
        [yc[:, g * gw:(g + 1) * gw]
         * lax.rsqrt(jnp.mean(yc[:, g * gw:(g + 1) * gw] ** 2, axis=-1, keepdims=True) + NORM_EPS)
         * cn[:, g * gw:(g + 1) * gw] for g in range(C_GROUPS)], axis=1)
    acc = jnp.zeros(xv.shape, F32)
    for k, y in enumerate((ya, yb_ref[...], yc, yd)):
        gate = jax.nn.sigmoid(jnp.dot(hb, wg_ref[:, k * D_MODEL:(k + 1) * D_MODEL], preferred_element_type=F32))
        acc = acc + gate * jnp.dot(y.astype(BF16), wb_ref[k], preferred_element_type=F32)
    o_ref[...] = xv + m[5:6, :] * jnp.dot(acc.astype(BF16), wo_ref[...], preferred_element_type=F32)


def _merge(lay, x, mod, nw, a0, a1, ag, yb, c0, c1, cx, cz, d0, d1, dg, an, cd, cn, dn, wg, wb, wo):
    n, d = x.shape
    row = lambda w: pl.BlockSpec((ROW_TILE, w), lambda i: (i, 0))
    once = lambda a: pl.BlockSpec(a.shape, lambda i: (0,) * a.ndim, pipeline_mode=pl.Buffered(1))
    return pl.pallas_call(
        _merge_kernel,
        grid=(lay.tiles,),
        in_specs=[row(d), pl.BlockSpec((None, N_MOD, d), lambda i: (lay.mod_index(i), 0, 0)), _full(nw.shape)]
                 + [row(BRANCH_W)] * 11 + [_full(an.shape), _full(cd.shape), _full(cn.shape), _full(dn.shape),
                                           once(wg), once(wb), once(wo)],
        out_specs=row(d),
        out_shape=jax.ShapeDtypeStruct((n, d), F32),
        compiler_params=_cparams(("parallel",)), name="merge",
    )(x, mod, nw, a0, a1, ag, yb, c0, c1, cx, cz, d0, d1, dg, an, cd, cn, dn, wg, wb, wo)


def _final_norm_kernel(x_ref, w_ref, o_ref):
    xv = x_ref[...]
    o_ref[...] = xv * lax.rsqrt(jnp.mean(xv * xv, axis=-1, keepdims=True) + NORM_EPS) * w_ref[...]


def _final_norm(lay, x, w):
    d = x.shape[1]
    return pl.pallas_call(
        _final_norm_kernel,
        grid=(lay.b * lay.tl,),
        in_specs=[pl.BlockSpec((ROW_TILE, d), lambda i: (lay.nct + i, 0)), _full((1, d))],
        out_specs=pl.BlockSpec((ROW_TILE, d), lambda i: (i, 0)),
        out_shape=jax.ShapeDtypeStruct((lay.b * lay.seq, d), F32),
        compiler_params=_cparams(("parallel",)), name="final_norm",
    )(x, w.reshape(1, d))


def kernel(x, c, ctx, c_ctx, w_ada, b_ada, norm_w, ffn_up, ffn_down, w_in, gdn_conv, gdn_a_log, gdn_dt_bias, gdn_norm, hy_conv_w, hy_conv_b, hy_w1, hy_b1, hy_w2, hy_b2, hy_wout, hy_freq, hy_bias, ssd_conv_w, ssd_conv_b, ssd_a_log, ssd_dt_bias, ssd_d, ssd_norm, gla_gk_w, gla_gk_b, gla_norm, w_branch, w_out, final_norm):
    b, seq, d = x.shape
    ctx_len = ctx.shape[1]
    depth = w_ada.shape[0]
    lay = _Layout(b, ctx_len, seq)
    consts_c = _dft_consts(2 * ctx_len // FFT_N1)
    consts_x = _dft_consts(2 * seq // FFT_N1)

    rp = -(-(1 + b) // 8) * 8
    cond = jnp.concatenate([c_ctx[None, :], c, jnp.zeros((rp - 1 - b, d), F32)], axis=0)
    mods = _ada(cond, w_ada, b_ada).reshape(depth, rp, N_MOD, d)

    xf = jnp.concatenate([ctx.reshape(b * ctx_len, d), x.reshape(b * seq, d)], axis=0)
    for l in range(depth):
        mod = mods[l]
        nw = norm_w[l]
        w_r, w_gate = _rearrange_w_in(w_in[l])
        xf = _ffn(lay, xf, mod, nw, ffn_up[l, 0].astype(BF16), ffn_down[l, 0].astype(BF16), 0)

        a_qkv, a_gate, zb, c_z, c_xbc, d_qkv, d_gate, small = _inproj(lay, xf, mod, nw, w_r)
        qkv = _dwconv(lay, a_qkv, gdn_conv[l], jnp.zeros((1, A_QKV), F32), True, 512)
        a0, a1 = _gdn_scan(lay, qkv, small, gdn_a_log[l], gdn_dt_bias[l])
        xbc = _dwconv(lay, c_xbc, ssd_conv_w[l], ssd_conv_b[l].reshape(1, C_XBC), True, 256)
        c0, c1 = _ssd_scan(lay, xbc, small, ssd_a_log[l], ssd_dt_bias[l])
        d0, d1 = _gla_scan(lay, d_qkv, small, gla_gk_w[l], gla_gk_b[l])
        fargs = (hy_w1[l], hy_b1[l], hy_w2[l], hy_b2[l], hy_wout[l], hy_freq[l])
        spec_c = _hyena_spectrum(ctx_len, consts_c, *fargs)
        spec_x = _hyena_spectrum(seq, consts_x, *fargs)
        yb = _hyena(lay, zb, hy_conv_w[l], hy_conv_b[l], spec_c, spec_x, hy_bias[l], consts_c, consts_x)

        xf = _merge(lay, xf, mod, nw, a0, a1, a_gate, yb, c0, c1, xbc, c_z, d0, d1, d_gate,
                    gdn_norm[l].reshape(1, A_DV), jnp.repeat(ssd_d[l], C_HEADDIM).reshape(1, C_INNER),
                    ssd_norm[l].reshape(1, C_INNER), gla_norm[l].reshape(1, D_DV),
                    w_gate, w_branch[l].astype(BF16), w_out[l].astype(BF16))
        xf = _ffn(lay, xf, mod, nw, ffn_up[l, 1].astype(BF16), ffn_down[l, 1].astype(BF16), 2)
    return _final_norm(lay, xf, final_norm).reshape(b, seq, d)
```

```python
import functools
import math

import numpy as np
import jax
import jax.numpy as jnp
from jax import lax
from jax.experimental import pallas as pl
from jax.experimental.pallas import tpu as pltpu

F32 = jnp.float32
BF16 = jnp.bfloat16
HI = lax.Precision.HIGHEST

D_MODEL = 1024
GRID_W = 64
CHUNK = 64
NORM_EPS = 1e-6
N_MOD = 9
D_FF = 2816
SHORT_CONV = 5

A_HEADS, A_DK, A_DV = 4, 128, 128
A_QKV = A_HEADS * (2 * A_DK + A_DV)
B_CH, B_SHORT, B_EMB, B_FFN, B_INNER_MLPS = 512, 3, 33, 64, 2
B_WINDOW_SHIFT, B_DECAY_SHORT_PCT, B_DECAY_LONG_PCT, B_DECAY_TARGET = 0.05, 0.3, 1.5, 1e-2
C_HEADS, C_HEADDIM, C_GROUPS, C_STATE = 8, 64, 2, 64
C_INNER = C_HEADS * C_HEADDIM
C_XBC = C_INNER + 2 * C_GROUPS * C_STATE
D_HEADS, D_DK, D_DV, D_RANK = 4, 64, 128, 16
D_GATE_NORM = 16.0
N_BRANCH, BRANCH_W = 4, 512

IN_SIZES = (A_QKV, A_HEADS * A_DV, 2 * A_HEADS, 2 * A_HEADS, 3 * B_CH, C_INNER, C_XBC, 2 * C_HEADS,
            D_HEADS * (2 * D_DK + D_DV), D_HEADS * D_DV, 2 * D_RANK, N_BRANCH * D_MODEL)
IN_OFFS = tuple(int(v) for v in np.cumsum((0,) + IN_SIZES))

ROW_TILE = 256
TOK_TILE = 512
SMALL_W = 128
VMEM_LIMIT = 56 * 1024 * 1024


def _cparams(sem):
    return pltpu.CompilerParams(dimension_semantics=sem, vmem_limit_bytes=VMEM_LIMIT)


def _mm(a, b):
    return jnp.dot(a.astype(BF16), b.astype(BF16), preferred_element_type=F32)


def _mm_nt(a, b):
    return lax.dot_general(a.astype(BF16), b.astype(BF16), (((1,), (1,)), ((), ())),
                           preferred_element_type=F32)


def _mm_hi(a, b):
    return jnp.dot(a, b, precision=HI, preferred_element_type=F32)


def _silu(v):
    return v * jax.nn.sigmoid(v)


def _softplus(v):
    return jnp.maximum(v, 0.0) + jnp.log1p(jnp.exp(-jnp.abs(v)))


def _log_sigmoid(v):
    return jnp.minimum(v, 0.0) - jnp.log1p(jnp.exp(-jnp.abs(v)))


def _prenorm(xv, gain, shift, scale):
    ms = jnp.mean(xv * xv, axis=-1, keepdims=True)
    return (xv * lax.rsqrt(ms + NORM_EPS) * gain) * (1.0 + scale) + shift


def _full(shape):
    nd = len(shape)
    return pl.BlockSpec(shape, lambda *_: (0,) * nd)


class _Layout:
    def __init__(self, batch, ctx_len, seq):
        assert ctx_len % ROW_TILE == 0 and seq % ROW_TILE == 0
        assert (batch * ctx_len) % TOK_TILE == 0 and seq % TOK_TILE == 0
        self.b, self.ctx, self.seq = batch, ctx_len, seq
        self.n_ctx = batch * ctx_len
        self.n = batch * (ctx_len + seq)
        self.tc, self.tl = ctx_len // ROW_TILE, seq // ROW_TILE
        self.nct = batch * self.tc
        self.tiles = self.n // ROW_TILE

    def mod_index(self, i, tile=TOK_TILE):
        nct = self.n_ctx // tile
        return jnp.where(i < nct, 0, 1 + (i - nct) // (self.seq // tile))

    def seg_first(self, i):
        if self.tl == 0:
            return lax.rem(i, self.tc) == 0
        return jnp.where(i < self.nct, lax.rem(i, self.tc) == 0, lax.rem(i - self.nct, self.tl) == 0)

    def seg_last(self, i):
        if self.tl == 0:
            return lax.rem(i, self.tc) == self.tc - 1
        return jnp.where(i < self.nct, lax.rem(i, self.tc) == self.tc - 1,
                         lax.rem(i - self.nct, self.tl) == self.tl - 1)

    def fwd_tile(self, b, s):
        return jnp.where(s < self.tc, b * self.tc + s, self.nct + b * self.tl + (s - self.tc))

    def rev_tile(self, b, s):
        return jnp.where(s < self.tc, b * self.tc + (self.tc - 1 - s),
                         self.nct + b * self.tl + (self.tl - 1 - (s - self.tc)))


def _ada_kernel(c_ref, w_ref, b_ref, o_ref):
    o_ref[...] = _mm_hi(_silu(c_ref[...]), w_ref[...]) + b_ref[...]


def _ada(cond, w_ada, b_ada):
    depth, d, nm = w_ada.shape
    rp = cond.shape[0]
    tn = 1152
    return pl.pallas_call(
        _ada_kernel,
        grid=(depth, nm // tn),
        in_specs=[_full((rp, d)),
                  pl.BlockSpec((None, d, tn), lambda l, j: (l, 0, j)),
                  pl.BlockSpec((None, 1, tn), lambda l, j: (l, 0, j))],
        out_specs=pl.BlockSpec((None, rp, tn), lambda l, j: (l, 0, j)),
        out_shape=jax.ShapeDtypeStruct((depth, rp, nm), F32),
        compiler_params=_cparams(("parallel", "parallel")), name="ada",
    )(cond, w_ada, b_ada.reshape(depth, 1, nm))


FF_CHUNK = 1408


def _ffn_kernel(x_ref, mod_ref, nw_ref, wup_ref, wdn_ref, o_ref, *, sub):
    xv = x_ref[...]
    m = mod_ref[...]
    h = _prenorm(xv, nw_ref[sub:sub + 1, :], m[3 * sub:3 * sub + 1, :], m[3 * sub + 1:3 * sub + 2, :])
    hb = h.astype(BF16)
    acc = jnp.zeros(xv.shape, F32)
    for c in range(D_FF // FF_CHUNK):
        lo = c * FF_CHUNK
        a = jnp.dot(hb, wup_ref[:, lo:lo + FF_CHUNK], preferred_element_type=F32)
        g = jnp.dot(hb, wup_ref[:, D_FF + lo:D_FF + lo + FF_CHUNK], preferred_element_type=F32)
        acc = acc + jnp.dot((_silu(a) * g).astype(BF16), wdn_ref[lo:lo + FF_CHUNK, :],
                            preferred_element_type=F32)
    o_ref[...] = xv + 0.5 * m[3 * sub + 2:3 * sub + 3, :] * acc


def _ffn(lay, x, mod, nw, wup, wdn, sub):
    n, d = x.shape
    return pl.pallas_call(
        functools.partial(_ffn_kernel, sub=sub),
        grid=(n // TOK_TILE,),
        in_specs=[pl.BlockSpec((TOK_TILE, d), lambda i: (i, 0)),
                  pl.BlockSpec((None, N_MOD, d), lambda i: (lay.mod_index(i), 0, 0)),
                  _full(nw.shape),
                  pl.BlockSpec(wup.shape, lambda i: (0, 0), pipeline_mode=pl.Buffered(1)),
                  pl.BlockSpec(wdn.shape, lambda i: (0, 0), pipeline_mode=pl.Buffered(1))],
        out_specs=pl.BlockSpec((TOK_TILE, d), lambda i: (i, 0)),
        out_shape=jax.ShapeDtypeStruct((n, d), F32),
        compiler_params=_cparams(("parallel",)), name="ffn",
    )(x, mod, nw, wup, wdn)


PROJ_W = (A_QKV, A_HEADS * A_DV, 3 * B_CH, C_INNER, C_XBC, D_HEADS * (2 * D_DK + D_DV), D_HEADS * D_DV, SMALL_W)
PROJ_O = tuple(int(v) for v in np.cumsum((0,) + PROJ_W))


def _rearrange_w_in(w_in):
    o = IN_OFFS
    small = jnp.concatenate([w_in[:, o[2]:o[4]], w_in[:, o[7]:o[8]], w_in[:, o[10]:o[11]],
                             jnp.zeros((w_in.shape[0], SMALL_W - 64), w_in.dtype)], axis=1)
    w_r = jnp.concatenate([w_in[:, o[0]:o[1]], w_in[:, o[1]:o[2]], w_in[:, o[4]:o[5]], w_in[:, o[5]:o[6]],
                           w_in[:, o[6]:o[7]], w_in[:, o[8]:o[9]], w_in[:, o[9]:o[10]], small], axis=1)
    return w_r.astype(BF16), w_in[:, o[11]:o[12]].astype(BF16)


PROJ_HALO = 16
PROJ_CONV = (0, 4)


def _inproj_kernel(xp_ref, x_ref, xn_ref, mod_ref, nw_ref, w_ref, cwa_ref, cwc_ref, cbc_ref, *rest, lay):
    o_refs, zbuf = rest[:-1], rest[-1]
    i = pl.program_id(0)
    m = mod_ref[...]
    norm = lambda r: _prenorm(r[...], nw_ref[1:2, :], m[3:4, :], m[4:5, :]).astype(BF16)
    hb = norm(x_ref)
    hb_all = jnp.concatenate([norm(xp_ref), hb, norm(xn_ref)], axis=0)
    keep_p = jnp.where(lay.seg_first(i), 0.0, 1.0)
    keep_n = jnp.where(lay.seg_last(i), 0.0, 1.0)
    convs = {PROJ_CONV[0]: (cwa_ref, None), PROJ_CONV[1]: (cwc_ref, cbc_ref)}
    for k, o_ref in enumerate(o_refs):
        wk = w_ref[:, PROJ_O[k]:PROJ_O[k + 1]]
        if k not in convs:
            o_ref[...] = jnp.dot(hb, wk, preferred_element_type=F32)
            continue
        cw_ref, cb_ref = convs[k]
        width = PROJ_W[k]
        z = jnp.dot(hb_all, wk, preferred_element_type=F32)
        zbuf[0:PROJ_HALO, 0:width] = z[:PROJ_HALO] * keep_p
        zbuf[PROJ_HALO:PROJ_HALO + ROW_TILE, 0:width] = z[PROJ_HALO:PROJ_HALO + ROW_TILE]
        zbuf[PROJ_HALO + ROW_TILE:, 0:width] = z[PROJ_HALO + ROW_TILE:] * keep_n
        taps = cw_ref.shape[0]
        left = (taps - 1) // 2
        acc = None
        for t in range(taps):
            off = t - left
            term = zbuf[PROJ_HALO + off:PROJ_HALO + off + ROW_TILE, 0:width] * cw_ref[t:t + 1, :]
            acc = term if acc is None else acc + term
        if cb_ref is not None:
            acc = acc + cb_ref[...]
        o_ref[...] = _silu(acc)


def _inproj(lay, x, mod, nw, w_r, conv_a, conv_c, conv_c_bias):
    n, d = x.shape
    rh = ROW_TILE // PROJ_HALO
    lasth = n // PROJ_HALO - 1
    return pl.pallas_call(
        functools.partial(_inproj_kernel, lay=lay),
        grid=(n // ROW_TILE,),
        in_specs=[pl.BlockSpec((PROJ_HALO, d), lambda i: (jnp.maximum(i * rh - 1, 0), 0)),
                  pl.BlockSpec((ROW_TILE, d), lambda i: (i, 0)),
                  pl.BlockSpec((PROJ_HALO, d), lambda i: (jnp.minimum((i + 1) * rh, lasth), 0)),
                  pl.BlockSpec((None, N_MOD, d), lambda i: (lay.mod_index(i, ROW_TILE), 0, 0)),
                  _full(nw.shape),
                  pl.BlockSpec(w_r.shape, lambda i: (0, 0), pipeline_mode=pl.Buffered(1)),
                  _full(conv_a.shape), _full(conv_c.shape), _full(conv_c_bias.shape)],
        out_specs=[pl.BlockSpec((ROW_TILE, w), lambda i: (i, 0)) for w in PROJ_W],
        out_shape=[jax.ShapeDtypeStruct((n, w), F32) for w in PROJ_W],
        scratch_shapes=[pltpu.VMEM((ROW_TILE + 2 * PROJ_HALO, max(PROJ_W[k] for k in PROJ_CONV)), F32)],
        compiler_params=_cparams(("parallel",)), name="inproj",
    )(x, x, x, mod, nw, w_r, conv_a, conv_c, conv_c_bias)


HALO = 8


def _conv_tile(lay, tile, xp_ref, x_ref, xn_ref, w_ref, b_ref, buf, act):
    taps = w_ref.shape[0]
    keep_p = jnp.where(lay.seg_first(tile), 0.0, 1.0)
    keep_n = jnp.where(lay.seg_last(tile), 0.0, 1.0)
    buf[0:HALO, :] = xp_ref[...] * keep_p
    buf[HALO:HALO + ROW_TILE, :] = x_ref[...]
    buf[HALO + ROW_TILE:2 * HALO + ROW_TILE, :] = xn_ref[...] * keep_n
    left = (taps - 1) // 2
    acc = b_ref[...] + buf[HALO - left:HALO - left + ROW_TILE, :] * w_ref[0:1, :]
    for k in range(1, taps):
        acc = acc + buf[HALO - left + k:HALO - left + k + ROW_TILE, :] * w_ref[k:k + 1, :]
    return _silu(acc) if act else acc


def _dwconv_kernel(xp_ref, x_ref, xn_ref, w_ref, b_ref, o_ref, buf, *, lay, act):
    o_ref[...] = _conv_tile(lay, pl.program_id(0), xp_ref, x_ref, xn_ref, w_ref, b_ref, buf, act)


def _dwconv(lay, x, w, bias, act, wb, split=False):
    n, width = x.shape
    taps = w.shape[0]
    nwb = width // wb
    r8 = ROW_TILE // HALO
    last8 = n // HALO - 1
    if split:
        out_spec = pl.BlockSpec((None, ROW_TILE, wb), lambda i, j: (j, i, 0))
        out_shape = jax.ShapeDtypeStruct((nwb, n, wb), F32)
    else:
        out_spec = pl.BlockSpec((ROW_TILE, wb), lambda i, j: (i, j))
        out_shape = jax.ShapeDtypeStruct((n, width), F32)
    return pl.pallas_call(
        functools.partial(_dwconv_kernel, lay=lay, act=act),
        grid=(lay.tiles, nwb),
        in_specs=[pl.BlockSpec((HALO, wb), lambda i, j: (jnp.maximum(i * r8 - 1, 0), j)),
                  pl.BlockSpec((ROW_TILE, wb), lambda i, j: (i, j)),
                  pl.BlockSpec((HALO, wb), lambda i, j: (jnp.minimum((i + 1) * r8, last8), j)),
                  pl.BlockSpec((taps, wb), lambda i, j: (0, j)),
                  pl.BlockSpec((1, wb), lambda i, j: (0, j))],
        out_specs=out_spec,
        out_shape=out_shape,
        scratch_shapes=[pltpu.VMEM((ROW_TILE + 2 * HALO, wb), F32)],
        compiler_params=_cparams(("parallel", "parallel")), name="dwconv",
    )(x, x, x, w, bias)


def _tri_masks(rev):
    ii = lax.broadcasted_iota(jnp.int32, (CHUNK, CHUNK), 0)
    jj = lax.broadcasted_iota(jnp.int32, (CHUNK, CHUNK), 1)
    return (ii <= jj, ii < jj) if rev else (ii >= jj, ii > jj)


def _masked_decay(col, row, incl):
    return jnp.where(incl, jnp.exp(jnp.where(incl, col - row, 0.0)), 0.0)


def _scan_call(kernel, lay, ins_tiled, ins_full, out_w, scratch):
    steps = lay.tc + lay.tl
    specs = []
    args = []
    for walk in (lay.fwd_tile, lay.rev_tile):
        for a in ins_tiled:
            specs.append(pl.BlockSpec((ROW_TILE, a.shape[1]), lambda b, s, walk=walk: (walk(b, s), 0)))
            args.append(a)
    for a in ins_full:
        specs.append(_full(a.shape))
        args.append(a)
    return pl.pallas_call(
        kernel,
        grid=(lay.b, steps),
        in_specs=specs,
        out_specs=[pl.BlockSpec((ROW_TILE, out_w), lambda b, s: (lay.fwd_tile(b, s), 0)),
                   pl.BlockSpec((ROW_TILE, out_w), lambda b, s: (lay.rev_tile(b, s), 0))],
        out_shape=[jax.ShapeDtypeStruct((lay.n, out_w), F32)] * 2,
        scratch_shapes=scratch,
        compiler_params=_cparams(("arbitrary", "arbitrary")), name=kernel.__name__.strip("_"),
    )(*args)


def _ssd_kernel(xf_ref, sf_ref, xr_ref, sr_ref, alog_ref, dtb_ref, alog_t_ref, dtb_t_ref, exp_ref,
                of_ref, or_ref, st_ref):
    @pl.when(pl.program_id(1) == 0)
    def _():
        st_ref[...] = jnp.zeros(st_ref.shape, F32)

    hpg = C_HEADS // C_GROUPS
    gw = hpg * C_HEADDIM
    nch = ROW_TILE // CHUNK
    refs = ((xf_ref, sf_ref, of_ref), (xr_ref, sr_ref, or_ref))
    units, pre = [], {}
    for d, (x_ref, s_ref, _) in enumerate(refs):
        incl, _ = _tri_masks(d == 1)
        m_col = incl.astype(F32)
        m_row = m_col.T
        sm = s_ref[...]
        sm_t = sm.T
        dt_all = _softplus(sm[:, 16:32] + dtb_ref[...])
        a_all = -jnp.exp(alog_ref[...]) * dt_all
        a_t_all = -jnp.exp(alog_t_ref[...]) * _softplus(sm_t[16:32, :] + dtb_t_ref[...])
        bm_t = x_ref[:, C_INNER:C_INNER + C_GROUPS * C_STATE].T
        for c in range(nch):
            rows = slice(c * CHUNK, (c + 1) * CHUNK)
            units.append((d, c))
            pre[d, c] = dict(incl=incl, m_col=m_col, m_row=m_row, a_ch=a_all[rows], a_t=a_t_all[:, rows],
                             dt=dt_all[rows], bm_t=bm_t[:, rows], e=exp_ref[d], xs=x_ref[rows, 0:C_INNER],
                             bm=x_ref[rows, C_INNER:C_INNER + C_GROUPS * C_STATE],
                             cm=x_ref[rows, C_INNER + C_GROUPS * C_STATE:C_XBC])
    for u in units:
        w = pre[u]
        w["ac"] = _mm_hi(w["m_col"], w["a_ch"])
        w["ac_t"] = _mm_hi(w["a_t"], w["m_row"])
        w["alast"] = jnp.sum(w["a_ch"], axis=0, keepdims=True)
    for u in units:
        w = pre[u]
        w["xdt"] = w["xs"] * _mm_hi(w["dt"], w["e"])
        w["eac_e"] = _mm_hi(jnp.exp(w["ac"]), w["e"])
        w["xd"] = w["xdt"] * _mm_hi(jnp.exp(w["alast"] - w["ac"]), w["e"])
        w["dle"] = _mm_hi(jnp.broadcast_to(jnp.exp(w["alast"]), (8, 2 * C_HEADS)), w["e"])[0:1]
    grp = lambda a, g: a[:, g * C_STATE:(g + 1) * C_STATE]
    cb = {(u, g): _mm_nt(grp(pre[u]["cm"], g), grp(pre[u]["bm"], g)) for u in units for g in range(C_GROUPS)}
    upd = {(u, g): _mm(pre[u]["bm_t"][g * C_STATE:(g + 1) * C_STATE], pre[u]["xd"][:, g * gw:(g + 1) * gw])
           for u in units for g in range(C_GROUPS)}
    y_diag = {}
    for u in units:
        w = pre[u]
        for h in range(C_HEADS):
            col = u[0] * C_HEADS + h
            seg = _masked_decay(w["ac"][:, col:col + 1], w["ac_t"][col:col + 1, :], w["incl"])
            y_diag[u, h] = _mm(cb[u, h // hpg] * seg, w["xdt"][:, h * C_HEADDIM:(h + 1) * C_HEADDIM])
    st = {(d, g): st_ref[d, g] for d in range(2) for g in range(C_GROUPS)}
    entry = {}
    for p in range(nch):
        for d in range(2):
            u = (d, (nch - 1 - p) if d == 1 else p)
            for g in range(C_GROUPS):
                entry[u, g] = st[d, g]
                st[d, g] = pre[u]["dle"][:, g * gw:(g + 1) * gw] * st[d, g] + upd[u, g]
    for d in range(2):
        for g in range(C_GROUPS):
            st_ref[d, g] = st[d, g]
    for u in units:
        y_off = [_mm(grp(pre[u]["cm"], g), entry[u, g]) for g in range(C_GROUPS)]
        refs[u[0]][2][u[1] * CHUNK:(u[1] + 1) * CHUNK, :] = (
            jnp.concatenate([y_diag[u, h] for h in range(C_HEADS)], axis=1)
            + pre[u]["eac_e"] * jnp.concatenate(y_off, axis=1))


def _ssd_scan(lay, xbc, small, a_log, dt_bias):
    expand = np.zeros((2, 2 * C_HEADS, C_INNER), np.float32)
    for d in range(2):
        for h in range(C_HEADS):
            expand[d, d * C_HEADS + h, h * C_HEADDIM:(h + 1) * C_HEADDIM] = 1.0
    al = a_log.reshape(1, 2 * C_HEADS)
    db = dt_bias.reshape(1, 2 * C_HEADS)
    return _scan_call(_ssd_kernel, lay, (xbc, small), (al, db, al.T, db.T, jnp.asarray(expand)), C_INNER,
                      [pltpu.VMEM((2, C_GROUPS, C_STATE, C_INNER // C_GROUPS), F32)])


def _gla_kernel(xf_ref, sf_ref, xr_ref, sr_ref, gkw_ref, gkb_ref, of_ref, or_ref, st_ref):
    @pl.when(pl.program_id(1) == 0)
    def _():
        st_ref[...] = jnp.zeros(st_ref.shape, F32)

    nk = D_HEADS * D_DK
    nch = ROW_TILE // CHUNK
    refs = ((xf_ref, sf_ref, of_ref), (xr_ref, sr_ref, or_ref))
    units, pre = [], {}
    for d, (x_ref, s_ref, _) in enumerate(refs):
        incl, _ = _tri_masks(d == 1)
        m_col = incl.astype(F32)
        lr = s_ref[:, 32 + D_RANK * d:32 + D_RANK * (d + 1)]
        g_all = _log_sigmoid(_mm_hi(lr, gkw_ref[d]) + gkb_ref[d:d + 1, :]) / D_GATE_NORM
        v_t = x_ref[:, 2 * nk:].T
        for c in range(nch):
            rows = slice(c * CHUNK, (c + 1) * CHUNK)
            g_ch = g_all[rows]
            gc = _mm_hi(m_col, g_ch)
            glast = jnp.sum(g_ch, axis=0, keepdims=True)
            q = x_ref[rows, 0:nk] * (D_DK ** -0.5)
            k = x_ref[rows, nk:2 * nk]
            v = x_ref[rows, 2 * nk:]
            kd = k * jnp.exp(glast - gc)
            qd = q * jnp.exp(gc)
            qr = q * jnp.exp(gc - glast)
            dlast = jnp.exp(glast)
            for h in range(D_HEADS):
                ks = slice(h * D_DK, (h + 1) * D_DK)
                vs = slice(h * D_DV, (h + 1) * D_DV)
                units.append((d, c, h))
                pre[d, c, h] = dict(incl=incl, qr=qr[:, ks], kd=kd[:, ks], qd=qd[:, ks], v=v[:, vs],
                                    v_t=v_t[vs, rows], dl=dlast[:, ks])
    aqk = [jnp.where(pre[u]["incl"], _mm_nt(pre[u]["qr"], pre[u]["kd"]), 0.0) for u in units]
    upd = {u: _mm(pre[u]["v_t"], pre[u]["kd"]) for u in units}
    intra = {u: _mm(a, pre[u]["v"]) for u, a in zip(units, aqk)}
    heads = [(d, h) for d in range(2) for h in range(D_HEADS)]
    st = {dh: st_ref[dh[0], dh[1]] for dh in heads}
    entry = {}
    for p in range(nch):
        for d, h in heads:
            u = (d, (nch - 1 - p) if d == 1 else p, h)
            entry[u] = st[d, h]
            st[d, h] = st[d, h] * pre[u]["dl"] + upd[u]
    for dh in heads:
        st_ref[dh[0], dh[1]] = st[dh]
    inter = {u: _mm_nt(pre[u]["qd"], entry[u]) for u in units}
    for d, (_, _, o_ref) in enumerate(refs):
        for c in range(nch):
            o_ref[c * CHUNK:(c + 1) * CHUNK, :] = jnp.concatenate(
                [intra[d, c, h] + inter[d, c, h] for h in range(D_HEADS)], axis=1)


def _gla_scan(lay, qkv, small, gk_w, gk_b):
    return _scan_call(_gla_kernel, lay, (qkv, small), (gk_w, gk_b), D_HEADS * D_DV,
                      [pltpu.VMEM((2, D_HEADS, D_DV, D_DK), F32)])


TRI_BLOCK = 16
GDN_GROUP = 32


def _unit_lower_inverse(nms, eye, blk):
    dg = [jnp.where(blk, nm, 0.0) for nm in nms]
    off = [nm - d for nm, d in zip(nms, dg)]
    t0 = [eye - d for d in dg]
    p = [_mm(d, d) for d in dg]
    for it in range(3):
        t0 = [t + _mm(t, q) for t, q in zip(t0, p)]
        if it < 2:
            p = [_mm(q, q) for q in p]
    m = [_mm(t, o) for t, o in zip(t0, off)]
    m2 = [_mm(a, a) for a in m]
    r = [eye - a for a in m]
    r = [a + _mm(a, b) for a, b in zip(r, m2)]
    return [_mm(a, t) for a, t in zip(r, t0)]


def _gdn_kernel(xf_ref, sf_ref, xr_ref, sr_ref, alog_ref, dtb_ref, alog_t_ref, dtb_t_ref,
                of_ref, or_ref, st_ref):
    @pl.when(pl.program_id(1) == 0)
    def _():
        st_ref[...] = jnp.zeros(st_ref.shape, F32)

    nk = A_HEADS * A_DK
    nch = ROW_TILE // CHUNK
    ii = lax.broadcasted_iota(jnp.int32, (CHUNK, CHUNK), 0)
    jj = lax.broadcasted_iota(jnp.int32, (CHUNK, CHUNK), 1)
    eye = (ii == jj).astype(F32)
    blk = (ii // TRI_BLOCK) == (jj // TRI_BLOCK)
    refs = ((xf_ref, sf_ref, of_ref), (xr_ref, sr_ref, or_ref))

    units, pre = [], {}
    for d, (x_ref, s_ref, _) in enumerate(refs):
        incl, strict = _tri_masks(d == 1)
        m_col = incl.astype(F32)
        m_row = m_col.T
        sm = s_ref[...]
        sm_t = sm.T
        beta_all = jax.nn.sigmoid(sm[:, 0:8])
        g_all = -jnp.exp(alog_ref[...]) * _softplus(sm[:, 8:16] + dtb_ref[...])
        g_t_all = -jnp.exp(alog_t_ref[...]) * _softplus(sm_t[8:16, :] + dtb_t_ref[...])
        qn, kn = [], []
        for h in range(A_HEADS):
            qh = x_ref[:, h * A_DK:(h + 1) * A_DK]
            kh = x_ref[:, nk + h * A_DK:nk + (h + 1) * A_DK]
            qn.append(qh * lax.rsqrt(jnp.sum(qh * qh, axis=-1, keepdims=True) + NORM_EPS) * (A_DK ** -0.5))
            kn.append(kh * lax.rsqrt(jnp.sum(kh * kh, axis=-1, keepdims=True) + NORM_EPS))
        kn_t = [kh.T for kh in kn]
        for c in range(nch):
            rows = slice(c * CHUNK, (c + 1) * CHUNK)
            g_ch = g_all[rows]
            gc = _mm_hi(m_col, g_ch)
            gc_t = _mm_hi(g_t_all[:, rows], m_row)
            glast = jnp.sum(g_ch, axis=0, keepdims=True)
            for h in range(A_HEADS):
                col = d * A_HEADS + h
                gcc, gcr, gl = gc[:, col:col + 1], gc_t[col:col + 1, :], glast[:, col:col + 1]
                bh = beta_all[rows, col:col + 1]
                kh = kn[h][rows]
                units.append((d, c, h))
                pre[d, c, h] = dict(
                    dmask=_masked_decay(gcc, gcr, incl), strict=strict, egc=jnp.exp(gcc), kh=kh, kb=kh * bh,
                    qh=qn[h][rows], vb=x_ref[rows, 2 * nk + h * A_DV:2 * nk + (h + 1) * A_DV] * bh,
                    kg_t=kn_t[h][:, rows] * jnp.exp(gl - gcr), dl=jnp.exp(gl))
    for g0 in range(0, len(units), GDN_GROUP):
        us = units[g0:g0 + GDN_GROUP]
        kk = [_mm_nt(pre[u]["kb"], pre[u]["kh"]) for u in us]
        qk = [_mm_nt(pre[u]["qh"], pre[u]["kh"]) for u in us]
        tinv = _unit_lower_inverse(
            [jnp.where(pre[u]["strict"], a * pre[u]["dmask"], 0.0) for u, a in zip(us, kk)], eye, blk)
        uw = [_mm(t, jnp.concatenate([pre[u]["vb"], pre[u]["kb"] * pre[u]["egc"]], axis=1))
              for u, t in zip(us, tinv)]
        for u, a, b in zip(us, uw, qk):
            pre[u]["u"] = a[:, :A_DV]
            pre[u]["wq"] = jnp.concatenate([a[:, A_DV:], pre[u]["qh"] * pre[u]["egc"]], axis=0)
            pre[u]["aqk"] = b * pre[u]["dmask"]

    heads = [(d, h) for d in range(2) for h in range(A_HEADS)]
    st = {dh: st_ref[dh[0], dh[1]] for dh in heads}
    for p in range(nch):
        us = [(d, (nch - 1 - p) if d == 1 else p, h) for d, h in heads]
        ws = [_mm(pre[u]["wq"], st[u[0], u[2]]) for u in us]
        v_new = [pre[u]["u"] - w[:CHUNK] for u, w in zip(us, ws)]
        outs = [w[CHUNK:] + _mm(pre[u]["aqk"], v) for u, w, v in zip(us, ws, v_new)]
        for u, v in zip(us, v_new):
            st[u[0], u[2]] = pre[u]["dl"] * st[u[0], u[2]] + _mm(pre[u]["kg_t"], v)
        for d, (_, _, o_ref) in enumerate(refs):
            c = us[d * A_HEADS][1]
            o_ref[c * CHUNK:(c + 1) * CHUNK, :] = jnp.concatenate(outs[d * A_HEADS:(d + 1) * A_HEADS], axis=1)
    for dh in heads:
        st_ref[dh[0], dh[1]] = st[dh]


def _gdn_scan(lay, qkv, small, a_log, dt_bias):
    al = a_log.reshape(1, 2 * A_HEADS)
    db = dt_bias.reshape(1, 2 * A_HEADS)
    return _scan_call(_gdn_kernel, lay, (qkv, small), (al, db, al.T, db.T), A_HEADS * A_DV,
                      [pltpu.VMEM((2, A_HEADS, A_DK, A_DV), F32)])


FFT_N1 = 64
FFT_KB = 8


def _dft_consts(n2):
    n1 = FFT_N1
    m = n1 * n2
    k1 = np.arange(n1, dtype=np.float64)
    ang1 = 2.0 * np.pi * np.outer(k1, k1) / n1
    f1_full = np.concatenate([np.cos(ang1), -np.sin(ang1)], axis=0)
    f1 = f1_full[:, :n1 // 2]
    f3 = np.concatenate([np.cos(ang1[:, :n1 // 2]).T, -np.sin(ang1[:, :n1 // 2]).T], axis=1) / m
    j = np.arange(n2, dtype=np.float64)
    theta = 2.0 * np.pi * (np.outer(j, j)[None] / n2 + (k1[:, None, None] * j[None, None, :]) / m)
    wr, wi = np.cos(theta), -np.sin(theta)
    w2 = np.concatenate([np.concatenate([wr, -wi], axis=2), np.concatenate([wi, wr], axis=2)], axis=1)
    phi = 2.0 * np.pi * (np.outer(j, j)[None] / n2 + (k1[:, None, None] * j[None, :, None]) / m)
    vr, vi = np.cos(phi), np.sin(phi)
    v2 = np.concatenate([np.concatenate([vr, -vi], axis=2), np.concatenate([vi, vr], axis=2)], axis=1)
    return f1_full, f1, w2, v2, f3


def _left_mm_kernel(w_ref, x_ref, o_ref, *, hi):
    if hi:
        o_ref[...] = _mm_hi(w_ref[...], x_ref[...]).astype(o_ref.dtype)
    else:
        o_ref[...] = _mm(w_ref[...], x_ref[...]).astype(o_ref.dtype)


def _left_mm_epi_kernel(w_ref, x_ref, xn_ref, yp_ref, bias_ref, o_ref):
    conv = _mm(w_ref[...], x_ref[...])
    yp = yp_ref[...]
    o_ref[...] = xn_ref[...] * (conv + yp * bias_ref[...])


def _left_mm(wm, x, out_dtype, hi=False, epi=None):
    bs, kdim, ncols = x.shape
    mdim = wm.shape[0]
    tn = min(ncols, 4096)
    xspec = lambda rows: pl.BlockSpec((None, rows, tn), lambda b, j: (b, 0, j))
    if epi is None:
        kern, ins, specs = functools.partial(_left_mm_kernel, hi=hi), (wm, x), [_full(wm.shape), xspec(kdim)]
    else:
        kern, ins = _left_mm_epi_kernel, (wm, x) + tuple(epi)
        specs = [_full(wm.shape), xspec(kdim), xspec(mdim), xspec(mdim), pl.BlockSpec((1, tn), lambda b, j: (0, j))]
    return pl.pallas_call(
        kern, grid=(bs, ncols // tn), in_specs=specs, out_specs=xspec(mdim),
        out_shape=jax.ShapeDtypeStruct((bs, mdim, ncols), out_dtype),
        compiler_params=_cparams(("parallel", "parallel")),
        name="dft_outer" + ("_gate" if epi is not None else "_hi" if hi else ""),
    )(*ins)


def _mid_kernel(a_ref, g_ref, w_ref, v_ref, o_ref, *, n2):
    for kk in range(FFT_KB):
        a = jnp.concatenate([a_ref[0, kk], a_ref[1, kk]], axis=0)
        xf = _mm(w_ref[kk], a)
        xr, xi = xf[:n2], xf[n2:]
        gr, gi = g_ref[kk, 0], g_ref[kk, 1]
        y = jnp.concatenate([xr * gr - xi * gi, xr * gi + xi * gr], axis=0)
        bm = _mm(v_ref[kk], y)
        o_ref[0, kk] = bm[:n2].astype(o_ref.dtype)
        o_ref[1, kk] = bm[n2:].astype(o_ref.dtype)


def _mid_stage(a, g, w2, v2):
    bs, _, n1, n2, ch = a.shape
    return pl.pallas_call(
        functools.partial(_mid_kernel, n2=n2),
        grid=(bs, n1 // FFT_KB),
        in_specs=[pl.BlockSpec((None, 2, FFT_KB, n2, ch), lambda b, j: (b, 0, j, 0, 0)),
                  pl.BlockSpec((FFT_KB, 2, n2, ch), lambda b, j: (j, 0, 0, 0)),
                  pl.BlockSpec((FFT_KB, 2 * n2, 2 * n2), lambda b, j: (j, 0, 0)),
                  pl.BlockSpec((FFT_KB, 2 * n2, 2 * n2), lambda b, j: (j, 0, 0))],
        out_specs=pl.BlockSpec((None, 2, FFT_KB, n2, ch), lambda b, j: (b, 0, j, 0, 0)),
        out_shape=jax.ShapeDtypeStruct(a.shape, a.dtype),
        compiler_params=_cparams(("parallel", "parallel")), name="dft_mid",
    )(a, g, w2, v2)


def _filt_mid_kernel(a_ref, w_ref, o_ref, *, n2):
    for kk in range(FFT_KB):
        a = jnp.concatenate([a_ref[0, kk], a_ref[1, kk]], axis=0)
        xf = _mm_hi(w_ref[kk], a)
        o_ref[kk, 0] = xf[:n2]
        o_ref[kk, 1] = xf[n2:]


def _filt_mid_stage(a, w2):
    bs, _, n1, n2, ch = a.shape
    return pl.pallas_call(
        functools.partial(_filt_mid_kernel, n2=n2),
        grid=(bs, n1 // FFT_KB),
        in_specs=[pl.BlockSpec((None, 2, FFT_KB, n2, ch), lambda b, j: (b, 0, j, 0, 0)),
                  pl.BlockSpec((FFT_KB, 2 * n2, 2 * n2), lambda b, j: (j, 0, 0))],
        out_specs=pl.BlockSpec((None, FFT_KB, 2, n2, ch), lambda b, j: (b, j, 0, 0, 0)),
        out_shape=jax.ShapeDtypeStruct((bs, n1, 2, n2, ch), F32),
        compiler_params=_cparams(("parallel", "parallel")), name="filter_dft_mid",
    )(a, w2)


FILT_ROWS = 512


def _filter_kernel(z_ref, win_ref, sel_ref, w1_ref, b1_ref, w2_ref, b2_ref, wo_ref, fr_ref, o_ref):
    fr = fr_ref[...]
    h = jnp.sin(fr * (_mm_hi(z_ref[...], w1_ref[...]) + b1_ref[...]))
    for i in range(B_INNER_MLPS):
        h = jnp.sin(fr * (_mm_hi(h, w2_ref[i]) + b2_ref[i]))
    ho = _mm_hi(h, wo_ref[...])
    fwd = sel_ref[...] > 0.5
    win = win_ref[...]
    for o in range(2):
        base = o * 2 * B_CH
        o_ref[o] = jnp.where(fwd, ho[:, base:base + B_CH], ho[:, base + B_CH:base + 2 * B_CH]) * win


def _hyena_filter_time(l, w1, b1, w2, b2, w_out, freq):
    bands = (B_EMB - 1) // 2
    circ = jnp.arange(2 * l)
    pos = jnp.where(circ < l, circ, 2 * l - circ)
    pos = jnp.where(circ == l, 0, pos)
    t = (jnp.linspace(0.0, 1.0, l, dtype=F32)[pos])[:, None]
    ang = 2.0 * math.pi * pos.astype(F32)[:, None] / l
    fr = jnp.linspace(1e-4, bands - 1, bands, dtype=F32)[None, :]
    z = jnp.concatenate([t, jnp.cos(fr * ang), -jnp.sin(fr * ang)], axis=-1)
    z = jnp.pad(z, ((0, 0), (0, 128 - B_EMB)))
    max_decay = math.log(B_DECAY_TARGET) / B_DECAY_SHORT_PCT
    min_decay = math.log(B_DECAY_TARGET) / B_DECAY_LONG_PCT
    deltas = jnp.abs(jnp.linspace(min_decay, max_decay, B_CH, dtype=F32))
    win = (jnp.exp(-t * deltas) + B_WINDOW_SHIFT) * (circ != l).astype(F32)[:, None]
    sel = (circ < l).astype(F32)[:, None]
    w1p = jnp.pad(w1, ((0, 128 - B_EMB), (0, 0)))
    rows = min(FILT_ROWS, 2 * l)
    return pl.pallas_call(
        _filter_kernel,
        grid=(2 * l // rows,),
        in_specs=[pl.BlockSpec((rows, 128), lambda i: (i, 0)),
                  pl.BlockSpec((rows, B_CH), lambda i: (i, 0)),
                  pl.BlockSpec((rows, 1), lambda i: (i, 0)),
                  _full(w1p.shape), _full((1, B_FFN)), _full(w2.shape), _full((B_INNER_MLPS, 1, B_FFN)),
                  _full(w_out.shape), _full((1, B_FFN))],
        out_specs=pl.BlockSpec((2, rows, B_CH), lambda i: (0, i, 0)),
        out_shape=jax.ShapeDtypeStruct((2, 2 * l, B_CH), F32),
        compiler_params=_cparams(("parallel",)), name="hyena_filter",
    )(z, win, sel, w1p, b1.reshape(1, B_FFN), w2, b2.reshape(B_INNER_MLPS, 1, B_FFN), w_out,
      freq.reshape(1, B_FFN))


def _hyena_segment(u3, l, bs, consts, g_spec, bias):
    f1_full, f1, w2, v2, f3 = consts
    n2 = 2 * l // FFT_N1
    ch = B_CH
    mid_dtype = BF16 if n2 % 16 == 0 else F32
    view = lambda a: a.reshape(bs, FFT_N1 // 2, n2 * ch)
    y = view(u3[0])
    for order in range(2):
        a = _left_mm(jnp.asarray(f1, BF16), y, mid_dtype)
        bm = _mid_stage(a.reshape(bs, 2, FFT_N1, n2, ch), g_spec[order],
                        jnp.asarray(w2, BF16), jnp.asarray(v2, BF16))
        bias_row = jnp.tile(bias[order], n2).reshape(1, n2 * ch)
        y = _left_mm(jnp.asarray(f3, BF16), bm.reshape(bs, 2 * FFT_N1, n2 * ch), F32,
                     epi=(view(u3[1 + order]), y, bias_row))
    return y.reshape(bs * l, ch)


def _hyena_spectrum(l, consts, w1, b1, w2m, b2, w_out, freq):
    f1_full, _, w2, _, _ = consts
    n2 = 2 * l // FFT_N1
    g_time = _hyena_filter_time(l, w1, b1, w2m, b2, w_out, freq)
    a = _left_mm(jnp.asarray(f1_full, F32), g_time.reshape(2, FFT_N1, n2 * B_CH), F32, hi=True)
    return _filt_mid_stage(a.reshape(2, 2, FFT_N1, n2, B_CH), jnp.asarray(w2, F32))


HY_SLAB = 128
HY_PAD = 8
HY_KG = 8


def _hyena_lat_kernel(zv_ref, z1_ref, z2_ref, cw_ref, cb_ref, g_ref, bias_ref, f1_ref, w2_ref, v2_ref, f3_ref,
                      o_ref, useq, ur, a_re, a_im, cbuf, *, col_mode, seq):
    n1h = FFT_N1 // 2
    n2c = seq // n1h
    pitch = n2c + HY_PAD
    grows = seq // GRID_W
    row_id = lax.broadcasted_iota(jnp.int32, (seq, HY_SLAB), 0)

    def short_conv(z_ref, k):
        z = z_ref[...]
        w = cw_ref[k]
        if col_mode:
            g0, gl = z[:GRID_W], z[seq - GRID_W:]
            cc = row_id[:GRID_W]
            wrap_p = jnp.where(cc == 0, 0.0, pltpu.roll(gl, 1, axis=0))
            wrap_n = jnp.where(cc == GRID_W - 1, 0.0, pltpu.roll(g0, GRID_W - 1, axis=0))
            prev = jnp.concatenate([wrap_p, z[:seq - GRID_W]], axis=0)
            nxt = jnp.concatenate([z[GRID_W:], wrap_n], axis=0)
        else:
            prev = jnp.where(row_id == 0, 0.0, pltpu.roll(z, 1, axis=0))
            nxt = jnp.where(row_id == seq - 1, 0.0, pltpu.roll(z, seq - 1, axis=0))
        return prev * w[0:1] + z * w[1:2] + nxt * w[2:3] + cb_ref[k]

    def seq_start(j):
        return (GRID_W * lax.rem(j, grows) + j // grows) if col_mode else j

    seq_stride = 2 if col_mode else pitch
    seq_ref = ur if col_mode else useq

    def put_seq(val):
        if col_mode:
            ur[...] = val
        else:
            for n1 in range(n1h):
                useq[n1 * pitch:n1 * pitch + n2c, :] = val[n1 * n2c:(n1 + 1) * n2c]

    def conv_out():
        if col_mode:
            return cbuf[...]
        return jnp.concatenate([cbuf[n1 * pitch:n1 * pitch + n2c, :] for n1 in range(n1h)], axis=0)

    y_prev = short_conv(zv_ref, 0)
    x_next = (short_conv(z1_ref, 1), short_conv(z2_ref, 2))
    put_seq(y_prev)
    f1 = f1_ref[...]
    f3 = f3_ref[...]
    for order in range(2):
        def stage1(j, carry):
            x = seq_ref[pl.ds(seq_start(j), n1h, stride=seq_stride), :]
            a = _mm(f1, x)
            a_re[pl.ds(j, FFT_N1, stride=pitch), :] = a[:FFT_N1]
            a_im[pl.ds(j, FFT_N1, stride=pitch), :] = a[FFT_N1:]
            return carry
        lax.fori_loop(0, n2c, stage1, 0, unroll=4)

        def mid(kg, carry):
            k1s = [kg * HY_KG + kk for kk in range(HY_KG)]
            offs = [pl.multiple_of(k1 * pitch, 8) for k1 in k1s]
            xin = [jnp.concatenate([a_re[pl.ds(o, n2c), :], a_im[pl.ds(o, n2c), :]], axis=0) for o in offs]
            xf = [_mm(w2_ref[k1], a) for k1, a in zip(k1s, xin)]
            ys = []
            for k1, x in zip(k1s, xf):
                xr, xi = x[:n2c], x[n2c:]
                gr, gi = g_ref[order, k1, 0], g_ref[order, k1, 1]
                ys.append(jnp.concatenate([xr * gr - xi * gi, xr * gi + xi * gr], axis=0))
            bm = [_mm(v2_ref[k1], y) for k1, y in zip(k1s, ys)]
            for o, b in zip(offs, bm):
                a_re[pl.ds(o, n2c), :] = b[:n2c]
                a_im[pl.ds(o, n2c), :] = b[n2c:]
            return carry
        lax.fori_loop(0, FFT_N1 // HY_KG, mid, 0)

        def stage3(j, carry):
            b = jnp.concatenate([a_re[pl.ds(j, FFT_N1, stride=pitch), :],
                                 a_im[pl.ds(j, FFT_N1, stride=pitch), :]], axis=0)
            cbuf[pl.ds(seq_start(j), n1h, stride=seq_stride), :] = _mm(f3, b)
            return carry
        lax.fori_loop(0, n2c, stage3, 0, unroll=4)

        y_prev = x_next[order] * (conv_out() + y_prev * bias_ref[order:order + 1, :])
        if order == 0:
            put_seq(y_prev)
    o_ref[...] = y_prev


def _hyena_latent(lay, z, conv_w, conv_b, spec_x, bias, consts_x, col_mode):
    _, f1, w2, v2, f3 = consts_x
    seq = lay.seq
    assert lay.n_ctx % seq == 0
    row0 = lay.n_ctx // seq
    n2c = seq // (FFT_N1 // 2)
    pitch = n2c + HY_PAD
    nslab = B_CH // 2 // HY_SLAB
    s0 = nslab if col_mode else 0
    cps = B_CH // HY_SLAB
    zspec = lambda k: pl.BlockSpec((seq, HY_SLAB), lambda j, b, k=k: (row0 + b, k * cps + s0 + j))
    once = lambda a: pl.BlockSpec(a.shape, lambda j, b: (0,) * a.ndim, pipeline_mode=pl.Buffered(1))
    mats = (jnp.asarray(f1, BF16), jnp.asarray(w2, BF16), jnp.asarray(v2, BF16), jnp.asarray(f3, BF16))
    seq_rows = (FFT_N1 // 2) * pitch
    return pl.pallas_call(
        functools.partial(_hyena_lat_kernel, col_mode=col_mode, seq=seq),
        grid=(nslab, lay.b),
        in_specs=[zspec(0), zspec(1), zspec(2),
                  pl.BlockSpec((3, B_SHORT, HY_SLAB), lambda j, b: (0, 0, s0 + j)),
                  pl.BlockSpec((3, 1, HY_SLAB), lambda j, b: (0, 0, s0 + j)),
                  pl.BlockSpec((2, FFT_N1, 2, n2c, HY_SLAB), lambda j, b: (0, 0, 0, 0, s0 + j),
                               pipeline_mode=pl.Buffered(1)),
                  pl.BlockSpec((2, HY_SLAB), lambda j, b: (0, s0 + j))] + [once(m) for m in mats],
        out_specs=pl.BlockSpec((seq, HY_SLAB), lambda j, b: (b, j)),
        out_shape=jax.ShapeDtypeStruct((lay.b * seq, nslab * HY_SLAB), F32),
        scratch_shapes=[pltpu.VMEM((seq_rows, HY_SLAB), F32), pltpu.VMEM((seq, HY_SLAB), F32),
                        pltpu.VMEM((FFT_N1 * pitch, HY_SLAB), F32), pltpu.VMEM((FFT_N1 * pitch, HY_SLAB), F32),
                        pltpu.VMEM((seq if col_mode else seq_rows, HY_SLAB), F32)],
        compiler_params=_cparams(("parallel", "parallel")),
        name="hyena_latent_col" if col_mode else "hyena_latent_row",
    )(z, z, z, jnp.transpose(conv_w, (1, 0, 2)), conv_b.reshape(3, 1, B_CH), spec_x, bias, *mats)


def _hyena(lay, z, conv_w, conv_b, spec_c, spec_x, bias, consts_c, consts_x):
    lay_c = _Layout(lay.b, lay.ctx, 0)
    u3 = _dwconv(lay_c, z[:lay.n_ctx], conv_w.reshape(B_SHORT, 3 * B_CH), conv_b.reshape(1, 3 * B_CH), False,
                 B_CH, split=True)
    yc = _hyena_segment(u3, lay.ctx, lay.b, consts_c, spec_c, bias)
    yl = jnp.concatenate([_hyena_latent(lay, z, conv_w, conv_b, spec_x, bias, consts_x, cm) for cm in (False, True)],
                         axis=1)
    return jnp.concatenate([yc, yl], axis=0)


def _head_norm_gate(o, gate, norm_w, heads, width):
    outs = []
    for h in range(heads):
        oh = o[:, h * width:(h + 1) * width]
        y = oh * lax.rsqrt(jnp.mean(oh * oh, axis=-1, keepdims=True) + NORM_EPS) * norm_w
        outs.append(y * _silu(gate[:, h * width:(h + 1) * width]))
    return jnp.concatenate(outs, axis=1)


def _merge_kernel(x_ref, mod_ref, nw_ref, a0_ref, a1_ref, ag_ref, yb_ref, c0_ref, c1_ref, cx_ref, cz_ref,
                  d0_ref, d1_ref, dg_ref, an_ref, cd_ref, cn_ref, dn_ref, wg_ref, wb_ref, wo_ref, o_ref):
    xv = x_ref[...]
    m = mod_ref[...]
    hb = _prenorm(xv, nw_ref[1:2, :], m[3:4, :], m[4:5, :]).astype(BF16)
    ya = _head_norm_gate(a0_ref[...] + a1_ref[...], ag_ref[...], an_ref[...], A_HEADS, A_DV)
    yd = _head_norm_gate(d0_ref[...] + d1_ref[...], dg_ref[...], dn_ref[...], D_HEADS, D_DV)
    yc = (c0_ref[...] + c1_ref[...] + cd_ref[...] * cx_ref[...]) * _silu(cz_ref[...])
    gw = C_INNER // C_GROUPS
    cn = cn_ref[...]
    yc = jnp.concatenate(
        [yc[:, g * gw:(g + 1) * gw]
         * lax.rsqrt(jnp.mean(yc[:, g * gw:(g + 1) * gw] ** 2, axis=-1, keepdims=True) + NORM_EPS)
         * cn[:, g * gw:(g + 1) * gw] for g in range(C_GROUPS)], axis=1)
    acc = jnp.zeros(xv.shape, F32)
    for k, y in enumerate((ya, yb_ref[...], yc, yd)):
        gate = jax.nn.sigmoid(jnp.dot(hb, wg_ref[:, k * D_MODEL:(k + 1) * D_MODEL], preferred_element_type=F32))
        acc = acc + gate * jnp.dot(y.astype(BF16), wb_ref[k], preferred_element_type=F32)
    o_ref[...] = xv + m[5:6, :] * jnp.dot(acc.astype(BF16), wo_ref[...], preferred_element_type=F32)


def _merge(lay, x, mod, nw, a0, a1, ag, yb, c0, c1, cx, cz, d0, d1, dg, an, cd, cn, dn, wg, wb, wo):
    n, d = x.shape
    row = lambda w: pl.BlockSpec((TOK_TILE, w), lambda i: (i, 0))
    once = lambda a: pl.BlockSpec(a.shape, lambda i: (0,) * a.ndim, pipeline_mode=pl.Buffered(1))
    return pl.pallas_call(
        _merge_kernel,
        grid=(n // TOK_TILE,),
        in_specs=[row(d), pl.BlockSpec((None, N_MOD, d), lambda i: (lay.mod_index(i), 0, 0)), _full(nw.shape)]
                 + [row(BRANCH_W)] * 11 + [_full(an.shape), _full(cd.shape), _full(cn.shape), _full(dn.shape),
                                           once(wg), once(wb), once(wo)],
        out_specs=row(d),
        out_shape=jax.ShapeDtypeStruct((n, d), F32),
        compiler_params=_cparams(("parallel",)), name="merge",
    )(x, mod, nw, a0, a1, ag, yb, c0, c1, cx, cz, d0, d1, dg, an, cd, cn, dn, wg, wb, wo)


def _final_norm_kernel(x_ref, w_ref, o_ref):
    xv = x_ref[...]
    o_ref[...] = xv * lax.rsqrt(jnp.mean(xv * xv, axis=-1, keepdims=True) + NORM_EPS) * w_ref[...]


def _final_norm(lay, x, w):
    d = x.shape[1]
    return pl.pallas_call(
        _final_norm_kernel,
        grid=(lay.b * lay.tl,),
        in_specs=[pl.BlockSpec((ROW_TILE, d), lambda i: (lay.nct + i, 0)), _full((1, d))],
        out_specs=pl.BlockSpec((ROW_TILE, d), lambda i: (i, 0)),
        out_shape=jax.ShapeDtypeStruct((lay.b * lay.seq, d), F32),
        compiler_params=_cparams(("parallel",)), name="final_norm",
    )(x, w.reshape(1, d))


def kernel(x, c, ctx, c_ctx, w_ada, b_ada, norm_w, ffn_up, ffn_down, w_in, gdn_conv, gdn_a_log, gdn_dt_bias, gdn_norm, hy_conv_w, hy_conv_b, hy_w1, hy_b1, hy_w2, hy_b2, hy_wout, hy_freq, hy_bias, ssd_conv_w, ssd_conv_b, ssd_a_log, ssd_dt_bias, ssd_d, ssd_norm, gla_gk_w, gla_gk_b, gla_norm, w_branch, w_out, final_norm):
    b, seq, d = x.shape
    ctx_len = ctx.shape[1]
    depth = w_ada.shape[0]
    lay = _Layout(b, ctx_len, seq)
    consts_c = _dft_consts(2 * ctx_len // FFT_N1)
    consts_x = _dft_consts(2 * seq // FFT_N1)

    rp = -(-(1 + b) // 8) * 8
    cond = jnp.concatenate([c_ctx[None, :], c, jnp.zeros((rp - 1 - b, d), F32)], axis=0)
    mods = _ada(cond, w_ada, b_ada).reshape(depth, rp, N_MOD, d)

    xf = jnp.concatenate([ctx.reshape(b * ctx_len, d), x.reshape(b * seq, d)], axis=0)
    for l in range(depth):
        mod = mods[l]
        nw = norm_w[l]
        w_r, w_gate = _rearrange_w_in(w_in[l])
        xf = _ffn(lay, xf, mod, nw, ffn_up[l, 0].astype(BF16), ffn_down[l, 0].astype(BF16), 0)

        qkv, a_gate, zb, c_z, xbc, d_qkv, d_gate, small = _inproj(
            lay, xf, mod, nw, w_r, gdn_conv[l], ssd_conv_w[l], ssd_conv_b[l].reshape(1, C_XBC))
        a0, a1 = _gdn_scan(lay, qkv, small, gdn_a_log[l], gdn_dt_bias[l])
        c0, c1 = _ssd_scan(lay, xbc, small, ssd_a_log[l], ssd_dt_bias[l])
        d0, d1 = _gla_scan(lay, d_qkv, small, gla_gk_w[l], gla_gk_b[l])
        fargs = (hy_w1[l], hy_b1[l], hy_w2[l], hy_b2[l], hy_wout[l], hy_freq[l])
        spec_c = _hyena_spectrum(ctx_len, consts_c, *fargs)
        spec_x = _hyena_spectrum(seq, consts_x, *fargs)
        yb = _hyena(lay, zb, hy_conv_w[l], hy_conv_b[l], spec_c, spec_x, hy_bias[l], consts_c, consts_x)

        xf = _merge(lay, xf, mod, nw, a0, a1, a_gate, yb, c0, c1, xbc, c_z, d0, d1, d_gate,
                    gdn_norm[l].reshape(1, A_DV), jnp.repeat(ssd_d[l], C_HEADDIM).reshape(1, C_INNER),
                    ssd_norm[l].reshape(1, C_INNER), gla_norm[l].reshape(1, D_DV),
                    w_gate, w_branch[l].astype(BF16), w_out[l].astype(BF16))
        xf = _ffn(lay, xf, mod, nw, ffn_up[l, 1].astype(BF16), ffn_down[l, 1].astype(BF16), 2)
    return _final_norm(lay, xf, final_norm).reshape(b, seq, d)
```

```python
import functools
import math

import numpy as np
import jax
import jax.numpy as jnp
from jax import lax
from jax.experimental import pallas as pl
from jax.experimental.pallas import tpu as pltpu

F32 = jnp.float32
BF16 = jnp.bfloat16
HI = lax.Precision.HIGHEST

D_MODEL = 1024
GRID_W = 64
CHUNK = 64
NORM_EPS = 1e-6
N_MOD = 9
D_FF = 2816
SHORT_CONV = 5

A_HEADS, A_DK, A_DV = 4, 128, 128
A_QKV = A_HEADS * (2 * A_DK + A_DV)
B_CH, B_SHORT, B_EMB, B_FFN, B_INNER_MLPS = 512, 3, 33, 64, 2
B_WINDOW_SHIFT, B_DECAY_SHORT_PCT, B_DECAY_LONG_PCT, B_DECAY_TARGET = 0.05, 0.3, 1.5, 1e-2
C_HEADS, C_HEADDIM, C_GROUPS, C_STATE = 8, 64, 2, 64
C_INNER = C_HEADS * C_HEADDIM
C_XBC = C_INNER + 2 * C_GROUPS * C_STATE
D_HEADS, D_DK, D_DV, D_RANK = 4, 64, 128, 16
D_GATE_NORM = 16.0
N_BRANCH, BRANCH_W = 4, 512

IN_SIZES = (A_QKV, A_HEADS * A_DV, 2 * A_HEADS, 2 * A_HEADS, 3 * B_CH, C_INNER, C_XBC, 2 * C_HEADS,
            D_HEADS * (2 * D_DK + D_DV), D_HEADS * D_DV, 2 * D_RANK, N_BRANCH * D_MODEL)
IN_OFFS = tuple(int(v) for v in np.cumsum((0,) + IN_SIZES))

ROW_TILE = 256
TOK_TILE = 512
SMALL_W = 128
VMEM_LIMIT = 56 * 1024 * 1024


def _cparams(sem):
    return pltpu.CompilerParams(dimension_semantics=sem, vmem_limit_bytes=VMEM_LIMIT)


def _mm(a, b):
    return jnp.dot(a.astype(BF16), b.astype(BF16), preferred_element_type=F32)


def _mm_nt(a, b):
    return lax.dot_general(a.astype(BF16), b.astype(BF16), (((1,), (1,)), ((), ())),
                           preferred_element_type=F32)


def _mm_hi(a, b):
    return jnp.dot(a, b, precision=HI, preferred_element_type=F32)


def _bf16_parts(v, parts):
    out, rest = [], v
    for _ in range(parts):
        hi = rest.astype(BF16)
        out.append(hi)
        rest = rest - hi.astype(F32)
    return out


def _mm_sel_l(sel, v, parts=3):
    sb = sel.astype(BF16)
    return sum(jnp.dot(sb, p, preferred_element_type=F32) for p in _bf16_parts(v, parts))


def _mm_sel_r(v, sel, parts=3):
    sb = sel.astype(BF16)
    return sum(jnp.dot(p, sb, preferred_element_type=F32) for p in _bf16_parts(v, parts))


def _silu(v):
    return v * jax.nn.sigmoid(v)


def _softplus(v):
    return jnp.maximum(v, 0.0) + jnp.log1p(jnp.exp(-jnp.abs(v)))


def _log_sigmoid(v):
    return jnp.minimum(v, 0.0) - jnp.log1p(jnp.exp(-jnp.abs(v)))


def _prenorm(xv, gain, shift, scale):
    ms = jnp.mean(xv * xv, axis=-1, keepdims=True)
    return (xv * lax.rsqrt(ms + NORM_EPS) * gain) * (1.0 + scale) + shift


def _full(shape):
    nd = len(shape)
    return pl.BlockSpec(shape, lambda *_: (0,) * nd)


class _Layout:
    def __init__(self, batch, ctx_len, seq):
        assert ctx_len % ROW_TILE == 0 and seq % ROW_TILE == 0
        assert (batch * ctx_len) % TOK_TILE == 0 and seq % TOK_TILE == 0
        self.b, self.ctx, self.seq = batch, ctx_len, seq
        self.n_ctx = batch * ctx_len
        self.n = batch * (ctx_len + seq)
        self.tc, self.tl = ctx_len // ROW_TILE, seq // ROW_TILE
        self.nct = batch * self.tc
        self.tiles = self.n // ROW_TILE

    def mod_index(self, i, tile=TOK_TILE):
        nct = self.n_ctx // tile
        return jnp.where(i < nct, 0, 1 + (i - nct) // (self.seq // tile))

    def seg_first(self, i):
        if self.tl == 0:
            return lax.rem(i, self.tc) == 0
        return jnp.where(i < self.nct, lax.rem(i, self.tc) == 0, lax.rem(i - self.nct, self.tl) == 0)

    def seg_last(self, i):
        if self.tl == 0:
            return lax.rem(i, self.tc) == self.tc - 1
        return jnp.where(i < self.nct, lax.rem(i, self.tc) == self.tc - 1,
                         lax.rem(i - self.nct, self.tl) == self.tl - 1)

    def fwd_tile(self, b, s):
        return jnp.where(s < self.tc, b * self.tc + s, self.nct + b * self.tl + (s - self.tc))

    def rev_tile(self, b, s):
        return jnp.where(s < self.tc, b * self.tc + (self.tc - 1 - s),
                         self.nct + b * self.tl + (self.tl - 1 - (s - self.tc)))


def _ada_kernel(c_ref, w_ref, b_ref, o_ref):
    o_ref[...] = _mm_hi(_silu(c_ref[...]), w_ref[...]) + b_ref[...]


def _ada(cond, w_ada, b_ada):
    depth, d, nm = w_ada.shape
    rp = cond.shape[0]
    tn = 1152
    return pl.pallas_call(
        _ada_kernel,
        grid=(depth, nm // tn),
        in_specs=[_full((rp, d)),
                  pl.BlockSpec((None, d, tn), lambda l, j: (l, 0, j)),
                  pl.BlockSpec((None, 1, tn), lambda l, j: (l, 0, j))],
        out_specs=pl.BlockSpec((None, rp, tn), lambda l, j: (l, 0, j)),
        out_shape=jax.ShapeDtypeStruct((depth, rp, nm), F32),
        compiler_params=_cparams(("parallel", "parallel")), name="ada",
    )(cond, w_ada, b_ada.reshape(depth, 1, nm))


FF_CHUNK = 256


def _ffn_kernel(x_ref, mod_ref, nw_ref, wup_ref, wdn_ref, o_ref, *, sub):
    xv = x_ref[...]
    m = mod_ref[...]
    h = _prenorm(xv, nw_ref[sub:sub + 1, :], m[3 * sub:3 * sub + 1, :], m[3 * sub + 1:3 * sub + 2, :])
    hb = h.astype(BF16)
    acc = jnp.zeros(xv.shape, F32)
    for c in range(D_FF // FF_CHUNK):
        lo = c * FF_CHUNK
        a = jnp.dot(hb, wup_ref[:, lo:lo + FF_CHUNK], preferred_element_type=F32)
        g = jnp.dot(hb, wup_ref[:, D_FF + lo:D_FF + lo + FF_CHUNK], preferred_element_type=F32)
        acc = acc + jnp.dot((_silu(a) * g).astype(BF16), wdn_ref[lo:lo + FF_CHUNK, :],
                            preferred_element_type=F32)
    o_ref[...] = xv + 0.5 * m[3 * sub + 2:3 * sub + 3, :] * acc


def _ffn(lay, x, mod, nw, wup, wdn, sub):
    n, d = x.shape
    return pl.pallas_call(
        functools.partial(_ffn_kernel, sub=sub),
        grid=(n // TOK_TILE,),
        in_specs=[pl.BlockSpec((TOK_TILE, d), lambda i: (i, 0)),
                  pl.BlockSpec((None, N_MOD, d), lambda i: (lay.mod_index(i), 0, 0)),
                  _full(nw.shape),
                  pl.BlockSpec(wup.shape, lambda i: (0, 0), pipeline_mode=pl.Buffered(1)),
                  pl.BlockSpec(wdn.shape, lambda i: (0, 0), pipeline_mode=pl.Buffered(1))],
        out_specs=pl.BlockSpec((TOK_TILE, d), lambda i: (i, 0)),
        out_shape=jax.ShapeDtypeStruct((n, d), F32),
        compiler_params=_cparams(("parallel",)), name="ffn",
    )(x, mod, nw, wup, wdn)


PROJ_W = (A_QKV, A_HEADS * A_DV, 3 * B_CH, C_INNER, C_XBC, D_HEADS * (2 * D_DK + D_DV), D_HEADS * D_DV, SMALL_W)
PROJ_O = tuple(int(v) for v in np.cumsum((0,) + PROJ_W))


def _rearrange_w_in(w_in):
    o = IN_OFFS
    small = jnp.concatenate([w_in[:, o[2]:o[4]], w_in[:, o[7]:o[8]], w_in[:, o[10]:o[11]],
                             jnp.zeros((w_in.shape[0], SMALL_W - 64), w_in.dtype)], axis=1)
    w_r = jnp.concatenate([w_in[:, o[0]:o[1]], w_in[:, o[1]:o[2]], w_in[:, o[4]:o[5]], w_in[:, o[5]:o[6]],
                           w_in[:, o[6]:o[7]], w_in[:, o[8]:o[9]], w_in[:, o[9]:o[10]], small], axis=1)
    return w_r.astype(BF16), w_in[:, o[11]:o[12]].astype(BF16)


PROJ_HALO = 16
PROJ_CONV = (0, 4)


def _inproj_kernel(xp_ref, x_ref, xn_ref, mod_ref, nw_ref, w_ref, cwa_ref, cwc_ref, cbc_ref, *rest, lay):
    o_refs, zbuf = rest[:-1], rest[-1]
    i = pl.program_id(0)
    m = mod_ref[...]
    norm = lambda r: _prenorm(r[...], nw_ref[1:2, :], m[3:4, :], m[4:5, :]).astype(BF16)
    hb = norm(x_ref)
    hb_all = jnp.concatenate([norm(xp_ref), hb, norm(xn_ref)], axis=0)
    keep_p = jnp.where(lay.seg_first(i), 0.0, 1.0)
    keep_n = jnp.where(lay.seg_last(i), 0.0, 1.0)
    convs = {PROJ_CONV[0]: (cwa_ref, None), PROJ_CONV[1]: (cwc_ref, cbc_ref)}
    for k, o_ref in enumerate(o_refs):
        wk = w_ref[:, PROJ_O[k]:PROJ_O[k + 1]]
        if k not in convs:
            o_ref[...] = jnp.dot(hb, wk, preferred_element_type=F32)
            continue
        cw_ref, cb_ref = convs[k]
        width = PROJ_W[k]
        z = jnp.dot(hb_all, wk, preferred_element_type=F32)
        zbuf[0:PROJ_HALO, 0:width] = z[:PROJ_HALO] * keep_p
        zbuf[PROJ_HALO:PROJ_HALO + ROW_TILE, 0:width] = z[PROJ_HALO:PROJ_HALO + ROW_TILE]
        zbuf[PROJ_HALO + ROW_TILE:, 0:width] = z[PROJ_HALO + ROW_TILE:] * keep_n
        taps = cw_ref.shape[0]
        left = (taps - 1) // 2
        acc = None
        for t in range(taps):
            off = t - left
            term = zbuf[PROJ_HALO + off:PROJ_HALO + off + ROW_TILE, 0:width] * cw_ref[t:t + 1, :]
            acc = term if acc is None else acc + term
        if cb_ref is not None:
            acc = acc + cb_ref[...]
        o_ref[...] = _silu(acc)


def _inproj(lay, x, mod, nw, w_r, conv_a, conv_c, conv_c_bias):
    n, d = x.shape
    rh = ROW_TILE // PROJ_HALO
    lasth = n // PROJ_HALO - 1
    return pl.pallas_call(
        functools.partial(_inproj_kernel, lay=lay),
        grid=(n // ROW_TILE,),
        in_specs=[pl.BlockSpec((PROJ_HALO, d), lambda i: (jnp.maximum(i * rh - 1, 0), 0)),
                  pl.BlockSpec((ROW_TILE, d), lambda i: (i, 0)),
                  pl.BlockSpec((PROJ_HALO, d), lambda i: (jnp.minimum((i + 1) * rh, lasth), 0)),
                  pl.BlockSpec((None, N_MOD, d), lambda i: (lay.mod_index(i, ROW_TILE), 0, 0)),
                  _full(nw.shape),
                  pl.BlockSpec(w_r.shape, lambda i: (0, 0), pipeline_mode=pl.Buffered(1)),
                  _full(conv_a.shape), _full(conv_c.shape), _full(conv_c_bias.shape)],
        out_specs=[pl.BlockSpec((ROW_TILE, w), lambda i: (i, 0)) for w in PROJ_W],
        out_shape=[jax.ShapeDtypeStruct((n, w), F32) for w in PROJ_W],
        scratch_shapes=[pltpu.VMEM((ROW_TILE + 2 * PROJ_HALO, max(PROJ_W[k] for k in PROJ_CONV)), F32)],
        compiler_params=_cparams(("parallel",)), name="inproj",
    )(x, x, x, mod, nw, w_r, conv_a, conv_c, conv_c_bias)


HALO = 8


def _conv_tile(lay, tile, xp_ref, x_ref, xn_ref, w_ref, b_ref, buf, act):
    taps = w_ref.shape[0]
    keep_p = jnp.where(lay.seg_first(tile), 0.0, 1.0)
    keep_n = jnp.where(lay.seg_last(tile), 0.0, 1.0)
    buf[0:HALO, :] = xp_ref[...] * keep_p
    buf[HALO:HALO + ROW_TILE, :] = x_ref[...]
    buf[HALO + ROW_TILE:2 * HALO + ROW_TILE, :] = xn_ref[...] * keep_n
    left = (taps - 1) // 2
    acc = b_ref[...] + buf[HALO - left:HALO - left + ROW_TILE, :] * w_ref[0:1, :]
    for k in range(1, taps):
        acc = acc + buf[HALO - left + k:HALO - left + k + ROW_TILE, :] * w_ref[k:k + 1, :]
    return _silu(acc) if act else acc


def _dwconv_kernel(xp_ref, x_ref, xn_ref, w_ref, b_ref, o_ref, buf, *, lay, act):
    o_ref[...] = _conv_tile(lay, pl.program_id(0), xp_ref, x_ref, xn_ref, w_ref, b_ref, buf, act)


def _dwconv(lay, x, w, bias, act, wb, split=False):
    n, width = x.shape
    taps = w.shape[0]
    nwb = width // wb
    r8 = ROW_TILE // HALO
    last8 = n // HALO - 1
    if split:
        out_spec = pl.BlockSpec((None, ROW_TILE, wb), lambda i, j: (j, i, 0))
        out_shape = jax.ShapeDtypeStruct((nwb, n, wb), F32)
    else:
        out_spec = pl.BlockSpec((ROW_TILE, wb), lambda i, j: (i, j))
        out_shape = jax.ShapeDtypeStruct((n, width), F32)
    return pl.pallas_call(
        functools.partial(_dwconv_kernel, lay=lay, act=act),
        grid=(lay.tiles, nwb),
        in_specs=[pl.BlockSpec((HALO, wb), lambda i, j: (jnp.maximum(i * r8 - 1, 0), j)),
                  pl.BlockSpec((ROW_TILE, wb), lambda i, j: (i, j)),
                  pl.BlockSpec((HALO, wb), lambda i, j: (jnp.minimum((i + 1) * r8, last8), j)),
                  pl.BlockSpec((taps, wb), lambda i, j: (0, j)),
                  pl.BlockSpec((1, wb), lambda i, j: (0, j))],
        out_specs=out_spec,
        out_shape=out_shape,
        scratch_shapes=[pltpu.VMEM((ROW_TILE + 2 * HALO, wb), F32)],
        compiler_params=_cparams(("parallel", "parallel")), name="dwconv",
    )(x, x, x, w, bias)


def _tri_masks(rev):
    ii = lax.broadcasted_iota(jnp.int32, (CHUNK, CHUNK), 0)
    jj = lax.broadcasted_iota(jnp.int32, (CHUNK, CHUNK), 1)
    return (ii <= jj, ii < jj) if rev else (ii >= jj, ii > jj)


def _masked_decay(col, row, incl):
    return jnp.where(incl, jnp.exp(jnp.where(incl, col - row, 0.0)), 0.0)


def _scan_call(kernel, lay, ins_tiled, ins_full, out_w, scratch):
    steps = lay.tc + lay.tl
    specs = []
    args = []
    for walk in (lay.fwd_tile, lay.rev_tile):
        for a in ins_tiled:
            specs.append(pl.BlockSpec((ROW_TILE, a.shape[1]), lambda b, s, walk=walk: (walk(b, s), 0)))
            args.append(a)
    for a in ins_full:
        specs.append(_full(a.shape))
        args.append(a)
    return pl.pallas_call(
        kernel,
        grid=(lay.b, steps),
        in_specs=specs,
        out_specs=[pl.BlockSpec((ROW_TILE, out_w), lambda b, s: (lay.fwd_tile(b, s), 0)),
                   pl.BlockSpec((ROW_TILE, out_w), lambda b, s: (lay.rev_tile(b, s), 0))],
        out_shape=[jax.ShapeDtypeStruct((lay.n, out_w), F32)] * 2,
        scratch_shapes=scratch,
        compiler_params=_cparams(("arbitrary", "arbitrary")), name=kernel.__name__.strip("_"),
    )(*args)


def _ssd_kernel(xf_ref, sf_ref, xr_ref, sr_ref, alog_ref, dtb_ref, alog_t_ref, dtb_t_ref, exp_ref,
                of_ref, or_ref, st_ref):
    @pl.when(pl.program_id(1) == 0)
    def _():
        st_ref[...] = jnp.zeros(st_ref.shape, F32)

    hpg = C_HEADS // C_GROUPS
    gw = hpg * C_HEADDIM
    nch = ROW_TILE // CHUNK
    refs = ((xf_ref, sf_ref, of_ref), (xr_ref, sr_ref, or_ref))
    units, pre = [], {}
    for d, (x_ref, s_ref, _) in enumerate(refs):
        incl, _ = _tri_masks(d == 1)
        m_col = incl.astype(F32)
        m_row = m_col.T
        sm = s_ref[...]
        sm_t = sm.T
        dt_all = _softplus(sm[:, 16:32] + dtb_ref[...])
        a_all = -jnp.exp(alog_ref[...]) * dt_all
        a_t_all = -jnp.exp(alog_t_ref[...]) * _softplus(sm_t[16:32, :] + dtb_t_ref[...])
        bm_t = x_ref[:, C_INNER:C_INNER + C_GROUPS * C_STATE].T
        for c in range(nch):
            rows = slice(c * CHUNK, (c + 1) * CHUNK)
            units.append((d, c))
            pre[d, c] = dict(incl=incl, m_col=m_col, m_row=m_row, a_ch=a_all[rows], a_t=a_t_all[:, rows],
                             dt=dt_all[rows], bm_t=bm_t[:, rows], e=exp_ref[d], xs=x_ref[rows, 0:C_INNER],
                             bm=x_ref[rows, C_INNER:C_INNER + C_GROUPS * C_STATE],
                             cm=x_ref[rows, C_INNER + C_GROUPS * C_STATE:C_XBC])
    for u in units:
        w = pre[u]
        w["ac"] = _mm_sel_l(w["m_col"], w["a_ch"])
        w["ac_t"] = _mm_sel_r(w["a_t"], w["m_row"])
        w["alast"] = jnp.sum(w["a_ch"], axis=0, keepdims=True)
    for u in units:
        w = pre[u]
        w["xdt"] = w["xs"] * _mm_sel_r(w["dt"], w["e"])
        w["eac_e"] = _mm_sel_r(jnp.exp(w["ac"]), w["e"])
        w["xd"] = w["xdt"] * _mm_sel_r(jnp.exp(w["alast"] - w["ac"]), w["e"])
        w["dle"] = _mm_sel_r(jnp.broadcast_to(jnp.exp(w["alast"]), (8, 2 * C_HEADS)), w["e"])[0:1]
    grp = lambda a, g: a[:, g * C_STATE:(g + 1) * C_STATE]
    cb = {(u, g): _mm_nt(grp(pre[u]["cm"], g), grp(pre[u]["bm"], g)) for u in units for g in range(C_GROUPS)}
    upd = {(u, g): _mm(pre[u]["bm_t"][g * C_STATE:(g + 1) * C_STATE], pre[u]["xd"][:, g * gw:(g + 1) * gw])
           for u in units for g in range(C_GROUPS)}
    y_diag = {}
    for u in units:
        w = pre[u]
        for h in range(C_HEADS):
            col = u[0] * C_HEADS + h
            seg = _masked_decay(w["ac"][:, col:col + 1], w["ac_t"][col:col + 1, :], w["incl"])
            y_diag[u, h] = _mm(cb[u, h // hpg] * seg, w["xdt"][:, h * C_HEADDIM:(h + 1) * C_HEADDIM])
    st = {(d, g): st_ref[d, g] for d in range(2) for g in range(C_GROUPS)}
    entry = {}
    for p in range(nch):
        for d in range(2):
            u = (d, (nch - 1 - p) if d == 1 else p)
            for g in range(C_GROUPS):
                entry[u, g] = st[d, g]
                st[d, g] = pre[u]["dle"][:, g * gw:(g + 1) * gw] * st[d, g] + upd[u, g]
    for d in range(2):
        for g in range(C_GROUPS):
            st_ref[d, g] = st[d, g]
    for u in units:
        y_off = [_mm(grp(pre[u]["cm"], g), entry[u, g]) for g in range(C_GROUPS)]
        refs[u[0]][2][u[1] * CHUNK:(u[1] + 1) * CHUNK, :] = (
            jnp.concatenate([y_diag[u, h] for h in range(C_HEADS)], axis=1)
            + pre[u]["eac_e"] * jnp.concatenate(y_off, axis=1))


def _ssd_scan(lay, xbc, small, a_log, dt_bias):
    expand = np.zeros((2, 2 * C_HEADS, C_INNER), np.float32)
    for d in range(2):
        for h in range(C_HEADS):
            expand[d, d * C_HEADS + h, h * C_HEADDIM:(h + 1) * C_HEADDIM] = 1.0
    al = a_log.reshape(1, 2 * C_HEADS)
    db = dt_bias.reshape(1, 2 * C_HEADS)
    return _scan_call(_ssd_kernel, lay, (xbc, small), (al, db, al.T, db.T, jnp.asarray(expand)), C_INNER,
                      [pltpu.VMEM((2, C_GROUPS, C_STATE, C_INNER // C_GROUPS), F32)])


def _gla_kernel(xf_ref, sf_ref, xr_ref, sr_ref, gkw_ref, gkb_ref, of_ref, or_ref, st_ref):
    @pl.when(pl.program_id(1) == 0)
    def _():
        st_ref[...] = jnp.zeros(st_ref.shape, F32)

    nk = D_HEADS * D_DK
    nch = ROW_TILE // CHUNK
    refs = ((xf_ref, sf_ref, of_ref), (xr_ref, sr_ref, or_ref))
    units, pre = [], {}
    for d, (x_ref, s_ref, _) in enumerate(refs):
        incl, _ = _tri_masks(d == 1)
        m_col = incl.astype(F32)
        lr = s_ref[:, 32 + D_RANK * d:32 + D_RANK * (d + 1)]
        g_all = _log_sigmoid(_mm_hi(lr, gkw_ref[d]) + gkb_ref[d:d + 1, :]) / D_GATE_NORM
        v_t = x_ref[:, 2 * nk:].T
        for c in range(nch):
            rows = slice(c * CHUNK, (c + 1) * CHUNK)
            g_ch = g_all[rows]
            gc = _mm_sel_l(m_col, g_ch)
            glast = jnp.sum(g_ch, axis=0, keepdims=True)
            q = x_ref[rows, 0:nk] * (D_DK ** -0.5)
            k = x_ref[rows, nk:2 * nk]
            v = x_ref[rows, 2 * nk:]
            kd = k * jnp.exp(glast - gc)
            qd = q * jnp.exp(gc)
            qr = q * jnp.exp(gc - glast)
            dlast = jnp.exp(glast)
            for h in range(D_HEADS):
                ks = slice(h * D_DK, (h + 1) * D_DK)
                vs = slice(h * D_DV, (h + 1) * D_DV)
                units.append((d, c, h))
                pre[d, c, h] = dict(incl=incl, qr=qr[:, ks], kd=kd[:, ks], qd=qd[:, ks], v=v[:, vs],
                                    v_t=v_t[vs, rows], dl=dlast[:, ks])
    aqk = [jnp.where(pre[u]["incl"], _mm_nt(pre[u]["qr"], pre[u]["kd"]), 0.0) for u in units]
    upd = {u: _mm(pre[u]["v_t"], pre[u]["kd"]) for u in units}
    intra = {u: _mm(a, pre[u]["v"]) for u, a in zip(units, aqk)}
    heads = [(d, h) for d in range(2) for h in range(D_HEADS)]
    st = {dh: st_ref[dh[0], dh[1]] for dh in heads}
    entry = {}
    for p in range(nch):
        for d, h in heads:
            u = (d, (nch - 1 - p) if d == 1 else p, h)
            entry[u] = st[d, h]
            st[d, h] = st[d, h] * pre[u]["dl"] + upd[u]
    for dh in heads:
        st_ref[dh[0], dh[1]] = st[dh]
    inter = {u: _mm_nt(pre[u]["qd"], entry[u]) for u in units}
    for d, (_, _, o_ref) in enumerate(refs):
        for c in range(nch):
            o_ref[c * CHUNK:(c + 1) * CHUNK, :] = jnp.concatenate(
                [intra[d, c, h] + inter[d, c, h] for h in range(D_HEADS)], axis=1)


def _gla_scan(lay, qkv, small, gk_w, gk_b):
    return _scan_call(_gla_kernel, lay, (qkv, small), (gk_w, gk_b), D_HEADS * D_DV,
                      [pltpu.VMEM((2, D_HEADS, D_DV, D_DK), F32)])


TRI_BLOCK = 16
GDN_GROUP = 32


def _unit_lower_inverse(nms, eye, blk):
    dg = [jnp.where(blk, nm, 0.0) for nm in nms]
    off = [nm - d for nm, d in zip(nms, dg)]
    t0 = [eye - d for d in dg]
    p = [_mm(d, d) for d in dg]
    for it in range(3):
        t0 = [t + _mm(t, q) for t, q in zip(t0, p)]
        if it < 2:
            p = [_mm(q, q) for q in p]
    m = [_mm(t, o) for t, o in zip(t0, off)]
    m2 = [_mm(a, a) for a in m]
    r = [eye - a for a in m]
    r = [a + _mm(a, b) for a, b in zip(r, m2)]
    return [_mm(a, t) for a, t in zip(r, t0)]


def _gdn_kernel(xf_ref, sf_ref, xr_ref, sr_ref, alog_ref, dtb_ref, alog_t_ref, dtb_t_ref,
                of_ref, or_ref, st_ref):
    @pl.when(pl.program_id(1) == 0)
    def _():
        st_ref[...] = jnp.zeros(st_ref.shape, F32)

    nk = A_HEADS * A_DK
    nch = ROW_TILE // CHUNK
    ii = lax.broadcasted_iota(jnp.int32, (CHUNK, CHUNK), 0)
    jj = lax.broadcasted_iota(jnp.int32, (CHUNK, CHUNK), 1)
    eye = (ii == jj).astype(F32)
    blk = (ii // TRI_BLOCK) == (jj // TRI_BLOCK)
    refs = ((xf_ref, sf_ref, of_ref), (xr_ref, sr_ref, or_ref))

    units, pre = [], {}
    for d, (x_ref, s_ref, _) in enumerate(refs):
        incl, strict = _tri_masks(d == 1)
        m_col = incl.astype(F32)
        m_row = m_col.T
        sm = s_ref[...]
        sm_t = sm.T
        beta_all = jax.nn.sigmoid(sm[:, 0:8])
        g_all = -jnp.exp(alog_ref[...]) * _softplus(sm[:, 8:16] + dtb_ref[...])
        g_t_all = -jnp.exp(alog_t_ref[...]) * _softplus(sm_t[8:16, :] + dtb_t_ref[...])
        qn, kn = [], []
        for h in range(A_HEADS):
            qh = x_ref[:, h * A_DK:(h + 1) * A_DK]
            kh = x_ref[:, nk + h * A_DK:nk + (h + 1) * A_DK]
            qn.append(qh * lax.rsqrt(jnp.sum(qh * qh, axis=-1, keepdims=True) + NORM_EPS) * (A_DK ** -0.5))
            kn.append(kh * lax.rsqrt(jnp.sum(kh * kh, axis=-1, keepdims=True) + NORM_EPS))
        kn_t = [kh.T for kh in kn]
        for c in range(nch):
            rows = slice(c * CHUNK, (c + 1) * CHUNK)
            g_ch = g_all[rows]
            gc = _mm_sel_l(m_col, g_ch)
            gc_t = _mm_sel_r(g_t_all[:, rows], m_row)
            glast = jnp.sum(g_ch, axis=0, keepdims=True)
            for h in range(A_HEADS):
                col = d * A_HEADS + h
                gcc, gcr, gl = gc[:, col:col + 1], gc_t[col:col + 1, :], glast[:, col:col + 1]
                bh = beta_all[rows, col:col + 1]
                kh = kn[h][rows]
                units.append((d, c, h))
                pre[d, c, h] = dict(
                    dmask=_masked_decay(gcc, gcr, incl), strict=strict, egc=jnp.exp(gcc), kh=kh, kb=kh * bh,
                    qh=qn[h][rows], vb=x_ref[rows, 2 * nk + h * A_DV:2 * nk + (h + 1) * A_DV] * bh,
                    kg_t=kn_t[h][:, rows] * jnp.exp(gl - gcr), dl=jnp.exp(gl))
    for g0 in range(0, len(units), GDN_GROUP):
        us = units[g0:g0 + GDN_GROUP]
        kk = [_mm_nt(pre[u]["kb"], pre[u]["kh"]) for u in us]
        qk = [_mm_nt(pre[u]["qh"], pre[u]["kh"]) for u in us]
        tinv = _unit_lower_inverse(
            [jnp.where(pre[u]["strict"], a * pre[u]["dmask"], 0.0) for u, a in zip(us, kk)], eye, blk)
        uw = [_mm(t, jnp.concatenate([pre[u]["vb"], pre[u]["kb"] * pre[u]["egc"]], axis=1))
              for u, t in zip(us, tinv)]
        for u, a, b in zip(us, uw, qk):
            pre[u]["u"] = a[:, :A_DV]
            pre[u]["wq"] = jnp.concatenate([a[:, A_DV:], pre[u]["qh"] * pre[u]["egc"]], axis=0)
            pre[u]["aqk"] = b * pre[u]["dmask"]

    heads = [(d, h) for d in range(2) for h in range(A_HEADS)]
    st = {dh: st_ref[dh[0], dh[1]] for dh in heads}
    for p in range(nch):
        us = [(d, (nch - 1 - p) if d == 1 else p, h) for d, h in heads]
        ws = [_mm(pre[u]["wq"], st[u[0], u[2]]) for u in us]
        v_new = [pre[u]["u"] - w[:CHUNK] for u, w in zip(us, ws)]
        outs = [w[CHUNK:] + _mm(pre[u]["aqk"], v) for u, w, v in zip(us, ws, v_new)]
        for u, v in zip(us, v_new):
            st[u[0], u[2]] = pre[u]["dl"] * st[u[0], u[2]] + _mm(pre[u]["kg_t"], v)
        for d, (_, _, o_ref) in enumerate(refs):
            c = us[d * A_HEADS][1]
            o_ref[c * CHUNK:(c + 1) * CHUNK, :] = jnp.concatenate(outs[d * A_HEADS:(d + 1) * A_HEADS], axis=1)
    for dh in heads:
        st_ref[dh[0], dh[1]] = st[dh]


def _gdn_scan(lay, qkv, small, a_log, dt_bias):
    al = a_log.reshape(1, 2 * A_HEADS)
    db = dt_bias.reshape(1, 2 * A_HEADS)
    return _scan_call(_gdn_kernel, lay, (qkv, small), (al, db, al.T, db.T), A_HEADS * A_DV,
                      [pltpu.VMEM((2, A_HEADS, A_DK, A_DV), F32)])


FFT_N1 = 64
FFT_KB = 8


def _dft_consts(n2):
    n1 = FFT_N1
    m = n1 * n2
    k1 = np.arange(n1, dtype=np.float64)
    ang1 = 2.0 * np.pi * np.outer(k1, k1) / n1
    f1_full = np.concatenate([np.cos(ang1), -np.sin(ang1)], axis=0)
    f1 = f1_full[:, :n1 // 2]
    f3 = np.concatenate([np.cos(ang1[:, :n1 // 2]).T, -np.sin(ang1[:, :n1 // 2]).T], axis=1) / m
    j = np.arange(n2, dtype=np.float64)
    theta = 2.0 * np.pi * (np.outer(j, j)[None] / n2 + (k1[:, None, None] * j[None, None, :]) / m)
    wr, wi = np.cos(theta), -np.sin(theta)
    w2 = np.concatenate([np.concatenate([wr, -wi], axis=2), np.concatenate([wi, wr], axis=2)], axis=1)
    phi = 2.0 * np.pi * (np.outer(j, j)[None] / n2 + (k1[:, None, None] * j[None, :, None]) / m)
    vr, vi = np.cos(phi), np.sin(phi)
    v2 = np.concatenate([np.concatenate([vr, -vi], axis=2), np.concatenate([vi, vr], axis=2)], axis=1)
    return f1_full, f1, w2, v2, f3


FFT_KV = FFT_N1 // 2 + 1
FFT_KH = 40


def _half_spectrum_consts(consts):
    _, f1, w2, v2, f3 = consts
    n1 = FFT_N1
    f1h = np.zeros((2 * FFT_KH, n1 // 2))
    f1h[:FFT_KV] = f1[:FFT_KV]
    f1h[FFT_KH:FFT_KH + FFT_KV] = f1[n1:n1 + FFT_KV]
    w2h = np.zeros((FFT_KH,) + w2.shape[1:])
    w2h[:FFT_KV] = w2[:FFT_KV]
    v2h = np.zeros((FFT_KH,) + v2.shape[1:])
    v2h[:FFT_KV] = v2[:FFT_KV]
    weight = np.full((FFT_KV,), 2.0)
    weight[0] = weight[-1] = 1.0
    f3h = np.zeros((n1 // 2, 2 * FFT_KH))
    f3h[:, :FFT_KV] = f3[:, :FFT_KV] * weight
    f3h[:, FFT_KH:FFT_KH + FFT_KV] = f3[:, n1:n1 + FFT_KV] * weight
    return f1h, w2h, v2h, f3h


def _left_mm_kernel(w_ref, x_ref, o_ref, *, hi):
    if hi:
        o_ref[...] = _mm_hi(w_ref[...], x_ref[...]).astype(o_ref.dtype)
    else:
        o_ref[...] = _mm(w_ref[...], x_ref[...]).astype(o_ref.dtype)


def _left_mm_epi_kernel(w_ref, x_ref, xn_ref, yp_ref, bias_ref, o_ref):
    conv = _mm(w_ref[...], x_ref[...])
    yp = yp_ref[...]
    o_ref[...] = xn_ref[...] * (conv + yp * bias_ref[...])


def _left_mm(wm, x, out_dtype, hi=False, epi=None):
    bs, kdim, ncols = x.shape
    mdim = wm.shape[0]
    tn = min(ncols, 4096)
    xspec = lambda rows: pl.BlockSpec((None, rows, tn), lambda b, j: (b, 0, j))
    if epi is None:
        kern, ins, specs = functools.partial(_left_mm_kernel, hi=hi), (wm, x), [_full(wm.shape), xspec(kdim)]
    else:
        kern, ins = _left_mm_epi_kernel, (wm, x) + tuple(epi)
        specs = [_full(wm.shape), xspec(kdim), xspec(mdim), xspec(mdim), pl.BlockSpec((1, tn), lambda b, j: (0, j))]
    return pl.pallas_call(
        kern, grid=(bs, ncols // tn), in_specs=specs, out_specs=xspec(mdim),
        out_shape=jax.ShapeDtypeStruct((bs, mdim, ncols), out_dtype),
        compiler_params=_cparams(("parallel", "parallel")),
        name="dft_outer" + ("_gate" if epi is not None else "_hi" if hi else ""),
    )(*ins)


def _mid_kernel(a_ref, g_ref, w_ref, v_ref, o_ref, *, n2):
    for kk in range(FFT_KB):
        a = jnp.concatenate([a_ref[0, kk], a_ref[1, kk]], axis=0)
        xf = _mm(w_ref[kk], a)
        xr, xi = xf[:n2], xf[n2:]
        gr, gi = g_ref[kk, 0], g_ref[kk, 1]
        y = jnp.concatenate([xr * gr - xi * gi, xr * gi + xi * gr], axis=0)
        bm = _mm(v_ref[kk], y)
        o_ref[0, kk] = bm[:n2].astype(o_ref.dtype)
        o_ref[1, kk] = bm[n2:].astype(o_ref.dtype)


def _mid_stage(a, g, w2, v2):
    bs, _, n1, n2, ch = a.shape
    return pl.pallas_call(
        functools.partial(_mid_kernel, n2=n2),
        grid=(bs, n1 // FFT_KB),
        in_specs=[pl.BlockSpec((None, 2, FFT_KB, n2, ch), lambda b, j: (b, 0, j, 0, 0)),
                  pl.BlockSpec((FFT_KB, 2, n2, ch), lambda b, j: (j, 0, 0, 0)),
                  pl.BlockSpec((FFT_KB, 2 * n2, 2 * n2), lambda b, j: (j, 0, 0)),
                  pl.BlockSpec((FFT_KB, 2 * n2, 2 * n2), lambda b, j: (j, 0, 0))],
        out_specs=pl.BlockSpec((None, 2, FFT_KB, n2, ch), lambda b, j: (b, 0, j, 0, 0)),
        out_shape=jax.ShapeDtypeStruct(a.shape, a.dtype),
        compiler_params=_cparams(("parallel", "parallel")), name="dft_mid",
    )(a, g, w2, v2)


def _filt_mid_kernel(a_ref, w_ref, o_ref, *, n2):
    for kk in range(FFT_KB):
        a = jnp.concatenate([a_ref[0, kk], a_ref[1, kk]], axis=0)
        xf = _mm_hi(w_ref[kk], a)
        o_ref[kk, 0] = xf[:n2]
        o_ref[kk, 1] = xf[n2:]


def _filt_mid_stage(a, w2):
    bs, _, n1, n2, ch = a.shape
    return pl.pallas_call(
        functools.partial(_filt_mid_kernel, n2=n2),
        grid=(bs, n1 // FFT_KB),
        in_specs=[pl.BlockSpec((None, 2, FFT_KB, n2, ch), lambda b, j: (b, 0, j, 0, 0)),
                  pl.BlockSpec((FFT_KB, 2 * n2, 2 * n2), lambda b, j: (j, 0, 0))],
        out_specs=pl.BlockSpec((None, FFT_KB, 2, n2, ch), lambda b, j: (b, j, 0, 0, 0)),
        out_shape=jax.ShapeDtypeStruct((bs, n1, 2, n2, ch), F32),
        compiler_params=_cparams(("parallel", "parallel")), name="filter_dft_mid",
    )(a, w2)


FILT_ROWS = 512


def _filter_kernel(z_ref, win_ref, sel_ref, w1_ref, b1_ref, w2_ref, b2_ref, wo_ref, fr_ref, o_ref):
    fr = fr_ref[...]
    h = jnp.sin(fr * (_mm_hi(z_ref[...], w1_ref[...]) + b1_ref[...]))
    for i in range(B_INNER_MLPS):
        h = jnp.sin(fr * (_mm_hi(h, w2_ref[i]) + b2_ref[i]))
    ho = _mm_hi(h, wo_ref[...])
    fwd = sel_ref[...] > 0.5
    win = win_ref[...]
    for o in range(2):
        base = o * 2 * B_CH
        o_ref[o] = jnp.where(fwd, ho[:, base:base + B_CH], ho[:, base + B_CH:base + 2 * B_CH]) * win


def _hyena_filter_time(l, w1, b1, w2, b2, w_out, freq):
    bands = (B_EMB - 1) // 2
    circ = jnp.arange(2 * l)
    pos = jnp.where(circ < l, circ, 2 * l - circ)
    pos = jnp.where(circ == l, 0, pos)
    t = (jnp.linspace(0.0, 1.0, l, dtype=F32)[pos])[:, None]
    ang = 2.0 * math.pi * pos.astype(F32)[:, None] / l
    fr = jnp.linspace(1e-4, bands - 1, bands, dtype=F32)[None, :]
    z = jnp.concatenate([t, jnp.cos(fr * ang), -jnp.sin(fr * ang)], axis=-1)
    z = jnp.pad(z, ((0, 0), (0, 128 - B_EMB)))
    max_decay = math.log(B_DECAY_TARGET) / B_DECAY_SHORT_PCT
    min_decay = math.log(B_DECAY_TARGET) / B_DECAY_LONG_PCT
    deltas = jnp.abs(jnp.linspace(min_decay, max_decay, B_CH, dtype=F32))
    win = (jnp.exp(-t * deltas) + B_WINDOW_SHIFT) * (circ != l).astype(F32)[:, None]
    sel = (circ < l).astype(F32)[:, None]
    w1p = jnp.pad(w1, ((0, 128 - B_EMB), (0, 0)))
    rows = min(FILT_ROWS, 2 * l)
    return pl.pallas_call(
        _filter_kernel,
        grid=(2 * l // rows,),
        in_specs=[pl.BlockSpec((rows, 128), lambda i: (i, 0)),
                  pl.BlockSpec((rows, B_CH), lambda i: (i, 0)),
                  pl.BlockSpec((rows, 1), lambda i: (i, 0)),
                  _full(w1p.shape), _full((1, B_FFN)), _full(w2.shape), _full((B_INNER_MLPS, 1, B_FFN)),
                  _full(w_out.shape), _full((1, B_FFN))],
        out_specs=pl.BlockSpec((2, rows, B_CH), lambda i: (0, i, 0)),
        out_shape=jax.ShapeDtypeStruct((2, 2 * l, B_CH), F32),
        compiler_params=_cparams(("parallel",)), name="hyena_filter",
    )(z, win, sel, w1p, b1.reshape(1, B_FFN), w2, b2.reshape(B_INNER_MLPS, 1, B_FFN), w_out,
      freq.reshape(1, B_FFN))


def _hyena_segment(u3, l, bs, consts, g_spec, bias):
    f1_full, f1, w2, v2, f3 = consts
    n2 = 2 * l // FFT_N1
    ch = B_CH
    mid_dtype = BF16 if n2 % 16 == 0 else F32
    view = lambda a: a.reshape(bs, FFT_N1 // 2, n2 * ch)
    y = view(u3[0])
    for order in range(2):
        a = _left_mm(jnp.asarray(f1, BF16), y, mid_dtype)
        bm = _mid_stage(a.reshape(bs, 2, FFT_N1, n2, ch), g_spec[order],
                        jnp.asarray(w2, BF16), jnp.asarray(v2, BF16))
        bias_row = jnp.tile(bias[order], n2).reshape(1, n2 * ch)
        y = _left_mm(jnp.asarray(f3, BF16), bm.reshape(bs, 2 * FFT_N1, n2 * ch), F32,
                     epi=(view(u3[1 + order]), y, bias_row))
    return y.reshape(bs * l, ch)


def _hyena_spectrum(l, consts, w1, b1, w2m, b2, w_out, freq):
    f1_full, _, w2, _, _ = consts
    n2 = 2 * l // FFT_N1
    g_time = _hyena_filter_time(l, w1, b1, w2m, b2, w_out, freq)
    a = _left_mm(jnp.asarray(f1_full, F32), g_time.reshape(2, FFT_N1, n2 * B_CH), F32, hi=True)
    return _filt_mid_stage(a.reshape(2, 2, FFT_N1, n2, B_CH), jnp.asarray(w2, F32))


HY_SLAB = 128
HY_PAD = 8
HY_KG = 8
HY_JG = 8


def _hyena_lat_kernel(zv_ref, z1_ref, z2_ref, cw_ref, cb_ref, g_ref, bias_ref, f1_ref, w2_ref, v2_ref, f3_ref,
                      o_ref, useq, ur, a_re, a_im, cbuf, *, col_mode, seq):
    n1h = FFT_N1 // 2
    n2c = seq // n1h
    pitch = n2c + HY_PAD
    grows = seq // GRID_W
    row_id = lax.broadcasted_iota(jnp.int32, (seq, HY_SLAB), 0)

    def short_conv(z_ref, k):
        z = z_ref[...]
        w = cw_ref[k]
        if col_mode:
            g0, gl = z[:GRID_W], z[seq - GRID_W:]
            cc = row_id[:GRID_W]
            wrap_p = jnp.where(cc == 0, 0.0, pltpu.roll(gl, 1, axis=0))
            wrap_n = jnp.where(cc == GRID_W - 1, 0.0, pltpu.roll(g0, GRID_W - 1, axis=0))
            prev = jnp.concatenate([wrap_p, z[:seq - GRID_W]], axis=0)
            nxt = jnp.concatenate([z[GRID_W:], wrap_n], axis=0)
        else:
            prev = jnp.where(row_id == 0, 0.0, pltpu.roll(z, 1, axis=0))
            nxt = jnp.where(row_id == seq - 1, 0.0, pltpu.roll(z, seq - 1, axis=0))
        return prev * w[0:1] + z * w[1:2] + nxt * w[2:3] + cb_ref[k]

    def seq_start(j):
        return (GRID_W * lax.rem(j, grows) + j // grows) if col_mode else j

    seq_stride = 2 if col_mode else pitch
    seq_ref = ur if col_mode else useq

    def put_seq(val):
        if col_mode:
            ur[...] = val
        else:
            for n1 in range(n1h):
                useq[n1 * pitch:n1 * pitch + n2c, :] = val[n1 * n2c:(n1 + 1) * n2c]

    def conv_out():
        if col_mode:
            return cbuf[...]
        return jnp.concatenate([cbuf[n1 * pitch:n1 * pitch + n2c, :] for n1 in range(n1h)], axis=0)

    y_prev = short_conv(zv_ref, 0)
    x_next = (short_conv(z1_ref, 1), short_conv(z2_ref, 2))
    put_seq(y_prev)
    f1 = f1_ref[...]
    f3 = f3_ref[...]
    for order in range(2):
        def stage1(jg, carry):
            js = [jg * HY_JG + jj for jj in range(HY_JG)]
            xs = [seq_ref[pl.ds(seq_start(j), n1h, stride=seq_stride), :] for j in js]
            outs = [_mm(f1, x) for x in xs]
            for j, a in zip(js, outs):
                a_re[pl.ds(j, FFT_KH, stride=pitch), :] = a[:FFT_KH]
                a_im[pl.ds(j, FFT_KH, stride=pitch), :] = a[FFT_KH:]
            return carry
        lax.fori_loop(0, n2c // HY_JG, stage1, 0)

        def mid(kg, carry):
            k1s = [kg * HY_KG + kk for kk in range(HY_KG)]
            offs = [pl.multiple_of(k1 * pitch, 8) for k1 in k1s]
            xin = [jnp.concatenate([a_re[pl.ds(o, n2c), :], a_im[pl.ds(o, n2c), :]], axis=0) for o in offs]
            xf = [_mm(w2_ref[k1], a) for k1, a in zip(k1s, xin)]
            ys = []
            for k1, x in zip(k1s, xf):
                xr, xi = x[:n2c], x[n2c:]
                gr, gi = g_ref[order, k1, 0], g_ref[order, k1, 1]
                ys.append(jnp.concatenate([xr * gr - xi * gi, xr * gi + xi * gr], axis=0))
            bm = [_mm(v2_ref[k1], y) for k1, y in zip(k1s, ys)]
            for o, b in zip(offs, bm):
                a_re[pl.ds(o, n2c), :] = b[:n2c]
                a_im[pl.ds(o, n2c), :] = b[n2c:]
            return carry
        lax.fori_loop(0, FFT_KH // HY_KG, mid, 0)

        def stage3(jg, carry):
            js = [jg * HY_JG + jj for jj in range(HY_JG)]
            bs = [jnp.concatenate([a_re[pl.ds(j, FFT_KH, stride=pitch), :],
                                   a_im[pl.ds(j, FFT_KH, stride=pitch), :]], axis=0) for j in js]
            outs = [_mm(f3, b) for b in bs]
            for j, y in zip(js, outs):
                cbuf[pl.ds(seq_start(j), n1h, stride=seq_stride), :] = y
            return carry
        lax.fori_loop(0, n2c // HY_JG, stage3, 0)

        y_prev = x_next[order] * (conv_out() + y_prev * bias_ref[order:order + 1, :])
        if order == 0:
            put_seq(y_prev)
    o_ref[...] = y_prev


def _hyena_latent(lay, z, conv_w, conv_b, spec_x, bias, consts_x, col_mode):
    f1, w2, v2, f3 = _half_spectrum_consts(consts_x)
    seq = lay.seq
    assert lay.n_ctx % seq == 0 and FFT_KH % HY_KG == 0
    row0 = lay.n_ctx // seq
    n2c = seq // (FFT_N1 // 2)
    pitch = n2c + HY_PAD
    nslab = B_CH // 2 // HY_SLAB
    s0 = nslab if col_mode else 0
    cps = B_CH // HY_SLAB
    zspec = lambda k: pl.BlockSpec((seq, HY_SLAB), lambda j, b, k=k: (row0 + b, k * cps + s0 + j))
    once = lambda a: pl.BlockSpec(a.shape, lambda j, b: (0,) * a.ndim, pipeline_mode=pl.Buffered(1))
    mats = (jnp.asarray(f1, BF16), jnp.asarray(w2, BF16), jnp.asarray(v2, BF16), jnp.asarray(f3, BF16))
    seq_rows = (FFT_N1 // 2) * pitch
    return pl.pallas_call(
        functools.partial(_hyena_lat_kernel, col_mode=col_mode, seq=seq),
        grid=(nslab, lay.b),
        in_specs=[zspec(0), zspec(1), zspec(2),
                  pl.BlockSpec((3, B_SHORT, HY_SLAB), lambda j, b: (0, 0, s0 + j)),
                  pl.BlockSpec((3, 1, HY_SLAB), lambda j, b: (0, 0, s0 + j)),
                  pl.BlockSpec((2, FFT_KH, 2, n2c, HY_SLAB), lambda j, b: (0, 0, 0, 0, s0 + j),
                               pipeline_mode=pl.Buffered(1)),
                  pl.BlockSpec((2, HY_SLAB), lambda j, b: (0, s0 + j))] + [once(m) for m in mats],
        out_specs=pl.BlockSpec((seq, HY_SLAB), lambda j, b: (b, j)),
        out_shape=jax.ShapeDtypeStruct((lay.b * seq, nslab * HY_SLAB), F32),
        scratch_shapes=[pltpu.VMEM((seq_rows, HY_SLAB), F32), pltpu.VMEM((seq, HY_SLAB), F32),
                        pltpu.VMEM((FFT_KH * pitch, HY_SLAB), F32), pltpu.VMEM((FFT_KH * pitch, HY_SLAB), F32),
                        pltpu.VMEM((seq if col_mode else seq_rows, HY_SLAB), F32)],
        compiler_params=_cparams(("parallel", "parallel")),
        name="hyena_latent_col" if col_mode else "hyena_latent_row",
    )(z, z, z, jnp.transpose(conv_w, (1, 0, 2)), conv_b.reshape(3, 1, B_CH), spec_x, bias, *mats)


def _hyena(lay, z, conv_w, conv_b, spec_c, spec_x, bias, consts_c, consts_x):
    lay_c = _Layout(lay.b, lay.ctx, 0)
    u3 = _dwconv(lay_c, z[:lay.n_ctx], conv_w.reshape(B_SHORT, 3 * B_CH), conv_b.reshape(1, 3 * B_CH), False,
                 B_CH, split=True)
    yc = _hyena_segment(u3, lay.ctx, lay.b, consts_c, spec_c, bias)
    yl = jnp.concatenate([_hyena_latent(lay, z, conv_w, conv_b, spec_x, bias, consts_x, cm) for cm in (False, True)],
                         axis=1)
    return jnp.concatenate([yc, yl], axis=0)


def _head_norm_gate(o, gate, norm_w, heads, width):
    outs = []
    for h in range(heads):
        oh = o[:, h * width:(h + 1) * width]
        y = oh * lax.rsqrt(jnp.mean(oh * oh, axis=-1, keepdims=True) + NORM_EPS) * norm_w
        outs.append(y * _silu(gate[:, h * width:(h + 1) * width]))
    return jnp.concatenate(outs, axis=1)


def _merge_kernel(x_ref, mod_ref, nw_ref, a0_ref, a1_ref, ag_ref, yb_ref, c0_ref, c1_ref, cx_ref, cz_ref,
                  d0_ref, d1_ref, dg_ref, an_ref, cd_ref, cn_ref, dn_ref, wg_ref, wb_ref, wo_ref, o_ref):
    xv = x_ref[...]
    m = mod_ref[...]
    hb = _prenorm(xv, nw_ref[1:2, :], m[3:4, :], m[4:5, :]).astype(BF16)
    ya = _head_norm_gate(a0_ref[...] + a1_ref[...], ag_ref[...], an_ref[...], A_HEADS, A_DV)
    yd = _head_norm_gate(d0_ref[...] + d1_ref[...], dg_ref[...], dn_ref[...], D_HEADS, D_DV)
    yc = (c0_ref[...] + c1_ref[...] + cd_ref[...] * cx_ref[...]) * _silu(cz_ref[...])
    gw = C_INNER // C_GROUPS
    cn = cn_ref[...]
    yc = jnp.concatenate(
        [yc[:, g * gw:(g + 1) * gw]
         * lax.rsqrt(jnp.mean(yc[:, g * gw:(g + 1) * gw] ** 2, axis=-1, keepdims=True) + NORM_EPS)
         * cn[:, g * gw:(g + 1) * gw] for g in range(C_GROUPS)], axis=1)
    acc = jnp.zeros(xv.shape, F32)
    for k, y in enumerate((ya, yb_ref[...], yc, yd)):
        gate = jax.nn.sigmoid(jnp.dot(hb, wg_ref[:, k * D_MODEL:(k + 1) * D_MODEL], preferred_element_type=F32))
        acc = acc + gate * jnp.dot(y.astype(BF16), wb_ref[k], preferred_element_type=F32)
    o_ref[...] = xv + m[5:6, :] * jnp.dot(acc.astype(BF16), wo_ref[...], preferred_element_type=F32)


def _merge(lay, x, mod, nw, a0, a1, ag, yb, c0, c1, cx, cz, d0, d1, dg, an, cd, cn, dn, wg, wb, wo):
    n, d = x.shape
    row = lambda w: pl.BlockSpec((TOK_TILE, w), lambda i: (i, 0))
    once = lambda a: pl.BlockSpec(a.shape, lambda i: (0,) * a.ndim, pipeline_mode=pl.Buffered(1))
    return pl.pallas_call(
        _merge_kernel,
        grid=(n // TOK_TILE,),
        in_specs=[row(d), pl.BlockSpec((None, N_MOD, d), lambda i: (lay.mod_index(i), 0, 0)), _full(nw.shape)]
                 + [row(BRANCH_W)] * 11 + [_full(an.shape), _full(cd.shape), _full(cn.shape), _full(dn.shape),
                                           once(wg), once(wb), once(wo)],
        out_specs=row(d),
        out_shape=jax.ShapeDtypeStruct((n, d), F32),
        compiler_params=_cparams(("parallel",)), name="merge",
    )(x, mod, nw, a0, a1, ag, yb, c0, c1, cx, cz, d0, d1, dg, an, cd, cn, dn, wg, wb, wo)


def _final_norm_kernel(x_ref, w_ref, o_ref):
    xv = x_ref[...]
    o_ref[...] = xv * lax.rsqrt(jnp.mean(xv * xv, axis=-1, keepdims=True) + NORM_EPS) * w_ref[...]


def _final_norm(lay, x, w):
    d = x.shape[1]
    return pl.pallas_call(
        _final_norm_kernel,
        grid=(lay.b * lay.tl,),
        in_specs=[pl.BlockSpec((ROW_TILE, d), lambda i: (lay.nct + i, 0)), _full((1, d))],
        out_specs=pl.BlockSpec((ROW_TILE, d), lambda i: (i, 0)),
        out_shape=jax.ShapeDtypeStruct((lay.b * lay.seq, d), F32),
        compiler_params=_cparams(("parallel",)), name="final_norm",
    )(x, w.reshape(1, d))


def kernel(x, c, ctx, c_ctx, w_ada, b_ada, norm_w, ffn_up, ffn_down, w_in, gdn_conv, gdn_a_log, gdn_dt_bias, gdn_norm, hy_conv_w, hy_conv_b, hy_w1, hy_b1, hy_w2, hy_b2, hy_wout, hy_freq, hy_bias, ssd_conv_w, ssd_conv_b, ssd_a_log, ssd_dt_bias, ssd_d, ssd_norm, gla_gk_w, gla_gk_b, gla_norm, w_branch, w_out, final_norm):
    b, seq, d = x.shape
    ctx_len = ctx.shape[1]
    depth = w_ada.shape[0]
    lay = _Layout(b, ctx_len, seq)
    consts_c = _dft_consts(2 * ctx_len // FFT_N1)
    consts_x = _dft_consts(2 * seq // FFT_N1)

    rp = -(-(1 + b) // 8) * 8
    cond = jnp.concatenate([c_ctx[None, :], c, jnp.zeros((rp - 1 - b, d), F32)], axis=0)
    mods = _ada(cond, w_ada, b_ada).reshape(depth, rp, N_MOD, d)

    xf = jnp.concatenate([ctx.reshape(b * ctx_len, d), x.reshape(b * seq, d)], axis=0)
    for l in range(depth):
        mod = mods[l]
        nw = norm_w[l]
        w_r, w_gate = _rearrange_w_in(w_in[l])
        xf = _ffn(lay, xf, mod, nw, ffn_up[l, 0].astype(BF16), ffn_down[l, 0].astype(BF16), 0)

        qkv, a_gate, zb, c_z, xbc, d_qkv, d_gate, small = _inproj(
            lay, xf, mod, nw, w_r, gdn_conv[l], ssd_conv_w[l], ssd_conv_b[l].reshape(1, C_XBC))
        a0, a1 = _gdn_scan(lay, qkv, small, gdn_a_log[l], gdn_dt_bias[l])
        c0, c1 = _ssd_scan(lay, xbc, small, ssd_a_log[l], ssd_dt_bias[l])
        d0, d1 = _gla_scan(lay, d_qkv, small, gla_gk_w[l], gla_gk_b[l])
        fargs = (hy_w1[l], hy_b1[l], hy_w2[l], hy_b2[l], hy_wout[l], hy_freq[l])
        spec_c = _hyena_spectrum(ctx_len, consts_c, *fargs)
        spec_x = _hyena_spectrum(seq, consts_x, *fargs)
        yb = _hyena(lay, zb, hy_conv_w[l], hy_conv_b[l], spec_c, spec_x, hy_bias[l], consts_c, consts_x)

        xf = _merge(lay, xf, mod, nw, a0, a1, a_gate, yb, c0, c1, xbc, c_z, d0, d1, d_gate,
                    gdn_norm[l].reshape(1, A_DV), jnp.repeat(ssd_d[l], C_HEADDIM).reshape(1, C_INNER),
                    ssd_norm[l].reshape(1, C_INNER), gla_norm[l].reshape(1, D_DV),
                    w_gate, w_branch[l].astype(BF16), w_out[l].astype(BF16))
        xf = _ffn(lay, xf, mod, nw, ffn_up[l, 1].astype(BF16), ffn_down[l, 1].astype(BF16), 2)
    return _final_norm(lay, xf, final_norm).reshape(b, seq, d)
```

```python
import functools
import math

import numpy as np
import jax
import jax.numpy as jnp
from jax import lax
from jax.experimental import pallas as pl
from jax.experimental.pallas import tpu as pltpu

F32 = jnp.float32
BF16 = jnp.bfloat16
HI = lax.Precision.HIGHEST

D_MODEL = 1024
GRID_W = 64
CHUNK = 64
NORM_EPS = 1e-6
N_MOD = 9
D_FF = 2816
SHORT_CONV = 5

A_HEADS, A_DK, A_DV = 4, 128, 128
A_QKV = A_HEADS * (2 * A_DK + A_DV)
B_CH, B_SHORT, B_EMB, B_FFN, B_INNER_MLPS = 512, 3, 33, 64, 2
B_WINDOW_SHIFT, B_DECAY_SHORT_PCT, B_DECAY_LONG_PCT, B_DECAY_TARGET = 0.05, 0.3, 1.5, 1e-2
C_HEADS, C_HEADDIM, C_GROUPS, C_STATE = 8, 64, 2, 64
C_INNER = C_HEADS * C_HEADDIM
C_XBC = C_INNER + 2 * C_GROUPS * C_STATE
D_HEADS, D_DK, D_DV, D_RANK = 4, 64, 128, 16
D_GATE_NORM = 16.0
N_BRANCH, BRANCH_W = 4, 512

IN_SIZES = (A_QKV, A_HEADS * A_DV, 2 * A_HEADS, 2 * A_HEADS, 3 * B_CH, C_INNER, C_XBC, 2 * C_HEADS,
            D_HEADS * (2 * D_DK + D_DV), D_HEADS * D_DV, 2 * D_RANK, N_BRANCH * D_MODEL)
IN_OFFS = tuple(int(v) for v in np.cumsum((0,) + IN_SIZES))

ROW_TILE = 256
TOK_TILE = 512
SMALL_W = 128
VMEM_LIMIT = 56 * 1024 * 1024


def _cparams(sem):
    return pltpu.CompilerParams(dimension_semantics=sem, vmem_limit_bytes=VMEM_LIMIT)


def _mm(a, b):
    return jnp.dot(a.astype(BF16), b.astype(BF16), preferred_element_type=F32)


def _mm_nt(a, b):
    return lax.dot_general(a.astype(BF16), b.astype(BF16), (((1,), (1,)), ((), ())),
                           preferred_element_type=F32)


def _mm_hi(a, b):
    return jnp.dot(a, b, precision=HI, preferred_element_type=F32)


def _bf16_parts(v, parts):
    out, rest = [], v
    for _ in range(parts):
        hi = rest.astype(BF16)
        out.append(hi)
        rest = rest - hi.astype(F32)
    return out


def _mm_sel_l(sel, v, parts=3):
    sb = sel.astype(BF16)
    return sum(jnp.dot(sb, p, preferred_element_type=F32) for p in _bf16_parts(v, parts))


def _mm_sel_r(v, sel, parts=3):
    sb = sel.astype(BF16)
    return sum(jnp.dot(p, sb, preferred_element_type=F32) for p in _bf16_parts(v, parts))


def _silu(v):
    return v * jax.nn.sigmoid(v)


def _softplus(v):
    return jnp.maximum(v, 0.0) + jnp.log1p(jnp.exp(-jnp.abs(v)))


def _log_sigmoid(v):
    return jnp.minimum(v, 0.0) - jnp.log1p(jnp.exp(-jnp.abs(v)))


def _prenorm(xv, gain, shift, scale):
    ms = jnp.mean(xv * xv, axis=-1, keepdims=True)
    return (xv * lax.rsqrt(ms + NORM_EPS) * gain) * (1.0 + scale) + shift


def _full(shape):
    nd = len(shape)
    return pl.BlockSpec(shape, lambda *_: (0,) * nd)


class _Layout:
    def __init__(self, batch, ctx_len, seq):
        assert ctx_len % ROW_TILE == 0 and seq % ROW_TILE == 0
        assert (batch * ctx_len) % TOK_TILE == 0 and seq % TOK_TILE == 0
        self.b, self.ctx, self.seq = batch, ctx_len, seq
        self.n_ctx = batch * ctx_len
        self.n = batch * (ctx_len + seq)
        self.tc, self.tl = ctx_len // ROW_TILE, seq // ROW_TILE
        self.nct = batch * self.tc
        self.tiles = self.n // ROW_TILE

    def mod_index(self, i, tile=TOK_TILE):
        nct = self.n_ctx // tile
        return jnp.where(i < nct, 0, 1 + (i - nct) // (self.seq // tile))

    def seg_first(self, i):
        return jnp.where(i < self.nct, lax.rem(i, self.tc) == 0, lax.rem(i - self.nct, self.tl) == 0)

    def seg_last(self, i):
        return jnp.where(i < self.nct, lax.rem(i, self.tc) == self.tc - 1,
                         lax.rem(i - self.nct, self.tl) == self.tl - 1)

    def fwd_tile(self, b, s):
        return jnp.where(s < self.tc, b * self.tc + s, self.nct + b * self.tl + (s - self.tc))

    def rev_tile(self, b, s):
        return jnp.where(s < self.tc, b * self.tc + (self.tc - 1 - s),
                         self.nct + b * self.tl + (self.tl - 1 - (s - self.tc)))


def _ada_kernel(c_ref, w_ref, b_ref, o_ref):
    o_ref[...] = _mm_hi(_silu(c_ref[...]), w_ref[...]) + b_ref[...]


def _ada(cond, w_ada, b_ada):
    depth, d, nm = w_ada.shape
    rp = cond.shape[0]
    tn = 1152
    return pl.pallas_call(
        _ada_kernel,
        grid=(depth, nm // tn),
        in_specs=[_full((rp, d)),
                  pl.BlockSpec((None, d, tn), lambda l, j: (l, 0, j)),
                  pl.BlockSpec((None, 1, tn), lambda l, j: (l, 0, j))],
        out_specs=pl.BlockSpec((None, rp, tn), lambda l, j: (l, 0, j)),
        out_shape=jax.ShapeDtypeStruct((depth, rp, nm), F32),
        compiler_params=_cparams(("parallel", "parallel")), name="ada",
    )(cond, w_ada, b_ada.reshape(depth, 1, nm))


FF_CHUNK = 256


def _ffn_kernel(x_ref, mod_ref, nw_ref, wup_ref, wdn_ref, o_ref, *, sub):
    xv = x_ref[...]
    m = mod_ref[...]
    h = _prenorm(xv, nw_ref[sub:sub + 1, :], m[3 * sub:3 * sub + 1, :], m[3 * sub + 1:3 * sub + 2, :])
    hb = h.astype(BF16)
    acc = jnp.zeros(xv.shape, F32)
    for c in range(D_FF // FF_CHUNK):
        lo = c * FF_CHUNK
        a = jnp.dot(hb, wup_ref[:, lo:lo + FF_CHUNK], preferred_element_type=F32)
        g = jnp.dot(hb, wup_ref[:, D_FF + lo:D_FF + lo + FF_CHUNK], preferred_element_type=F32)
        acc = acc + jnp.dot((_silu(a) * g).astype(BF16), wdn_ref[lo:lo + FF_CHUNK, :],
                            preferred_element_type=F32)
    o_ref[...] = xv + 0.5 * m[3 * sub + 2:3 * sub + 3, :] * acc


def _ffn(lay, x, mod, nw, wup, wdn, sub):
    n, d = x.shape
    return pl.pallas_call(
        functools.partial(_ffn_kernel, sub=sub),
        grid=(n // TOK_TILE,),
        in_specs=[pl.BlockSpec((TOK_TILE, d), lambda i: (i, 0)),
                  pl.BlockSpec((None, N_MOD, d), lambda i: (lay.mod_index(i), 0, 0)),
                  _full(nw.shape),
                  pl.BlockSpec(wup.shape, lambda i: (0, 0), pipeline_mode=pl.Buffered(1)),
                  pl.BlockSpec(wdn.shape, lambda i: (0, 0), pipeline_mode=pl.Buffered(1))],
        out_specs=pl.BlockSpec((TOK_TILE, d), lambda i: (i, 0)),
        out_shape=jax.ShapeDtypeStruct((n, d), F32),
        compiler_params=_cparams(("parallel",)), name="ffn",
    )(x, mod, nw, wup, wdn)


PROJ_W = (A_QKV, A_HEADS * A_DV, 3 * B_CH, C_INNER, C_XBC, D_HEADS * (2 * D_DK + D_DV), D_HEADS * D_DV, SMALL_W)
PROJ_O = tuple(int(v) for v in np.cumsum((0,) + PROJ_W))


def _rearrange_w_in(w_in):
    o = IN_OFFS
    small = jnp.concatenate([w_in[:, o[2]:o[4]], w_in[:, o[7]:o[8]], w_in[:, o[10]:o[11]],
                             jnp.zeros((w_in.shape[0], SMALL_W - 64), w_in.dtype)], axis=1)
    w_r = jnp.concatenate([w_in[:, o[0]:o[1]], w_in[:, o[1]:o[2]], w_in[:, o[4]:o[5]], w_in[:, o[5]:o[6]],
                           w_in[:, o[6]:o[7]], w_in[:, o[8]:o[9]], w_in[:, o[9]:o[10]], small], axis=1)
    return w_r.astype(BF16), w_in[:, o[11]:o[12]].astype(BF16)


PROJ_HALO = 16
PROJ_CONV = (0, 4)


def _inproj_kernel(xp_ref, x_ref, xn_ref, mod_ref, nw_ref, w_ref, cwa_ref, cwc_ref, cbc_ref, *rest, lay):
    o_refs, zbuf = rest[:-1], rest[-1]
    i = pl.program_id(0)
    m = mod_ref[...]
    norm = lambda r: _prenorm(r[...], nw_ref[1:2, :], m[3:4, :], m[4:5, :]).astype(BF16)
    hb = norm(x_ref)
    hb_all = jnp.concatenate([norm(xp_ref), hb, norm(xn_ref)], axis=0)
    keep_p = jnp.where(lay.seg_first(i), 0.0, 1.0)
    keep_n = jnp.where(lay.seg_last(i), 0.0, 1.0)
    convs = {PROJ_CONV[0]: (cwa_ref, None), PROJ_CONV[1]: (cwc_ref, cbc_ref)}
    for k, o_ref in enumerate(o_refs):
        wk = w_ref[:, PROJ_O[k]:PROJ_O[k + 1]]
        if k not in convs:
            o_ref[...] = jnp.dot(hb, wk, preferred_element_type=F32)
            continue
        cw_ref, cb_ref = convs[k]
        width = PROJ_W[k]
        z = jnp.dot(hb_all, wk, preferred_element_type=F32)
        zbuf[0:PROJ_HALO, 0:width] = z[:PROJ_HALO] * keep_p
        zbuf[PROJ_HALO:PROJ_HALO + ROW_TILE, 0:width] = z[PROJ_HALO:PROJ_HALO + ROW_TILE]
        zbuf[PROJ_HALO + ROW_TILE:, 0:width] = z[PROJ_HALO + ROW_TILE:] * keep_n
        taps = cw_ref.shape[0]
        left = (taps - 1) // 2
        acc = None
        for t in range(taps):
            off = t - left
            term = zbuf[PROJ_HALO + off:PROJ_HALO + off + ROW_TILE, 0:width] * cw_ref[t:t + 1, :]
            acc = term if acc is None else acc + term
        if cb_ref is not None:
            acc = acc + cb_ref[...]
        o_ref[...] = _silu(acc)


def _inproj(lay, x, mod, nw, w_r, conv_a, conv_c, conv_c_bias):
    n, d = x.shape
    rh = ROW_TILE // PROJ_HALO
    lasth = n // PROJ_HALO - 1
    return pl.pallas_call(
        functools.partial(_inproj_kernel, lay=lay),
        grid=(n // ROW_TILE,),
        in_specs=[pl.BlockSpec((PROJ_HALO, d), lambda i: (jnp.maximum(i * rh - 1, 0), 0)),
                  pl.BlockSpec((ROW_TILE, d), lambda i: (i, 0)),
                  pl.BlockSpec((PROJ_HALO, d), lambda i: (jnp.minimum((i + 1) * rh, lasth), 0)),
                  pl.BlockSpec((None, N_MOD, d), lambda i: (lay.mod_index(i, ROW_TILE), 0, 0)),
                  _full(nw.shape),
                  pl.BlockSpec(w_r.shape, lambda i: (0, 0), pipeline_mode=pl.Buffered(1)),
                  _full(conv_a.shape), _full(conv_c.shape), _full(conv_c_bias.shape)],
        out_specs=[pl.BlockSpec((ROW_TILE, w), lambda i: (i, 0)) for w in PROJ_W],
        out_shape=[jax.ShapeDtypeStruct((n, w), F32) for w in PROJ_W],
        scratch_shapes=[pltpu.VMEM((ROW_TILE + 2 * PROJ_HALO, max(PROJ_W[k] for k in PROJ_CONV)), F32)],
        compiler_params=_cparams(("parallel",)), name="inproj",
    )(x, x, x, mod, nw, w_r, conv_a, conv_c, conv_c_bias)


def _tri_masks(rev):
    ii = lax.broadcasted_iota(jnp.int32, (CHUNK, CHUNK), 0)
    jj = lax.broadcasted_iota(jnp.int32, (CHUNK, CHUNK), 1)
    return (ii <= jj, ii < jj) if rev else (ii >= jj, ii > jj)


def _masked_decay(col, row, incl):
    return jnp.where(incl, jnp.exp(jnp.where(incl, col - row, 0.0)), 0.0)


def _scan_call(kernel, lay, ins_tiled, ins_full, out_w, scratch):
    steps = lay.tc + lay.tl
    specs = []
    args = []
    for walk in (lay.fwd_tile, lay.rev_tile):
        for a in ins_tiled:
            specs.append(pl.BlockSpec((ROW_TILE, a.shape[1]), lambda b, s, walk=walk: (walk(b, s), 0)))
            args.append(a)
    for a in ins_full:
        specs.append(_full(a.shape))
        args.append(a)
    return pl.pallas_call(
        kernel,
        grid=(lay.b, steps),
        in_specs=specs,
        out_specs=[pl.BlockSpec((ROW_TILE, out_w), lambda b, s: (lay.fwd_tile(b, s), 0)),
                   pl.BlockSpec((ROW_TILE, out_w), lambda b, s: (lay.rev_tile(b, s), 0))],
        out_shape=[jax.ShapeDtypeStruct((lay.n, out_w), F32)] * 2,
        scratch_shapes=scratch,
        compiler_params=_cparams(("arbitrary", "arbitrary")), name=kernel.__name__.strip("_"),
    )(*args)


def _ssd_kernel(xf_ref, sf_ref, xr_ref, sr_ref, alog_ref, dtb_ref, alog_t_ref, dtb_t_ref, exp_ref,
                of_ref, or_ref, st_ref):
    @pl.when(pl.program_id(1) == 0)
    def _():
        st_ref[...] = jnp.zeros(st_ref.shape, F32)

    hpg = C_HEADS // C_GROUPS
    gw = hpg * C_HEADDIM
    nch = ROW_TILE // CHUNK
    refs = ((xf_ref, sf_ref, of_ref), (xr_ref, sr_ref, or_ref))
    units, pre = [], {}
    for d, (x_ref, s_ref, _) in enumerate(refs):
        incl, _ = _tri_masks(d == 1)
        m_col = incl.astype(F32)
        m_row = m_col.T
        sm = s_ref[...]
        sm_t = sm.T
        dt_all = _softplus(sm[:, 16:32] + dtb_ref[...])
        a_all = -jnp.exp(alog_ref[...]) * dt_all
        a_t_all = -jnp.exp(alog_t_ref[...]) * _softplus(sm_t[16:32, :] + dtb_t_ref[...])
        bm_t = x_ref[:, C_INNER:C_INNER + C_GROUPS * C_STATE].T
        for c in range(nch):
            rows = slice(c * CHUNK, (c + 1) * CHUNK)
            units.append((d, c))
            pre[d, c] = dict(incl=incl, m_col=m_col, m_row=m_row, a_ch=a_all[rows], a_t=a_t_all[:, rows],
                             dt=dt_all[rows], bm_t=bm_t[:, rows], e=exp_ref[d], xs=x_ref[rows, 0:C_INNER],
                             bm=x_ref[rows, C_INNER:C_INNER + C_GROUPS * C_STATE],
                             cm=x_ref[rows, C_INNER + C_GROUPS * C_STATE:C_XBC])
    for u in units:
        w = pre[u]
        w["ac"] = _mm_sel_l(w["m_col"], w["a_ch"])
        w["ac_t"] = _mm_sel_r(w["a_t"], w["m_row"])
        w["alast"] = jnp.sum(w["a_ch"], axis=0, keepdims=True)
    for u in units:
        w = pre[u]
        w["xdt"] = w["xs"] * _mm_sel_r(w["dt"], w["e"])
        w["eac_e"] = _mm_sel_r(jnp.exp(w["ac"]), w["e"])
        w["xd"] = w["xdt"] * _mm_sel_r(jnp.exp(w["alast"] - w["ac"]), w["e"])
        w["dle"] = _mm_sel_r(jnp.broadcast_to(jnp.exp(w["alast"]), (8, 2 * C_HEADS)), w["e"])[0:1]
    grp = lambda a, g: a[:, g * C_STATE:(g + 1) * C_STATE]
    cb = {(u, g): _mm_nt(grp(pre[u]["cm"], g), grp(pre[u]["bm"], g)) for u in units for g in range(C_GROUPS)}
    upd = {(u, g): _mm(pre[u]["bm_t"][g * C_STATE:(g + 1) * C_STATE], pre[u]["xd"][:, g * gw:(g + 1) * gw])
           for u in units for g in range(C_GROUPS)}
    y_diag = {}
    for u in units:
        w = pre[u]
        for h in range(C_HEADS):
            col = u[0] * C_HEADS + h
            seg = _masked_decay(w["ac"][:, col:col + 1], w["ac_t"][col:col + 1, :], w["incl"])
            y_diag[u, h] = _mm(cb[u, h // hpg] * seg, w["xdt"][:, h * C_HEADDIM:(h + 1) * C_HEADDIM])
    st = {(d, g): st_ref[d, g] for d in range(2) for g in range(C_GROUPS)}
    entry = {}
    for p in range(nch):
        for d in range(2):
            u = (d, (nch - 1 - p) if d == 1 else p)
            for g in range(C_GROUPS):
                entry[u, g] = st[d, g]
                st[d, g] = pre[u]["dle"][:, g * gw:(g + 1) * gw] * st[d, g] + upd[u, g]
    for d in range(2):
        for g in range(C_GROUPS):
            st_ref[d, g] = st[d, g]
    for u in units:
        y_off = [_mm(grp(pre[u]["cm"], g), entry[u, g]) for g in range(C_GROUPS)]
        refs[u[0]][2][u[1] * CHUNK:(u[1] + 1) * CHUNK, :] = (
            jnp.concatenate([y_diag[u, h] for h in range(C_HEADS)], axis=1)
            + pre[u]["eac_e"] * jnp.concatenate(y_off, axis=1))


def _ssd_scan(lay, xbc, small, a_log, dt_bias):
    expand = np.zeros((2, 2 * C_HEADS, C_INNER), np.float32)
    for d in range(2):
        for h in range(C_HEADS):
            expand[d, d * C_HEADS + h, h * C_HEADDIM:(h + 1) * C_HEADDIM] = 1.0
    al = a_log.reshape(1, 2 * C_HEADS)
    db = dt_bias.reshape(1, 2 * C_HEADS)
    return _scan_call(_ssd_kernel, lay, (xbc, small), (al, db, al.T, db.T, jnp.asarray(expand)), C_INNER,
                      [pltpu.VMEM((2, C_GROUPS, C_STATE, C_INNER // C_GROUPS), F32)])


def _gla_kernel(xf_ref, sf_ref, xr_ref, sr_ref, gkw_ref, gkb_ref, of_ref, or_ref, st_ref):
    @pl.when(pl.program_id(1) == 0)
    def _():
        st_ref[...] = jnp.zeros(st_ref.shape, F32)

    nk = D_HEADS * D_DK
    nch = ROW_TILE // CHUNK
    refs = ((xf_ref, sf_ref, of_ref), (xr_ref, sr_ref, or_ref))
    units, pre = [], {}
    for d, (x_ref, s_ref, _) in enumerate(refs):
        incl, _ = _tri_masks(d == 1)
        m_col = incl.astype(F32)
        lr = s_ref[:, 32 + D_RANK * d:32 + D_RANK * (d + 1)]
        g_all = _log_sigmoid(_mm_hi(lr, gkw_ref[d]) + gkb_ref[d:d + 1, :]) / D_GATE_NORM
        v_t = x_ref[:, 2 * nk:].T
        for c in range(nch):
            rows = slice(c * CHUNK, (c + 1) * CHUNK)
            g_ch = g_all[rows]
            gc = _mm_sel_l(m_col, g_ch)
            glast = jnp.sum(g_ch, axis=0, keepdims=True)
            q = x_ref[rows, 0:nk] * (D_DK ** -0.5)
            k = x_ref[rows, nk:2 * nk]
            v = x_ref[rows, 2 * nk:]
            kd = k * jnp.exp(glast - gc)
            qd = q * jnp.exp(gc)
            qr = q * jnp.exp(gc - glast)
            dlast = jnp.exp(glast)
            for h in range(D_HEADS):
                ks = slice(h * D_DK, (h + 1) * D_DK)
                vs = slice(h * D_DV, (h + 1) * D_DV)
                units.append((d, c, h))
                pre[d, c, h] = dict(incl=incl, qr=qr[:, ks], kd=kd[:, ks], qd=qd[:, ks], v=v[:, vs],
                                    v_t=v_t[vs, rows], dl=dlast[:, ks])
    aqk = [jnp.where(pre[u]["incl"], _mm_nt(pre[u]["qr"], pre[u]["kd"]), 0.0) for u in units]
    upd = {u: _mm(pre[u]["v_t"], pre[u]["kd"]) for u in units}
    intra = {u: _mm(a, pre[u]["v"]) for u, a in zip(units, aqk)}
    heads = [(d, h) for d in range(2) for h in range(D_HEADS)]
    st = {dh: st_ref[dh[0], dh[1]] for dh in heads}
    entry = {}
    for p in range(nch):
        for d, h in heads:
            u = (d, (nch - 1 - p) if d == 1 else p, h)
            entry[u] = st[d, h]
            st[d, h] = st[d, h] * pre[u]["dl"] + upd[u]
    for dh in heads:
        st_ref[dh[0], dh[1]] = st[dh]
    inter = {u: _mm_nt(pre[u]["qd"], entry[u]) for u in units}
    for d, (_, _, o_ref) in enumerate(refs):
        for c in range(nch):
            o_ref[c * CHUNK:(c + 1) * CHUNK, :] = jnp.concatenate(
                [intra[d, c, h] + inter[d, c, h] for h in range(D_HEADS)], axis=1)


def _gla_scan(lay, qkv, small, gk_w, gk_b):
    return _scan_call(_gla_kernel, lay, (qkv, small), (gk_w, gk_b), D_HEADS * D_DV,
                      [pltpu.VMEM((2, D_HEADS, D_DV, D_DK), F32)])


TRI_BLOCK = 16
GDN_GROUP = 32


def _unit_lower_inverse(nms, eye, blk):
    dg = [jnp.where(blk, nm, 0.0) for nm in nms]
    off = [nm - d for nm, d in zip(nms, dg)]
    t0 = [eye - d for d in dg]
    p = [_mm(d, d) for d in dg]
    for it in range(3):
        t0 = [t + _mm(t, q) for t, q in zip(t0, p)]
        if it < 2:
            p = [_mm(q, q) for q in p]
    m = [_mm(t, o) for t, o in zip(t0, off)]
    m2 = [_mm(a, a) for a in m]
    r = [eye - a for a in m]
    r = [a + _mm(a, b) for a, b in zip(r, m2)]
    return [_mm(a, t) for a, t in zip(r, t0)]


def _gdn_kernel(xf_ref, sf_ref, xr_ref, sr_ref, alog_ref, dtb_ref, alog_t_ref, dtb_t_ref,
                of_ref, or_ref, st_ref):
    @pl.when(pl.program_id(1) == 0)
    def _():
        st_ref[...] = jnp.zeros(st_ref.shape, F32)

    nk = A_HEADS * A_DK
    nch = ROW_TILE // CHUNK
    ii = lax.broadcasted_iota(jnp.int32, (CHUNK, CHUNK), 0)
    jj = lax.broadcasted_iota(jnp.int32, (CHUNK, CHUNK), 1)
    eye = (ii == jj).astype(F32)
    blk = (ii // TRI_BLOCK) == (jj // TRI_BLOCK)
    refs = ((xf_ref, sf_ref, of_ref), (xr_ref, sr_ref, or_ref))

    units, pre = [], {}
    for d, (x_ref, s_ref, _) in enumerate(refs):
        incl, strict = _tri_masks(d == 1)
        m_col = incl.astype(F32)
        m_row = m_col.T
        sm = s_ref[...]
        sm_t = sm.T
        beta_all = jax.nn.sigmoid(sm[:, 0:8])
        g_all = -jnp.exp(alog_ref[...]) * _softplus(sm[:, 8:16] + dtb_ref[...])
        g_t_all = -jnp.exp(alog_t_ref[...]) * _softplus(sm_t[8:16, :] + dtb_t_ref[...])
        qn, kn = [], []
        for h in range(A_HEADS):
            qh = x_ref[:, h * A_DK:(h + 1) * A_DK]
            kh = x_ref[:, nk + h * A_DK:nk + (h + 1) * A_DK]
            qn.append(qh * lax.rsqrt(jnp.sum(qh * qh, axis=-1, keepdims=True) + NORM_EPS) * (A_DK ** -0.5))
            kn.append(kh * lax.rsqrt(jnp.sum(kh * kh, axis=-1, keepdims=True) + NORM_EPS))
        kn_t = [kh.T for kh in kn]
        for c in range(nch):
            rows = slice(c * CHUNK, (c + 1) * CHUNK)
            g_ch = g_all[rows]
            gc = _mm_sel_l(m_col, g_ch)
            gc_t = _mm_sel_r(g_t_all[:, rows], m_row)
            glast = jnp.sum(g_ch, axis=0, keepdims=True)
            for h in range(A_HEADS):
                col = d * A_HEADS + h
                gcc, gcr, gl = gc[:, col:col + 1], gc_t[col:col + 1, :], glast[:, col:col + 1]
                bh = beta_all[rows, col:col + 1]
                kh = kn[h][rows]
                units.append((d, c, h))
                pre[d, c, h] = dict(
                    dmask=_masked_decay(gcc, gcr, incl), strict=strict, egc=jnp.exp(gcc), kh=kh, kb=kh * bh,
                    qh=qn[h][rows], vb=x_ref[rows, 2 * nk + h * A_DV:2 * nk + (h + 1) * A_DV] * bh,
                    kg_t=kn_t[h][:, rows] * jnp.exp(gl - gcr), dl=jnp.exp(gl))
    for g0 in range(0, len(units), GDN_GROUP):
        us = units[g0:g0 + GDN_GROUP]
        kk = [_mm_nt(pre[u]["kb"], pre[u]["kh"]) for u in us]
        qk = [_mm_nt(pre[u]["qh"], pre[u]["kh"]) for u in us]
        tinv = _unit_lower_inverse(
            [jnp.where(pre[u]["strict"], a * pre[u]["dmask"], 0.0) for u, a in zip(us, kk)], eye, blk)
        uw = [_mm(t, jnp.concatenate([pre[u]["vb"], pre[u]["kb"] * pre[u]["egc"]], axis=1))
              for u, t in zip(us, tinv)]
        for u, a, b in zip(us, uw, qk):
            pre[u]["u"] = a[:, :A_DV]
            pre[u]["wq"] = jnp.concatenate([a[:, A_DV:], pre[u]["qh"] * pre[u]["egc"]], axis=0)
            pre[u]["aqk"] = b * pre[u]["dmask"]

    heads = [(d, h) for d in range(2) for h in range(A_HEADS)]
    st = {dh: st_ref[dh[0], dh[1]] for dh in heads}
    for p in range(nch):
        us = [(d, (nch - 1 - p) if d == 1 else p, h) for d, h in heads]
        ws = [_mm(pre[u]["wq"], st[u[0], u[2]]) for u in us]
        v_new = [pre[u]["u"] - w[:CHUNK] for u, w in zip(us, ws)]
        outs = [w[CHUNK:] + _mm(pre[u]["aqk"], v) for u, w, v in zip(us, ws, v_new)]
        for u, v in zip(us, v_new):
            st[u[0], u[2]] = pre[u]["dl"] * st[u[0], u[2]] + _mm(pre[u]["kg_t"], v)
        for d, (_, _, o_ref) in enumerate(refs):
            c = us[d * A_HEADS][1]
            o_ref[c * CHUNK:(c + 1) * CHUNK, :] = jnp.concatenate(outs[d * A_HEADS:(d + 1) * A_HEADS], axis=1)
    for dh in heads:
        st_ref[dh[0], dh[1]] = st[dh]


def _gdn_scan(lay, qkv, small, a_log, dt_bias):
    al = a_log.reshape(1, 2 * A_HEADS)
    db = dt_bias.reshape(1, 2 * A_HEADS)
    return _scan_call(_gdn_kernel, lay, (qkv, small), (al, db, al.T, db.T), A_HEADS * A_DV,
                      [pltpu.VMEM((2, A_HEADS, A_DK, A_DV), F32)])


FFT_N1 = 64
FFT_KB = 8


def _dft_consts(n2):
    n1 = FFT_N1
    m = n1 * n2
    k1 = np.arange(n1, dtype=np.float64)
    ang1 = 2.0 * np.pi * np.outer(k1, k1) / n1
    f1_full = np.concatenate([np.cos(ang1), -np.sin(ang1)], axis=0)
    f1 = f1_full[:, :n1 // 2]
    f3 = np.concatenate([np.cos(ang1[:, :n1 // 2]).T, -np.sin(ang1[:, :n1 // 2]).T], axis=1) / m
    j = np.arange(n2, dtype=np.float64)
    theta = 2.0 * np.pi * (np.outer(j, j)[None] / n2 + (k1[:, None, None] * j[None, None, :]) / m)
    wr, wi = np.cos(theta), -np.sin(theta)
    w2 = np.concatenate([np.concatenate([wr, -wi], axis=2), np.concatenate([wi, wr], axis=2)], axis=1)
    phi = 2.0 * np.pi * (np.outer(j, j)[None] / n2 + (k1[:, None, None] * j[None, :, None]) / m)
    vr, vi = np.cos(phi), np.sin(phi)
    v2 = np.concatenate([np.concatenate([vr, -vi], axis=2), np.concatenate([vi, vr], axis=2)], axis=1)
    return f1_full, f1, w2, v2, f3


FFT_KV = FFT_N1 // 2 + 1
FFT_KH = 40


def _half_spectrum_consts(consts):
    _, f1, w2, v2, f3 = consts
    n1 = FFT_N1
    f1h = np.zeros((2 * FFT_KH, n1 // 2))
    f1h[:FFT_KV] = f1[:FFT_KV]
    f1h[FFT_KH:FFT_KH + FFT_KV] = f1[n1:n1 + FFT_KV]
    w2h = np.zeros((FFT_KH,) + w2.shape[1:])
    w2h[:FFT_KV] = w2[:FFT_KV]
    v2h = np.zeros((FFT_KH,) + v2.shape[1:])
    v2h[:FFT_KV] = v2[:FFT_KV]
    weight = np.full((FFT_KV,), 2.0)
    weight[0] = weight[-1] = 1.0
    f3h = np.zeros((n1 // 2, 2 * FFT_KH))
    f3h[:, :FFT_KV] = f3[:, :FFT_KV] * weight
    f3h[:, FFT_KH:FFT_KH + FFT_KV] = f3[:, n1:n1 + FFT_KV] * weight
    return f1h, w2h, v2h, f3h


def _left_mm_kernel(w_ref, x_ref, o_ref):
    o_ref[...] = _mm_hi(w_ref[...], x_ref[...])


def _left_mm(wm, x):
    bs, kdim, ncols = x.shape
    mdim = wm.shape[0]
    tn = min(ncols, 4096)
    xspec = lambda rows: pl.BlockSpec((None, rows, tn), lambda b, j: (b, 0, j))
    return pl.pallas_call(
        _left_mm_kernel, grid=(bs, ncols // tn), in_specs=[_full(wm.shape), xspec(kdim)], out_specs=xspec(mdim),
        out_shape=jax.ShapeDtypeStruct((bs, mdim, ncols), F32),
        compiler_params=_cparams(("parallel", "parallel")), name="filter_dft_outer",
    )(wm, x)


def _filt_mid_kernel(a_ref, w_ref, o_ref, *, n2):
    for kk in range(FFT_KB):
        a = jnp.concatenate([a_ref[0, kk], a_ref[1, kk]], axis=0)
        xf = _mm_hi(w_ref[kk], a)
        o_ref[kk, 0] = xf[:n2]
        o_ref[kk, 1] = xf[n2:]


def _filt_mid_stage(a, w2):
    bs, _, n1, n2, ch = a.shape
    return pl.pallas_call(
        functools.partial(_filt_mid_kernel, n2=n2),
        grid=(bs, n1 // FFT_KB),
        in_specs=[pl.BlockSpec((None, 2, FFT_KB, n2, ch), lambda b, j: (b, 0, j, 0, 0)),
                  pl.BlockSpec((FFT_KB, 2 * n2, 2 * n2), lambda b, j: (j, 0, 0))],
        out_specs=pl.BlockSpec((None, FFT_KB, 2, n2, ch), lambda b, j: (b, j, 0, 0, 0)),
        out_shape=jax.ShapeDtypeStruct((bs, n1, 2, n2, ch), F32),
        compiler_params=_cparams(("parallel", "parallel")), name="filter_dft_mid",
    )(a, w2)


FILT_ROWS = 512


def _filter_kernel(z_ref, win_ref, sel_ref, w1_ref, b1_ref, w2_ref, b2_ref, wo_ref, fr_ref, o_ref):
    fr = fr_ref[...]
    h = jnp.sin(fr * (_mm_hi(z_ref[...], w1_ref[...]) + b1_ref[...]))
    for i in range(B_INNER_MLPS):
        h = jnp.sin(fr * (_mm_hi(h, w2_ref[i]) + b2_ref[i]))
    ho = _mm_hi(h, wo_ref[...])
    fwd = sel_ref[...] > 0.5
    win = win_ref[...]
    for o in range(2):
        base = o * 2 * B_CH
        o_ref[o] = jnp.where(fwd, ho[:, base:base + B_CH], ho[:, base + B_CH:base + 2 * B_CH]) * win


def _hyena_filter_time(l, w1, b1, w2, b2, w_out, freq):
    bands = (B_EMB - 1) // 2
    circ = jnp.arange(2 * l)
    pos = jnp.where(circ < l, circ, 2 * l - circ)
    pos = jnp.where(circ == l, 0, pos)
    t = (jnp.linspace(0.0, 1.0, l, dtype=F32)[pos])[:, None]
    ang = 2.0 * math.pi * pos.astype(F32)[:, None] / l
    fr = jnp.linspace(1e-4, bands - 1, bands, dtype=F32)[None, :]
    z = jnp.concatenate([t, jnp.cos(fr * ang), -jnp.sin(fr * ang)], axis=-1)
    z = jnp.pad(z, ((0, 0), (0, 128 - B_EMB)))
    max_decay = math.log(B_DECAY_TARGET) / B_DECAY_SHORT_PCT
    min_decay = math.log(B_DECAY_TARGET) / B_DECAY_LONG_PCT
    deltas = jnp.abs(jnp.linspace(min_decay, max_decay, B_CH, dtype=F32))
    win = (jnp.exp(-t * deltas) + B_WINDOW_SHIFT) * (circ != l).astype(F32)[:, None]
    sel = (circ < l).astype(F32)[:, None]
    w1p = jnp.pad(w1, ((0, 128 - B_EMB), (0, 0)))
    rows = min(FILT_ROWS, 2 * l)
    return pl.pallas_call(
        _filter_kernel,
        grid=(2 * l // rows,),
        in_specs=[pl.BlockSpec((rows, 128), lambda i: (i, 0)),
                  pl.BlockSpec((rows, B_CH), lambda i: (i, 0)),
                  pl.BlockSpec((rows, 1), lambda i: (i, 0)),
                  _full(w1p.shape), _full((1, B_FFN)), _full(w2.shape), _full((B_INNER_MLPS, 1, B_FFN)),
                  _full(w_out.shape), _full((1, B_FFN))],
        out_specs=pl.BlockSpec((2, rows, B_CH), lambda i: (0, i, 0)),
        out_shape=jax.ShapeDtypeStruct((2, 2 * l, B_CH), F32),
        compiler_params=_cparams(("parallel",)), name="hyena_filter",
    )(z, win, sel, w1p, b1.reshape(1, B_FFN), w2, b2.reshape(B_INNER_MLPS, 1, B_FFN), w_out,
      freq.reshape(1, B_FFN))


def _hyena_spectrum(l, consts, w1, b1, w2m, b2, w_out, freq):
    f1_full, _, w2, _, _ = consts
    n2 = 2 * l // FFT_N1
    g_time = _hyena_filter_time(l, w1, b1, w2m, b2, w_out, freq)
    f1_rows = np.concatenate([f1_full[:FFT_KH], f1_full[FFT_N1:FFT_N1 + FFT_KH]], axis=0)
    a = _left_mm(jnp.asarray(f1_rows, F32), g_time.reshape(2, FFT_N1, n2 * B_CH))
    return _filt_mid_stage(a.reshape(2, 2, FFT_KH, n2, B_CH), jnp.asarray(w2[:FFT_KH], F32))


HY_SLAB = 128


def _hy_pitch(n2c):
    return n2c + 8 if (n2c // 8) % 2 == 0 else n2c + 16


HY_KG = 8
HY_JG = 8


def _hyena_lat_kernel(zv_ref, z1_ref, z2_ref, cw_ref, cb_ref, g_ref, bias_ref, f1_ref, w2_ref, v2_ref, f3_ref,
                      o_ref, useq, ur, a_re, a_im, cbuf, *, col_mode, seq):
    n1h = FFT_N1 // 2
    n2c = seq // n1h
    pitch = _hy_pitch(n2c)
    grows = seq // GRID_W
    row_id = lax.broadcasted_iota(jnp.int32, (seq, HY_SLAB), 0)

    def short_conv(z_ref, k):
        z = z_ref[...]
        w = cw_ref[k]
        if col_mode:
            g0, gl = z[:GRID_W], z[seq - GRID_W:]
            cc = row_id[:GRID_W]
            wrap_p = jnp.where(cc == 0, 0.0, pltpu.roll(gl, 1, axis=0))
            wrap_n = jnp.where(cc == GRID_W - 1, 0.0, pltpu.roll(g0, GRID_W - 1, axis=0))
            prev = jnp.concatenate([wrap_p, z[:seq - GRID_W]], axis=0)
            nxt = jnp.concatenate([z[GRID_W:], wrap_n], axis=0)
        else:
            prev = jnp.where(row_id == 0, 0.0, pltpu.roll(z, 1, axis=0))
            nxt = jnp.where(row_id == seq - 1, 0.0, pltpu.roll(z, seq - 1, axis=0))
        return prev * w[0:1] + z * w[1:2] + nxt * w[2:3] + cb_ref[k]

    def seq_start(j):
        return (GRID_W * lax.rem(j, grows) + j // grows) if col_mode else j

    seq_stride = 2 if col_mode else pitch
    seq_ref = ur if col_mode else useq

    def put_seq(val):
        if col_mode:
            ur[...] = val
        else:
            for n1 in range(n1h):
                useq[n1 * pitch:n1 * pitch + n2c, :] = val[n1 * n2c:(n1 + 1) * n2c]

    def conv_out():
        if col_mode:
            return cbuf[...]
        return jnp.concatenate([cbuf[n1 * pitch:n1 * pitch + n2c, :] for n1 in range(n1h)], axis=0)

    y_prev = short_conv(zv_ref, 0)
    x_next = (short_conv(z1_ref, 1), short_conv(z2_ref, 2))
    put_seq(y_prev)
    f1 = f1_ref[...]
    f3 = f3_ref[...]
    for order in range(2):
        def stage1(jg, carry):
            js = [jg * HY_JG + jj for jj in range(HY_JG)]
            xs = [seq_ref[pl.ds(seq_start(j), n1h, stride=seq_stride), :] for j in js]
            outs = [_mm(f1, x) for x in xs]
            for j, a in zip(js, outs):
                a_re[pl.ds(j, FFT_KH, stride=pitch), :] = a[:FFT_KH]
                a_im[pl.ds(j, FFT_KH, stride=pitch), :] = a[FFT_KH:]
            return carry
        lax.fori_loop(0, n2c // HY_JG, stage1, 0)

        def mid(kg, carry):
            k1s = [kg * HY_KG + kk for kk in range(HY_KG)]
            offs = [pl.multiple_of(k1 * pitch, 8) for k1 in k1s]
            xin = [jnp.concatenate([a_re[pl.ds(o, n2c), :], a_im[pl.ds(o, n2c), :]], axis=0) for o in offs]
            xf = [_mm(w2_ref[k1], a) for k1, a in zip(k1s, xin)]
            ys = []
            for k1, x in zip(k1s, xf):
                xr, xi = x[:n2c], x[n2c:]
                gr, gi = g_ref[order, k1, 0], g_ref[order, k1, 1]
                ys.append(jnp.concatenate([xr * gr - xi * gi, xr * gi + xi * gr], axis=0))
            bm = [_mm(v2_ref[k1], y) for k1, y in zip(k1s, ys)]
            for o, b in zip(offs, bm):
                a_re[pl.ds(o, n2c), :] = b[:n2c]
                a_im[pl.ds(o, n2c), :] = b[n2c:]
            return carry
        lax.fori_loop(0, FFT_KH // HY_KG, mid, 0)

        def stage3(jg, carry):
            js = [jg * HY_JG + jj for jj in range(HY_JG)]
            bs = [jnp.concatenate([a_re[pl.ds(j, FFT_KH, stride=pitch), :],
                                   a_im[pl.ds(j, FFT_KH, stride=pitch), :]], axis=0) for j in js]
            outs = [_mm(f3, b) for b in bs]
            for j, y in zip(js, outs):
                cbuf[pl.ds(seq_start(j), n1h, stride=seq_stride), :] = y
            return carry
        lax.fori_loop(0, n2c // HY_JG, stage3, 0)

        y_prev = x_next[order] * (conv_out() + y_prev * bias_ref[order:order + 1, :])
        if order == 0:
            put_seq(y_prev)
    o_ref[...] = y_prev


def _hyena_fused(lay, z, conv_w, conv_b, spec, bias, consts, seq, row0, s0, nslab, col_mode):
    f1, w2, v2, f3 = _half_spectrum_consts(consts)
    n2c = seq // (FFT_N1 // 2)
    assert FFT_KH % HY_KG == 0 and n2c % HY_JG == 0
    pitch = _hy_pitch(n2c)
    cps = B_CH // HY_SLAB
    zspec = lambda k: pl.BlockSpec((seq, HY_SLAB), lambda j, b, k=k: (row0 + b, k * cps + s0 + j))
    once = lambda a: pl.BlockSpec(a.shape, lambda j, b: (0,) * a.ndim, pipeline_mode=pl.Buffered(1))
    mats = (jnp.asarray(f1, BF16), jnp.asarray(w2, BF16), jnp.asarray(v2, BF16), jnp.asarray(f3, BF16))
    seq_rows = (FFT_N1 // 2) * pitch
    return pl.pallas_call(
        functools.partial(_hyena_lat_kernel, col_mode=col_mode, seq=seq),
        grid=(nslab, lay.b),
        in_specs=[zspec(0), zspec(1), zspec(2),
                  pl.BlockSpec((3, B_SHORT, HY_SLAB), lambda j, b: (0, 0, s0 + j)),
                  pl.BlockSpec((3, 1, HY_SLAB), lambda j, b: (0, 0, s0 + j)),
                  pl.BlockSpec((2, FFT_KH, 2, n2c, HY_SLAB), lambda j, b: (0, 0, 0, 0, s0 + j),
                               pipeline_mode=pl.Buffered(1)),
                  pl.BlockSpec((2, HY_SLAB), lambda j, b: (0, s0 + j))] + [once(m) for m in mats],
        out_specs=pl.BlockSpec((seq, HY_SLAB), lambda j, b: (b, j)),
        out_shape=jax.ShapeDtypeStruct((lay.b * seq, nslab * HY_SLAB), F32),
        scratch_shapes=[pltpu.VMEM((seq_rows, HY_SLAB), F32), pltpu.VMEM((seq, HY_SLAB), F32),
                        pltpu.VMEM((FFT_KH * pitch, HY_SLAB), F32), pltpu.VMEM((FFT_KH * pitch, HY_SLAB), F32),
                        pltpu.VMEM((seq if col_mode else seq_rows, HY_SLAB), F32)],
        compiler_params=_cparams(("parallel", "parallel")),
        name="hyena_col" if col_mode else "hyena_row",
    )(z, z, z, jnp.transpose(conv_w, (1, 0, 2)), conv_b.reshape(3, 1, B_CH), spec, bias, *mats)


def _hyena(lay, z, conv_w, conv_b, spec_c, spec_x, bias, consts_c, consts_x):
    assert lay.n_ctx % lay.seq == 0
    half = B_CH // 2 // HY_SLAB
    args = (lay, z, conv_w, conv_b)
    yc = _hyena_fused(*args, spec_c, bias, consts_c, lay.ctx, 0, 0, 2 * half, False)
    yr = _hyena_fused(*args, spec_x, bias, consts_x, lay.seq, lay.n_ctx // lay.seq, 0, half, False)
    ycol = _hyena_fused(*args, spec_x, bias, consts_x, lay.seq, lay.n_ctx // lay.seq, half, half, True)
    return jnp.concatenate([yc, jnp.concatenate([yr, ycol], axis=1)], axis=0)


def _head_norm_gate(o, gate, norm_w, heads, width):
    outs = []
    for h in range(heads):
        oh = o[:, h * width:(h + 1) * width]
        y = oh * lax.rsqrt(jnp.mean(oh * oh, axis=-1, keepdims=True) + NORM_EPS) * norm_w
        outs.append(y * _silu(gate[:, h * width:(h + 1) * width]))
    return jnp.concatenate(outs, axis=1)


def _merge_kernel(x_ref, mod_ref, nw_ref, a0_ref, a1_ref, ag_ref, yb_ref, c0_ref, c1_ref, cx_ref, cz_ref,
                  d0_ref, d1_ref, dg_ref, an_ref, cd_ref, cn_ref, dn_ref, wg_ref, wb_ref, wo_ref, o_ref):
    xv = x_ref[...]
    m = mod_ref[...]
    hb = _prenorm(xv, nw_ref[1:2, :], m[3:4, :], m[4:5, :]).astype(BF16)
    ya = _head_norm_gate(a0_ref[...] + a1_ref[...], ag_ref[...], an_ref[...], A_HEADS, A_DV)
    yd = _head_norm_gate(d0_ref[...] + d1_ref[...], dg_ref[...], dn_ref[...], D_HEADS, D_DV)
    yc = (c0_ref[...] + c1_ref[...] + cd_ref[...] * cx_ref[...]) * _silu(cz_ref[...])
    gw = C_INNER // C_GROUPS
    cn = cn_ref[...]
    yc = jnp.concatenate(
        [yc[:, g * gw:(g + 1) * gw]
         * lax.rsqrt(jnp.mean(yc[:, g * gw:(g + 1) * gw] ** 2, axis=-1, keepdims=True) + NORM_EPS)
         * cn[:, g * gw:(g + 1) * gw] for g in range(C_GROUPS)], axis=1)
    acc = jnp.zeros(xv.shape, F32)
    for k, y in enumerate((ya, yb_ref[...], yc, yd)):
        gate = jax.nn.sigmoid(jnp.dot(hb, wg_ref[:, k * D_MODEL:(k + 1) * D_MODEL], preferred_element_type=F32))
        acc = acc + gate * jnp.dot(y.astype(BF16), wb_ref[k], preferred_element_type=F32)
    o_ref[...] = xv + m[5:6, :] * jnp.dot(acc.astype(BF16), wo_ref[...], preferred_element_type=F32)


def _merge(lay, x, mod, nw, a0, a1, ag, yb, c0, c1, cx, cz, d0, d1, dg, an, cd, cn, dn, wg, wb, wo):
    n, d = x.shape
    row = lambda w: pl.BlockSpec((TOK_TILE, w), lambda i: (i, 0))
    once = lambda a: pl.BlockSpec(a.shape, lambda i: (0,) * a.ndim, pipeline_mode=pl.Buffered(1))
    return pl.pallas_call(
        _merge_kernel,
        grid=(n // TOK_TILE,),
        in_specs=[row(d), pl.BlockSpec((None, N_MOD, d), lambda i: (lay.mod_index(i), 0, 0)), _full(nw.shape)]
                 + [row(BRANCH_W)] * 11 + [_full(an.shape), _full(cd.shape), _full(cn.shape), _full(dn.shape),
                                           once(wg), once(wb), once(wo)],
        out_specs=row(d),
        out_shape=jax.ShapeDtypeStruct((n, d), F32),
        compiler_params=_cparams(("parallel",)), name="merge",
    )(x, mod, nw, a0, a1, ag, yb, c0, c1, cx, cz, d0, d1, dg, an, cd, cn, dn, wg, wb, wo)


def _final_norm_kernel(x_ref, w_ref, o_ref):
    xv = x_ref[...]
    o_ref[...] = xv * lax.rsqrt(jnp.mean(xv * xv, axis=-1, keepdims=True) + NORM_EPS) * w_ref[...]


def _final_norm(lay, x, w):
    d = x.shape[1]
    return pl.pallas_call(
        _final_norm_kernel,
        grid=(lay.b * lay.tl,),
        in_specs=[pl.BlockSpec((ROW_TILE, d), lambda i: (lay.nct + i, 0)), _full((1, d))],
        out_specs=pl.BlockSpec((ROW_TILE, d), lambda i: (i, 0)),
        out_shape=jax.ShapeDtypeStruct((lay.b * lay.seq, d), F32),
        compiler_params=_cparams(("parallel",)), name="final_norm",
    )(x, w.reshape(1, d))


def kernel(x, c, ctx, c_ctx, w_ada, b_ada, norm_w, ffn_up, ffn_down, w_in, gdn_conv, gdn_a_log, gdn_dt_bias, gdn_norm, hy_conv_w, hy_conv_b, hy_w1, hy_b1, hy_w2, hy_b2, hy_wout, hy_freq, hy_bias, ssd_conv_w, ssd_conv_b, ssd_a_log, ssd_dt_bias, ssd_d, ssd_norm, gla_gk_w, gla_gk_b, gla_norm, w_branch, w_out, final_norm):
    b, seq, d = x.shape
    ctx_len = ctx.shape[1]
    depth = w_ada.shape[0]
    lay = _Layout(b, ctx_len, seq)
    consts_c = _dft_consts(2 * ctx_len // FFT_N1)
    consts_x = _dft_consts(2 * seq // FFT_N1)

    rp = -(-(1 + b) // 8) * 8
    cond = jnp.concatenate([c_ctx[None, :], c, jnp.zeros((rp - 1 - b, d), F32)], axis=0)
    mods = _ada(cond, w_ada, b_ada).reshape(depth, rp, N_MOD, d)

    xf = jnp.concatenate([ctx.reshape(b * ctx_len, d), x.reshape(b * seq, d)], axis=0)
    for l in range(depth):
        mod = mods[l]
        nw = norm_w[l]
        w_r, w_gate = _rearrange_w_in(w_in[l])
        xf = _ffn(lay, xf, mod, nw, ffn_up[l, 0].astype(BF16), ffn_down[l, 0].astype(BF16), 0)

        qkv, a_gate, zb, c_z, xbc, d_qkv, d_gate, small = _inproj(
            lay, xf, mod, nw, w_r, gdn_conv[l], ssd_conv_w[l], ssd_conv_b[l].reshape(1, C_XBC))
        a0, a1 = _gdn_scan(lay, qkv, small, gdn_a_log[l], gdn_dt_bias[l])
        c0, c1 = _ssd_scan(lay, xbc, small, ssd_a_log[l], ssd_dt_bias[l])
        d0, d1 = _gla_scan(lay, d_qkv, small, gla_gk_w[l], gla_gk_b[l])
        fargs = (hy_w1[l], hy_b1[l], hy_w2[l], hy_b2[l], hy_wout[l], hy_freq[l])
        spec_c = _hyena_spectrum(ctx_len, consts_c, *fargs)
        spec_x = _hyena_spectrum(seq, consts_x, *fargs)
        yb = _hyena(lay, zb, hy_conv_w[l], hy_conv_b[l], spec_c, spec_x, hy_bias[l], consts_c, consts_x)

        xf = _merge(lay, xf, mod, nw, a0, a1, a_gate, yb, c0, c1, xbc, c_z, d0, d1, d_gate,
                    gdn_norm[l].reshape(1, A_DV), jnp.repeat(ssd_d[l], C_HEADDIM).reshape(1, C_INNER),
                    ssd_norm[l].reshape(1, C_INNER), gla_norm[l].reshape(1, D_DV),
                    w_gate, w_branch[l].astype(BF16), w_out[l].astype(BF16))
        xf = _ffn(lay, xf, mod, nw, ffn_up[l, 1].astype(BF16), ffn_down[l, 1].astype(BF16), 2)
    return _final_norm(lay, xf, final_norm).reshape(b, seq, d)
```

```python
import functools
import math

import numpy as np
import jax
import jax.numpy as jnp
from jax import lax
from jax.experimental import pallas as pl
from jax.experimental.pallas import tpu as pltpu

F32 = jnp.float32
BF16 = jnp.bfloat16
HI = lax.Precision.HIGHEST

D_MODEL = 1024
GRID_W = 64
CHUNK = 64
NORM_EPS = 1e-6
N_MOD = 9
D_FF = 2816
SHORT_CONV = 5

A_HEADS, A_DK, A_DV = 4, 128, 128
A_QKV = A_HEADS * (2 * A_DK + A_DV)
B_CH, B_SHORT, B_EMB, B_FFN, B_INNER_MLPS = 512, 3, 33, 64, 2
B_WINDOW_SHIFT, B_DECAY_SHORT_PCT, B_DECAY_LONG_PCT, B_DECAY_TARGET = 0.05, 0.3, 1.5, 1e-2
C_HEADS, C_HEADDIM, C_GROUPS, C_STATE = 8, 64, 2, 64
C_INNER = C_HEADS * C_HEADDIM
C_XBC = C_INNER + 2 * C_GROUPS * C_STATE
D_HEADS, D_DK, D_DV, D_RANK = 4, 64, 128, 16
D_GATE_NORM = 16.0
N_BRANCH, BRANCH_W = 4, 512

IN_SIZES = (A_QKV, A_HEADS * A_DV, 2 * A_HEADS, 2 * A_HEADS, 3 * B_CH, C_INNER, C_XBC, 2 * C_HEADS,
            D_HEADS * (2 * D_DK + D_DV), D_HEADS * D_DV, 2 * D_RANK, N_BRANCH * D_MODEL)
IN_OFFS = tuple(int(v) for v in np.cumsum((0,) + IN_SIZES))

ROW_TILE = 256
TOK_TILE = 512
SMALL_W = 128
VMEM_LIMIT = 56 * 1024 * 1024


def _cparams(sem):
    return pltpu.CompilerParams(dimension_semantics=sem, vmem_limit_bytes=VMEM_LIMIT)


def _mm(a, b):
    return jnp.dot(a.astype(BF16), b.astype(BF16), preferred_element_type=F32)


def _mm_nt(a, b):
    return lax.dot_general(a.astype(BF16), b.astype(BF16), (((1,), (1,)), ((), ())),
                           preferred_element_type=F32)


def _mm_hi(a, b):
    return jnp.dot(a, b, precision=HI, preferred_element_type=F32)


def _bf16_parts(v, parts):
    out, rest = [], v
    for _ in range(parts):
        hi = rest.astype(BF16)
        out.append(hi)
        rest = rest - hi.astype(F32)
    return out


def _mm_3x(a, b):
    ah, al = _bf16_parts(a, 2)
    bh, bl = _bf16_parts(b, 2)
    dot = functools.partial(jnp.dot, preferred_element_type=F32)
    return dot(ah, bh) + (dot(ah, bl) + dot(al, bh))


def _mm_sel_l(sel, v, parts=3):
    sb = sel.astype(BF16)
    return sum(jnp.dot(sb, p, preferred_element_type=F32) for p in _bf16_parts(v, parts))


def _mm_sel_r(v, sel, parts=3):
    sb = sel.astype(BF16)
    return sum(jnp.dot(p, sb, preferred_element_type=F32) for p in _bf16_parts(v, parts))


def _silu(v):
    return v * jax.nn.sigmoid(v)


def _softplus(v):
    return jnp.maximum(v, 0.0) + jnp.log1p(jnp.exp(-jnp.abs(v)))


def _log_sigmoid(v):
    return jnp.minimum(v, 0.0) - jnp.log1p(jnp.exp(-jnp.abs(v)))


def _prenorm(xv, gain, shift, scale):
    ms = jnp.mean(xv * xv, axis=-1, keepdims=True)
    return (xv * lax.rsqrt(ms + NORM_EPS) * gain) * (1.0 + scale) + shift


def _full(shape):
    nd = len(shape)
    return pl.BlockSpec(shape, lambda *_: (0,) * nd)


class _Layout:
    def __init__(self, batch, ctx_len, seq):
        assert ctx_len % ROW_TILE == 0 and seq % ROW_TILE == 0
        assert (batch * ctx_len) % TOK_TILE == 0 and seq % TOK_TILE == 0
        self.b, self.ctx, self.seq = batch, ctx_len, seq
        self.n_ctx = batch * ctx_len
        self.n = batch * (ctx_len + seq)
        self.tc, self.tl = ctx_len // ROW_TILE, seq // ROW_TILE
        self.nct = batch * self.tc
        self.tiles = self.n // ROW_TILE

    def mod_index(self, i, tile=TOK_TILE):
        nct = self.n_ctx // tile
        return jnp.where(i < nct, 0, 1 + (i - nct) // (self.seq // tile))

    def seg_first(self, i):
        return jnp.where(i < self.nct, lax.rem(i, self.tc) == 0, lax.rem(i - self.nct, self.tl) == 0)

    def seg_last(self, i):
        return jnp.where(i < self.nct, lax.rem(i, self.tc) == self.tc - 1,
                         lax.rem(i - self.nct, self.tl) == self.tl - 1)

    def fwd_tile(self, b, s):
        return jnp.where(s < self.tc, b * self.tc + s, self.nct + b * self.tl + (s - self.tc))

    def rev_tile(self, b, s):
        return jnp.where(s < self.tc, b * self.tc + (self.tc - 1 - s),
                         self.nct + b * self.tl + (self.tl - 1 - (s - self.tc)))


def _ada_kernel(c_ref, w_ref, b_ref, o_ref):
    o_ref[...] = _mm_hi(_silu(c_ref[...]), w_ref[...]) + b_ref[...]


def _ada(cond, w_ada, b_ada):
    depth, d, nm = w_ada.shape
    rp = cond.shape[0]
    tn = 1152
    return pl.pallas_call(
        _ada_kernel,
        grid=(depth, nm // tn),
        in_specs=[_full((rp, d)),
                  pl.BlockSpec((None, d, tn), lambda l, j: (l, 0, j)),
                  pl.BlockSpec((None, 1, tn), lambda l, j: (l, 0, j))],
        out_specs=pl.BlockSpec((None, rp, tn), lambda l, j: (l, 0, j)),
        out_shape=jax.ShapeDtypeStruct((depth, rp, nm), F32),
        compiler_params=_cparams(("parallel", "parallel")), name="ada",
    )(cond, w_ada, b_ada.reshape(depth, 1, nm))


FF_CHUNK = 256


def _ffn_kernel(x_ref, mod_ref, nw_ref, wup_ref, wdn_ref, o_ref, *, sub):
    xv = x_ref[...]
    m = mod_ref[...]
    h = _prenorm(xv, nw_ref[sub:sub + 1, :], m[3 * sub:3 * sub + 1, :], m[3 * sub + 1:3 * sub + 2, :])
    hb = h.astype(BF16)
    acc = jnp.zeros(xv.shape, F32)
    for c in range(D_FF // FF_CHUNK):
        lo = c * FF_CHUNK
        a = jnp.dot(hb, wup_ref[:, lo:lo + FF_CHUNK], preferred_element_type=F32)
        g = jnp.dot(hb, wup_ref[:, D_FF + lo:D_FF + lo + FF_CHUNK], preferred_element_type=F32)
        acc = acc + jnp.dot((_silu(a) * g).astype(BF16), wdn_ref[lo:lo + FF_CHUNK, :],
                            preferred_element_type=F32)
    o_ref[...] = xv + 0.5 * m[3 * sub + 2:3 * sub + 3, :] * acc


def _ffn(lay, x, mod, nw, wup, wdn, sub):
    n, d = x.shape
    return pl.pallas_call(
        functools.partial(_ffn_kernel, sub=sub),
        grid=(n // TOK_TILE,),
        in_specs=[pl.BlockSpec((TOK_TILE, d), lambda i: (i, 0)),
                  pl.BlockSpec((None, N_MOD, d), lambda i: (lay.mod_index(i), 0, 0)),
                  _full(nw.shape),
                  pl.BlockSpec(wup.shape, lambda i: (0, 0), pipeline_mode=pl.Buffered(1)),
                  pl.BlockSpec(wdn.shape, lambda i: (0, 0), pipeline_mode=pl.Buffered(1))],
        out_specs=pl.BlockSpec((TOK_TILE, d), lambda i: (i, 0)),
        out_shape=jax.ShapeDtypeStruct((n, d), F32),
        compiler_params=_cparams(("parallel",)), name="ffn",
    )(x, mod, nw, wup, wdn)


PROJ_W = (A_QKV, A_HEADS * A_DV, 3 * B_CH, C_INNER, C_XBC, D_HEADS * (2 * D_DK + D_DV), D_HEADS * D_DV, SMALL_W)
PROJ_O = tuple(int(v) for v in np.cumsum((0,) + PROJ_W))


def _rearrange_w_in(w_in):
    o = IN_OFFS
    small = jnp.concatenate([w_in[:, o[2]:o[4]], w_in[:, o[7]:o[8]], w_in[:, o[10]:o[11]],
                             jnp.zeros((w_in.shape[0], SMALL_W - 64), w_in.dtype)], axis=1)
    w_r = jnp.concatenate([w_in[:, o[0]:o[1]], w_in[:, o[1]:o[2]], w_in[:, o[4]:o[5]], w_in[:, o[5]:o[6]],
                           w_in[:, o[6]:o[7]], w_in[:, o[8]:o[9]], w_in[:, o[9]:o[10]], small], axis=1)
    return w_r.astype(BF16), w_in[:, o[11]:o[12]].astype(BF16)


PROJ_HALO = 16
PROJ_CONV = (0, 4)


def _inproj_kernel(xp_ref, x_ref, xn_ref, mod_ref, nw_ref, w_ref, cwa_ref, cwc_ref, cbc_ref, *rest, lay):
    o_refs, zbuf = rest[:-1], rest[-1]
    i = pl.program_id(0)
    m = mod_ref[...]
    norm = lambda r: _prenorm(r[...], nw_ref[1:2, :], m[3:4, :], m[4:5, :]).astype(BF16)
    hb = norm(x_ref)
    hb_all = jnp.concatenate([norm(xp_ref), hb, norm(xn_ref)], axis=0)
    keep_p = jnp.where(lay.seg_first(i), 0.0, 1.0)
    keep_n = jnp.where(lay.seg_last(i), 0.0, 1.0)
    convs = {PROJ_CONV[0]: (cwa_ref, None), PROJ_CONV[1]: (cwc_ref, cbc_ref)}
    for k, o_ref in enumerate(o_refs):
        wk = w_ref[:, PROJ_O[k]:PROJ_O[k + 1]]
        if k not in convs:
            o_ref[...] = jnp.dot(hb, wk, preferred_element_type=F32)
            continue
        cw_ref, cb_ref = convs[k]
        width = PROJ_W[k]
        z = jnp.dot(hb_all, wk, preferred_element_type=F32)
        zbuf[0:PROJ_HALO, 0:width] = z[:PROJ_HALO] * keep_p
        zbuf[PROJ_HALO:PROJ_HALO + ROW_TILE, 0:width] = z[PROJ_HALO:PROJ_HALO + ROW_TILE]
        zbuf[PROJ_HALO + ROW_TILE:, 0:width] = z[PROJ_HALO + ROW_TILE:] * keep_n
        taps = cw_ref.shape[0]
        left = (taps - 1) // 2
        acc = None
        for t in range(taps):
            off = t - left
            term = zbuf[PROJ_HALO + off:PROJ_HALO + off + ROW_TILE, 0:width] * cw_ref[t:t + 1, :]
            acc = term if acc is None else acc + term
        if cb_ref is not None:
            acc = acc + cb_ref[...]
        o_ref[...] = _silu(acc)


def _inproj(lay, x, mod, nw, w_r, conv_a, conv_c, conv_c_bias):
    n, d = x.shape
    rh = ROW_TILE // PROJ_HALO
    lasth = n // PROJ_HALO - 1
    return pl.pallas_call(
        functools.partial(_inproj_kernel, lay=lay),
        grid=(n // ROW_TILE,),
        in_specs=[pl.BlockSpec((PROJ_HALO, d), lambda i: (jnp.maximum(i * rh - 1, 0), 0)),
                  pl.BlockSpec((ROW_TILE, d), lambda i: (i, 0)),
                  pl.BlockSpec((PROJ_HALO, d), lambda i: (jnp.minimum((i + 1) * rh, lasth), 0)),
                  pl.BlockSpec((None, N_MOD, d), lambda i: (lay.mod_index(i, ROW_TILE), 0, 0)),
                  _full(nw.shape),
                  pl.BlockSpec(w_r.shape, lambda i: (0, 0), pipeline_mode=pl.Buffered(1)),
                  _full(conv_a.shape), _full(conv_c.shape), _full(conv_c_bias.shape)],
        out_specs=[pl.BlockSpec((ROW_TILE, w), lambda i: (i, 0)) for w in PROJ_W],
        out_shape=[jax.ShapeDtypeStruct((n, w), F32) for w in PROJ_W],
        scratch_shapes=[pltpu.VMEM((ROW_TILE + 2 * PROJ_HALO, max(PROJ_W[k] for k in PROJ_CONV)), F32)],
        compiler_params=_cparams(("parallel",)), name="inproj",
    )(x, x, x, mod, nw, w_r, conv_a, conv_c, conv_c_bias)


def _tri_masks(rev):
    ii = lax.broadcasted_iota(jnp.int32, (CHUNK, CHUNK), 0)
    jj = lax.broadcasted_iota(jnp.int32, (CHUNK, CHUNK), 1)
    return (ii <= jj, ii < jj) if rev else (ii >= jj, ii > jj)


def _masked_decay(col, row, incl):
    return jnp.where(incl, jnp.exp(jnp.where(incl, col - row, 0.0)), 0.0)


def _scan_call(kernel, lay, ins_tiled, ins_full, out_w, scratch):
    steps = lay.tc + lay.tl
    specs = []
    args = []
    for walk in (lay.fwd_tile, lay.rev_tile):
        for a in ins_tiled:
            specs.append(pl.BlockSpec((ROW_TILE, a.shape[1]), lambda b, s, walk=walk: (walk(b, s), 0)))
            args.append(a)
    for a in ins_full:
        specs.append(_full(a.shape))
        args.append(a)
    return pl.pallas_call(
        kernel,
        grid=(lay.b, steps),
        in_specs=specs,
        out_specs=[pl.BlockSpec((ROW_TILE, out_w), lambda b, s: (lay.fwd_tile(b, s), 0)),
                   pl.BlockSpec((ROW_TILE, out_w), lambda b, s: (lay.rev_tile(b, s), 0))],
        out_shape=[jax.ShapeDtypeStruct((lay.n, out_w), F32)] * 2,
        scratch_shapes=scratch,
        compiler_params=_cparams(("arbitrary", "arbitrary")), name=kernel.__name__.strip("_"),
    )(*args)


def _ssd_kernel(xf_ref, sf_ref, xr_ref, sr_ref, alog_ref, dtb_ref, alog_t_ref, dtb_t_ref, exp_ref,
                of_ref, or_ref, st_ref):
    @pl.when(pl.program_id(1) == 0)
    def _():
        st_ref[...] = jnp.zeros(st_ref.shape, F32)

    hpg = C_HEADS // C_GROUPS
    gw = hpg * C_HEADDIM
    nch = ROW_TILE // CHUNK
    refs = ((xf_ref, sf_ref, of_ref), (xr_ref, sr_ref, or_ref))
    units, pre = [], {}
    for d, (x_ref, s_ref, _) in enumerate(refs):
        incl, _ = _tri_masks(d == 1)
        m_col = incl.astype(F32)
        m_row = m_col.T
        sm = s_ref[...]
        sm_t = sm.T
        dt_all = _softplus(sm[:, 16:32] + dtb_ref[...])
        a_all = -jnp.exp(alog_ref[...]) * dt_all
        a_t_all = -jnp.exp(alog_t_ref[...]) * _softplus(sm_t[16:32, :] + dtb_t_ref[...])
        bm_t = x_ref[:, C_INNER:C_INNER + C_GROUPS * C_STATE].T
        for c in range(nch):
            rows = slice(c * CHUNK, (c + 1) * CHUNK)
            units.append((d, c))
            pre[d, c] = dict(incl=incl, m_col=m_col, m_row=m_row, a_ch=a_all[rows], a_t=a_t_all[:, rows],
                             dt=dt_all[rows], bm_t=bm_t[:, rows], e=exp_ref[d], xs=x_ref[rows, 0:C_INNER],
                             bm=x_ref[rows, C_INNER:C_INNER + C_GROUPS * C_STATE],
                             cm=x_ref[rows, C_INNER + C_GROUPS * C_STATE:C_XBC])
    for u in units:
        w = pre[u]
        w["ac"] = _mm_sel_l(w["m_col"], w["a_ch"])
        w["ac_t"] = _mm_sel_r(w["a_t"], w["m_row"])
        w["alast"] = jnp.sum(w["a_ch"], axis=0, keepdims=True)
    for u in units:
        w = pre[u]
        w["xdt"] = w["xs"] * _mm_sel_r(w["dt"], w["e"])
        w["eac_e"] = _mm_sel_r(jnp.exp(w["ac"]), w["e"])
        w["xd"] = w["xdt"] * _mm_sel_r(jnp.exp(w["alast"] - w["ac"]), w["e"])
        w["dle"] = _mm_sel_r(jnp.broadcast_to(jnp.exp(w["alast"]), (8, 2 * C_HEADS)), w["e"])[0:1]
    grp = lambda a, g: a[:, g * C_STATE:(g + 1) * C_STATE]
    cb = {(u, g): _mm_nt(grp(pre[u]["cm"], g), grp(pre[u]["bm"], g)) for u in units for g in range(C_GROUPS)}
    upd = {(u, g): _mm(pre[u]["bm_t"][g * C_STATE:(g + 1) * C_STATE], pre[u]["xd"][:, g * gw:(g + 1) * gw])
           for u in units for g in range(C_GROUPS)}
    y_diag = {}
    for u in units:
        w = pre[u]
        for h in range(C_HEADS):
            col = u[0] * C_HEADS + h
            seg = _masked_decay(w["ac"][:, col:col + 1], w["ac_t"][col:col + 1, :], w["incl"])
            y_diag[u, h] = _mm(cb[u, h // hpg] * seg, w["xdt"][:, h * C_HEADDIM:(h + 1) * C_HEADDIM])
    st = {(d, g): st_ref[d, g] for d in range(2) for g in range(C_GROUPS)}
    entry = {}
    for p in range(nch):
        for d in range(2):
            u = (d, (nch - 1 - p) if d == 1 else p)
            for g in range(C_GROUPS):
                entry[u, g] = st[d, g]
                st[d, g] = pre[u]["dle"][:, g * gw:(g + 1) * gw] * st[d, g] + upd[u, g]
    for d in range(2):
        for g in range(C_GROUPS):
            st_ref[d, g] = st[d, g]
    for u in units:
        y_off = [_mm(grp(pre[u]["cm"], g), entry[u, g]) for g in range(C_GROUPS)]
        refs[u[0]][2][u[1] * CHUNK:(u[1] + 1) * CHUNK, :] = (
            jnp.concatenate([y_diag[u, h] for h in range(C_HEADS)], axis=1)
            + pre[u]["eac_e"] * jnp.concatenate(y_off, axis=1))


def _ssd_scan(lay, xbc, small, a_log, dt_bias):
    expand = np.zeros((2, 2 * C_HEADS, C_INNER), np.float32)
    for d in range(2):
        for h in range(C_HEADS):
            expand[d, d * C_HEADS + h, h * C_HEADDIM:(h + 1) * C_HEADDIM] = 1.0
    al = a_log.reshape(1, 2 * C_HEADS)
    db = dt_bias.reshape(1, 2 * C_HEADS)
    return _scan_call(_ssd_kernel, lay, (xbc, small), (al, db, al.T, db.T, jnp.asarray(expand)), C_INNER,
                      [pltpu.VMEM((2, C_GROUPS, C_STATE, C_INNER // C_GROUPS), F32)])


def _gla_kernel(xf_ref, sf_ref, xr_ref, sr_ref, gkw_ref, gkb_ref, of_ref, or_ref, st_ref):
    @pl.when(pl.program_id(1) == 0)
    def _():
        st_ref[...] = jnp.zeros(st_ref.shape, F32)

    nk = D_HEADS * D_DK
    nch = ROW_TILE // CHUNK
    refs = ((xf_ref, sf_ref, of_ref), (xr_ref, sr_ref, or_ref))
    units, pre = [], {}
    for d, (x_ref, s_ref, _) in enumerate(refs):
        incl, _ = _tri_masks(d == 1)
        m_col = incl.astype(F32)
        lr = s_ref[:, 32 + D_RANK * d:32 + D_RANK * (d + 1)]
        g_all = _log_sigmoid(_mm_3x(lr, gkw_ref[d]) + gkb_ref[d:d + 1, :]) / D_GATE_NORM
        v_t = x_ref[:, 2 * nk:].T
        for c in range(nch):
            rows = slice(c * CHUNK, (c + 1) * CHUNK)
            g_ch = g_all[rows]
            gc = _mm_sel_l(m_col, g_ch)
            glast = jnp.sum(g_ch, axis=0, keepdims=True)
            q = x_ref[rows, 0:nk] * (D_DK ** -0.5)
            k = x_ref[rows, nk:2 * nk]
            v = x_ref[rows, 2 * nk:]
            kd = k * jnp.exp(glast - gc)
            qd = q * jnp.exp(gc)
            qr = q * jnp.exp(gc - glast)
            dlast = jnp.exp(glast)
            for h in range(D_HEADS):
                ks = slice(h * D_DK, (h + 1) * D_DK)
                vs = slice(h * D_DV, (h + 1) * D_DV)
                units.append((d, c, h))
                pre[d, c, h] = dict(incl=incl, qr=qr[:, ks], kd=kd[:, ks], qd=qd[:, ks], v=v[:, vs],
                                    v_t=v_t[vs, rows], dl=dlast[:, ks])
    aqk = [jnp.where(pre[u]["incl"], _mm_nt(pre[u]["qr"], pre[u]["kd"]), 0.0) for u in units]
    upd = {u: _mm(pre[u]["v_t"], pre[u]["kd"]) for u in units}
    intra = {u: _mm(a, pre[u]["v"]) for u, a in zip(units, aqk)}
    heads = [(d, h) for d in range(2) for h in range(D_HEADS)]
    st = {dh: st_ref[dh[0], dh[1]] for dh in heads}
    entry = {}
    for p in range(nch):
        for d, h in heads:
            u = (d, (nch - 1 - p) if d == 1 else p, h)
            entry[u] = st[d, h]
            st[d, h] = st[d, h] * pre[u]["dl"] + upd[u]
    for dh in heads:
        st_ref[dh[0], dh[1]] = st[dh]
    inter = {u: _mm_nt(pre[u]["qd"], entry[u]) for u in units}
    for d, (_, _, o_ref) in enumerate(refs):
        for c in range(nch):
            o_ref[c * CHUNK:(c + 1) * CHUNK, :] = jnp.concatenate(
                [intra[d, c, h] + inter[d, c, h] for h in range(D_HEADS)], axis=1)


def _gla_scan(lay, qkv, small, gk_w, gk_b):
    return _scan_call(_gla_kernel, lay, (qkv, small), (gk_w, gk_b), D_HEADS * D_DV,
                      [pltpu.VMEM((2, D_HEADS, D_DV, D_DK), F32)])


TRI_BLOCK = 16
GDN_GROUP = 32


def _unit_lower_inverse(nms, eye, blk):
    dg = [jnp.where(blk, nm, 0.0) for nm in nms]
    off = [nm - d for nm, d in zip(nms, dg)]
    t0 = [eye - d for d in dg]
    p = [_mm(d, d) for d in dg]
    for it in range(3):
        t0 = [t + _mm(t, q) for t, q in zip(t0, p)]
        if it < 2:
            p = [_mm(q, q) for q in p]
    m = [_mm(t, o) for t, o in zip(t0, off)]
    m2 = [_mm(a, a) for a in m]
    r = [eye - a for a in m]
    r = [a + _mm(a, b) for a, b in zip(r, m2)]
    return [_mm(a, t) for a, t in zip(r, t0)]


def _gdn_kernel(xf_ref, sf_ref, xr_ref, sr_ref, alog_ref, dtb_ref, alog_t_ref, dtb_t_ref,
                of_ref, or_ref, st_ref):
    @pl.when(pl.program_id(1) == 0)
    def _():
        st_ref[...] = jnp.zeros(st_ref.shape, F32)

    nk = A_HEADS * A_DK
    nch = ROW_TILE // CHUNK
    ii = lax.broadcasted_iota(jnp.int32, (CHUNK, CHUNK), 0)
    jj = lax.broadcasted_iota(jnp.int32, (CHUNK, CHUNK), 1)
    eye = (ii == jj).astype(F32)
    blk = (ii // TRI_BLOCK) == (jj // TRI_BLOCK)
    refs = ((xf_ref, sf_ref, of_ref), (xr_ref, sr_ref, or_ref))

    units, pre = [], {}
    for d, (x_ref, s_ref, _) in enumerate(refs):
        incl, strict = _tri_masks(d == 1)
        m_col = incl.astype(F32)
        m_row = m_col.T
        sm = s_ref[...]
        sm_t = sm.T
        beta_all = jax.nn.sigmoid(sm[:, 0:8])
        g_all = -jnp.exp(alog_ref[...]) * _softplus(sm[:, 8:16] + dtb_ref[...])
        g_t_all = -jnp.exp(alog_t_ref[...]) * _softplus(sm_t[8:16, :] + dtb_t_ref[...])
        qn, kn = [], []
        for h in range(A_HEADS):
            qh = x_ref[:, h * A_DK:(h + 1) * A_DK]
            kh = x_ref[:, nk + h * A_DK:nk + (h + 1) * A_DK]
            qn.append(qh * lax.rsqrt(jnp.sum(qh * qh, axis=-1, keepdims=True) + NORM_EPS) * (A_DK ** -0.5))
            kn.append(kh * lax.rsqrt(jnp.sum(kh * kh, axis=-1, keepdims=True) + NORM_EPS))
        kn_t = [kh.T for kh in kn]
        for c in range(nch):
            rows = slice(c * CHUNK, (c + 1) * CHUNK)
            g_ch = g_all[rows]
            gc = _mm_sel_l(m_col, g_ch)
            gc_t = _mm_sel_r(g_t_all[:, rows], m_row)
            glast = jnp.sum(g_ch, axis=0, keepdims=True)
            for h in range(A_HEADS):
                col = d * A_HEADS + h
                gcc, gcr, gl = gc[:, col:col + 1], gc_t[col:col + 1, :], glast[:, col:col + 1]
                bh = beta_all[rows, col:col + 1]
                kh = kn[h][rows]
                units.append((d, c, h))
                pre[d, c, h] = dict(
                    dmask=_masked_decay(gcc, gcr, incl), strict=strict, egc=jnp.exp(gcc), kh=kh, kb=kh * bh,
                    qh=qn[h][rows], vb=x_ref[rows, 2 * nk + h * A_DV:2 * nk + (h + 1) * A_DV] * bh,
                    kg_t=kn_t[h][:, rows] * jnp.exp(gl - gcr), dl=jnp.exp(gl))
    for g0 in range(0, len(units), GDN_GROUP):
        us = units[g0:g0 + GDN_GROUP]
        kk = [_mm_nt(pre[u]["kb"], pre[u]["kh"]) for u in us]
        qk = [_mm_nt(pre[u]["qh"], pre[u]["kh"]) for u in us]
        tinv = _unit_lower_inverse(
            [jnp.where(pre[u]["strict"], a * pre[u]["dmask"], 0.0) for u, a in zip(us, kk)], eye, blk)
        uw = [_mm(t, jnp.concatenate([pre[u]["vb"], pre[u]["kb"] * pre[u]["egc"]], axis=1))
              for u, t in zip(us, tinv)]
        for u, a, b in zip(us, uw, qk):
            pre[u]["u"] = a[:, :A_DV]
            pre[u]["wq"] = jnp.concatenate([a[:, A_DV:], pre[u]["qh"] * pre[u]["egc"]], axis=0)
            pre[u]["aqk"] = b * pre[u]["dmask"]

    heads = [(d, h) for d in range(2) for h in range(A_HEADS)]
    st = {dh: st_ref[dh[0], dh[1]] for dh in heads}
    for p in range(nch):
        us = [(d, (nch - 1 - p) if d == 1 else p, h) for d, h in heads]
        ws = [_mm(pre[u]["wq"], st[u[0], u[2]]) for u in us]
        v_new = [pre[u]["u"] - w[:CHUNK] for u, w in zip(us, ws)]
        outs = [w[CHUNK:] + _mm(pre[u]["aqk"], v) for u, w, v in zip(us, ws, v_new)]
        for u, v in zip(us, v_new):
            st[u[0], u[2]] = pre[u]["dl"] * st[u[0], u[2]] + _mm(pre[u]["kg_t"], v)
        for d, (_, _, o_ref) in enumerate(refs):
            c = us[d * A_HEADS][1]
            o_ref[c * CHUNK:(c + 1) * CHUNK, :] = jnp.concatenate(outs[d * A_HEADS:(d + 1) * A_HEADS], axis=1)
    for dh in heads:
        st_ref[dh[0], dh[1]] = st[dh]


def _gdn_scan(lay, qkv, small, a_log, dt_bias):
    al = a_log.reshape(1, 2 * A_HEADS)
    db = dt_bias.reshape(1, 2 * A_HEADS)
    return _scan_call(_gdn_kernel, lay, (qkv, small), (al, db, al.T, db.T), A_HEADS * A_DV,
                      [pltpu.VMEM((2, A_HEADS, A_DK, A_DV), F32)])


FFT_N1 = 64
FFT_KB = 8


def _dft_consts(n2):
    n1 = FFT_N1
    m = n1 * n2
    k1 = np.arange(n1, dtype=np.float64)
    ang1 = 2.0 * np.pi * np.outer(k1, k1) / n1
    f1_full = np.concatenate([np.cos(ang1), -np.sin(ang1)], axis=0)
    f1 = f1_full[:, :n1 // 2]
    f3 = np.concatenate([np.cos(ang1[:, :n1 // 2]).T, -np.sin(ang1[:, :n1 // 2]).T], axis=1) / m
    j = np.arange(n2, dtype=np.float64)
    theta = 2.0 * np.pi * (np.outer(j, j)[None] / n2 + (k1[:, None, None] * j[None, None, :]) / m)
    wr, wi = np.cos(theta), -np.sin(theta)
    w2 = np.concatenate([np.concatenate([wr, -wi], axis=2), np.concatenate([wi, wr], axis=2)], axis=1)
    phi = 2.0 * np.pi * (np.outer(j, j)[None] / n2 + (k1[:, None, None] * j[None, :, None]) / m)
    vr, vi = np.cos(phi), np.sin(phi)
    v2 = np.concatenate([np.concatenate([vr, -vi], axis=2), np.concatenate([vi, vr], axis=2)], axis=1)
    return f1_full, f1, w2, v2, f3


FFT_KV = FFT_N1 // 2 + 1
FFT_KH = 40


def _half_spectrum_consts(consts):
    _, f1, w2, v2, f3 = consts
    n1 = FFT_N1
    f1h = np.zeros((2 * FFT_KH, n1 // 2))
    f1h[:FFT_KV] = f1[:FFT_KV]
    f1h[FFT_KH:FFT_KH + FFT_KV] = f1[n1:n1 + FFT_KV]
    w2h = np.zeros((FFT_KH,) + w2.shape[1:])
    w2h[:FFT_KV] = w2[:FFT_KV]
    v2h = np.zeros((FFT_KH,) + v2.shape[1:])
    v2h[:FFT_KV] = v2[:FFT_KV]
    weight = np.full((FFT_KV,), 2.0)
    weight[0] = weight[-1] = 1.0
    f3h = np.zeros((n1 // 2, 2 * FFT_KH))
    f3h[:, :FFT_KV] = f3[:, :FFT_KV] * weight
    f3h[:, FFT_KH:FFT_KH + FFT_KV] = f3[:, n1:n1 + FFT_KV] * weight
    return f1h, w2h, v2h, f3h


def _left_mm_kernel(w_ref, x_ref, o_ref):
    o_ref[...] = _mm_3x(w_ref[...], x_ref[...])


def _left_mm(wm, x):
    bs, kdim, ncols = x.shape
    mdim = wm.shape[0]
    tn = min(ncols, 4096)
    xspec = lambda rows: pl.BlockSpec((None, rows, tn), lambda b, j: (b, 0, j))
    return pl.pallas_call(
        _left_mm_kernel, grid=(bs, ncols // tn), in_specs=[_full(wm.shape), xspec(kdim)], out_specs=xspec(mdim),
        out_shape=jax.ShapeDtypeStruct((bs, mdim, ncols), F32),
        compiler_params=_cparams(("parallel", "parallel")), name="filter_dft_outer",
    )(wm, x)


def _filt_mid_kernel(a_ref, w_ref, o_ref, *, n2):
    for kk in range(FFT_KB):
        a = jnp.concatenate([a_ref[0, kk], a_ref[1, kk]], axis=0)
        xf = _mm_3x(w_ref[kk], a)
        o_ref[kk, 0] = xf[:n2]
        o_ref[kk, 1] = xf[n2:]


def _filt_mid_stage(a, w2):
    bs, _, n1, n2, ch = a.shape
    return pl.pallas_call(
        functools.partial(_filt_mid_kernel, n2=n2),
        grid=(bs, n1 // FFT_KB),
        in_specs=[pl.BlockSpec((None, 2, FFT_KB, n2, ch), lambda b, j: (b, 0, j, 0, 0)),
                  pl.BlockSpec((FFT_KB, 2 * n2, 2 * n2), lambda b, j: (j, 0, 0))],
        out_specs=pl.BlockSpec((None, FFT_KB, 2, n2, ch), lambda b, j: (b, j, 0, 0, 0)),
        out_shape=jax.ShapeDtypeStruct((bs, n1, 2, n2, ch), F32),
        compiler_params=_cparams(("parallel", "parallel")), name="filter_dft_mid",
    )(a, w2)


FILT_ROWS = 512


def _filter_kernel(z_ref, win_ref, sel_ref, w1_ref, b1_ref, w2_ref, b2_ref, wo_ref, fr_ref, o_ref):
    fr = fr_ref[...]
    h = jnp.sin(fr * (_mm_3x(z_ref[...], w1_ref[...]) + b1_ref[...]))
    for i in range(B_INNER_MLPS):
        h = jnp.sin(fr * (_mm_3x(h, w2_ref[i]) + b2_ref[i]))
    ho = _mm_3x(h, wo_ref[...])
    fwd = sel_ref[...] > 0.5
    win = win_ref[...]
    for o in range(2):
        base = o * 2 * B_CH
        o_ref[o] = jnp.where(fwd, ho[:, base:base + B_CH], ho[:, base + B_CH:base + 2 * B_CH]) * win


def _hyena_filter_time(l, w1, b1, w2, b2, w_out, freq):
    bands = (B_EMB - 1) // 2
    circ = jnp.arange(2 * l)
    pos = jnp.where(circ < l, circ, 2 * l - circ)
    pos = jnp.where(circ == l, 0, pos)
    t = (jnp.linspace(0.0, 1.0, l, dtype=F32)[pos])[:, None]
    ang = 2.0 * math.pi * pos.astype(F32)[:, None] / l
    fr = jnp.linspace(1e-4, bands - 1, bands, dtype=F32)[None, :]
    z = jnp.concatenate([t, jnp.cos(fr * ang), -jnp.sin(fr * ang)], axis=-1)
    z = jnp.pad(z, ((0, 0), (0, 128 - B_EMB)))
    max_decay = math.log(B_DECAY_TARGET) / B_DECAY_SHORT_PCT
    min_decay = math.log(B_DECAY_TARGET) / B_DECAY_LONG_PCT
    deltas = jnp.abs(jnp.linspace(min_decay, max_decay, B_CH, dtype=F32))
    win = (jnp.exp(-t * deltas) + B_WINDOW_SHIFT) * (circ != l).astype(F32)[:, None]
    sel = (circ < l).astype(F32)[:, None]
    w1p = jnp.pad(w1, ((0, 128 - B_EMB), (0, 0)))
    rows = min(FILT_ROWS, 2 * l)
    return pl.pallas_call(
        _filter_kernel,
        grid=(2 * l // rows,),
        in_specs=[pl.BlockSpec((rows, 128), lambda i: (i, 0)),
                  pl.BlockSpec((rows, B_CH), lambda i: (i, 0)),
                  pl.BlockSpec((rows, 1), lambda i: (i, 0)),
                  _full(w1p.shape), _full((1, B_FFN)), _full(w2.shape), _full((B_INNER_MLPS, 1, B_FFN)),
                  _full(w_out.shape), _full((1, B_FFN))],
        out_specs=pl.BlockSpec((2, rows, B_CH), lambda i: (0, i, 0)),
        out_shape=jax.ShapeDtypeStruct((2, 2 * l, B_CH), F32),
        compiler_params=_cparams(("parallel",)), name="hyena_filter",
    )(z, win, sel, w1p, b1.reshape(1, B_FFN), w2, b2.reshape(B_INNER_MLPS, 1, B_FFN), w_out,
      freq.reshape(1, B_FFN))


def _hyena_spectrum(l, consts, w1, b1, w2m, b2, w_out, freq):
    f1_full, _, w2, _, _ = consts
    n2 = 2 * l // FFT_N1
    g_time = _hyena_filter_time(l, w1, b1, w2m, b2, w_out, freq)
    f1_rows = np.concatenate([f1_full[:FFT_KH], f1_full[FFT_N1:FFT_N1 + FFT_KH]], axis=0)
    a = _left_mm(jnp.asarray(f1_rows, F32), g_time.reshape(2, FFT_N1, n2 * B_CH))
    return _filt_mid_stage(a.reshape(2, 2, FFT_KH, n2, B_CH), jnp.asarray(w2[:FFT_KH], F32))


HY_SLAB = 128


def _hy_pitch(n2c):
    return n2c + 8 if (n2c // 8) % 2 == 0 else n2c + 16


HY_KG = 20
HY_JG = 16


def _hyena_lat_kernel(zv_ref, z1_ref, z2_ref, cw_ref, cb_ref, g_ref, bias_ref, f1_ref, w2_ref, v2_ref, f3_ref,
                      o_ref, useq, ur, a_re, a_im, cbuf, *, col_mode, seq):
    n1h = FFT_N1 // 2
    n2c = seq // n1h
    pitch = _hy_pitch(n2c)
    grows = seq // GRID_W
    row_id = lax.broadcasted_iota(jnp.int32, (seq, HY_SLAB), 0)

    def short_conv(z_ref, k):
        z = z_ref[...]
        w = cw_ref[k]
        if col_mode:
            g0, gl = z[:GRID_W], z[seq - GRID_W:]
            cc = row_id[:GRID_W]
            wrap_p = jnp.where(cc == 0, 0.0, pltpu.roll(gl, 1, axis=0))
            wrap_n = jnp.where(cc == GRID_W - 1, 0.0, pltpu.roll(g0, GRID_W - 1, axis=0))
            prev = jnp.concatenate([wrap_p, z[:seq - GRID_W]], axis=0)
            nxt = jnp.concatenate([z[GRID_W:], wrap_n], axis=0)
        else:
            prev = jnp.where(row_id == 0, 0.0, pltpu.roll(z, 1, axis=0))
            nxt = jnp.where(row_id == seq - 1, 0.0, pltpu.roll(z, seq - 1, axis=0))
        return prev * w[0:1] + z * w[1:2] + nxt * w[2:3] + cb_ref[k]

    def seq_start(j):
        return (GRID_W * lax.rem(j, grows) + j // grows) if col_mode else j

    seq_stride = 2 if col_mode else pitch
    seq_ref = ur if col_mode else useq

    def put_seq(val):
        if col_mode:
            ur[...] = val
        else:
            for n1 in range(n1h):
                useq[n1 * pitch:n1 * pitch + n2c, :] = val[n1 * n2c:(n1 + 1) * n2c]

    def conv_out():
        if col_mode:
            return cbuf[...]
        return jnp.concatenate([cbuf[n1 * pitch:n1 * pitch + n2c, :] for n1 in range(n1h)], axis=0)

    y_prev = short_conv(zv_ref, 0)
    x_next = (short_conv(z1_ref, 1), short_conv(z2_ref, 2))
    put_seq(y_prev)
    f1 = f1_ref[...]
    f3 = f3_ref[...]
    jgrp = min(HY_JG, n2c)
    for order in range(2):
        def stage1(jg, carry):
            js = [jg * jgrp + jj for jj in range(jgrp)]
            xs = [seq_ref[pl.ds(seq_start(j), n1h, stride=seq_stride), :] for j in js]
            outs = [_mm(f1, x) for x in xs]
            for j, a in zip(js, outs):
                a_re[pl.ds(j, FFT_KH, stride=pitch), :] = a[:FFT_KH]
                a_im[pl.ds(j, FFT_KH, stride=pitch), :] = a[FFT_KH:]
            return carry
        lax.fori_loop(0, n2c // jgrp, stage1, 0)

        def mid(kg, carry):
            k1s = [kg * HY_KG + kk for kk in range(HY_KG)]
            offs = [pl.multiple_of(k1 * pitch, 8) for k1 in k1s]
            xin = [jnp.concatenate([a_re[pl.ds(o, n2c), :], a_im[pl.ds(o, n2c), :]], axis=0) for o in offs]
            xf = [_mm(w2_ref[k1], a) for k1, a in zip(k1s, xin)]
            ys = []
            for k1, x in zip(k1s, xf):
                xr, xi = x[:n2c], x[n2c:]
                gr, gi = g_ref[order, k1, 0], g_ref[order, k1, 1]
                ys.append(jnp.concatenate([xr * gr - xi * gi, xr * gi + xi * gr], axis=0))
            bm = [_mm(v2_ref[k1], y) for k1, y in zip(k1s, ys)]
            for o, b in zip(offs, bm):
                a_re[pl.ds(o, n2c), :] = b[:n2c]
                a_im[pl.ds(o, n2c), :] = b[n2c:]
            return carry
        lax.fori_loop(0, FFT_KH // HY_KG, mid, 0)

        def stage3(jg, carry):
            js = [jg * jgrp + jj for jj in range(jgrp)]
            bs = [jnp.concatenate([a_re[pl.ds(j, FFT_KH, stride=pitch), :],
                                   a_im[pl.ds(j, FFT_KH, stride=pitch), :]], axis=0) for j in js]
            outs = [_mm(f3, b) for b in bs]
            for j, y in zip(js, outs):
                cbuf[pl.ds(seq_start(j), n1h, stride=seq_stride), :] = y
            return carry
        lax.fori_loop(0, n2c // jgrp, stage3, 0)

        y_prev = x_next[order] * (conv_out() + y_prev * bias_ref[order:order + 1, :])
        if order == 0:
            put_seq(y_prev)
    o_ref[...] = y_prev


def _hyena_fused(lay, z, conv_w, conv_b, spec, bias, consts, seq, row0, s0, nslab, col_mode):
    f1, w2, v2, f3 = _half_spectrum_consts(consts)
    n2c = seq // (FFT_N1 // 2)
    assert FFT_KH % HY_KG == 0 and n2c % min(HY_JG, n2c) == 0
    pitch = _hy_pitch(n2c)
    cps = B_CH // HY_SLAB
    zspec = lambda k: pl.BlockSpec((seq, HY_SLAB), lambda j, b, k=k: (row0 + b, k * cps + s0 + j))
    once = lambda a: pl.BlockSpec(a.shape, lambda j, b: (0,) * a.ndim, pipeline_mode=pl.Buffered(1))
    mats = (jnp.asarray(f1, BF16), jnp.asarray(w2, BF16), jnp.asarray(v2, BF16), jnp.asarray(f3, BF16))
    seq_rows = (FFT_N1 // 2) * pitch
    return pl.pallas_call(
        functools.partial(_hyena_lat_kernel, col_mode=col_mode, seq=seq),
        grid=(nslab, lay.b),
        in_specs=[zspec(0), zspec(1), zspec(2),
                  pl.BlockSpec((3, B_SHORT, HY_SLAB), lambda j, b: (0, 0, s0 + j)),
                  pl.BlockSpec((3, 1, HY_SLAB), lambda j, b: (0, 0, s0 + j)),
                  pl.BlockSpec((2, FFT_KH, 2, n2c, HY_SLAB), lambda j, b: (0, 0, 0, 0, s0 + j),
                               pipeline_mode=pl.Buffered(1)),
                  pl.BlockSpec((2, HY_SLAB), lambda j, b: (0, s0 + j))] + [once(m) for m in mats],
        out_specs=pl.BlockSpec((seq, HY_SLAB), lambda j, b: (b, j)),
        out_shape=jax.ShapeDtypeStruct((lay.b * seq, nslab * HY_SLAB), F32),
        scratch_shapes=[pltpu.VMEM((seq_rows, HY_SLAB), F32), pltpu.VMEM((seq, HY_SLAB), F32),
                        pltpu.VMEM((FFT_KH * pitch, HY_SLAB), F32), pltpu.VMEM((FFT_KH * pitch, HY_SLAB), F32),
                        pltpu.VMEM((seq if col_mode else seq_rows, HY_SLAB), F32)],
        compiler_params=_cparams(("parallel", "parallel")),
        name="hyena_col" if col_mode else "hyena_row",
    )(z, z, z, jnp.transpose(conv_w, (1, 0, 2)), conv_b.reshape(3, 1, B_CH), spec, bias, *mats)


def _hyena(lay, z, conv_w, conv_b, spec_c, spec_x, bias, consts_c, consts_x):
    assert lay.n_ctx % lay.seq == 0
    half = B_CH // 2 // HY_SLAB
    args = (lay, z, conv_w, conv_b)
    yc = _hyena_fused(*args, spec_c, bias, consts_c, lay.ctx, 0, 0, 2 * half, False)
    yr = _hyena_fused(*args, spec_x, bias, consts_x, lay.seq, lay.n_ctx // lay.seq, 0, half, False)
    ycol = _hyena_fused(*args, spec_x, bias, consts_x, lay.seq, lay.n_ctx // lay.seq, half, half, True)
    return jnp.concatenate([yc, jnp.concatenate([yr, ycol], axis=1)], axis=0)


def _head_norm_gate(o, gate, norm_w, heads, width):
    outs = []
    for h in range(heads):
        oh = o[:, h * width:(h + 1) * width]
        y = oh * lax.rsqrt(jnp.mean(oh * oh, axis=-1, keepdims=True) + NORM_EPS) * norm_w
        outs.append(y * _silu(gate[:, h * width:(h + 1) * width]))
    return jnp.concatenate(outs, axis=1)


def _merge_kernel(x_ref, mod_ref, nw_ref, a0_ref, a1_ref, ag_ref, yb_ref, c0_ref, c1_ref, cx_ref, cz_ref,
                  d0_ref, d1_ref, dg_ref, an_ref, cd_ref, cn_ref, dn_ref, wg_ref, wb_ref, wo_ref, o_ref):
    xv = x_ref[...]
    m = mod_ref[...]
    hb = _prenorm(xv, nw_ref[1:2, :], m[3:4, :], m[4:5, :]).astype(BF16)
    ya = _head_norm_gate(a0_ref[...] + a1_ref[...], ag_ref[...], an_ref[...], A_HEADS, A_DV)
    yd = _head_norm_gate(d0_ref[...] + d1_ref[...], dg_ref[...], dn_ref[...], D_HEADS, D_DV)
    yc = (c0_ref[...] + c1_ref[...] + cd_ref[...] * cx_ref[...]) * _silu(cz_ref[...])
    gw = C_INNER // C_GROUPS
    cn = cn_ref[...]
    yc = jnp.concatenate(
        [yc[:, g * gw:(g + 1) * gw]
         * lax.rsqrt(jnp.mean(yc[:, g * gw:(g + 1) * gw] ** 2, axis=-1, keepdims=True) + NORM_EPS)
         * cn[:, g * gw:(g + 1) * gw] for g in range(C_GROUPS)], axis=1)
    acc = jnp.zeros(xv.shape, F32)
    for k, y in enumerate((ya, yb_ref[...], yc, yd)):
        gate = jax.nn.sigmoid(jnp.dot(hb, wg_ref[:, k * D_MODEL:(k + 1) * D_MODEL], preferred_element_type=F32))
        acc = acc + gate * jnp.dot(y.astype(BF16), wb_ref[k], preferred_element_type=F32)
    o_ref[...] = xv + m[5:6, :] * jnp.dot(acc.astype(BF16), wo_ref[...], preferred_element_type=F32)


def _merge(lay, x, mod, nw, a0, a1, ag, yb, c0, c1, cx, cz, d0, d1, dg, an, cd, cn, dn, wg, wb, wo):
    n, d = x.shape
    row = lambda w: pl.BlockSpec((TOK_TILE, w), lambda i: (i, 0))
    once = lambda a: pl.BlockSpec(a.shape, lambda i: (0,) * a.ndim, pipeline_mode=pl.Buffered(1))
    return pl.pallas_call(
        _merge_kernel,
        grid=(n // TOK_TILE,),
        in_specs=[row(d), pl.BlockSpec((None, N_MOD, d), lambda i: (lay.mod_index(i), 0, 0)), _full(nw.shape)]
                 + [row(BRANCH_W)] * 11 + [_full(an.shape), _full(cd.shape), _full(cn.shape), _full(dn.shape),
                                           once(wg), once(wb), once(wo)],
        out_specs=row(d),
        out_shape=jax.ShapeDtypeStruct((n, d), F32),
        compiler_params=_cparams(("parallel",)), name="merge",
    )(x, mod, nw, a0, a1, ag, yb, c0, c1, cx, cz, d0, d1, dg, an, cd, cn, dn, wg, wb, wo)


def _final_norm_kernel(x_ref, w_ref, o_ref):
    xv = x_ref[...]
    o_ref[...] = xv * lax.rsqrt(jnp.mean(xv * xv, axis=-1, keepdims=True) + NORM_EPS) * w_ref[...]


def _final_norm(lay, x, w):
    d = x.shape[1]
    return pl.pallas_call(
        _final_norm_kernel,
        grid=(lay.b * lay.tl,),
        in_specs=[pl.BlockSpec((ROW_TILE, d), lambda i: (lay.nct + i, 0)), _full((1, d))],
        out_specs=pl.BlockSpec((ROW_TILE, d), lambda i: (i, 0)),
        out_shape=jax.ShapeDtypeStruct((lay.b * lay.seq, d), F32),
        compiler_params=_cparams(("parallel",)), name="final_norm",
    )(x, w.reshape(1, d))


def kernel(x, c, ctx, c_ctx, w_ada, b_ada, norm_w, ffn_up, ffn_down, w_in, gdn_conv, gdn_a_log, gdn_dt_bias, gdn_norm, hy_conv_w, hy_conv_b, hy_w1, hy_b1, hy_w2, hy_b2, hy_wout, hy_freq, hy_bias, ssd_conv_w, ssd_conv_b, ssd_a_log, ssd_dt_bias, ssd_d, ssd_norm, gla_gk_w, gla_gk_b, gla_norm, w_branch, w_out, final_norm):
    b, seq, d = x.shape
    ctx_len = ctx.shape[1]
    depth = w_ada.shape[0]
    lay = _Layout(b, ctx_len, seq)
    consts_c = _dft_consts(2 * ctx_len // FFT_N1)
    consts_x = _dft_consts(2 * seq // FFT_N1)

    rp = -(-(1 + b) // 8) * 8
    cond = jnp.concatenate([c_ctx[None, :], c, jnp.zeros((rp - 1 - b, d), F32)], axis=0)
    mods = _ada(cond, w_ada, b_ada).reshape(depth, rp, N_MOD, d)

    xf = jnp.concatenate([ctx.reshape(b * ctx_len, d), x.reshape(b * seq, d)], axis=0)
    for l in range(depth):
        mod = mods[l]
        nw = norm_w[l]
        w_r, w_gate = _rearrange_w_in(w_in[l])
        xf = _ffn(lay, xf, mod, nw, ffn_up[l, 0].astype(BF16), ffn_down[l, 0].astype(BF16), 0)

        qkv, a_gate, zb, c_z, xbc, d_qkv, d_gate, small = _inproj(
            lay, xf, mod, nw, w_r, gdn_conv[l], ssd_conv_w[l], ssd_conv_b[l].reshape(1, C_XBC))
        a0, a1 = _gdn_scan(lay, qkv, small, gdn_a_log[l], gdn_dt_bias[l])
        c0, c1 = _ssd_scan(lay, xbc, small, ssd_a_log[l], ssd_dt_bias[l])
        d0, d1 = _gla_scan(lay, d_qkv, small, gla_gk_w[l], gla_gk_b[l])
        fargs = (hy_w1[l], hy_b1[l], hy_w2[l], hy_b2[l], hy_wout[l], hy_freq[l])
        spec_c = _hyena_spectrum(ctx_len, consts_c, *fargs)
        spec_x = _hyena_spectrum(seq, consts_x, *fargs)
        yb = _hyena(lay, zb, hy_conv_w[l], hy_conv_b[l], spec_c, spec_x, hy_bias[l], consts_c, consts_x)

        xf = _merge(lay, xf, mod, nw, a0, a1, a_gate, yb, c0, c1, xbc, c_z, d0, d1, d_gate,
                    gdn_norm[l].reshape(1, A_DV), jnp.repeat(ssd_d[l], C_HEADDIM).reshape(1, C_INNER),
                    ssd_norm[l].reshape(1, C_INNER), gla_norm[l].reshape(1, D_DV),
                    w_gate, w_branch[l].astype(BF16), w_out[l].astype(BF16))
        xf = _ffn(lay, xf, mod, nw, ffn_up[l, 1].astype(BF16), ffn_down[l, 1].astype(BF16), 2)
    return _final_norm(lay, xf, final_norm).reshape(b, seq, d)
```

```python
import functools
import math

import numpy as np
import jax
import jax.numpy as jnp
from jax import lax
from jax.experimental import pallas as pl
from jax.experimental.pallas import tpu as pltpu

F32 = jnp.float32
BF16 = jnp.bfloat16
HI = lax.Precision.HIGHEST

D_MODEL = 1024
GRID_W = 64
CHUNK = 64
NORM_EPS = 1e-6
N_MOD = 9
D_FF = 2816
SHORT_CONV = 5

A_HEADS, A_DK, A_DV = 4, 128, 128
A_QKV = A_HEADS * (2 * A_DK + A_DV)
B_CH, B_SHORT, B_EMB, B_FFN, B_INNER_MLPS = 512, 3, 33, 64, 2
B_WINDOW_SHIFT, B_DECAY_SHORT_PCT, B_DECAY_LONG_PCT, B_DECAY_TARGET = 0.05, 0.3, 1.5, 1e-2
C_HEADS, C_HEADDIM, C_GROUPS, C_STATE = 8, 64, 2, 64
C_INNER = C_HEADS * C_HEADDIM
C_XBC = C_INNER + 2 * C_GROUPS * C_STATE
D_HEADS, D_DK, D_DV, D_RANK = 4, 64, 128, 16
D_GATE_NORM = 16.0
N_BRANCH, BRANCH_W = 4, 512

IN_SIZES = (A_QKV, A_HEADS * A_DV, 2 * A_HEADS, 2 * A_HEADS, 3 * B_CH, C_INNER, C_XBC, 2 * C_HEADS,
            D_HEADS * (2 * D_DK + D_DV), D_HEADS * D_DV, 2 * D_RANK, N_BRANCH * D_MODEL)
IN_OFFS = tuple(int(v) for v in np.cumsum((0,) + IN_SIZES))

ROW_TILE = 256
TOK_TILE = 512
SMALL_W = 128
VMEM_LIMIT = 56 * 1024 * 1024


def _cparams(sem):
    return pltpu.CompilerParams(dimension_semantics=sem, vmem_limit_bytes=VMEM_LIMIT)


def _mm(a, b):
    return jnp.dot(a.astype(BF16), b.astype(BF16), preferred_element_type=F32)


def _mm_nt(a, b):
    return lax.dot_general(a.astype(BF16), b.astype(BF16), (((1,), (1,)), ((), ())),
                           preferred_element_type=F32)


def _mm_hi(a, b):
    return jnp.dot(a, b, precision=HI, preferred_element_type=F32)


def _bf16_parts(v, parts):
    out, rest = [], v
    for _ in range(parts):
        hi = rest.astype(BF16)
        out.append(hi)
        rest = rest - hi.astype(F32)
    return out


def _mm_3x(a, b):
    ah, al = _bf16_parts(a, 2)
    bh, bl = _bf16_parts(b, 2)
    dot = functools.partial(jnp.dot, preferred_element_type=F32)
    return dot(ah, bh) + (dot(ah, bl) + dot(al, bh))


def _mm_sel_l(sel, v, parts=3):
    sb = sel.astype(BF16)
    return sum(jnp.dot(sb, p, preferred_element_type=F32) for p in _bf16_parts(v, parts))


def _mm_sel_r(v, sel, parts=3):
    sb = sel.astype(BF16)
    return sum(jnp.dot(p, sb, preferred_element_type=F32) for p in _bf16_parts(v, parts))


def _silu(v):
    return v * jax.nn.sigmoid(v)


def _softplus(v):
    return jnp.maximum(v, 0.0) + jnp.log1p(jnp.exp(-jnp.abs(v)))


def _log_sigmoid(v):
    return jnp.minimum(v, 0.0) - jnp.log1p(jnp.exp(-jnp.abs(v)))


def _prenorm(xv, gain, shift, scale):
    ms = jnp.mean(xv * xv, axis=-1, keepdims=True)
    return (xv * lax.rsqrt(ms + NORM_EPS) * gain) * (1.0 + scale) + shift


def _full(shape):
    nd = len(shape)
    return pl.BlockSpec(shape, lambda *_: (0,) * nd)


class _Layout:
    def __init__(self, batch, ctx_len, seq):
        assert ctx_len % ROW_TILE == 0 and seq % ROW_TILE == 0
        assert (batch * ctx_len) % TOK_TILE == 0 and seq % TOK_TILE == 0
        self.b, self.ctx, self.seq = batch, ctx_len, seq
        self.n_ctx = batch * ctx_len
        self.n = batch * (ctx_len + seq)
        self.tc, self.tl = ctx_len // ROW_TILE, seq // ROW_TILE
        self.nct = batch * self.tc
        self.tiles = self.n // ROW_TILE

    def mod_index(self, i, tile=TOK_TILE):
        nct = self.n_ctx // tile
        return jnp.where(i < nct, 0, 1 + (i - nct) // (self.seq // tile))

    def seg_first(self, i):
        return jnp.where(i < self.nct, lax.rem(i, self.tc) == 0, lax.rem(i - self.nct, self.tl) == 0)

    def seg_last(self, i):
        return jnp.where(i < self.nct, lax.rem(i, self.tc) == self.tc - 1,
                         lax.rem(i - self.nct, self.tl) == self.tl - 1)

    def fwd_tile(self, b, s):
        return jnp.where(s < self.tc, b * self.tc + s, self.nct + b * self.tl + (s - self.tc))

    def rev_tile(self, b, s):
        return jnp.where(s < self.tc, b * self.tc + (self.tc - 1 - s),
                         self.nct + b * self.tl + (self.tl - 1 - (s - self.tc)))


def _ada_kernel(c_ref, w_ref, b_ref, o_ref):
    o_ref[...] = _mm_hi(_silu(c_ref[...]), w_ref[...]) + b_ref[...]


def _ada(cond, w_ada, b_ada):
    depth, d, nm = w_ada.shape
    rp = cond.shape[0]
    tn = 1152
    return pl.pallas_call(
        _ada_kernel,
        grid=(depth, nm // tn),
        in_specs=[_full((rp, d)),
                  pl.BlockSpec((None, d, tn), lambda l, j: (l, 0, j)),
                  pl.BlockSpec((None, 1, tn), lambda l, j: (l, 0, j))],
        out_specs=pl.BlockSpec((None, rp, tn), lambda l, j: (l, 0, j)),
        out_shape=jax.ShapeDtypeStruct((depth, rp, nm), F32),
        compiler_params=_cparams(("parallel", "parallel")), name="ada",
    )(cond, w_ada, b_ada.reshape(depth, 1, nm))


FF_CHUNK = 256


def _ffn_kernel(x_ref, mod_ref, nw_ref, wup_ref, wdn_ref, *rest, sub):
    o_ref = rest[-1]
    xv = x_ref[...]
    m = mod_ref[...]
    h = _prenorm(xv, nw_ref[sub:sub + 1, :], m[3 * sub:3 * sub + 1, :], m[3 * sub + 1:3 * sub + 2, :])
    hb = h.astype(BF16)
    acc = jnp.zeros(xv.shape, F32)
    for c in range(D_FF // FF_CHUNK):
        lo = c * FF_CHUNK
        a = jnp.dot(hb, wup_ref[:, lo:lo + FF_CHUNK], preferred_element_type=F32)
        g = jnp.dot(hb, wup_ref[:, D_FF + lo:D_FF + lo + FF_CHUNK], preferred_element_type=F32)
        acc = acc + jnp.dot((_silu(a) * g).astype(BF16), wdn_ref[lo:lo + FF_CHUNK, :],
                            preferred_element_type=F32)
    out = xv + 0.5 * m[3 * sub + 2:3 * sub + 3, :] * acc
    if len(rest) == 2:
        out = out * lax.rsqrt(jnp.mean(out * out, axis=-1, keepdims=True) + NORM_EPS) * rest[0][...]
    o_ref[...] = out


def _ffn(lay, x, mod, nw, wup, wdn, sub, final_w=None):
    n, d = x.shape
    t0 = 0 if final_w is None else lay.n_ctx // TOK_TILE
    extra = [] if final_w is None else [final_w.reshape(1, d)]
    return pl.pallas_call(
        functools.partial(_ffn_kernel, sub=sub),
        grid=(n // TOK_TILE - t0,),
        in_specs=[pl.BlockSpec((TOK_TILE, d), lambda i: (i + t0, 0)),
                  pl.BlockSpec((None, N_MOD, d), lambda i: (lay.mod_index(i + t0), 0, 0)),
                  _full(nw.shape),
                  pl.BlockSpec(wup.shape, lambda i: (0, 0), pipeline_mode=pl.Buffered(1)),
                  pl.BlockSpec(wdn.shape, lambda i: (0, 0), pipeline_mode=pl.Buffered(1))]
                 + [_full(e.shape) for e in extra],
        out_specs=pl.BlockSpec((TOK_TILE, d), lambda i: (i, 0)),
        out_shape=jax.ShapeDtypeStruct((n - t0 * TOK_TILE, d), F32),
        compiler_params=_cparams(("parallel",)), name="ffn",
    )(x, mod, nw, wup, wdn, *extra)


PROJ_W = (A_QKV, A_HEADS * A_DV, 3 * B_CH, C_INNER, C_XBC, D_HEADS * (2 * D_DK + D_DV), D_HEADS * D_DV, SMALL_W)
PROJ_O = tuple(int(v) for v in np.cumsum((0,) + PROJ_W))


def _rearrange_w_in(w_in):
    o = IN_OFFS
    small = jnp.concatenate([w_in[:, o[2]:o[4]], w_in[:, o[7]:o[8]], w_in[:, o[10]:o[11]],
                             jnp.zeros((w_in.shape[0], SMALL_W - 64), w_in.dtype)], axis=1)
    w_r = jnp.concatenate([w_in[:, o[0]:o[1]], w_in[:, o[1]:o[2]], w_in[:, o[4]:o[5]], w_in[:, o[5]:o[6]],
                           w_in[:, o[6]:o[7]], w_in[:, o[8]:o[9]], w_in[:, o[9]:o[10]], small], axis=1)
    return w_r.astype(BF16), w_in[:, o[11]:o[12]].astype(BF16)


PROJ_HALO = 16
PROJ_CONV = (0, 4)


def _inproj_kernel(xp_ref, x_ref, xn_ref, mod_ref, nw_ref, w_ref, cwa_ref, cwc_ref, cbc_ref, *rest, lay):
    o_refs, zbuf = rest[:-1], rest[-1]
    i = pl.program_id(0)
    m = mod_ref[...]
    norm = lambda r: _prenorm(r[...], nw_ref[1:2, :], m[3:4, :], m[4:5, :]).astype(BF16)
    hb = norm(x_ref)
    hb_all = jnp.concatenate([norm(xp_ref), hb, norm(xn_ref)], axis=0)
    keep_p = jnp.where(lay.seg_first(i), 0.0, 1.0)
    keep_n = jnp.where(lay.seg_last(i), 0.0, 1.0)
    convs = {PROJ_CONV[0]: (cwa_ref, None), PROJ_CONV[1]: (cwc_ref, cbc_ref)}
    for k, o_ref in enumerate(o_refs):
        wk = w_ref[:, PROJ_O[k]:PROJ_O[k + 1]]
        if k not in convs:
            o_ref[...] = jnp.dot(hb, wk, preferred_element_type=F32)
            continue
        cw_ref, cb_ref = convs[k]
        width = PROJ_W[k]
        z = jnp.dot(hb_all, wk, preferred_element_type=F32)
        zbuf[0:PROJ_HALO, 0:width] = z[:PROJ_HALO] * keep_p
        zbuf[PROJ_HALO:PROJ_HALO + ROW_TILE, 0:width] = z[PROJ_HALO:PROJ_HALO + ROW_TILE]
        zbuf[PROJ_HALO + ROW_TILE:, 0:width] = z[PROJ_HALO + ROW_TILE:] * keep_n
        taps = cw_ref.shape[0]
        left = (taps - 1) // 2
        acc = None
        for t in range(taps):
            off = t - left
            term = zbuf[PROJ_HALO + off:PROJ_HALO + off + ROW_TILE, 0:width] * cw_ref[t:t + 1, :]
            acc = term if acc is None else acc + term
        if cb_ref is not None:
            acc = acc + cb_ref[...]
        o_ref[...] = _silu(acc)


def _inproj(lay, x, mod, nw, w_r, conv_a, conv_c, conv_c_bias):
    n, d = x.shape
    rh = ROW_TILE // PROJ_HALO
    lasth = n // PROJ_HALO - 1
    return pl.pallas_call(
        functools.partial(_inproj_kernel, lay=lay),
        grid=(n // ROW_TILE,),
        in_specs=[pl.BlockSpec((PROJ_HALO, d), lambda i: (jnp.maximum(i * rh - 1, 0), 0)),
                  pl.BlockSpec((ROW_TILE, d), lambda i: (i, 0)),
                  pl.BlockSpec((PROJ_HALO, d), lambda i: (jnp.minimum((i + 1) * rh, lasth), 0)),
                  pl.BlockSpec((None, N_MOD, d), lambda i: (lay.mod_index(i, ROW_TILE), 0, 0)),
                  _full(nw.shape),
                  pl.BlockSpec(w_r.shape, lambda i: (0, 0), pipeline_mode=pl.Buffered(1)),
                  _full(conv_a.shape), _full(conv_c.shape), _full(conv_c_bias.shape)],
        out_specs=[pl.BlockSpec((ROW_TILE, w), lambda i: (i, 0)) for w in PROJ_W],
        out_shape=[jax.ShapeDtypeStruct((n, w), F32) for w in PROJ_W],
        scratch_shapes=[pltpu.VMEM((ROW_TILE + 2 * PROJ_HALO, max(PROJ_W[k] for k in PROJ_CONV)), F32)],
        compiler_params=_cparams(("parallel",)), name="inproj",
    )(x, x, x, mod, nw, w_r, conv_a, conv_c, conv_c_bias)


def _tri_masks(rev):
    ii = lax.broadcasted_iota(jnp.int32, (CHUNK, CHUNK), 0)
    jj = lax.broadcasted_iota(jnp.int32, (CHUNK, CHUNK), 1)
    return (ii <= jj, ii < jj) if rev else (ii >= jj, ii > jj)


def _masked_decay(col, row, incl):
    return jnp.where(incl, jnp.exp(jnp.where(incl, col - row, 0.0)), 0.0)


def _scan_call(lay, parts):
    steps = lay.tc + lay.tl
    specs, args, out_specs, out_shape, scratch, counts = [], [], [], [], [], []
    for _, ins_tiled, ins_full, out_w, scr in parts:
        n0 = len(specs)
        for walk in (lay.fwd_tile, lay.rev_tile):
            for a in ins_tiled:
                specs.append(pl.BlockSpec((ROW_TILE, a.shape[1]), lambda b, s, walk=walk: (walk(b, s), 0)))
                args.append(a)
            out_specs.append(pl.BlockSpec((ROW_TILE, out_w), lambda b, s, walk=walk: (walk(b, s), 0)))
            out_shape.append(jax.ShapeDtypeStruct((lay.n, out_w), F32))
        for a in ins_full:
            specs.append(_full(a.shape))
            args.append(a)
        counts.append((len(specs) - n0, len(scr)))
        scratch += list(scr)

    def body(*refs):
        n_in, n_out = len(specs), len(out_specs)
        i, s = 0, n_in + n_out
        for p, (kern, *_) in enumerate(parts):
            ci, cs = counts[p]
            kern(*refs[i:i + ci], *refs[n_in + 2 * p:n_in + 2 * p + 2], *refs[s:s + cs])
            i, s = i + ci, s + cs

    res = pl.pallas_call(
        body,
        grid=(lay.b, steps),
        in_specs=specs,
        out_specs=out_specs,
        out_shape=out_shape,
        scratch_shapes=scratch,
        compiler_params=_cparams(("arbitrary", "arbitrary")), name="scans",
    )(*args)
    return [tuple(res[2 * p:2 * p + 2]) for p in range(len(parts))]


def _ssd_kernel(xf_ref, sf_ref, xr_ref, sr_ref, alog_ref, dtb_ref, alog_t_ref, dtb_t_ref, exp_ref,
                of_ref, or_ref, st_ref):
    @pl.when(pl.program_id(1) == 0)
    def _():
        st_ref[...] = jnp.zeros(st_ref.shape, F32)

    hpg = C_HEADS // C_GROUPS
    gw = hpg * C_HEADDIM
    nch = ROW_TILE // CHUNK
    refs = ((xf_ref, sf_ref, of_ref), (xr_ref, sr_ref, or_ref))
    units, pre = [], {}
    for d, (x_ref, s_ref, _) in enumerate(refs):
        incl, _ = _tri_masks(d == 1)
        m_col = incl.astype(F32)
        m_row = m_col.T
        sm = s_ref[...]
        sm_t = sm.T
        dt_all = _softplus(sm[:, 16:32] + dtb_ref[...])
        a_all = -jnp.exp(alog_ref[...]) * dt_all
        a_t_all = -jnp.exp(alog_t_ref[...]) * _softplus(sm_t[16:32, :] + dtb_t_ref[...])
        bm_t = x_ref[:, C_INNER:C_INNER + C_GROUPS * C_STATE].T
        for c in range(nch):
            rows = slice(c * CHUNK, (c + 1) * CHUNK)
            units.append((d, c))
            pre[d, c] = dict(incl=incl, m_col=m_col, m_row=m_row, a_ch=a_all[rows], a_t=a_t_all[:, rows],
                             dt=dt_all[rows], bm_t=bm_t[:, rows], e=exp_ref[d], xs=x_ref[rows, 0:C_INNER],
                             bm=x_ref[rows, C_INNER:C_INNER + C_GROUPS * C_STATE],
                             cm=x_ref[rows, C_INNER + C_GROUPS * C_STATE:C_XBC])
    for u in units:
        w = pre[u]
        w["ac"] = _mm_sel_l(w["m_col"], w["a_ch"])
        w["ac_t"] = _mm_sel_r(w["a_t"], w["m_row"])
        w["alast"] = jnp.sum(w["a_ch"], axis=0, keepdims=True)
    for u in units:
        w = pre[u]
        w["xdt"] = w["xs"] * _mm_sel_r(w["dt"], w["e"])
        w["eac_e"] = _mm_sel_r(jnp.exp(w["ac"]), w["e"])
        w["xd"] = w["xdt"] * _mm_sel_r(jnp.exp(w["alast"] - w["ac"]), w["e"])
        w["dle"] = _mm_sel_r(jnp.broadcast_to(jnp.exp(w["alast"]), (8, 2 * C_HEADS)), w["e"])[0:1]
    grp = lambda a, g: a[:, g * C_STATE:(g + 1) * C_STATE]
    cb = {(u, g): _mm_nt(grp(pre[u]["cm"], g), grp(pre[u]["bm"], g)) for u in units for g in range(C_GROUPS)}
    upd = {(u, g): _mm(pre[u]["bm_t"][g * C_STATE:(g + 1) * C_STATE], pre[u]["xd"][:, g * gw:(g + 1) * gw])
           for u in units for g in range(C_GROUPS)}
    y_diag = {}
    for u in units:
        w = pre[u]
        for h in range(C_HEADS):
            col = u[0] * C_HEADS + h
            seg = _masked_decay(w["ac"][:, col:col + 1], w["ac_t"][col:col + 1, :], w["incl"])
            y_diag[u, h] = _mm(cb[u, h // hpg] * seg, w["xdt"][:, h * C_HEADDIM:(h + 1) * C_HEADDIM])
    st = {(d, g): st_ref[d, g] for d in range(2) for g in range(C_GROUPS)}
    entry = {}
    for p in range(nch):
        for d in range(2):
            u = (d, (nch - 1 - p) if d == 1 else p)
            for g in range(C_GROUPS):
                entry[u, g] = st[d, g]
                st[d, g] = pre[u]["dle"][:, g * gw:(g + 1) * gw] * st[d, g] + upd[u, g]
    for d in range(2):
        for g in range(C_GROUPS):
            st_ref[d, g] = st[d, g]
    for u in units:
        y_off = [_mm(grp(pre[u]["cm"], g), entry[u, g]) for g in range(C_GROUPS)]
        refs[u[0]][2][u[1] * CHUNK:(u[1] + 1) * CHUNK, :] = (
            jnp.concatenate([y_diag[u, h] for h in range(C_HEADS)], axis=1)
            + pre[u]["eac_e"] * jnp.concatenate(y_off, axis=1))


def _ssd_scan(lay, xbc, small, a_log, dt_bias):
    expand = np.zeros((2, 2 * C_HEADS, C_INNER), np.float32)
    for d in range(2):
        for h in range(C_HEADS):
            expand[d, d * C_HEADS + h, h * C_HEADDIM:(h + 1) * C_HEADDIM] = 1.0
    al = a_log.reshape(1, 2 * C_HEADS)
    db = dt_bias.reshape(1, 2 * C_HEADS)
    return (_ssd_kernel, (xbc, small), (al, db, al.T, db.T, jnp.asarray(expand)), C_INNER,
            [pltpu.VMEM((2, C_GROUPS, C_STATE, C_INNER // C_GROUPS), F32)])


def _gla_kernel(xf_ref, sf_ref, xr_ref, sr_ref, gkw_ref, gkb_ref, of_ref, or_ref, st_ref):
    @pl.when(pl.program_id(1) == 0)
    def _():
        st_ref[...] = jnp.zeros(st_ref.shape, F32)

    nk = D_HEADS * D_DK
    nch = ROW_TILE // CHUNK
    refs = ((xf_ref, sf_ref, of_ref), (xr_ref, sr_ref, or_ref))
    units, pre = [], {}
    for d, (x_ref, s_ref, _) in enumerate(refs):
        incl, _ = _tri_masks(d == 1)
        m_col = incl.astype(F32)
        lr = s_ref[:, 32 + D_RANK * d:32 + D_RANK * (d + 1)]
        g_all = _log_sigmoid(_mm_3x(lr, gkw_ref[d]) + gkb_ref[d:d + 1, :]) / D_GATE_NORM
        v_t = x_ref[:, 2 * nk:].T
        for c in range(nch):
            rows = slice(c * CHUNK, (c + 1) * CHUNK)
            g_ch = g_all[rows]
            gc = _mm_sel_l(m_col, g_ch)
            glast = jnp.sum(g_ch, axis=0, keepdims=True)
            q = x_ref[rows, 0:nk] * (D_DK ** -0.5)
            k = x_ref[rows, nk:2 * nk]
            v = x_ref[rows, 2 * nk:]
            kd = k * jnp.exp(glast - gc)
            qd = q * jnp.exp(gc)
            qr = q * jnp.exp(gc - glast)
            dlast = jnp.exp(glast)
            for h in range(D_HEADS):
                ks = slice(h * D_DK, (h + 1) * D_DK)
                vs = slice(h * D_DV, (h + 1) * D_DV)
                units.append((d, c, h))
                pre[d, c, h] = dict(incl=incl, qr=qr[:, ks], kd=kd[:, ks], qd=qd[:, ks], v=v[:, vs],
                                    v_t=v_t[vs, rows], dl=dlast[:, ks])
    aqk = [jnp.where(pre[u]["incl"], _mm_nt(pre[u]["qr"], pre[u]["kd"]), 0.0) for u in units]
    upd = {u: _mm(pre[u]["v_t"], pre[u]["kd"]) for u in units}
    intra = {u: _mm(a, pre[u]["v"]) for u, a in zip(units, aqk)}
    heads = [(d, h) for d in range(2) for h in range(D_HEADS)]
    st = {dh: st_ref[dh[0], dh[1]] for dh in heads}
    entry = {}
    for p in range(nch):
        for d, h in heads:
            u = (d, (nch - 1 - p) if d == 1 else p, h)
            entry[u] = st[d, h]
            st[d, h] = st[d, h] * pre[u]["dl"] + upd[u]
    for dh in heads:
        st_ref[dh[0], dh[1]] = st[dh]
    inter = {u: _mm_nt(pre[u]["qd"], entry[u]) for u in units}
    for d, (_, _, o_ref) in enumerate(refs):
        for c in range(nch):
            o_ref[c * CHUNK:(c + 1) * CHUNK, :] = jnp.concatenate(
                [intra[d, c, h] + inter[d, c, h] for h in range(D_HEADS)], axis=1)


def _gla_scan(lay, qkv, small, gk_w, gk_b):
    return (_gla_kernel, (qkv, small), (gk_w, gk_b), D_HEADS * D_DV, [pltpu.VMEM((2, D_HEADS, D_DV, D_DK), F32)])


TRI_BLOCK = 16
GDN_GROUP = 32


def _unit_lower_inverse(nms, eye, blk):
    dg = [jnp.where(blk, nm, 0.0) for nm in nms]
    off = [nm - d for nm, d in zip(nms, dg)]
    t0 = [eye - d for d in dg]
    p = [_mm(d, d) for d in dg]
    for it in range(3):
        t0 = [t + _mm(t, q) for t, q in zip(t0, p)]
        if it < 2:
            p = [_mm(q, q) for q in p]
    m = [_mm(t, o) for t, o in zip(t0, off)]
    m2 = [_mm(a, a) for a in m]
    r = [eye - a for a in m]
    r = [a + _mm(a, b) for a, b in zip(r, m2)]
    return [_mm(a, t) for a, t in zip(r, t0)]


def _gdn_kernel(xf_ref, sf_ref, xr_ref, sr_ref, alog_ref, dtb_ref, alog_t_ref, dtb_t_ref,
                of_ref, or_ref, st_ref):
    @pl.when(pl.program_id(1) == 0)
    def _():
        st_ref[...] = jnp.zeros(st_ref.shape, F32)

    nk = A_HEADS * A_DK
    nch = ROW_TILE // CHUNK
    ii = lax.broadcasted_iota(jnp.int32, (CHUNK, CHUNK), 0)
    jj = lax.broadcasted_iota(jnp.int32, (CHUNK, CHUNK), 1)
    eye = (ii == jj).astype(F32)
    blk = (ii // TRI_BLOCK) == (jj // TRI_BLOCK)
    refs = ((xf_ref, sf_ref, of_ref), (xr_ref, sr_ref, or_ref))

    units, pre = [], {}
    for d, (x_ref, s_ref, _) in enumerate(refs):
        incl, strict = _tri_masks(d == 1)
        m_col = incl.astype(F32)
        m_row = m_col.T
        sm = s_ref[...]
        sm_t = sm.T
        beta_all = jax.nn.sigmoid(sm[:, 0:8])
        g_all = -jnp.exp(alog_ref[...]) * _softplus(sm[:, 8:16] + dtb_ref[...])
        g_t_all = -jnp.exp(alog_t_ref[...]) * _softplus(sm_t[8:16, :] + dtb_t_ref[...])
        qn, kn = [], []
        for h in range(A_HEADS):
            qh = x_ref[:, h * A_DK:(h + 1) * A_DK]
            kh = x_ref[:, nk + h * A_DK:nk + (h + 1) * A_DK]
            qn.append(qh * lax.rsqrt(jnp.sum(qh * qh, axis=-1, keepdims=True) + NORM_EPS) * (A_DK ** -0.5))
            kn.append(kh * lax.rsqrt(jnp.sum(kh * kh, axis=-1, keepdims=True) + NORM_EPS))
        kn_t = [kh.T for kh in kn]
        for c in range(nch):
            rows = slice(c * CHUNK, (c + 1) * CHUNK)
            g_ch = g_all[rows]
            gc = _mm_sel_l(m_col, g_ch)
            gc_t = _mm_sel_r(g_t_all[:, rows], m_row)
            glast = jnp.sum(g_ch, axis=0, keepdims=True)
            for h in range(A_HEADS):
                col = d * A_HEADS + h
                gcc, gcr, gl = gc[:, col:col + 1], gc_t[col:col + 1, :], glast[:, col:col + 1]
                bh = beta_all[rows, col:col + 1]
                kh = kn[h][rows]
                units.append((d, c, h))
                pre[d, c, h] = dict(
                    dmask=_masked_decay(gcc, gcr, incl), strict=strict, egc=jnp.exp(gcc), kh=kh, kb=kh * bh,
                    qh=qn[h][rows], vb=x_ref[rows, 2 * nk + h * A_DV:2 * nk + (h + 1) * A_DV] * bh,
                    kg_t=kn_t[h][:, rows] * jnp.exp(gl - gcr), dl=jnp.exp(gl))
    for g0 in range(0, len(units), GDN_GROUP):
        us = units[g0:g0 + GDN_GROUP]
        kk = [_mm_nt(pre[u]["kb"], pre[u]["kh"]) for u in us]
        qk = [_mm_nt(pre[u]["qh"], pre[u]["kh"]) for u in us]
        tinv = _unit_lower_inverse(
            [jnp.where(pre[u]["strict"], a * pre[u]["dmask"], 0.0) for u, a in zip(us, kk)], eye, blk)
        uw = [_mm(t, jnp.concatenate([pre[u]["vb"], pre[u]["kb"] * pre[u]["egc"]], axis=1))
              for u, t in zip(us, tinv)]
        for u, a, b in zip(us, uw, qk):
            pre[u]["u"] = a[:, :A_DV]
            pre[u]["wq"] = jnp.concatenate([a[:, A_DV:], pre[u]["qh"] * pre[u]["egc"]], axis=0)
            pre[u]["aqk"] = b * pre[u]["dmask"]

    heads = [(d, h) for d in range(2) for h in range(A_HEADS)]
    st = {dh: st_ref[dh[0], dh[1]] for dh in heads}
    for p in range(nch):
        us = [(d, (nch - 1 - p) if d == 1 else p, h) for d, h in heads]
        ws = [_mm(pre[u]["wq"], st[u[0], u[2]]) for u in us]
        v_new = [pre[u]["u"] - w[:CHUNK] for u, w in zip(us, ws)]
        outs = [w[CHUNK:] + _mm(pre[u]["aqk"], v) for u, w, v in zip(us, ws, v_new)]
        for u, v in zip(us, v_new):
            st[u[0], u[2]] = pre[u]["dl"] * st[u[0], u[2]] + _mm(pre[u]["kg_t"], v)
        for d, (_, _, o_ref) in enumerate(refs):
            c = us[d * A_HEADS][1]
            o_ref[c * CHUNK:(c + 1) * CHUNK, :] = jnp.concatenate(outs[d * A_HEADS:(d + 1) * A_HEADS], axis=1)
    for dh in heads:
        st_ref[dh[0], dh[1]] = st[dh]


def _gdn_scan(lay, qkv, small, a_log, dt_bias):
    al = a_log.reshape(1, 2 * A_HEADS)
    db = dt_bias.reshape(1, 2 * A_HEADS)
    return (_gdn_kernel, (qkv, small), (al, db, al.T, db.T), A_HEADS * A_DV,
            [pltpu.VMEM((2, A_HEADS, A_DK, A_DV), F32)])


FFT_N1 = 64
FFT_KB = 8


def _dft_consts(n2):
    n1 = FFT_N1
    m = n1 * n2
    k1 = np.arange(n1, dtype=np.float64)
    ang1 = 2.0 * np.pi * np.outer(k1, k1) / n1
    f1_full = np.concatenate([np.cos(ang1), -np.sin(ang1)], axis=0)
    f1 = f1_full[:, :n1 // 2]
    f3 = np.concatenate([np.cos(ang1[:, :n1 // 2]).T, -np.sin(ang1[:, :n1 // 2]).T], axis=1) / m
    j = np.arange(n2, dtype=np.float64)
    theta = 2.0 * np.pi * (np.outer(j, j)[None] / n2 + (k1[:, None, None] * j[None, None, :]) / m)
    wr, wi = np.cos(theta), -np.sin(theta)
    w2 = np.concatenate([np.concatenate([wr, -wi], axis=2), np.concatenate([wi, wr], axis=2)], axis=1)
    phi = 2.0 * np.pi * (np.outer(j, j)[None] / n2 + (k1[:, None, None] * j[None, :, None]) / m)
    vr, vi = np.cos(phi), np.sin(phi)
    v2 = np.concatenate([np.concatenate([vr, -vi], axis=2), np.concatenate([vi, vr], axis=2)], axis=1)
    return f1_full, f1, w2, v2, f3


FFT_KV = FFT_N1 // 2 + 1
FFT_KH = 40


def _half_spectrum_consts(consts):
    _, f1, w2, v2, f3 = consts
    n1 = FFT_N1
    f1h = np.zeros((2 * FFT_KH, n1 // 2))
    f1h[:FFT_KV] = f1[:FFT_KV]
    f1h[FFT_KH:FFT_KH + FFT_KV] = f1[n1:n1 + FFT_KV]
    w2h = np.zeros((FFT_KH,) + w2.shape[1:])
    w2h[:FFT_KV] = w2[:FFT_KV]
    v2h = np.zeros((FFT_KH,) + v2.shape[1:])
    v2h[:FFT_KV] = v2[:FFT_KV]
    weight = np.full((FFT_KV,), 2.0)
    weight[0] = weight[-1] = 1.0
    f3h = np.zeros((n1 // 2, 2 * FFT_KH))
    f3h[:, :FFT_KV] = f3[:, :FFT_KV] * weight
    f3h[:, FFT_KH:FFT_KH + FFT_KV] = f3[:, n1:n1 + FFT_KV] * weight
    return f1h, w2h, v2h, f3h


def _left_mm_kernel(w_ref, x_ref, o_ref):
    o_ref[...] = _mm_3x(w_ref[...], x_ref[...])


def _left_mm(wm, x):
    bs, kdim, ncols = x.shape
    mdim = wm.shape[0]
    tn = min(ncols, 4096)
    xspec = lambda rows: pl.BlockSpec((None, rows, tn), lambda b, j: (b, 0, j))
    return pl.pallas_call(
        _left_mm_kernel, grid=(bs, ncols // tn), in_specs=[_full(wm.shape), xspec(kdim)], out_specs=xspec(mdim),
        out_shape=jax.ShapeDtypeStruct((bs, mdim, ncols), F32),
        compiler_params=_cparams(("parallel", "parallel")), name="filter_dft_outer",
    )(wm, x)


def _filt_mid_kernel(a_ref, w_ref, o_ref, *, n2):
    for kk in range(FFT_KB):
        a = jnp.concatenate([a_ref[0, kk], a_ref[1, kk]], axis=0)
        xf = _mm_3x(w_ref[kk], a)
        o_ref[kk, 0] = xf[:n2]
        o_ref[kk, 1] = xf[n2:]


def _filt_mid_stage(a, w2):
    bs, _, n1, n2, ch = a.shape
    return pl.pallas_call(
        functools.partial(_filt_mid_kernel, n2=n2),
        grid=(bs, n1 // FFT_KB),
        in_specs=[pl.BlockSpec((None, 2, FFT_KB, n2, ch), lambda b, j: (b, 0, j, 0, 0)),
                  pl.BlockSpec((FFT_KB, 2 * n2, 2 * n2), lambda b, j: (j, 0, 0))],
        out_specs=pl.BlockSpec((None, FFT_KB, 2, n2, ch), lambda b, j: (b, j, 0, 0, 0)),
        out_shape=jax.ShapeDtypeStruct((bs, n1, 2, n2, ch), F32),
        compiler_params=_cparams(("parallel", "parallel")), name="filter_dft_mid",
    )(a, w2)


FILT_ROWS = 512


def _filter_kernel(z_ref, win_ref, sel_ref, w1_ref, b1_ref, w2_ref, b2_ref, wo_ref, fr_ref, o_ref):
    fr = fr_ref[...]
    h = jnp.sin(fr * (_mm_3x(z_ref[...], w1_ref[...]) + b1_ref[...]))
    for i in range(B_INNER_MLPS):
        h = jnp.sin(fr * (_mm_3x(h, w2_ref[i]) + b2_ref[i]))
    ho = _mm_3x(h, wo_ref[...])
    fwd = sel_ref[...] > 0.5
    win = win_ref[...]
    for o in range(2):
        base = o * 2 * B_CH
        o_ref[o] = jnp.where(fwd, ho[:, base:base + B_CH], ho[:, base + B_CH:base + 2 * B_CH]) * win


def _hyena_filter_time(l, w1, b1, w2, b2, w_out, freq):
    bands = (B_EMB - 1) // 2
    circ = jnp.arange(2 * l)
    pos = jnp.where(circ < l, circ, 2 * l - circ)
    pos = jnp.where(circ == l, 0, pos)
    t = (jnp.linspace(0.0, 1.0, l, dtype=F32)[pos])[:, None]
    ang = 2.0 * math.pi * pos.astype(F32)[:, None] / l
    fr = jnp.linspace(1e-4, bands - 1, bands, dtype=F32)[None, :]
    z = jnp.concatenate([t, jnp.cos(fr * ang), -jnp.sin(fr * ang)], axis=-1)
    z = jnp.pad(z, ((0, 0), (0, 128 - B_EMB)))
    max_decay = math.log(B_DECAY_TARGET) / B_DECAY_SHORT_PCT
    min_decay = math.log(B_DECAY_TARGET) / B_DECAY_LONG_PCT
    deltas = jnp.abs(jnp.linspace(min_decay, max_decay, B_CH, dtype=F32))
    win = (jnp.exp(-t * deltas) + B_WINDOW_SHIFT) * (circ != l).astype(F32)[:, None]
    sel = (circ < l).astype(F32)[:, None]
    w1p = jnp.pad(w1, ((0, 128 - B_EMB), (0, 0)))
    rows = min(FILT_ROWS, 2 * l)
    return pl.pallas_call(
        _filter_kernel,
        grid=(2 * l // rows,),
        in_specs=[pl.BlockSpec((rows, 128), lambda i: (i, 0)),
                  pl.BlockSpec((rows, B_CH), lambda i: (i, 0)),
                  pl.BlockSpec((rows, 1), lambda i: (i, 0)),
                  _full(w1p.shape), _full((1, B_FFN)), _full(w2.shape), _full((B_INNER_MLPS, 1, B_FFN)),
                  _full(w_out.shape), _full((1, B_FFN))],
        out_specs=pl.BlockSpec((2, rows, B_CH), lambda i: (0, i, 0)),
        out_shape=jax.ShapeDtypeStruct((2, 2 * l, B_CH), F32),
        compiler_params=_cparams(("parallel",)), name="hyena_filter",
    )(z, win, sel, w1p, b1.reshape(1, B_FFN), w2, b2.reshape(B_INNER_MLPS, 1, B_FFN), w_out,
      freq.reshape(1, B_FFN))


def _hyena_spectrum(l, consts, w1, b1, w2m, b2, w_out, freq):
    f1_full, _, w2, _, _ = consts
    n2 = 2 * l // FFT_N1
    g_time = _hyena_filter_time(l, w1, b1, w2m, b2, w_out, freq)
    f1_rows = np.concatenate([f1_full[:FFT_KH], f1_full[FFT_N1:FFT_N1 + FFT_KH]], axis=0)
    a = _left_mm(jnp.asarray(f1_rows, F32), g_time.reshape(2, FFT_N1, n2 * B_CH))
    return _filt_mid_stage(a.reshape(2, 2, FFT_KH, n2, B_CH), jnp.asarray(w2[:FFT_KH], F32))


HY_SLAB = 128


def _hy_pitch(n2c):
    return n2c + 8 if (n2c // 8) % 2 == 0 else n2c + 16


HY_KG = 20
HY_JG = 16


def _hyena_lat_kernel(zv_ref, z1_ref, z2_ref, cw_ref, cb_ref, g_ref, bias_ref, f1_ref, w2_ref, v2_ref, f3_ref,
                      o_ref, useq, ur, a_re, a_im, cbuf, *, col_mode, seq):
    n1h = FFT_N1 // 2
    n2c = seq // n1h
    pitch = _hy_pitch(n2c)
    grows = seq // GRID_W
    row_id = lax.broadcasted_iota(jnp.int32, (seq, HY_SLAB), 0)

    def short_conv(z_ref, k):
        z = z_ref[...]
        w = cw_ref[k]
        if col_mode:
            g0, gl = z[:GRID_W], z[seq - GRID_W:]
            cc = row_id[:GRID_W]
            wrap_p = jnp.where(cc == 0, 0.0, pltpu.roll(gl, 1, axis=0))
            wrap_n = jnp.where(cc == GRID_W - 1, 0.0, pltpu.roll(g0, GRID_W - 1, axis=0))
            prev = jnp.concatenate([wrap_p, z[:seq - GRID_W]], axis=0)
            nxt = jnp.concatenate([z[GRID_W:], wrap_n], axis=0)
        else:
            prev = jnp.where(row_id == 0, 0.0, pltpu.roll(z, 1, axis=0))
            nxt = jnp.where(row_id == seq - 1, 0.0, pltpu.roll(z, seq - 1, axis=0))
        return prev * w[0:1] + z * w[1:2] + nxt * w[2:3] + cb_ref[k]

    def seq_start(j):
        return (GRID_W * lax.rem(j, grows) + j // grows) if col_mode else j

    seq_stride = 2 if col_mode else pitch
    seq_ref = ur if col_mode else useq

    def put_seq(val):
        if col_mode:
            ur[...] = val
        else:
            for n1 in range(n1h):
                useq[n1 * pitch:n1 * pitch + n2c, :] = val[n1 * n2c:(n1 + 1) * n2c]

    def conv_out():
        if col_mode:
            return cbuf[...]
        return jnp.concatenate([cbuf[n1 * pitch:n1 * pitch + n2c, :] for n1 in range(n1h)], axis=0)

    y_prev = short_conv(zv_ref, 0)
    x_next = (short_conv(z1_ref, 1), short_conv(z2_ref, 2))
    put_seq(y_prev)
    f1 = f1_ref[...]
    f3 = f3_ref[...]
    jgrp = min(HY_JG, n2c)
    for order in range(2):
        def stage1(jg, carry):
            js = [jg * jgrp + jj for jj in range(jgrp)]
            xs = [seq_ref[pl.ds(seq_start(j), n1h, stride=seq_stride), :] for j in js]
            outs = [_mm(f1, x) for x in xs]
            for j, a in zip(js, outs):
                a_re[pl.ds(j, FFT_KH, stride=pitch), :] = a[:FFT_KH]
                a_im[pl.ds(j, FFT_KH, stride=pitch), :] = a[FFT_KH:]
            return carry
        lax.fori_loop(0, n2c // jgrp, stage1, 0)

        def mid(kg, carry):
            k1s = [kg * HY_KG + kk for kk in range(HY_KG)]
            offs = [pl.multiple_of(k1 * pitch, 8) for k1 in k1s]
            xin = [jnp.concatenate([a_re[pl.ds(o, n2c), :], a_im[pl.ds(o, n2c), :]], axis=0) for o in offs]
            xf = [_mm(w2_ref[k1], a) for k1, a in zip(k1s, xin)]
            ys = []
            for k1, x in zip(k1s, xf):
                xr, xi = x[:n2c], x[n2c:]
                gr, gi = g_ref[order, k1, 0], g_ref[order, k1, 1]
                ys.append(jnp.concatenate([xr * gr - xi * gi, xr * gi + xi * gr], axis=0))
            bm = [_mm(v2_ref[k1], y) for k1, y in zip(k1s, ys)]
            for o, b in zip(offs, bm):
                a_re[pl.ds(o, n2c), :] = b[:n2c]
                a_im[pl.ds(o, n2c), :] = b[n2c:]
            return carry
        lax.fori_loop(0, FFT_KH // HY_KG, mid, 0)

        def stage3(jg, carry):
            js = [jg * jgrp + jj for jj in range(jgrp)]
            bs = [jnp.concatenate([a_re[pl.ds(j, FFT_KH, stride=pitch), :],
                                   a_im[pl.ds(j, FFT_KH, stride=pitch), :]], axis=0) for j in js]
            outs = [_mm(f3, b) for b in bs]
            for j, y in zip(js, outs):
                cbuf[pl.ds(seq_start(j), n1h, stride=seq_stride), :] = y
            return carry
        lax.fori_loop(0, n2c // jgrp, stage3, 0)

        y_prev = x_next[order] * (conv_out() + y_prev * bias_ref[order:order + 1, :])
        if order == 0:
            put_seq(y_prev)
    o_ref[...] = y_prev


def _hyena_fused(lay, z, conv_w, conv_b, spec, bias, consts, seq, row0, s0, nslab, col_mode):
    f1, w2, v2, f3 = _half_spectrum_consts(consts)
    n2c = seq // (FFT_N1 // 2)
    assert FFT_KH % HY_KG == 0 and n2c % min(HY_JG, n2c) == 0
    pitch = _hy_pitch(n2c)
    cps = B_CH // HY_SLAB
    zspec = lambda k: pl.BlockSpec((seq, HY_SLAB), lambda j, b, k=k: (row0 + b, k * cps + s0 + j))
    once = lambda a: pl.BlockSpec(a.shape, lambda j, b: (0,) * a.ndim, pipeline_mode=pl.Buffered(1))
    mats = (jnp.asarray(f1, BF16), jnp.asarray(w2, BF16), jnp.asarray(v2, BF16), jnp.asarray(f3, BF16))
    seq_rows = (FFT_N1 // 2) * pitch
    return pl.pallas_call(
        functools.partial(_hyena_lat_kernel, col_mode=col_mode, seq=seq),
        grid=(nslab, lay.b),
        in_specs=[zspec(0), zspec(1), zspec(2),
                  pl.BlockSpec((3, B_SHORT, HY_SLAB), lambda j, b: (0, 0, s0 + j)),
                  pl.BlockSpec((3, 1, HY_SLAB), lambda j, b: (0, 0, s0 + j)),
                  pl.BlockSpec((2, FFT_KH, 2, n2c, HY_SLAB), lambda j, b: (0, 0, 0, 0, s0 + j),
                               pipeline_mode=pl.Buffered(1)),
                  pl.BlockSpec((2, HY_SLAB), lambda j, b: (0, s0 + j))] + [once(m) for m in mats],
        out_specs=pl.BlockSpec((seq, HY_SLAB), lambda j, b: (b, j)),
        out_shape=jax.ShapeDtypeStruct((lay.b * seq, nslab * HY_SLAB), F32),
        scratch_shapes=[pltpu.VMEM((seq_rows, HY_SLAB), F32), pltpu.VMEM((seq, HY_SLAB), F32),
                        pltpu.VMEM((FFT_KH * pitch, HY_SLAB), F32), pltpu.VMEM((FFT_KH * pitch, HY_SLAB), F32),
                        pltpu.VMEM((seq if col_mode else seq_rows, HY_SLAB), F32)],
        compiler_params=_cparams(("parallel", "parallel")),
        name="hyena_col" if col_mode else "hyena_row",
    )(z, z, z, jnp.transpose(conv_w, (1, 0, 2)), conv_b.reshape(3, 1, B_CH), spec, bias, *mats)


def _hyena(lay, z, conv_w, conv_b, spec_c, spec_x, bias, consts_c, consts_x):
    assert lay.n_ctx % lay.seq == 0
    half = B_CH // 2 // HY_SLAB
    args = (lay, z, conv_w, conv_b)
    yr = _hyena_fused(*args, spec_x, bias, consts_x, lay.seq, lay.n_ctx // lay.seq, 0, half, False)
    ycol = _hyena_fused(*args, spec_x, bias, consts_x, lay.seq, lay.n_ctx // lay.seq, half, half, True)
    yl = jnp.concatenate([yr, ycol], axis=1)
    if spec_c is None:
        return yl
    yc = _hyena_fused(*args, spec_c, bias, consts_c, lay.ctx, 0, 0, 2 * half, False)
    return jnp.concatenate([yc, yl], axis=0)


def _head_norm_gate(o, gate, norm_w, heads, width):
    outs = []
    for h in range(heads):
        oh = o[:, h * width:(h + 1) * width]
        y = oh * lax.rsqrt(jnp.mean(oh * oh, axis=-1, keepdims=True) + NORM_EPS) * norm_w
        outs.append(y * _silu(gate[:, h * width:(h + 1) * width]))
    return jnp.concatenate(outs, axis=1)


def _merge_kernel(x_ref, mod_ref, nw_ref, a0_ref, a1_ref, ag_ref, yb_ref, c0_ref, c1_ref, cx_ref, cz_ref,
                  d0_ref, d1_ref, dg_ref, an_ref, cd_ref, cn_ref, dn_ref, wg_ref, wb_ref, wo_ref, o_ref):
    xv = x_ref[...]
    m = mod_ref[...]
    hb = _prenorm(xv, nw_ref[1:2, :], m[3:4, :], m[4:5, :]).astype(BF16)
    ya = _head_norm_gate(a0_ref[...] + a1_ref[...], ag_ref[...], an_ref[...], A_HEADS, A_DV)
    yd = _head_norm_gate(d0_ref[...] + d1_ref[...], dg_ref[...], dn_ref[...], D_HEADS, D_DV)
    yc = (c0_ref[...] + c1_ref[...] + cd_ref[...] * cx_ref[...]) * _silu(cz_ref[...])
    gw = C_INNER // C_GROUPS
    cn = cn_ref[...]
    yc = jnp.concatenate(
        [yc[:, g * gw:(g + 1) * gw]
         * lax.rsqrt(jnp.mean(yc[:, g * gw:(g + 1) * gw] ** 2, axis=-1, keepdims=True) + NORM_EPS)
         * cn[:, g * gw:(g + 1) * gw] for g in range(C_GROUPS)], axis=1)
    acc = jnp.zeros(xv.shape, F32)
    for k, y in enumerate((ya, yb_ref[...], yc, yd)):
        gate = jax.nn.sigmoid(jnp.dot(hb, wg_ref[:, k * D_MODEL:(k + 1) * D_MODEL], preferred_element_type=F32))
        acc = acc + gate * jnp.dot(y.astype(BF16), wb_ref[k], preferred_element_type=F32)
    o_ref[...] = xv + m[5:6, :] * jnp.dot(acc.astype(BF16), wo_ref[...], preferred_element_type=F32)


def _merge(lay, x, mod, nw, a0, a1, ag, yb, c0, c1, cx, cz, d0, d1, dg, an, cd, cn, dn, wg, wb, wo,
           latent_only=False):
    n, d = x.shape
    t0 = lay.n_ctx // TOK_TILE if latent_only else 0
    row = lambda w, off=t0: pl.BlockSpec((TOK_TILE, w), lambda i: (i + off, 0))
    once = lambda a: pl.BlockSpec(a.shape, lambda i: (0,) * a.ndim, pipeline_mode=pl.Buffered(1))
    branch_rows = [row(BRANCH_W)] * 3 + [row(BRANCH_W, 0 if latent_only else t0)] + [row(BRANCH_W)] * 7
    return pl.pallas_call(
        _merge_kernel,
        grid=(n // TOK_TILE - t0,),
        in_specs=[row(d), pl.BlockSpec((None, N_MOD, d), lambda i: (lay.mod_index(i + t0), 0, 0)), _full(nw.shape)]
                 + branch_rows + [_full(an.shape), _full(cd.shape), _full(cn.shape), _full(dn.shape),
                                  once(wg), once(wb), once(wo)],
        out_specs=row(d),
        out_shape=jax.ShapeDtypeStruct((n, d), F32),
        compiler_params=_cparams(("parallel",)), name="merge",
    )(x, mod, nw, a0, a1, ag, yb, c0, c1, cx, cz, d0, d1, dg, an, cd, cn, dn, wg, wb, wo)


def kernel(x, c, ctx, c_ctx, w_ada, b_ada, norm_w, ffn_up, ffn_down, w_in, gdn_conv, gdn_a_log, gdn_dt_bias, gdn_norm, hy_conv_w, hy_conv_b, hy_w1, hy_b1, hy_w2, hy_b2, hy_wout, hy_freq, hy_bias, ssd_conv_w, ssd_conv_b, ssd_a_log, ssd_dt_bias, ssd_d, ssd_norm, gla_gk_w, gla_gk_b, gla_norm, w_branch, w_out, final_norm):
    b, seq, d = x.shape
    ctx_len = ctx.shape[1]
    depth = w_ada.shape[0]
    lay = _Layout(b, ctx_len, seq)
    consts_c = _dft_consts(2 * ctx_len // FFT_N1)
    consts_x = _dft_consts(2 * seq // FFT_N1)

    rp = -(-(1 + b) // 8) * 8
    cond = jnp.concatenate([c_ctx[None, :], c, jnp.zeros((rp - 1 - b, d), F32)], axis=0)
    mods = _ada(cond, w_ada, b_ada).reshape(depth, rp, N_MOD, d)

    xf = jnp.concatenate([ctx.reshape(b * ctx_len, d), x.reshape(b * seq, d)], axis=0)
    for l in range(depth):
        mod = mods[l]
        nw = norm_w[l]
        w_r, w_gate = _rearrange_w_in(w_in[l])
        xf = _ffn(lay, xf, mod, nw, ffn_up[l, 0].astype(BF16), ffn_down[l, 0].astype(BF16), 0)

        qkv, a_gate, zb, c_z, xbc, d_qkv, d_gate, small = _inproj(
            lay, xf, mod, nw, w_r, gdn_conv[l], ssd_conv_w[l], ssd_conv_b[l].reshape(1, C_XBC))
        (a0, a1), (c0, c1), (d0, d1) = _scan_call(lay, [
            _gdn_scan(lay, qkv, small, gdn_a_log[l], gdn_dt_bias[l]),
            _ssd_scan(lay, xbc, small, ssd_a_log[l], ssd_dt_bias[l]),
            _gla_scan(lay, d_qkv, small, gla_gk_w[l], gla_gk_b[l])])
        last = l == depth - 1
        fargs = (hy_w1[l], hy_b1[l], hy_w2[l], hy_b2[l], hy_wout[l], hy_freq[l])
        spec_c = None if last else _hyena_spectrum(ctx_len, consts_c, *fargs)
        spec_x = _hyena_spectrum(seq, consts_x, *fargs)
        yb = _hyena(lay, zb, hy_conv_w[l], hy_conv_b[l], spec_c, spec_x, hy_bias[l], consts_c, consts_x)

        xf = _merge(lay, xf, mod, nw, a0, a1, a_gate, yb, c0, c1, xbc, c_z, d0, d1, d_gate,
                    gdn_norm[l].reshape(1, A_DV), jnp.repeat(ssd_d[l], C_HEADDIM).reshape(1, C_INNER),
                    ssd_norm[l].reshape(1, C_INNER), gla_norm[l].reshape(1, D_DV),
                    w_gate, w_branch[l].astype(BF16), w_out[l].astype(BF16), latent_only=last)
        xf = _ffn(lay, xf, mod, nw, ffn_up[l, 1].astype(BF16), ffn_down[l, 1].astype(BF16), 2,
                  final_w=final_norm if last else None)
    return xf.reshape(b, seq, d)
```

```python
import functools
import math

import numpy as np
import jax
import jax.numpy as jnp
from jax import lax
from jax.experimental import pallas as pl
from jax.experimental.pallas import tpu as pltpu

F32 = jnp.float32
BF16 = jnp.bfloat16
HI = lax.Precision.HIGHEST

D_MODEL = 1024
GRID_W = 64
CHUNK = 64
NORM_EPS = 1e-6
N_MOD = 9
D_FF = 2816
SHORT_CONV = 5

A_HEADS, A_DK, A_DV = 4, 128, 128
A_QKV = A_HEADS * (2 * A_DK + A_DV)
B_CH, B_SHORT, B_EMB, B_FFN, B_INNER_MLPS = 512, 3, 33, 64, 2
B_WINDOW_SHIFT, B_DECAY_SHORT_PCT, B_DECAY_LONG_PCT, B_DECAY_TARGET = 0.05, 0.3, 1.5, 1e-2
C_HEADS, C_HEADDIM, C_GROUPS, C_STATE = 8, 64, 2, 64
C_INNER = C_HEADS * C_HEADDIM
C_XBC = C_INNER + 2 * C_GROUPS * C_STATE
D_HEADS, D_DK, D_DV, D_RANK = 4, 64, 128, 16
D_GATE_NORM = 16.0
N_BRANCH, BRANCH_W = 4, 512

IN_SIZES = (A_QKV, A_HEADS * A_DV, 2 * A_HEADS, 2 * A_HEADS, 3 * B_CH, C_INNER, C_XBC, 2 * C_HEADS,
            D_HEADS * (2 * D_DK + D_DV), D_HEADS * D_DV, 2 * D_RANK, N_BRANCH * D_MODEL)
IN_OFFS = tuple(int(v) for v in np.cumsum((0,) + IN_SIZES))

ROW_TILE = 256
TOK_TILE = 512
SMALL_W = 128
VMEM_LIMIT = 56 * 1024 * 1024


def _cparams(sem):
    return pltpu.CompilerParams(dimension_semantics=sem, vmem_limit_bytes=VMEM_LIMIT)


def _mm(a, b):
    return jnp.dot(a.astype(BF16), b.astype(BF16), preferred_element_type=F32)


def _mm_nt(a, b):
    return lax.dot_general(a.astype(BF16), b.astype(BF16), (((1,), (1,)), ((), ())),
                           preferred_element_type=F32)


def _mm_hi(a, b):
    return jnp.dot(a, b, precision=HI, preferred_element_type=F32)


def _bf16_parts(v, parts):
    out, rest = [], v
    for _ in range(parts):
        hi = rest.astype(BF16)
        out.append(hi)
        rest = rest - hi.astype(F32)
    return out


def _mm_3x(a, b):
    ah, al = _bf16_parts(a, 2)
    bh, bl = _bf16_parts(b, 2)
    dot = functools.partial(jnp.dot, preferred_element_type=F32)
    return dot(ah, bh) + (dot(ah, bl) + dot(al, bh))


def _mm_sel_l(sel, v, parts=3):
    sb = sel.astype(BF16)
    return sum(jnp.dot(sb, p, preferred_element_type=F32) for p in _bf16_parts(v, parts))


def _mm_sel_r(v, sel, parts=3):
    sb = sel.astype(BF16)
    return sum(jnp.dot(p, sb, preferred_element_type=F32) for p in _bf16_parts(v, parts))


def _silu(v):
    return v * jax.nn.sigmoid(v)


def _softplus(v):
    return jnp.maximum(v, 0.0) + jnp.log1p(jnp.exp(-jnp.abs(v)))


def _log_sigmoid(v):
    return jnp.minimum(v, 0.0) - jnp.log1p(jnp.exp(-jnp.abs(v)))


def _prenorm(xv, gain, shift, scale):
    ms = jnp.mean(xv * xv, axis=-1, keepdims=True)
    return (xv * lax.rsqrt(ms + NORM_EPS) * gain) * (1.0 + scale) + shift


def _full(shape):
    nd = len(shape)
    return pl.BlockSpec(shape, lambda *_: (0,) * nd)


class _Layout:
    def __init__(self, batch, ctx_len, seq):
        assert ctx_len % ROW_TILE == 0 and seq % ROW_TILE == 0
        assert (batch * ctx_len) % TOK_TILE == 0 and seq % TOK_TILE == 0
        self.b, self.ctx, self.seq = batch, ctx_len, seq
        self.n_ctx = batch * ctx_len
        self.n = batch * (ctx_len + seq)
        self.tc, self.tl = ctx_len // ROW_TILE, seq // ROW_TILE
        self.nct = batch * self.tc
        self.tiles = self.n // ROW_TILE

    def mod_index(self, i, tile=TOK_TILE):
        nct = self.n_ctx // tile
        return jnp.where(i < nct, 0, 1 + (i - nct) // (self.seq // tile))

    def seg_first(self, i):
        return jnp.where(i < self.nct, lax.rem(i, self.tc) == 0, lax.rem(i - self.nct, self.tl) == 0)

    def seg_last(self, i):
        return jnp.where(i < self.nct, lax.rem(i, self.tc) == self.tc - 1,
                         lax.rem(i - self.nct, self.tl) == self.tl - 1)

    def fwd_tile(self, b, s):
        return jnp.where(s < self.tc, b * self.tc + s, self.nct + b * self.tl + (s - self.tc))

    def rev_tile(self, b, s):
        return jnp.where(s < self.tc, b * self.tc + (self.tc - 1 - s),
                         self.nct + b * self.tl + (self.tl - 1 - (s - self.tc)))


def _ada_kernel(c_ref, w_ref, b_ref, o_ref):
    o_ref[...] = _mm_hi(_silu(c_ref[...]), w_ref[...]) + b_ref[...]


def _ada(cond, w_ada, b_ada):
    depth, d, nm = w_ada.shape
    rp = cond.shape[0]
    tn = 1152
    return pl.pallas_call(
        _ada_kernel,
        grid=(depth, nm // tn),
        in_specs=[_full((rp, d)),
                  pl.BlockSpec((None, d, tn), lambda l, j: (l, 0, j)),
                  pl.BlockSpec((None, 1, tn), lambda l, j: (l, 0, j))],
        out_specs=pl.BlockSpec((None, rp, tn), lambda l, j: (l, 0, j)),
        out_shape=jax.ShapeDtypeStruct((depth, rp, nm), F32),
        compiler_params=_cparams(("parallel", "parallel")), name="ada",
    )(cond, w_ada, b_ada.reshape(depth, 1, nm))


FF_CHUNK = 256


def _ffn_kernel(x_ref, mod_ref, nw_ref, wup_ref, wdn_ref, *rest, sub):
    o_ref = rest[-1]
    xv = x_ref[...]
    m = mod_ref[...]
    h = _prenorm(xv, nw_ref[sub:sub + 1, :], m[3 * sub:3 * sub + 1, :], m[3 * sub + 1:3 * sub + 2, :])
    hb = h.astype(BF16)
    acc = jnp.zeros(xv.shape, F32)
    for c in range(D_FF // FF_CHUNK):
        lo = c * FF_CHUNK
        a = jnp.dot(hb, wup_ref[:, lo:lo + FF_CHUNK], preferred_element_type=F32)
        g = jnp.dot(hb, wup_ref[:, D_FF + lo:D_FF + lo + FF_CHUNK], preferred_element_type=F32)
        acc = acc + jnp.dot((_silu(a) * g).astype(BF16), wdn_ref[lo:lo + FF_CHUNK, :],
                            preferred_element_type=F32)
    out = xv + 0.5 * m[3 * sub + 2:3 * sub + 3, :] * acc
    if len(rest) == 2:
        out = out * lax.rsqrt(jnp.mean(out * out, axis=-1, keepdims=True) + NORM_EPS) * rest[0][...]
    o_ref[...] = out


def _of_layer(a, l, *lead, once=False):
    idx = (l,) + lead
    block = (None,) * len(idx) + a.shape[len(idx):]
    mode = dict(pipeline_mode=pl.Buffered(1)) if once else {}
    return pl.BlockSpec(block, lambda *_: idx + (0,) * (a.ndim - len(idx)), **mode)


def _mod_spec(lay, mods, l, tile, t0=0):
    return pl.BlockSpec((None, None, N_MOD, mods.shape[-1]), lambda i: (l, lay.mod_index(i + t0, tile), 0, 0))


def _ffn(lay, x, mods, norm_w, wup, wdn, l, sub, final_w=None):
    n, d = x.shape
    t0 = 0 if final_w is None else lay.n_ctx // TOK_TILE
    extra = [] if final_w is None else [final_w.reshape(1, d)]
    return pl.pallas_call(
        functools.partial(_ffn_kernel, sub=sub),
        grid=(n // TOK_TILE - t0,),
        in_specs=[pl.BlockSpec((TOK_TILE, d), lambda i: (i + t0, 0)),
                  _mod_spec(lay, mods, l, TOK_TILE, t0),
                  _of_layer(norm_w, l),
                  _of_layer(wup, l, sub // 2, once=True),
                  _of_layer(wdn, l, sub // 2, once=True)]
                 + [_full(e.shape) for e in extra],
        out_specs=pl.BlockSpec((TOK_TILE, d), lambda i: (i, 0)),
        out_shape=jax.ShapeDtypeStruct((n - t0 * TOK_TILE, d), F32),
        compiler_params=_cparams(("parallel",)), name="ffn",
    )(x, mods, norm_w, wup, wdn, *extra)


PROJ_W = (A_QKV, A_HEADS * A_DV, 3 * B_CH, C_INNER, C_XBC, D_HEADS * (2 * D_DK + D_DV), D_HEADS * D_DV, SMALL_W)
PROJ_O = tuple(int(v) for v in np.cumsum((0,) + PROJ_W))


def _rearrange_w_in(w_in):
    o = IN_OFFS
    col = lambda a, b: w_in[..., o[a]:o[b]]
    small = jnp.concatenate([col(2, 4), col(7, 8), col(10, 11),
                             jnp.zeros(w_in.shape[:-1] + (SMALL_W - 64,), w_in.dtype)], axis=-1)
    w_r = jnp.concatenate([col(0, 1), col(1, 2), col(4, 5), col(5, 6), col(6, 7), col(8, 9), col(9, 10), small],
                          axis=-1)
    return w_r.astype(BF16), col(11, 12).astype(BF16)


PROJ_HALO = 16
PROJ_CONV = (0, 4)


def _inproj_kernel(xp_ref, x_ref, xn_ref, mod_ref, nw_ref, w_ref, cwa_ref, cwc_ref, cbc_ref, *rest, lay):
    o_refs, zbuf = rest[:-1], rest[-1]
    i = pl.program_id(0)
    m = mod_ref[...]
    norm = lambda r: _prenorm(r[...], nw_ref[1:2, :], m[3:4, :], m[4:5, :]).astype(BF16)
    hb = norm(x_ref)
    hb_all = jnp.concatenate([norm(xp_ref), hb, norm(xn_ref)], axis=0)
    keep_p = jnp.where(lay.seg_first(i), 0.0, 1.0)
    keep_n = jnp.where(lay.seg_last(i), 0.0, 1.0)
    convs = {PROJ_CONV[0]: (cwa_ref, None), PROJ_CONV[1]: (cwc_ref, cbc_ref)}
    for k, o_ref in enumerate(o_refs):
        wk = w_ref[:, PROJ_O[k]:PROJ_O[k + 1]]
        if k not in convs:
            o_ref[...] = jnp.dot(hb, wk, preferred_element_type=F32)
            continue
        cw_ref, cb_ref = convs[k]
        width = PROJ_W[k]
        z = jnp.dot(hb_all, wk, preferred_element_type=F32)
        zbuf[0:PROJ_HALO, 0:width] = z[:PROJ_HALO] * keep_p
        zbuf[PROJ_HALO:PROJ_HALO + ROW_TILE, 0:width] = z[PROJ_HALO:PROJ_HALO + ROW_TILE]
        zbuf[PROJ_HALO + ROW_TILE:, 0:width] = z[PROJ_HALO + ROW_TILE:] * keep_n
        taps = cw_ref.shape[0]
        left = (taps - 1) // 2
        acc = None
        for t in range(taps):
            off = t - left
            term = zbuf[PROJ_HALO + off:PROJ_HALO + off + ROW_TILE, 0:width] * cw_ref[t:t + 1, :]
            acc = term if acc is None else acc + term
        if cb_ref is not None:
            acc = acc + cb_ref[...]
        o_ref[...] = _silu(acc)


def _inproj(lay, x, mods, norm_w, w_r, conv_a, conv_c, conv_c_bias, l):
    n, d = x.shape
    rh = ROW_TILE // PROJ_HALO
    lasth = n // PROJ_HALO - 1
    return pl.pallas_call(
        functools.partial(_inproj_kernel, lay=lay),
        grid=(n // ROW_TILE,),
        in_specs=[pl.BlockSpec((PROJ_HALO, d), lambda i: (jnp.maximum(i * rh - 1, 0), 0)),
                  pl.BlockSpec((ROW_TILE, d), lambda i: (i, 0)),
                  pl.BlockSpec((PROJ_HALO, d), lambda i: (jnp.minimum((i + 1) * rh, lasth), 0)),
                  _mod_spec(lay, mods, l, ROW_TILE),
                  _of_layer(norm_w, l),
                  _of_layer(w_r, l, once=True),
                  _of_layer(conv_a, l), _of_layer(conv_c, l), _of_layer(conv_c_bias, l)],
        out_specs=[pl.BlockSpec((ROW_TILE, w), lambda i: (i, 0)) for w in PROJ_W],
        out_shape=[jax.ShapeDtypeStruct((n, w), F32) for w in PROJ_W],
        scratch_shapes=[pltpu.VMEM((ROW_TILE + 2 * PROJ_HALO, max(PROJ_W[k] for k in PROJ_CONV)), F32)],
        compiler_params=_cparams(("parallel",)), name="inproj",
    )(x, x, x, mods, norm_w, w_r, conv_a, conv_c, conv_c_bias)


def _tri_masks(rev):
    ii = lax.broadcasted_iota(jnp.int32, (CHUNK, CHUNK), 0)
    jj = lax.broadcasted_iota(jnp.int32, (CHUNK, CHUNK), 1)
    return (ii <= jj, ii < jj) if rev else (ii >= jj, ii > jj)


def _masked_decay(col, row, incl):
    return jnp.where(incl, jnp.exp(jnp.where(incl, col - row, 0.0)), 0.0)


def _scan_call(lay, parts):
    steps = lay.tc + lay.tl
    specs, args, out_specs, out_shape, scratch, counts = [], [], [], [], [], []
    for _, ins_tiled, ins_full, out_w, scr in parts:
        n0 = len(specs)
        for walk in (lay.fwd_tile, lay.rev_tile):
            for a in ins_tiled:
                specs.append(pl.BlockSpec((ROW_TILE, a.shape[1]), lambda b, s, walk=walk: (walk(b, s), 0)))
                args.append(a)
            out_specs.append(pl.BlockSpec((ROW_TILE, out_w), lambda b, s, walk=walk: (walk(b, s), 0)))
            out_shape.append(jax.ShapeDtypeStruct((lay.n, out_w), F32))
        for a in ins_full:
            specs.append(_full(a.shape))
            args.append(a)
        counts.append((len(specs) - n0, len(scr)))
        scratch += list(scr)

    def body(*refs):
        n_in, n_out = len(specs), len(out_specs)
        i, s = 0, n_in + n_out
        for p, (kern, *_) in enumerate(parts):
            ci, cs = counts[p]
            kern(*refs[i:i + ci], *refs[n_in + 2 * p:n_in + 2 * p + 2], *refs[s:s + cs])
            i, s = i + ci, s + cs

    res = pl.pallas_call(
        body,
        grid=(lay.b, steps),
        in_specs=specs,
        out_specs=out_specs,
        out_shape=out_shape,
        scratch_shapes=scratch,
        compiler_params=_cparams(("arbitrary", "arbitrary")), name="scans",
    )(*args)
    return [tuple(res[2 * p:2 * p + 2]) for p in range(len(parts))]


def _ssd_kernel(xf_ref, sf_ref, xr_ref, sr_ref, alog_ref, dtb_ref, alog_t_ref, dtb_t_ref, exp_ref,
                of_ref, or_ref, st_ref):
    @pl.when(pl.program_id(1) == 0)
    def _():
        st_ref[...] = jnp.zeros(st_ref.shape, F32)

    hpg = C_HEADS // C_GROUPS
    gw = hpg * C_HEADDIM
    nch = ROW_TILE // CHUNK
    refs = ((xf_ref, sf_ref, of_ref), (xr_ref, sr_ref, or_ref))
    units, pre = [], {}
    for d, (x_ref, s_ref, _) in enumerate(refs):
        incl, _ = _tri_masks(d == 1)
        m_col = incl.astype(F32)
        m_row = m_col.T
        sm = s_ref[...]
        sm_t = sm.T
        dt_all = _softplus(sm[:, 16:32] + dtb_ref[...])
        a_all = -jnp.exp(alog_ref[...]) * dt_all
        a_t_all = -jnp.exp(alog_t_ref[...]) * _softplus(sm_t[16:32, :] + dtb_t_ref[...])
        bm_t = x_ref[:, C_INNER:C_INNER + C_GROUPS * C_STATE].T
        for c in range(nch):
            rows = slice(c * CHUNK, (c + 1) * CHUNK)
            units.append((d, c))
            pre[d, c] = dict(incl=incl, m_col=m_col, m_row=m_row, a_ch=a_all[rows], a_t=a_t_all[:, rows],
                             dt=dt_all[rows], bm_t=bm_t[:, rows], e=exp_ref[d], xs=x_ref[rows, 0:C_INNER],
                             bm=x_ref[rows, C_INNER:C_INNER + C_GROUPS * C_STATE],
                             cm=x_ref[rows, C_INNER + C_GROUPS * C_STATE:C_XBC])
    for u in units:
        w = pre[u]
        w["ac"] = _mm_sel_l(w["m_col"], w["a_ch"])
        w["ac_t"] = _mm_sel_r(w["a_t"], w["m_row"])
        w["alast"] = jnp.sum(w["a_ch"], axis=0, keepdims=True)
    for u in units:
        w = pre[u]
        w["xdt"] = w["xs"] * _mm_sel_r(w["dt"], w["e"])
        w["eac_e"] = _mm_sel_r(jnp.exp(w["ac"]), w["e"])
        w["xd"] = w["xdt"] * _mm_sel_r(jnp.exp(w["alast"] - w["ac"]), w["e"])
        w["dle"] = _mm_sel_r(jnp.broadcast_to(jnp.exp(w["alast"]), (8, 2 * C_HEADS)), w["e"])[0:1]
    grp = lambda a, g: a[:, g * C_STATE:(g + 1) * C_STATE]
    cb = {(u, g): _mm_nt(grp(pre[u]["cm"], g), grp(pre[u]["bm"], g)) for u in units for g in range(C_GROUPS)}
    upd = {(u, g): _mm(pre[u]["bm_t"][g * C_STATE:(g + 1) * C_STATE], pre[u]["xd"][:, g * gw:(g + 1) * gw])
           for u in units for g in range(C_GROUPS)}
    y_diag = {}
    for u in units:
        w = pre[u]
        for h in range(C_HEADS):
            col = u[0] * C_HEADS + h
            seg = _masked_decay(w["ac"][:, col:col + 1], w["ac_t"][col:col + 1, :], w["incl"])
            y_diag[u, h] = _mm(cb[u, h // hpg] * seg, w["xdt"][:, h * C_HEADDIM:(h + 1) * C_HEADDIM])
    st = {(d, g): st_ref[d, g] for d in range(2) for g in range(C_GROUPS)}
    entry = {}
    for p in range(nch):
        for d in range(2):
            u = (d, (nch - 1 - p) if d == 1 else p)
            for g in range(C_GROUPS):
                entry[u, g] = st[d, g]
                st[d, g] = pre[u]["dle"][:, g * gw:(g + 1) * gw] * st[d, g] + upd[u, g]
    for d in range(2):
        for g in range(C_GROUPS):
            st_ref[d, g] = st[d, g]
    for u in units:
        y_off = [_mm(grp(pre[u]["cm"], g), entry[u, g]) for g in range(C_GROUPS)]
        refs[u[0]][2][u[1] * CHUNK:(u[1] + 1) * CHUNK, :] = (
            jnp.concatenate([y_diag[u, h] for h in range(C_HEADS)], axis=1)
            + pre[u]["eac_e"] * jnp.concatenate(y_off, axis=1))


def _ssd_scan(lay, xbc, small, a_log, dt_bias):
    expand = np.zeros((2, 2 * C_HEADS, C_INNER), np.float32)
    for d in range(2):
        for h in range(C_HEADS):
            expand[d, d * C_HEADS + h, h * C_HEADDIM:(h + 1) * C_HEADDIM] = 1.0
    al = a_log.reshape(1, 2 * C_HEADS)
    db = dt_bias.reshape(1, 2 * C_HEADS)
    return (_ssd_kernel, (xbc, small), (al, db, al.T, db.T, jnp.asarray(expand)), C_INNER,
            [pltpu.VMEM((2, C_GROUPS, C_STATE, C_INNER // C_GROUPS), F32)])


def _gla_kernel(xf_ref, sf_ref, xr_ref, sr_ref, gkw_ref, gkb_ref, of_ref, or_ref, st_ref):
    @pl.when(pl.program_id(1) == 0)
    def _():
        st_ref[...] = jnp.zeros(st_ref.shape, F32)

    nk = D_HEADS * D_DK
    nch = ROW_TILE // CHUNK
    refs = ((xf_ref, sf_ref, of_ref), (xr_ref, sr_ref, or_ref))
    units, pre = [], {}
    for d, (x_ref, s_ref, _) in enumerate(refs):
        incl, _ = _tri_masks(d == 1)
        m_col = incl.astype(F32)
        lr = s_ref[:, 32 + D_RANK * d:32 + D_RANK * (d + 1)]
        g_all = _log_sigmoid(_mm_3x(lr, gkw_ref[d]) + gkb_ref[d:d + 1, :]) / D_GATE_NORM
        v_t = x_ref[:, 2 * nk:].T
        for c in range(nch):
            rows = slice(c * CHUNK, (c + 1) * CHUNK)
            g_ch = g_all[rows]
            gc = _mm_sel_l(m_col, g_ch)
            glast = jnp.sum(g_ch, axis=0, keepdims=True)
            q = x_ref[rows, 0:nk] * (D_DK ** -0.5)
            k = x_ref[rows, nk:2 * nk]
            v = x_ref[rows, 2 * nk:]
            kd = k * jnp.exp(glast - gc)
            qd = q * jnp.exp(gc)
            qr = q * jnp.exp(gc - glast)
            dlast = jnp.exp(glast)
            for h in range(D_HEADS):
                ks = slice(h * D_DK, (h + 1) * D_DK)
                vs = slice(h * D_DV, (h + 1) * D_DV)
                units.append((d, c, h))
                pre[d, c, h] = dict(incl=incl, qr=qr[:, ks], kd=kd[:, ks], qd=qd[:, ks], v=v[:, vs],
                                    v_t=v_t[vs, rows], dl=dlast[:, ks])
    aqk = [jnp.where(pre[u]["incl"], _mm_nt(pre[u]["qr"], pre[u]["kd"]), 0.0) for u in units]
    upd = {u: _mm(pre[u]["v_t"], pre[u]["kd"]) for u in units}
    intra = {u: _mm(a, pre[u]["v"]) for u, a in zip(units, aqk)}
    heads = [(d, h) for d in range(2) for h in range(D_HEADS)]
    st = {dh: st_ref[dh[0], dh[1]] for dh in heads}
    entry = {}
    for p in range(nch):
        for d, h in heads:
            u = (d, (nch - 1 - p) if d == 1 else p, h)
            entry[u] = st[d, h]
            st[d, h] = st[d, h] * pre[u]["dl"] + upd[u]
    for dh in heads:
        st_ref[dh[0], dh[1]] = st[dh]
    inter = {u: _mm_nt(pre[u]["qd"], entry[u]) for u in units}
    for d, (_, _, o_ref) in enumerate(refs):
        for c in range(nch):
            o_ref[c * CHUNK:(c + 1) * CHUNK, :] = jnp.concatenate(
                [intra[d, c, h] + inter[d, c, h] for h in range(D_HEADS)], axis=1)


def _gla_scan(lay, qkv, small, gk_w, gk_b):
    return (_gla_kernel, (qkv, small), (gk_w, gk_b), D_HEADS * D_DV, [pltpu.VMEM((2, D_HEADS, D_DV, D_DK), F32)])


TRI_BLOCK = 16
GDN_GROUP = 32


def _unit_lower_inverse(nms, eye, blk):
    dg = [jnp.where(blk, nm, 0.0) for nm in nms]
    off = [nm - d for nm, d in zip(nms, dg)]
    t0 = [eye - d for d in dg]
    p = [_mm(d, d) for d in dg]
    for it in range(3):
        t0 = [t + _mm(t, q) for t, q in zip(t0, p)]
        if it < 2:
            p = [_mm(q, q) for q in p]
    m = [_mm(t, o) for t, o in zip(t0, off)]
    m2 = [_mm(a, a) for a in m]
    r = [eye - a for a in m]
    r = [a + _mm(a, b) for a, b in zip(r, m2)]
    return [_mm(a, t) for a, t in zip(r, t0)]


def _gdn_kernel(xf_ref, sf_ref, xr_ref, sr_ref, alog_ref, dtb_ref, alog_t_ref, dtb_t_ref,
                of_ref, or_ref, st_ref):
    @pl.when(pl.program_id(1) == 0)
    def _():
        st_ref[...] = jnp.zeros(st_ref.shape, F32)

    nk = A_HEADS * A_DK
    nch = ROW_TILE // CHUNK
    ii = lax.broadcasted_iota(jnp.int32, (CHUNK, CHUNK), 0)
    jj = lax.broadcasted_iota(jnp.int32, (CHUNK, CHUNK), 1)
    eye = (ii == jj).astype(F32)
    blk = (ii // TRI_BLOCK) == (jj // TRI_BLOCK)
    refs = ((xf_ref, sf_ref, of_ref), (xr_ref, sr_ref, or_ref))

    units, pre = [], {}
    for d, (x_ref, s_ref, _) in enumerate(refs):
        incl, strict = _tri_masks(d == 1)
        m_col = incl.astype(F32)
        m_row = m_col.T
        sm = s_ref[...]
        sm_t = sm.T
        beta_all = jax.nn.sigmoid(sm[:, 0:8])
        g_all = -jnp.exp(alog_ref[...]) * _softplus(sm[:, 8:16] + dtb_ref[...])
        g_t_all = -jnp.exp(alog_t_ref[...]) * _softplus(sm_t[8:16, :] + dtb_t_ref[...])
        qn, kn = [], []
        for h in range(A_HEADS):
            qh = x_ref[:, h * A_DK:(h + 1) * A_DK]
            kh = x_ref[:, nk + h * A_DK:nk + (h + 1) * A_DK]
            qn.append(qh * lax.rsqrt(jnp.sum(qh * qh, axis=-1, keepdims=True) + NORM_EPS) * (A_DK ** -0.5))
            kn.append(kh * lax.rsqrt(jnp.sum(kh * kh, axis=-1, keepdims=True) + NORM_EPS))
        kn_t = [kh.T for kh in kn]
        for c in range(nch):
            rows = slice(c * CHUNK, (c + 1) * CHUNK)
            g_ch = g_all[rows]
            gc = _mm_sel_l(m_col, g_ch)
            gc_t = _mm_sel_r(g_t_all[:, rows], m_row)
            glast = jnp.sum(g_ch, axis=0, keepdims=True)
            for h in range(A_HEADS):
                col = d * A_HEADS + h
                gcc, gcr, gl = gc[:, col:col + 1], gc_t[col:col + 1, :], glast[:, col:col + 1]
                bh = beta_all[rows, col:col + 1]
                kh = kn[h][rows]
                units.append((d, c, h))
                pre[d, c, h] = dict(
                    dmask=_masked_decay(gcc, gcr, incl), strict=strict, egc=jnp.exp(gcc), kh=kh, kb=kh * bh,
                    qh=qn[h][rows], vb=x_ref[rows, 2 * nk + h * A_DV:2 * nk + (h + 1) * A_DV] * bh,
                    kg_t=kn_t[h][:, rows] * jnp.exp(gl - gcr), dl=jnp.exp(gl))
    for g0 in range(0, len(units), GDN_GROUP):
        us = units[g0:g0 + GDN_GROUP]
        kk = [_mm_nt(pre[u]["kb"], pre[u]["kh"]) for u in us]
        qk = [_mm_nt(pre[u]["qh"], pre[u]["kh"]) for u in us]
        tinv = _unit_lower_inverse(
            [jnp.where(pre[u]["strict"], a * pre[u]["dmask"], 0.0) for u, a in zip(us, kk)], eye, blk)
        uw = [_mm(t, jnp.concatenate([pre[u]["vb"], pre[u]["kb"] * pre[u]["egc"]], axis=1))
              for u, t in zip(us, tinv)]
        for u, a, b in zip(us, uw, qk):
            pre[u]["u"] = a[:, :A_DV]
            pre[u]["wq"] = jnp.concatenate([a[:, A_DV:], pre[u]["qh"] * pre[u]["egc"]], axis=0)
            pre[u]["aqk"] = b * pre[u]["dmask"]

    heads = [(d, h) for d in range(2) for h in range(A_HEADS)]
    st = {dh: st_ref[dh[0], dh[1]] for dh in heads}
    for p in range(nch):
        us = [(d, (nch - 1 - p) if d == 1 else p, h) for d, h in heads]
        ws = [_mm(pre[u]["wq"], st[u[0], u[2]]) for u in us]
        v_new = [pre[u]["u"] - w[:CHUNK] for u, w in zip(us, ws)]
        outs = [w[CHUNK:] + _mm(pre[u]["aqk"], v) for u, w, v in zip(us, ws, v_new)]
        for u, v in zip(us, v_new):
            st[u[0], u[2]] = pre[u]["dl"] * st[u[0], u[2]] + _mm(pre[u]["kg_t"], v)
        for d, (_, _, o_ref) in enumerate(refs):
            c = us[d * A_HEADS][1]
            o_ref[c * CHUNK:(c + 1) * CHUNK, :] = jnp.concatenate(outs[d * A_HEADS:(d + 1) * A_HEADS], axis=1)
    for dh in heads:
        st_ref[dh[0], dh[1]] = st[dh]


def _gdn_scan(lay, qkv, small, a_log, dt_bias):
    al = a_log.reshape(1, 2 * A_HEADS)
    db = dt_bias.reshape(1, 2 * A_HEADS)
    return (_gdn_kernel, (qkv, small), (al, db, al.T, db.T), A_HEADS * A_DV,
            [pltpu.VMEM((2, A_HEADS, A_DK, A_DV), F32)])


FFT_N1 = 64
FFT_KB = 8


def _dft_consts(n2):
    n1 = FFT_N1
    m = n1 * n2
    k1 = np.arange(n1, dtype=np.float64)
    ang1 = 2.0 * np.pi * np.outer(k1, k1) / n1
    f1_full = np.concatenate([np.cos(ang1), -np.sin(ang1)], axis=0)
    f1 = f1_full[:, :n1 // 2]
    f3 = np.concatenate([np.cos(ang1[:, :n1 // 2]).T, -np.sin(ang1[:, :n1 // 2]).T], axis=1) / m
    j = np.arange(n2, dtype=np.float64)
    theta = 2.0 * np.pi * (np.outer(j, j)[None] / n2 + (k1[:, None, None] * j[None, None, :]) / m)
    wr, wi = np.cos(theta), -np.sin(theta)
    w2 = np.concatenate([np.concatenate([wr, -wi], axis=2), np.concatenate([wi, wr], axis=2)], axis=1)
    phi = 2.0 * np.pi * (np.outer(j, j)[None] / n2 + (k1[:, None, None] * j[None, :, None]) / m)
    vr, vi = np.cos(phi), np.sin(phi)
    v2 = np.concatenate([np.concatenate([vr, -vi], axis=2), np.concatenate([vi, vr], axis=2)], axis=1)
    return f1_full, f1, w2, v2, f3


FFT_KV = FFT_N1 // 2 + 1
FFT_KH = 40


def _half_spectrum_consts(consts):
    _, f1, w2, v2, f3 = consts
    n1 = FFT_N1
    f1h = np.zeros((2 * FFT_KH, n1 // 2))
    f1h[:FFT_KV] = f1[:FFT_KV]
    f1h[FFT_KH:FFT_KH + FFT_KV] = f1[n1:n1 + FFT_KV]
    w2h = np.zeros((FFT_KH,) + w2.shape[1:])
    w2h[:FFT_KV] = w2[:FFT_KV]
    v2h = np.zeros((FFT_KH,) + v2.shape[1:])
    v2h[:FFT_KV] = v2[:FFT_KV]
    weight = np.full((FFT_KV,), 2.0)
    weight[0] = weight[-1] = 1.0
    f3h = np.zeros((n1 // 2, 2 * FFT_KH))
    f3h[:, :FFT_KV] = f3[:, :FFT_KV] * weight
    f3h[:, FFT_KH:FFT_KH + FFT_KV] = f3[:, n1:n1 + FFT_KV] * weight
    return f1h, w2h, v2h, f3h


def _left_mm_kernel(w_ref, x_ref, o_ref):
    o_ref[...] = _mm_3x(w_ref[...], x_ref[...])


def _left_mm(wm, x):
    bs, kdim, ncols = x.shape
    mdim = wm.shape[0]
    tn = min(ncols, 4096)
    xspec = lambda rows: pl.BlockSpec((None, rows, tn), lambda b, j: (b, 0, j))
    return pl.pallas_call(
        _left_mm_kernel, grid=(bs, ncols // tn), in_specs=[_full(wm.shape), xspec(kdim)], out_specs=xspec(mdim),
        out_shape=jax.ShapeDtypeStruct((bs, mdim, ncols), F32),
        compiler_params=_cparams(("parallel", "parallel")), name="filter_dft_outer",
    )(wm, x)


def _filt_mid_kernel(a_ref, w_ref, o_ref, *, n2):
    for kk in range(FFT_KB):
        a = jnp.concatenate([a_ref[0, kk], a_ref[1, kk]], axis=0)
        xf = _mm_3x(w_ref[kk], a)
        o_ref[kk, 0] = xf[:n2]
        o_ref[kk, 1] = xf[n2:]


def _filt_mid_stage(a, w2):
    bs, _, n1, n2, ch = a.shape
    return pl.pallas_call(
        functools.partial(_filt_mid_kernel, n2=n2),
        grid=(bs, n1 // FFT_KB),
        in_specs=[pl.BlockSpec((None, 2, FFT_KB, n2, ch), lambda b, j: (b, 0, j, 0, 0)),
                  pl.BlockSpec((FFT_KB, 2 * n2, 2 * n2), lambda b, j: (j, 0, 0))],
        out_specs=pl.BlockSpec((None, FFT_KB, 2, n2, ch), lambda b, j: (b, j, 0, 0, 0)),
        out_shape=jax.ShapeDtypeStruct((bs, n1, 2, n2, ch), F32),
        compiler_params=_cparams(("parallel", "parallel")), name="filter_dft_mid",
    )(a, w2)


FILT_ROWS = 512


def _filter_kernel(z_ref, win_ref, sel_ref, w1_ref, b1_ref, w2_ref, b2_ref, wo_ref, fr_ref, o_ref):
    fr = fr_ref[...]
    h = jnp.sin(fr * (_mm_3x(z_ref[...], w1_ref[...]) + b1_ref[...]))
    for i in range(B_INNER_MLPS):
        h = jnp.sin(fr * (_mm_3x(h, w2_ref[i]) + b2_ref[i]))
    ho = _mm_3x(h, wo_ref[...])
    fwd = sel_ref[...] > 0.5
    win = win_ref[...]
    for o in range(2):
        base = o * 2 * B_CH
        o_ref[o] = jnp.where(fwd, ho[:, base:base + B_CH], ho[:, base + B_CH:base + 2 * B_CH]) * win


def _hyena_filter_time(l, w1, b1, w2, b2, w_out, freq):
    bands = (B_EMB - 1) // 2
    circ = jnp.arange(2 * l)
    pos = jnp.where(circ < l, circ, 2 * l - circ)
    pos = jnp.where(circ == l, 0, pos)
    t = (jnp.linspace(0.0, 1.0, l, dtype=F32)[pos])[:, None]
    ang = 2.0 * math.pi * pos.astype(F32)[:, None] / l
    fr = jnp.linspace(1e-4, bands - 1, bands, dtype=F32)[None, :]
    z = jnp.concatenate([t, jnp.cos(fr * ang), -jnp.sin(fr * ang)], axis=-1)
    z = jnp.pad(z, ((0, 0), (0, 128 - B_EMB)))
    max_decay = math.log(B_DECAY_TARGET) / B_DECAY_SHORT_PCT
    min_decay = math.log(B_DECAY_TARGET) / B_DECAY_LONG_PCT
    deltas = jnp.abs(jnp.linspace(min_decay, max_decay, B_CH, dtype=F32))
    win = (jnp.exp(-t * deltas) + B_WINDOW_SHIFT) * (circ != l).astype(F32)[:, None]
    sel = (circ < l).astype(F32)[:, None]
    w1p = jnp.pad(w1, ((0, 128 - B_EMB), (0, 0)))
    rows = min(FILT_ROWS, 2 * l)
    return pl.pallas_call(
        _filter_kernel,
        grid=(2 * l // rows,),
        in_specs=[pl.BlockSpec((rows, 128), lambda i: (i, 0)),
                  pl.BlockSpec((rows, B_CH), lambda i: (i, 0)),
                  pl.BlockSpec((rows, 1), lambda i: (i, 0)),
                  _full(w1p.shape), _full((1, B_FFN)), _full(w2.shape), _full((B_INNER_MLPS, 1, B_FFN)),
                  _full(w_out.shape), _full((1, B_FFN))],
        out_specs=pl.BlockSpec((2, rows, B_CH), lambda i: (0, i, 0)),
        out_shape=jax.ShapeDtypeStruct((2, 2 * l, B_CH), F32),
        compiler_params=_cparams(("parallel",)), name="hyena_filter",
    )(z, win, sel, w1p, b1.reshape(1, B_FFN), w2, b2.reshape(B_INNER_MLPS, 1, B_FFN), w_out,
      freq.reshape(1, B_FFN))


def _hyena_spectrum(l, consts, w1, b1, w2m, b2, w_out, freq):
    f1_full, _, w2, _, _ = consts
    n2 = 2 * l // FFT_N1
    g_time = _hyena_filter_time(l, w1, b1, w2m, b2, w_out, freq)
    f1_rows = np.concatenate([f1_full[:FFT_KH], f1_full[FFT_N1:FFT_N1 + FFT_KH]], axis=0)
    a = _left_mm(jnp.asarray(f1_rows, F32), g_time.reshape(2, FFT_N1, n2 * B_CH))
    return _filt_mid_stage(a.reshape(2, 2, FFT_KH, n2, B_CH), jnp.asarray(w2[:FFT_KH], F32))


HY_SLAB = 128


def _hy_pitch(n2c):
    return n2c + 8 if (n2c // 8) % 2 == 0 else n2c + 16


HY_KG = 20
HY_JG = 16


def _hyena_lat_kernel(zv_ref, z1_ref, z2_ref, cw_ref, cb_ref, g_ref, bias_ref, f1_ref, w2_ref, v2_ref, f3_ref,
                      o_ref, useq, ur, a_re, a_im, cbuf, *, col_mode, seq):
    n1h = FFT_N1 // 2
    n2c = seq // n1h
    pitch = _hy_pitch(n2c)
    grows = seq // GRID_W
    row_id = lax.broadcasted_iota(jnp.int32, (seq, HY_SLAB), 0)

    def short_conv(z_ref, k):
        z = z_ref[...]
        w = cw_ref[k]
        if col_mode:
            g0, gl = z[:GRID_W], z[seq - GRID_W:]
            cc = row_id[:GRID_W]
            wrap_p = jnp.where(cc == 0, 0.0, pltpu.roll(gl, 1, axis=0))
            wrap_n = jnp.where(cc == GRID_W - 1, 0.0, pltpu.roll(g0, GRID_W - 1, axis=0))
            prev = jnp.concatenate([wrap_p, z[:seq - GRID_W]], axis=0)
            nxt = jnp.concatenate([z[GRID_W:], wrap_n], axis=0)
        else:
            prev = jnp.where(row_id == 0, 0.0, pltpu.roll(z, 1, axis=0))
            nxt = jnp.where(row_id == seq - 1, 0.0, pltpu.roll(z, seq - 1, axis=0))
        return prev * w[0:1] + z * w[1:2] + nxt * w[2:3] + cb_ref[k]

    def seq_start(j):
        return (GRID_W * lax.rem(j, grows) + j // grows) if col_mode else j

    seq_stride = 2 if col_mode else pitch
    seq_ref = ur if col_mode else useq

    def put_seq(val):
        if col_mode:
            ur[...] = val
        else:
            for n1 in range(n1h):
                useq[n1 * pitch:n1 * pitch + n2c, :] = val[n1 * n2c:(n1 + 1) * n2c]

    def conv_out():
        if col_mode:
            return cbuf[...]
        return jnp.concatenate([cbuf[n1 * pitch:n1 * pitch + n2c, :] for n1 in range(n1h)], axis=0)

    y_prev = short_conv(zv_ref, 0)
    x_next = (short_conv(z1_ref, 1), short_conv(z2_ref, 2))
    put_seq(y_prev)
    f1 = f1_ref[...]
    f3 = f3_ref[...]
    jgrp = min(HY_JG, n2c)
    for order in range(2):
        def stage1(jg, carry):
            js = [jg * jgrp + jj for jj in range(jgrp)]
            xs = [seq_ref[pl.ds(seq_start(j), n1h, stride=seq_stride), :] for j in js]
            outs = [_mm(f1, x) for x in xs]
            for j, a in zip(js, outs):
                a_re[pl.ds(j, FFT_KH, stride=pitch), :] = a[:FFT_KH]
                a_im[pl.ds(j, FFT_KH, stride=pitch), :] = a[FFT_KH:]
            return carry
        lax.fori_loop(0, n2c // jgrp, stage1, 0)

        def mid(kg, carry):
            k1s = [kg * HY_KG + kk for kk in range(HY_KG)]
            offs = [pl.multiple_of(k1 * pitch, 8) for k1 in k1s]
            xin = [jnp.concatenate([a_re[pl.ds(o, n2c), :], a_im[pl.ds(o, n2c), :]], axis=0) for o in offs]
            xf = [_mm(w2_ref[k1], a) for k1, a in zip(k1s, xin)]
            ys = []
            for k1, x in zip(k1s, xf):
                xr, xi = x[:n2c], x[n2c:]
                gr, gi = g_ref[order, k1, 0], g_ref[order, k1, 1]
                ys.append(jnp.concatenate([xr * gr - xi * gi, xr * gi + xi * gr], axis=0))
            bm = [_mm(v2_ref[k1], y) for k1, y in zip(k1s, ys)]
            for o, b in zip(offs, bm):
                a_re[pl.ds(o, n2c), :] = b[:n2c]
                a_im[pl.ds(o, n2c), :] = b[n2c:]
            return carry
        lax.fori_loop(0, FFT_KH // HY_KG, mid, 0)

        def stage3(jg, carry):
            js = [jg * jgrp + jj for jj in range(jgrp)]
            bs = [jnp.concatenate([a_re[pl.ds(j, FFT_KH, stride=pitch), :],
                                   a_im[pl.ds(j, FFT_KH, stride=pitch), :]], axis=0) for j in js]
            outs = [_mm(f3, b) for b in bs]
            for j, y in zip(js, outs):
                cbuf[pl.ds(seq_start(j), n1h, stride=seq_stride), :] = y
            return carry
        lax.fori_loop(0, n2c // jgrp, stage3, 0)

        y_prev = x_next[order] * (conv_out() + y_prev * bias_ref[order:order + 1, :])
        if order == 0:
            put_seq(y_prev)
    o_ref[...] = y_prev


def _hyena_fused(lay, z, conv_w, conv_b, spec, bias, consts, seq, row0, s0, nslab, col_mode):
    f1, w2, v2, f3 = _half_spectrum_consts(consts)
    n2c = seq // (FFT_N1 // 2)
    assert FFT_KH % HY_KG == 0 and n2c % min(HY_JG, n2c) == 0
    pitch = _hy_pitch(n2c)
    cps = B_CH // HY_SLAB
    zspec = lambda k: pl.BlockSpec((seq, HY_SLAB), lambda j, b, k=k: (row0 + b, k * cps + s0 + j))
    once = lambda a: pl.BlockSpec(a.shape, lambda j, b: (0,) * a.ndim, pipeline_mode=pl.Buffered(1))
    mats = (jnp.asarray(f1, BF16), jnp.asarray(w2, BF16), jnp.asarray(v2, BF16), jnp.asarray(f3, BF16))
    seq_rows = (FFT_N1 // 2) * pitch
    return pl.pallas_call(
        functools.partial(_hyena_lat_kernel, col_mode=col_mode, seq=seq),
        grid=(nslab, lay.b),
        in_specs=[zspec(0), zspec(1), zspec(2),
                  pl.BlockSpec((3, B_SHORT, HY_SLAB), lambda j, b: (0, 0, s0 + j)),
                  pl.BlockSpec((3, 1, HY_SLAB), lambda j, b: (0, 0, s0 + j)),
                  pl.BlockSpec((2, FFT_KH, 2, n2c, HY_SLAB), lambda j, b: (0, 0, 0, 0, s0 + j),
                               pipeline_mode=pl.Buffered(1)),
                  pl.BlockSpec((2, HY_SLAB), lambda j, b: (0, s0 + j))] + [once(m) for m in mats],
        out_specs=pl.BlockSpec((seq, HY_SLAB), lambda j, b: (b, j)),
        out_shape=jax.ShapeDtypeStruct((lay.b * seq, nslab * HY_SLAB), F32),
        scratch_shapes=[pltpu.VMEM((seq_rows, HY_SLAB), F32), pltpu.VMEM((seq, HY_SLAB), F32),
                        pltpu.VMEM((FFT_KH * pitch, HY_SLAB), F32), pltpu.VMEM((FFT_KH * pitch, HY_SLAB), F32),
                        pltpu.VMEM((seq if col_mode else seq_rows, HY_SLAB), F32)],
        compiler_params=_cparams(("parallel", "parallel")),
        name="hyena_col" if col_mode else "hyena_row",
    )(z, z, z, jnp.transpose(conv_w, (1, 0, 2)), conv_b.reshape(3, 1, B_CH), spec, bias, *mats)


def _hyena(lay, z, conv_w, conv_b, spec_c, spec_x, bias, consts_c, consts_x):
    assert lay.n_ctx % lay.seq == 0
    half = B_CH // 2 // HY_SLAB
    args = (lay, z, conv_w, conv_b)
    yr = _hyena_fused(*args, spec_x, bias, consts_x, lay.seq, lay.n_ctx // lay.seq, 0, half, False)
    ycol = _hyena_fused(*args, spec_x, bias, consts_x, lay.seq, lay.n_ctx // lay.seq, half, half, True)
    yl = jnp.concatenate([yr, ycol], axis=1)
    if spec_c is None:
        return yl
    yc = _hyena_fused(*args, spec_c, bias, consts_c, lay.ctx, 0, 0, 2 * half, False)
    return jnp.concatenate([yc, yl], axis=0)


def _head_norm_gate(o, gate, norm_w, heads, width):
    outs = []
    for h in range(heads):
        oh = o[:, h * width:(h + 1) * width]
        y = oh * lax.rsqrt(jnp.mean(oh * oh, axis=-1, keepdims=True) + NORM_EPS) * norm_w
        outs.append(y * _silu(gate[:, h * width:(h + 1) * width]))
    return jnp.concatenate(outs, axis=1)


def _merge_kernel(x_ref, mod_ref, nw_ref, a0_ref, a1_ref, ag_ref, yb_ref, c0_ref, c1_ref, cx_ref, cz_ref,
                  d0_ref, d1_ref, dg_ref, an_ref, cd_ref, cn_ref, dn_ref, wg_ref, wb_ref, wo_ref, o_ref):
    xv = x_ref[...]
    m = mod_ref[...]
    hb = _prenorm(xv, nw_ref[1:2, :], m[3:4, :], m[4:5, :]).astype(BF16)
    ya = _head_norm_gate(a0_ref[...] + a1_ref[...], ag_ref[...], an_ref[...], A_HEADS, A_DV)
    yd = _head_norm_gate(d0_ref[...] + d1_ref[...], dg_ref[...], dn_ref[...], D_HEADS, D_DV)
    yc = (c0_ref[...] + c1_ref[...] + cd_ref[...] * cx_ref[...]) * _silu(cz_ref[...])
    gw = C_INNER // C_GROUPS
    cn = cn_ref[...]
    yc = jnp.concatenate(
        [yc[:, g * gw:(g + 1) * gw]
         * lax.rsqrt(jnp.mean(yc[:, g * gw:(g + 1) * gw] ** 2, axis=-1, keepdims=True) + NORM_EPS)
         * cn[:, g * gw:(g + 1) * gw] for g in range(C_GROUPS)], axis=1)
    acc = jnp.zeros(xv.shape, F32)
    for k, y in enumerate((ya, yb_ref[...], yc, yd)):
        gate = jax.nn.sigmoid(jnp.dot(hb, wg_ref[:, k * D_MODEL:(k + 1) * D_MODEL], preferred_element_type=F32))
        acc = acc + gate * jnp.dot(y.astype(BF16), wb_ref[k], preferred_element_type=F32)
    o_ref[...] = xv + m[5:6, :] * jnp.dot(acc.astype(BF16), wo_ref[...], preferred_element_type=F32)


def _merge(lay, x, mods, norm_w, a0, a1, ag, yb, c0, c1, cx, cz, d0, d1, dg, an, cd, cn, dn, wg, wb, wo, l,
           latent_only=False):
    n, d = x.shape
    t0 = lay.n_ctx // TOK_TILE if latent_only else 0
    row = lambda w, off=t0: pl.BlockSpec((TOK_TILE, w), lambda i: (i + off, 0))
    branch_rows = [row(BRANCH_W)] * 3 + [row(BRANCH_W, 0 if latent_only else t0)] + [row(BRANCH_W)] * 7
    return pl.pallas_call(
        _merge_kernel,
        grid=(n // TOK_TILE - t0,),
        in_specs=[row(d), _mod_spec(lay, mods, l, TOK_TILE, t0), _of_layer(norm_w, l)]
                 + branch_rows + [_of_layer(p, l) for p in (an, cd, cn, dn)]
                 + [_of_layer(w, l, once=True) for w in (wg, wb, wo)],
        out_specs=row(d),
        out_shape=jax.ShapeDtypeStruct((n, d), F32),
        compiler_params=_cparams(("parallel",)), name="merge",
    )(x, mods, norm_w, a0, a1, ag, yb, c0, c1, cx, cz, d0, d1, dg, an, cd, cn, dn, wg, wb, wo)


def kernel(x, c, ctx, c_ctx, w_ada, b_ada, norm_w, ffn_up, ffn_down, w_in, gdn_conv, gdn_a_log, gdn_dt_bias, gdn_norm, hy_conv_w, hy_conv_b, hy_w1, hy_b1, hy_w2, hy_b2, hy_wout, hy_freq, hy_bias, ssd_conv_w, ssd_conv_b, ssd_a_log, ssd_dt_bias, ssd_d, ssd_norm, gla_gk_w, gla_gk_b, gla_norm, w_branch, w_out, final_norm):
    b, seq, d = x.shape
    ctx_len = ctx.shape[1]
    depth = w_ada.shape[0]
    lay = _Layout(b, ctx_len, seq)
    consts_c = _dft_consts(2 * ctx_len // FFT_N1)
    consts_x = _dft_consts(2 * seq // FFT_N1)

    rp = -(-(1 + b) // 8) * 8
    cond = jnp.concatenate([c_ctx[None, :], c, jnp.zeros((rp - 1 - b, d), F32)], axis=0)
    mods = _ada(cond, w_ada, b_ada).reshape(depth, rp, N_MOD, d)

    ffn_up_b, ffn_down_b = ffn_up.astype(BF16), ffn_down.astype(BF16)
    w_r, w_gate = _rearrange_w_in(w_in)
    w_branch_b, w_out_b = w_branch.astype(BF16), w_out.astype(BF16)
    ssd_conv_b3 = ssd_conv_b.reshape(depth, 1, C_XBC)
    an = gdn_norm.reshape(depth, 1, A_DV)
    cd = jnp.repeat(ssd_d, C_HEADDIM, axis=-1).reshape(depth, 1, C_INNER)
    cn = ssd_norm.reshape(depth, 1, C_INNER)
    dn = gla_norm.reshape(depth, 1, D_DV)

    xf = jnp.concatenate([ctx.reshape(b * ctx_len, d), x.reshape(b * seq, d)], axis=0)
    for l in range(depth):
        xf = _ffn(lay, xf, mods, norm_w, ffn_up_b, ffn_down_b, l, 0)

        qkv, a_gate, zb, c_z, xbc, d_qkv, d_gate, small = _inproj(
            lay, xf, mods, norm_w, w_r, gdn_conv, ssd_conv_w, ssd_conv_b3, l)
        (a0, a1), (c0, c1), (d0, d1) = _scan_call(lay, [
            _gdn_scan(lay, qkv, small, gdn_a_log[l], gdn_dt_bias[l]),
            _ssd_scan(lay, xbc, small, ssd_a_log[l], ssd_dt_bias[l]),
            _gla_scan(lay, d_qkv, small, gla_gk_w[l], gla_gk_b[l])])
        last = l == depth - 1
        fargs = (hy_w1[l], hy_b1[l], hy_w2[l], hy_b2[l], hy_wout[l], hy_freq[l])
        spec_c = None if last else _hyena_spectrum(ctx_len, consts_c, *fargs)
        spec_x = _hyena_spectrum(seq, consts_x, *fargs)
        yb = _hyena(lay, zb, hy_conv_w[l], hy_conv_b[l], spec_c, spec_x, hy_bias[l], consts_c, consts_x)

        xf = _merge(lay, xf, mods, norm_w, a0, a1, a_gate, yb, c0, c1, xbc, c_z, d0, d1, d_gate,
                    an, cd, cn, dn, w_gate, w_branch_b, w_out_b, l, latent_only=last)
        xf = _ffn(lay, xf, mods, norm_w, ffn_up_b, ffn_down_b, l, 2, final_w=final_norm if last else None)
    return xf.reshape(b, seq, d)
```

```python
import functools
import math

import numpy as np
import jax
import jax.numpy as jnp
from jax import lax
from jax.experimental import pallas as pl
from jax.experimental.pallas import tpu as pltpu

F32 = jnp.float32
BF16 = jnp.bfloat16
HI = lax.Precision.HIGHEST

D_MODEL = 1024
GRID_W = 64
CHUNK = 64
NORM_EPS = 1e-6
N_MOD = 9
D_FF = 2816
SHORT_CONV = 5

A_HEADS, A_DK, A_DV = 4, 128, 128
A_QKV = A_HEADS * (2 * A_DK + A_DV)
B_CH, B_SHORT, B_EMB, B_FFN, B_INNER_MLPS = 512, 3, 33, 64, 2
B_WINDOW_SHIFT, B_DECAY_SHORT_PCT, B_DECAY_LONG_PCT, B_DECAY_TARGET = 0.05, 0.3, 1.5, 1e-2
C_HEADS, C_HEADDIM, C_GROUPS, C_STATE = 8, 64, 2, 64
C_INNER = C_HEADS * C_HEADDIM
C_XBC = C_INNER + 2 * C_GROUPS * C_STATE
D_HEADS, D_DK, D_DV, D_RANK = 4, 64, 128, 16
D_GATE_NORM = 16.0
N_BRANCH, BRANCH_W = 4, 512

IN_SIZES = (A_QKV, A_HEADS * A_DV, 2 * A_HEADS, 2 * A_HEADS, 3 * B_CH, C_INNER, C_XBC, 2 * C_HEADS,
            D_HEADS * (2 * D_DK + D_DV), D_HEADS * D_DV, 2 * D_RANK, N_BRANCH * D_MODEL)
IN_OFFS = tuple(int(v) for v in np.cumsum((0,) + IN_SIZES))

ROW_TILE = 256
TOK_TILE = 512
SMALL_W = 128
VMEM_LIMIT = 56 * 1024 * 1024


def _cparams(sem):
    return pltpu.CompilerParams(dimension_semantics=sem, vmem_limit_bytes=VMEM_LIMIT)


def _mm(a, b):
    return jnp.dot(a.astype(BF16), b.astype(BF16), preferred_element_type=F32)


def _mm_nt(a, b):
    return lax.dot_general(a.astype(BF16), b.astype(BF16), (((1,), (1,)), ((), ())),
                           preferred_element_type=F32)


def _mm_hi(a, b):
    return jnp.dot(a, b, precision=HI, preferred_element_type=F32)


def _bf16_parts(v, parts):
    out, rest = [], v
    for _ in range(parts):
        hi = rest.astype(BF16)
        out.append(hi)
        rest = rest - hi.astype(F32)
    return out


def _mm_3x(a, b):
    ah, al = _bf16_parts(a, 2)
    bh, bl = _bf16_parts(b, 2)
    dot = functools.partial(jnp.dot, preferred_element_type=F32)
    return dot(ah, bh) + (dot(ah, bl) + dot(al, bh))


def _mm_sel_l(sel, v, parts=3):
    sb = sel.astype(BF16)
    return sum(jnp.dot(sb, p, preferred_element_type=F32) for p in _bf16_parts(v, parts))


def _mm_sel_r(v, sel, parts=3):
    sb = sel.astype(BF16)
    return sum(jnp.dot(p, sb, preferred_element_type=F32) for p in _bf16_parts(v, parts))


def _silu(v):
    return v * jax.nn.sigmoid(v)


def _softplus(v):
    return jnp.maximum(v, 0.0) + jnp.log1p(jnp.exp(-jnp.abs(v)))


def _log_sigmoid(v):
    return jnp.minimum(v, 0.0) - jnp.log1p(jnp.exp(-jnp.abs(v)))


def _prenorm(xv, gain, shift, scale):
    ms = jnp.mean(xv * xv, axis=-1, keepdims=True)
    return (xv * lax.rsqrt(ms + NORM_EPS) * gain) * (1.0 + scale) + shift


def _full(shape):
    nd = len(shape)
    return pl.BlockSpec(shape, lambda *_: (0,) * nd)


class _Layout:
    def __init__(self, batch, ctx_len, seq):
        assert ctx_len % ROW_TILE == 0 and seq % ROW_TILE == 0
        assert (batch * ctx_len) % TOK_TILE == 0 and seq % TOK_TILE == 0
        self.b, self.ctx, self.seq = batch, ctx_len, seq
        self.n_ctx = batch * ctx_len
        self.n = batch * (ctx_len + seq)
        self.tc, self.tl = ctx_len // ROW_TILE, seq // ROW_TILE
        self.nct = batch * self.tc
        self.tiles = self.n // ROW_TILE

    def mod_index(self, i, tile=TOK_TILE):
        nct = self.n_ctx // tile
        return jnp.where(i < nct, 0, 1 + (i - nct) // (self.seq // tile))

    def seg_first(self, i):
        return jnp.where(i < self.nct, lax.rem(i, self.tc) == 0, lax.rem(i - self.nct, self.tl) == 0)

    def seg_last(self, i):
        return jnp.where(i < self.nct, lax.rem(i, self.tc) == self.tc - 1,
                         lax.rem(i - self.nct, self.tl) == self.tl - 1)

    def fwd_tile(self, b, s):
        return jnp.where(s < self.tc, b * self.tc + s, self.nct + b * self.tl + (s - self.tc))

    def rev_tile(self, b, s):
        return jnp.where(s < self.tc, b * self.tc + (self.tc - 1 - s),
                         self.nct + b * self.tl + (self.tl - 1 - (s - self.tc)))


def _ada_kernel(c_ref, w_ref, b_ref, o_ref):
    o_ref[...] = _mm_hi(_silu(c_ref[...]), w_ref[...]) + b_ref[...]


def _ada(cond, w_ada, b_ada):
    depth, d, nm = w_ada.shape
    rp = cond.shape[0]
    tn = 1152
    return pl.pallas_call(
        _ada_kernel,
        grid=(depth, nm // tn),
        in_specs=[_full((rp, d)),
                  pl.BlockSpec((None, d, tn), lambda l, j: (l, 0, j)),
                  pl.BlockSpec((None, 1, tn), lambda l, j: (l, 0, j))],
        out_specs=pl.BlockSpec((None, rp, tn), lambda l, j: (l, 0, j)),
        out_shape=jax.ShapeDtypeStruct((depth, rp, nm), F32),
        compiler_params=_cparams(("parallel", "parallel")), name="ada",
    )(cond, w_ada, b_ada.reshape(depth, 1, nm))


FF_CHUNK = 256


def _ffn_kernel(x_ref, mod_ref, nw_ref, wup_ref, wdn_ref, *rest, sub):
    o_ref = rest[-1]
    xv = x_ref[...]
    m = mod_ref[...]
    h = _prenorm(xv, nw_ref[sub:sub + 1, :], m[3 * sub:3 * sub + 1, :], m[3 * sub + 1:3 * sub + 2, :])
    hb = h.astype(BF16)
    acc = jnp.zeros(xv.shape, F32)
    for c in range(D_FF // FF_CHUNK):
        lo = c * FF_CHUNK
        a = jnp.dot(hb, wup_ref[:, lo:lo + FF_CHUNK], preferred_element_type=F32)
        g = jnp.dot(hb, wup_ref[:, D_FF + lo:D_FF + lo + FF_CHUNK], preferred_element_type=F32)
        acc = acc + jnp.dot((_silu(a) * g).astype(BF16), wdn_ref[lo:lo + FF_CHUNK, :],
                            preferred_element_type=F32)
    out = xv + 0.5 * m[3 * sub + 2:3 * sub + 3, :] * acc
    if len(rest) == 2:
        out = out * lax.rsqrt(jnp.mean(out * out, axis=-1, keepdims=True) + NORM_EPS) * rest[0][...]
    o_ref[...] = out


def _of_layer(a, l, *lead, once=False):
    idx = (l,) + lead
    block = (None,) * len(idx) + a.shape[len(idx):]
    mode = dict(pipeline_mode=pl.Buffered(1)) if once else {}
    return pl.BlockSpec(block, lambda *_: idx + (0,) * (a.ndim - len(idx)), **mode)


def _mod_spec(lay, mods, l, tile, t0=0):
    return pl.BlockSpec((None, None, N_MOD, mods.shape[-1]), lambda i: (l, lay.mod_index(i + t0, tile), 0, 0))


def _ffn(lay, x, mods, norm_w, wup, wdn, l, sub, final_w=None):
    n, d = x.shape
    t0 = 0 if final_w is None else lay.n_ctx // TOK_TILE
    extra = [] if final_w is None else [final_w.reshape(1, d)]
    return pl.pallas_call(
        functools.partial(_ffn_kernel, sub=sub),
        grid=(n // TOK_TILE - t0,),
        in_specs=[pl.BlockSpec((TOK_TILE, d), lambda i: (i + t0, 0)),
                  _mod_spec(lay, mods, l, TOK_TILE, t0),
                  _of_layer(norm_w, l),
                  _of_layer(wup, l, sub // 2, once=True),
                  _of_layer(wdn, l, sub // 2, once=True)]
                 + [_full(e.shape) for e in extra],
        out_specs=pl.BlockSpec((TOK_TILE, d), lambda i: (i, 0)),
        out_shape=jax.ShapeDtypeStruct((n - t0 * TOK_TILE, d), F32),
        compiler_params=_cparams(("parallel",)), name="ffn",
    )(x, mods, norm_w, wup, wdn, *extra)


PROJ_W = (A_QKV, A_HEADS * A_DV, 3 * B_CH, C_INNER, C_XBC, D_HEADS * (2 * D_DK + D_DV), D_HEADS * D_DV, SMALL_W)
PROJ_O = tuple(int(v) for v in np.cumsum((0,) + PROJ_W))


def _rearrange_w_in(w_in):
    o = IN_OFFS
    col = lambda a, b: w_in[..., o[a]:o[b]]
    small = jnp.concatenate([col(2, 4), col(7, 8), col(10, 11),
                             jnp.zeros(w_in.shape[:-1] + (SMALL_W - 64,), w_in.dtype)], axis=-1)
    w_r = jnp.concatenate([col(0, 1), col(1, 2), col(4, 5), col(5, 6), col(6, 7), col(8, 9), col(9, 10), small],
                          axis=-1)
    return w_r.astype(BF16), col(11, 12).astype(BF16)


PROJ_HALO = 16
PROJ_CONV = (0, 4)


def _inproj_kernel(xp_ref, x_ref, xn_ref, mod_ref, nw_ref, w_ref, cwa_ref, cwc_ref, cbc_ref, *rest, lay):
    o_refs, zbuf = rest[:-1], rest[-1]
    i = pl.program_id(0)
    m = mod_ref[...]
    norm = lambda r: _prenorm(r[...], nw_ref[1:2, :], m[3:4, :], m[4:5, :]).astype(BF16)
    hb = norm(x_ref)
    hb_all = jnp.concatenate([norm(xp_ref), hb, norm(xn_ref)], axis=0)
    keep_p = jnp.where(lay.seg_first(i), 0.0, 1.0)
    keep_n = jnp.where(lay.seg_last(i), 0.0, 1.0)
    convs = {PROJ_CONV[0]: (cwa_ref, None), PROJ_CONV[1]: (cwc_ref, cbc_ref)}
    for k, o_ref in enumerate(o_refs):
        wk = w_ref[:, PROJ_O[k]:PROJ_O[k + 1]]
        if k not in convs:
            o_ref[...] = jnp.dot(hb, wk, preferred_element_type=F32)
            continue
        cw_ref, cb_ref = convs[k]
        width = PROJ_W[k]
        z = jnp.dot(hb_all, wk, preferred_element_type=F32)
        zbuf[0:PROJ_HALO, 0:width] = z[:PROJ_HALO] * keep_p
        zbuf[PROJ_HALO:PROJ_HALO + ROW_TILE, 0:width] = z[PROJ_HALO:PROJ_HALO + ROW_TILE]
        zbuf[PROJ_HALO + ROW_TILE:, 0:width] = z[PROJ_HALO + ROW_TILE:] * keep_n
        taps = cw_ref.shape[0]
        left = (taps - 1) // 2
        acc = None
        for t in range(taps):
            off = t - left
            term = zbuf[PROJ_HALO + off:PROJ_HALO + off + ROW_TILE, 0:width] * cw_ref[t:t + 1, :]
            acc = term if acc is None else acc + term
        if cb_ref is not None:
            acc = acc + cb_ref[...]
        o_ref[...] = _silu(acc)


def _inproj(lay, x, mods, norm_w, w_r, conv_a, conv_c, conv_c_bias, l):
    n, d = x.shape
    rh = ROW_TILE // PROJ_HALO
    lasth = n // PROJ_HALO - 1
    return pl.pallas_call(
        functools.partial(_inproj_kernel, lay=lay),
        grid=(n // ROW_TILE,),
        in_specs=[pl.BlockSpec((PROJ_HALO, d), lambda i: (jnp.maximum(i * rh - 1, 0), 0)),
                  pl.BlockSpec((ROW_TILE, d), lambda i: (i, 0)),
                  pl.BlockSpec((PROJ_HALO, d), lambda i: (jnp.minimum((i + 1) * rh, lasth), 0)),
                  _mod_spec(lay, mods, l, ROW_TILE),
                  _of_layer(norm_w, l),
                  _of_layer(w_r, l, once=True),
                  _of_layer(conv_a, l), _of_layer(conv_c, l), _of_layer(conv_c_bias, l)],
        out_specs=[pl.BlockSpec((ROW_TILE, w), lambda i: (i, 0)) for w in PROJ_W],
        out_shape=[jax.ShapeDtypeStruct((n, w), F32) for w in PROJ_W],
        scratch_shapes=[pltpu.VMEM((ROW_TILE + 2 * PROJ_HALO, max(PROJ_W[k] for k in PROJ_CONV)), F32)],
        compiler_params=_cparams(("parallel",)), name="inproj",
    )(x, x, x, mods, norm_w, w_r, conv_a, conv_c, conv_c_bias)


def _tri_masks(rev):
    ii = lax.broadcasted_iota(jnp.int32, (CHUNK, CHUNK), 0)
    jj = lax.broadcasted_iota(jnp.int32, (CHUNK, CHUNK), 1)
    return (ii <= jj, ii < jj) if rev else (ii >= jj, ii > jj)


def _masked_decay(col, row, incl):
    return jnp.where(incl, jnp.exp(jnp.where(incl, col - row, 0.0)), 0.0)


def _scan_call(lay, parts):
    steps = lay.tc + lay.tl
    specs, args, out_specs, out_shape, scratch, counts = [], [], [], [], [], []
    for _, ins_tiled, ins_full, out_w, scr in parts:
        n0 = len(specs)
        for walk in (lay.fwd_tile, lay.rev_tile):
            for a in ins_tiled:
                specs.append(pl.BlockSpec((ROW_TILE, a.shape[1]), lambda b, s, walk=walk: (walk(b, s), 0)))
                args.append(a)
            out_specs.append(pl.BlockSpec((ROW_TILE, out_w), lambda b, s, walk=walk: (walk(b, s), 0)))
            out_shape.append(jax.ShapeDtypeStruct((lay.n, out_w), F32))
        for a in ins_full:
            specs.append(_full(a.shape))
            args.append(a)
        counts.append((len(specs) - n0, len(scr)))
        scratch += list(scr)

    def body(*refs):
        n_in, n_out = len(specs), len(out_specs)
        i, s = 0, n_in + n_out
        for p, (kern, *_) in enumerate(parts):
            ci, cs = counts[p]
            kern(*refs[i:i + ci], *refs[n_in + 2 * p:n_in + 2 * p + 2], *refs[s:s + cs])
            i, s = i + ci, s + cs

    res = pl.pallas_call(
        body,
        grid=(lay.b, steps),
        in_specs=specs,
        out_specs=out_specs,
        out_shape=out_shape,
        scratch_shapes=scratch,
        compiler_params=_cparams(("arbitrary", "arbitrary")), name="scans",
    )(*args)
    return [tuple(res[2 * p:2 * p + 2]) for p in range(len(parts))]


def _ssd_kernel(xf_ref, sf_ref, xr_ref, sr_ref, alog_ref, dtb_ref, alog_t_ref, dtb_t_ref, exp_ref,
                of_ref, or_ref, st_ref):
    @pl.when(pl.program_id(1) == 0)
    def _():
        st_ref[...] = jnp.zeros(st_ref.shape, F32)

    hpg = C_HEADS // C_GROUPS
    gw = hpg * C_HEADDIM
    nch = ROW_TILE // CHUNK
    refs = ((xf_ref, sf_ref, of_ref), (xr_ref, sr_ref, or_ref))
    units, pre = [], {}
    for d, (x_ref, s_ref, _) in enumerate(refs):
        incl, _ = _tri_masks(d == 1)
        m_col = incl.astype(F32)
        m_row = m_col.T
        sm = s_ref[...]
        sm_t = sm.T
        dt_all = _softplus(sm[:, 16:32] + dtb_ref[...])
        a_all = -jnp.exp(alog_ref[...]) * dt_all
        a_t_all = -jnp.exp(alog_t_ref[...]) * _softplus(sm_t[16:32, :] + dtb_t_ref[...])
        bm_t = x_ref[:, C_INNER:C_INNER + C_GROUPS * C_STATE].T
        for c in range(nch):
            rows = slice(c * CHUNK, (c + 1) * CHUNK)
            units.append((d, c))
            pre[d, c] = dict(incl=incl, m_col=m_col, m_row=m_row, a_ch=a_all[rows], a_t=a_t_all[:, rows],
                             dt=dt_all[rows], bm_t=bm_t[:, rows], e=exp_ref[d], xs=x_ref[rows, 0:C_INNER],
                             bm=x_ref[rows, C_INNER:C_INNER + C_GROUPS * C_STATE],
                             cm=x_ref[rows, C_INNER + C_GROUPS * C_STATE:C_XBC])
    for u in units:
        w = pre[u]
        w["ac"] = _mm_sel_l(w["m_col"], w["a_ch"])
        w["ac_t"] = _mm_sel_r(w["a_t"], w["m_row"])
        w["alast"] = jnp.sum(w["a_ch"], axis=0, keepdims=True)
    for u in units:
        w = pre[u]
        w["xdt"] = w["xs"] * _mm_sel_r(w["dt"], w["e"])
        w["eac_e"] = _mm_sel_r(jnp.exp(w["ac"]), w["e"])
        w["xd"] = w["xdt"] * _mm_sel_r(jnp.exp(w["alast"] - w["ac"]), w["e"])
        w["dle"] = _mm_sel_r(jnp.broadcast_to(jnp.exp(w["alast"]), (8, 2 * C_HEADS)), w["e"])[0:1]
    grp = lambda a, g: a[:, g * C_STATE:(g + 1) * C_STATE]
    cb = {(u, g): _mm_nt(grp(pre[u]["cm"], g), grp(pre[u]["bm"], g)) for u in units for g in range(C_GROUPS)}
    upd = {(u, g): _mm(pre[u]["bm_t"][g * C_STATE:(g + 1) * C_STATE], pre[u]["xd"][:, g * gw:(g + 1) * gw])
           for u in units for g in range(C_GROUPS)}
    y_diag = {}
    for u in units:
        w = pre[u]
        for h in range(C_HEADS):
            col = u[0] * C_HEADS + h
            seg = _masked_decay(w["ac"][:, col:col + 1], w["ac_t"][col:col + 1, :], w["incl"])
            y_diag[u, h] = _mm(cb[u, h // hpg] * seg, w["xdt"][:, h * C_HEADDIM:(h + 1) * C_HEADDIM])
    st = {(d, g): st_ref[d, g] for d in range(2) for g in range(C_GROUPS)}
    entry = {}
    for p in range(nch):
        for d in range(2):
            u = (d, (nch - 1 - p) if d == 1 else p)
            for g in range(C_GROUPS):
                entry[u, g] = st[d, g]
                st[d, g] = pre[u]["dle"][:, g * gw:(g + 1) * gw] * st[d, g] + upd[u, g]
    for d in range(2):
        for g in range(C_GROUPS):
            st_ref[d, g] = st[d, g]
    for u in units:
        y_off = [_mm(grp(pre[u]["cm"], g), entry[u, g]) for g in range(C_GROUPS)]
        refs[u[0]][2][u[1] * CHUNK:(u[1] + 1) * CHUNK, :] = (
            jnp.concatenate([y_diag[u, h] for h in range(C_HEADS)], axis=1)
            + pre[u]["eac_e"] * jnp.concatenate(y_off, axis=1))


def _ssd_scan(lay, xbc, small, a_log, dt_bias):
    expand = np.zeros((2, 2 * C_HEADS, C_INNER), np.float32)
    for d in range(2):
        for h in range(C_HEADS):
            expand[d, d * C_HEADS + h, h * C_HEADDIM:(h + 1) * C_HEADDIM] = 1.0
    al = a_log.reshape(1, 2 * C_HEADS)
    db = dt_bias.reshape(1, 2 * C_HEADS)
    return (_ssd_kernel, (xbc, small), (al, db, al.T, db.T, jnp.asarray(expand)), C_INNER,
            [pltpu.VMEM((2, C_GROUPS, C_STATE, C_INNER // C_GROUPS), F32)])


def _gla_kernel(xf_ref, sf_ref, xr_ref, sr_ref, gkw_ref, gkb_ref, of_ref, or_ref, st_ref):
    @pl.when(pl.program_id(1) == 0)
    def _():
        st_ref[...] = jnp.zeros(st_ref.shape, F32)

    nk = D_HEADS * D_DK
    nch = ROW_TILE // CHUNK
    refs = ((xf_ref, sf_ref, of_ref), (xr_ref, sr_ref, or_ref))
    units, pre = [], {}
    for d, (x_ref, s_ref, _) in enumerate(refs):
        incl, _ = _tri_masks(d == 1)
        m_col = incl.astype(F32)
        lr = s_ref[:, 32 + D_RANK * d:32 + D_RANK * (d + 1)]
        g_all = _log_sigmoid(_mm_3x(lr, gkw_ref[d]) + gkb_ref[d:d + 1, :]) / D_GATE_NORM
        v_t = x_ref[:, 2 * nk:].T
        for c in range(nch):
            rows = slice(c * CHUNK, (c + 1) * CHUNK)
            g_ch = g_all[rows]
            gc = _mm_sel_l(m_col, g_ch)
            glast = jnp.sum(g_ch, axis=0, keepdims=True)
            q = x_ref[rows, 0:nk] * (D_DK ** -0.5)
            k = x_ref[rows, nk:2 * nk]
            v = x_ref[rows, 2 * nk:]
            kd = k * jnp.exp(glast - gc)
            qd = q * jnp.exp(gc)
            qr = q * jnp.exp(gc - glast)
            dlast = jnp.exp(glast)
            for h in range(D_HEADS):
                ks = slice(h * D_DK, (h + 1) * D_DK)
                vs = slice(h * D_DV, (h + 1) * D_DV)
                units.append((d, c, h))
                pre[d, c, h] = dict(incl=incl, qr=qr[:, ks], kd=kd[:, ks], qd=qd[:, ks], v=v[:, vs],
                                    v_t=v_t[vs, rows], dl=dlast[:, ks])
    aqk = [jnp.where(pre[u]["incl"], _mm_nt(pre[u]["qr"], pre[u]["kd"]), 0.0) for u in units]
    upd = {u: _mm(pre[u]["v_t"], pre[u]["kd"]) for u in units}
    intra = {u: _mm(a, pre[u]["v"]) for u, a in zip(units, aqk)}
    heads = [(d, h) for d in range(2) for h in range(D_HEADS)]
    st = {dh: st_ref[dh[0], dh[1]] for dh in heads}
    entry = {}
    for p in range(nch):
        for d, h in heads:
            u = (d, (nch - 1 - p) if d == 1 else p, h)
            entry[u] = st[d, h]
            st[d, h] = st[d, h] * pre[u]["dl"] + upd[u]
    for dh in heads:
        st_ref[dh[0], dh[1]] = st[dh]
    inter = {u: _mm_nt(pre[u]["qd"], entry[u]) for u in units}
    for d, (_, _, o_ref) in enumerate(refs):
        for c in range(nch):
            o_ref[c * CHUNK:(c + 1) * CHUNK, :] = jnp.concatenate(
                [intra[d, c, h] + inter[d, c, h] for h in range(D_HEADS)], axis=1)


def _gla_scan(lay, qkv, small, gk_w, gk_b):
    return (_gla_kernel, (qkv, small), (gk_w, gk_b), D_HEADS * D_DV, [pltpu.VMEM((2, D_HEADS, D_DV, D_DK), F32)])


TRI_BLOCK = 16
GDN_GROUP = 32


def _unit_lower_inverse(nms, eye, blk):
    dg = [jnp.where(blk, nm, 0.0) for nm in nms]
    off = [nm - d for nm, d in zip(nms, dg)]
    t0 = [eye - d for d in dg]
    p = [_mm(d, d) for d in dg]
    for it in range(3):
        t0 = [t + _mm(t, q) for t, q in zip(t0, p)]
        if it < 2:
            p = [_mm(q, q) for q in p]
    m = [_mm(t, o) for t, o in zip(t0, off)]
    m2 = [_mm(a, a) for a in m]
    r = [eye - a for a in m]
    r = [a + _mm(a, b) for a, b in zip(r, m2)]
    return [_mm(a, t) for a, t in zip(r, t0)]


def _gdn_kernel(xf_ref, sf_ref, xr_ref, sr_ref, alog_ref, dtb_ref, alog_t_ref, dtb_t_ref,
                of_ref, or_ref, st_ref):
    @pl.when(pl.program_id(1) == 0)
    def _():
        st_ref[...] = jnp.zeros(st_ref.shape, F32)

    nk = A_HEADS * A_DK
    nch = ROW_TILE // CHUNK
    ii = lax.broadcasted_iota(jnp.int32, (CHUNK, CHUNK), 0)
    jj = lax.broadcasted_iota(jnp.int32, (CHUNK, CHUNK), 1)
    eye = (ii == jj).astype(F32)
    blk = (ii // TRI_BLOCK) == (jj // TRI_BLOCK)
    refs = ((xf_ref, sf_ref, of_ref), (xr_ref, sr_ref, or_ref))

    units, pre = [], {}
    for d, (x_ref, s_ref, _) in enumerate(refs):
        incl, strict = _tri_masks(d == 1)
        m_col = incl.astype(F32)
        m_row = m_col.T
        sm = s_ref[...]
        sm_t = sm.T
        beta_all = jax.nn.sigmoid(sm[:, 0:8])
        g_all = -jnp.exp(alog_ref[...]) * _softplus(sm[:, 8:16] + dtb_ref[...])
        g_t_all = -jnp.exp(alog_t_ref[...]) * _softplus(sm_t[8:16, :] + dtb_t_ref[...])
        qn, kn = [], []
        for h in range(A_HEADS):
            qh = x_ref[:, h * A_DK:(h + 1) * A_DK]
            kh = x_ref[:, nk + h * A_DK:nk + (h + 1) * A_DK]
            qn.append(qh * lax.rsqrt(jnp.sum(qh * qh, axis=-1, keepdims=True) + NORM_EPS) * (A_DK ** -0.5))
            kn.append(kh * lax.rsqrt(jnp.sum(kh * kh, axis=-1, keepdims=True) + NORM_EPS))
        kn_t = [kh.T for kh in kn]
        for c in range(nch):
            rows = slice(c * CHUNK, (c + 1) * CHUNK)
            g_ch = g_all[rows]
            gc = _mm_sel_l(m_col, g_ch)
            gc_t = _mm_sel_r(g_t_all[:, rows], m_row)
            glast = jnp.sum(g_ch, axis=0, keepdims=True)
            for h in range(A_HEADS):
                col = d * A_HEADS + h
                gcc, gcr, gl = gc[:, col:col + 1], gc_t[col:col + 1, :], glast[:, col:col + 1]
                bh = beta_all[rows, col:col + 1]
                kh = kn[h][rows]
                units.append((d, c, h))
                pre[d, c, h] = dict(
                    dmask=_masked_decay(gcc, gcr, incl), strict=strict, egc=jnp.exp(gcc), kh=kh, kb=kh * bh,
                    qh=qn[h][rows], vb=x_ref[rows, 2 * nk + h * A_DV:2 * nk + (h + 1) * A_DV] * bh,
                    kg_t=kn_t[h][:, rows] * jnp.exp(gl - gcr), dl=jnp.exp(gl))
    for g0 in range(0, len(units), GDN_GROUP):
        us = units[g0:g0 + GDN_GROUP]
        kk = [_mm_nt(pre[u]["kb"], pre[u]["kh"]) for u in us]
        qk = [_mm_nt(pre[u]["qh"], pre[u]["kh"]) for u in us]
        tinv = _unit_lower_inverse(
            [jnp.where(pre[u]["strict"], a * pre[u]["dmask"], 0.0) for u, a in zip(us, kk)], eye, blk)
        uw = [_mm(t, jnp.concatenate([pre[u]["vb"], pre[u]["kb"] * pre[u]["egc"]], axis=1))
              for u, t in zip(us, tinv)]
        for u, a, b in zip(us, uw, qk):
            pre[u]["u"] = a[:, :A_DV]
            pre[u]["wq"] = jnp.concatenate([a[:, A_DV:], pre[u]["qh"] * pre[u]["egc"]], axis=0)
            pre[u]["aqk"] = b * pre[u]["dmask"]

    heads = [(d, h) for d in range(2) for h in range(A_HEADS)]
    st = {dh: st_ref[dh[0], dh[1]] for dh in heads}
    for p in range(nch):
        us = [(d, (nch - 1 - p) if d == 1 else p, h) for d, h in heads]
        ws = [_mm(pre[u]["wq"], st[u[0], u[2]]) for u in us]
        v_new = [pre[u]["u"] - w[:CHUNK] for u, w in zip(us, ws)]
        outs = [w[CHUNK:] + _mm(pre[u]["aqk"], v) for u, w, v in zip(us, ws, v_new)]
        for u, v in zip(us, v_new):
            st[u[0], u[2]] = pre[u]["dl"] * st[u[0], u[2]] + _mm(pre[u]["kg_t"], v)
        for d, (_, _, o_ref) in enumerate(refs):
            c = us[d * A_HEADS][1]
            o_ref[c * CHUNK:(c + 1) * CHUNK, :] = jnp.concatenate(outs[d * A_HEADS:(d + 1) * A_HEADS], axis=1)
    for dh in heads:
        st_ref[dh[0], dh[1]] = st[dh]


def _gdn_scan(lay, qkv, small, a_log, dt_bias):
    al = a_log.reshape(1, 2 * A_HEADS)
    db = dt_bias.reshape(1, 2 * A_HEADS)
    return (_gdn_kernel, (qkv, small), (al, db, al.T, db.T), A_HEADS * A_DV,
            [pltpu.VMEM((2, A_HEADS, A_DK, A_DV), F32)])


FFT_N1 = 64
FFT_KB = 8


def _dft_consts(n2):
    n1 = FFT_N1
    m = n1 * n2
    k1 = np.arange(n1, dtype=np.float64)
    ang1 = 2.0 * np.pi * np.outer(k1, k1) / n1
    f1_full = np.concatenate([np.cos(ang1), -np.sin(ang1)], axis=0)
    f1 = f1_full[:, :n1 // 2]
    f3 = np.concatenate([np.cos(ang1[:, :n1 // 2]).T, -np.sin(ang1[:, :n1 // 2]).T], axis=1) / m
    j = np.arange(n2, dtype=np.float64)
    theta = 2.0 * np.pi * (np.outer(j, j)[None] / n2 + (k1[:, None, None] * j[None, None, :]) / m)
    wr, wi = np.cos(theta), -np.sin(theta)
    w2 = np.concatenate([np.concatenate([wr, -wi], axis=2), np.concatenate([wi, wr], axis=2)], axis=1)
    phi = 2.0 * np.pi * (np.outer(j, j)[None] / n2 + (k1[:, None, None] * j[None, :, None]) / m)
    vr, vi = np.cos(phi), np.sin(phi)
    v2 = np.concatenate([np.concatenate([vr, -vi], axis=2), np.concatenate([vi, vr], axis=2)], axis=1)
    return f1_full, f1, w2, v2, f3


FFT_KV = FFT_N1 // 2 + 1
FFT_KH = 40


def _half_spectrum_consts(consts):
    _, f1, w2, v2, f3 = consts
    n1 = FFT_N1
    f1h = np.zeros((2 * FFT_KH, n1 // 2))
    f1h[:FFT_KV] = f1[:FFT_KV]
    f1h[FFT_KH:FFT_KH + FFT_KV] = f1[n1:n1 + FFT_KV]
    w2h = np.zeros((FFT_KH,) + w2.shape[1:])
    w2h[:FFT_KV] = w2[:FFT_KV]
    v2h = np.zeros((FFT_KH,) + v2.shape[1:])
    v2h[:FFT_KV] = v2[:FFT_KV]
    weight = np.full((FFT_KV,), 2.0)
    weight[0] = weight[-1] = 1.0
    f3h = np.zeros((n1 // 2, 2 * FFT_KH))
    f3h[:, :FFT_KV] = f3[:, :FFT_KV] * weight
    f3h[:, FFT_KH:FFT_KH + FFT_KV] = f3[:, n1:n1 + FFT_KV] * weight
    return f1h, w2h, v2h, f3h


def _left_mm_kernel(w_ref, x_ref, o_ref):
    o_ref[...] = _mm_3x(w_ref[...], x_ref[...])


def _left_mm(wm, x):
    bs, kdim, ncols = x.shape
    mdim = wm.shape[0]
    tn = min(ncols, 4096)
    xspec = lambda rows: pl.BlockSpec((None, rows, tn), lambda b, j: (b, 0, j))
    return pl.pallas_call(
        _left_mm_kernel, grid=(bs, ncols // tn), in_specs=[_full(wm.shape), xspec(kdim)], out_specs=xspec(mdim),
        out_shape=jax.ShapeDtypeStruct((bs, mdim, ncols), F32),
        compiler_params=_cparams(("parallel", "parallel")), name="filter_dft_outer",
    )(wm, x)


def _filt_mid_kernel(a_ref, w_ref, o_ref, *, n2):
    for kk in range(FFT_KB):
        a = jnp.concatenate([a_ref[0, kk], a_ref[1, kk]], axis=0)
        xf = _mm_3x(w_ref[kk], a)
        o_ref[kk, 0] = xf[:n2]
        o_ref[kk, 1] = xf[n2:]


def _filt_mid_stage(a, w2):
    bs, _, n1, n2, ch = a.shape
    return pl.pallas_call(
        functools.partial(_filt_mid_kernel, n2=n2),
        grid=(bs, n1 // FFT_KB),
        in_specs=[pl.BlockSpec((None, 2, FFT_KB, n2, ch), lambda b, j: (b, 0, j, 0, 0)),
                  pl.BlockSpec((FFT_KB, 2 * n2, 2 * n2), lambda b, j: (j, 0, 0))],
        out_specs=pl.BlockSpec((None, FFT_KB, 2, n2, ch), lambda b, j: (b, j, 0, 0, 0)),
        out_shape=jax.ShapeDtypeStruct((bs, n1, 2, n2, ch), F32),
        compiler_params=_cparams(("parallel", "parallel")), name="filter_dft_mid",
    )(a, w2)


FILT_ROWS = 512


def _filter_kernel(z_ref, win_ref, sel_ref, w1_ref, b1_ref, w2_ref, b2_ref, wo_ref, fr_ref, o_ref):
    fr = fr_ref[...]
    h = jnp.sin(fr * (_mm_3x(z_ref[...], w1_ref[...]) + b1_ref[...]))
    for i in range(B_INNER_MLPS):
        h = jnp.sin(fr * (_mm_3x(h, w2_ref[i]) + b2_ref[i]))
    ho = _mm_3x(h, wo_ref[...])
    fwd = sel_ref[...] > 0.5
    win = win_ref[...]
    for o in range(2):
        base = o * 2 * B_CH
        o_ref[o] = jnp.where(fwd, ho[:, base:base + B_CH], ho[:, base + B_CH:base + 2 * B_CH]) * win


def _hyena_filter_time(l, w1, b1, w2, b2, w_out, freq):
    bands = (B_EMB - 1) // 2
    circ = jnp.arange(2 * l)
    pos = jnp.where(circ < l, circ, 2 * l - circ)
    pos = jnp.where(circ == l, 0, pos)
    t = (jnp.linspace(0.0, 1.0, l, dtype=F32)[pos])[:, None]
    ang = 2.0 * math.pi * pos.astype(F32)[:, None] / l
    fr = jnp.linspace(1e-4, bands - 1, bands, dtype=F32)[None, :]
    z = jnp.concatenate([t, jnp.cos(fr * ang), -jnp.sin(fr * ang)], axis=-1)
    z = jnp.pad(z, ((0, 0), (0, 128 - B_EMB)))
    max_decay = math.log(B_DECAY_TARGET) / B_DECAY_SHORT_PCT
    min_decay = math.log(B_DECAY_TARGET) / B_DECAY_LONG_PCT
    deltas = jnp.abs(jnp.linspace(min_decay, max_decay, B_CH, dtype=F32))
    win = (jnp.exp(-t * deltas) + B_WINDOW_SHIFT) * (circ != l).astype(F32)[:, None]
    sel = (circ < l).astype(F32)[:, None]
    depth = w1.shape[0]
    w1p = jnp.pad(w1, ((0, 0), (0, 128 - B_EMB), (0, 0)))
    rows = min(FILT_ROWS, 2 * l)
    per_layer = lambda a: pl.BlockSpec((None,) + a.shape[1:], lambda li, i: (li,) + (0,) * (a.ndim - 1))
    params = (w1p, b1.reshape(depth, 1, B_FFN), w2, b2.reshape(depth, B_INNER_MLPS, 1, B_FFN), w_out,
              freq.reshape(depth, 1, B_FFN))
    return pl.pallas_call(
        _filter_kernel,
        grid=(depth, 2 * l // rows),
        in_specs=[pl.BlockSpec((rows, 128), lambda li, i: (i, 0)),
                  pl.BlockSpec((rows, B_CH), lambda li, i: (i, 0)),
                  pl.BlockSpec((rows, 1), lambda li, i: (i, 0))] + [per_layer(p) for p in params],
        out_specs=pl.BlockSpec((None, 2, rows, B_CH), lambda li, i: (li, 0, i, 0)),
        out_shape=jax.ShapeDtypeStruct((depth, 2, 2 * l, B_CH), F32),
        compiler_params=_cparams(("parallel", "parallel")), name="hyena_filter",
    )(z, win, sel, *params)


def _hyena_spectrum(l, consts, w1, b1, w2m, b2, w_out, freq):
    f1_full, _, w2, _, _ = consts
    n2 = 2 * l // FFT_N1
    depth = w1.shape[0]
    g_time = _hyena_filter_time(l, w1, b1, w2m, b2, w_out, freq)
    f1_rows = np.concatenate([f1_full[:FFT_KH], f1_full[FFT_N1:FFT_N1 + FFT_KH]], axis=0)
    a = _left_mm(jnp.asarray(f1_rows, F32), g_time.reshape(depth * 2, FFT_N1, n2 * B_CH))
    g = _filt_mid_stage(a.reshape(depth * 2, 2, FFT_KH, n2, B_CH), jnp.asarray(w2[:FFT_KH], F32))
    return g.reshape(depth, 2, FFT_KH, 2, n2, B_CH)


HY_SLAB = 128


def _hy_pitch(n2c):
    return n2c + 8 if (n2c // 8) % 2 == 0 else n2c + 16


HY_KG = 20
HY_JG = 16


def _hyena_lat_kernel(zv_ref, z1_ref, z2_ref, cw_ref, cb_ref, g_ref, bias_ref, f1_ref, w2_ref, v2_ref, f3_ref,
                      o_ref, useq, ur, a_re, a_im, cbuf, *, col_mode, seq):
    n1h = FFT_N1 // 2
    n2c = seq // n1h
    pitch = _hy_pitch(n2c)
    grows = seq // GRID_W
    row_id = lax.broadcasted_iota(jnp.int32, (seq, HY_SLAB), 0)

    def short_conv(z_ref, k):
        z = z_ref[...]
        w = cw_ref[k]
        if col_mode:
            g0, gl = z[:GRID_W], z[seq - GRID_W:]
            cc = row_id[:GRID_W]
            wrap_p = jnp.where(cc == 0, 0.0, pltpu.roll(gl, 1, axis=0))
            wrap_n = jnp.where(cc == GRID_W - 1, 0.0, pltpu.roll(g0, GRID_W - 1, axis=0))
            prev = jnp.concatenate([wrap_p, z[:seq - GRID_W]], axis=0)
            nxt = jnp.concatenate([z[GRID_W:], wrap_n], axis=0)
        else:
            prev = jnp.where(row_id == 0, 0.0, pltpu.roll(z, 1, axis=0))
            nxt = jnp.where(row_id == seq - 1, 0.0, pltpu.roll(z, seq - 1, axis=0))
        return prev * w[0:1] + z * w[1:2] + nxt * w[2:3] + cb_ref[k]

    def seq_start(j):
        return (GRID_W * lax.rem(j, grows) + j // grows) if col_mode else j

    seq_stride = 2 if col_mode else pitch
    seq_ref = ur if col_mode else useq

    def put_seq(val):
        if col_mode:
            ur[...] = val
        else:
            for n1 in range(n1h):
                useq[n1 * pitch:n1 * pitch + n2c, :] = val[n1 * n2c:(n1 + 1) * n2c]

    def conv_out():
        if col_mode:
            return cbuf[...]
        return jnp.concatenate([cbuf[n1 * pitch:n1 * pitch + n2c, :] for n1 in range(n1h)], axis=0)

    y_prev = short_conv(zv_ref, 0)
    x_next = (short_conv(z1_ref, 1), short_conv(z2_ref, 2))
    put_seq(y_prev)
    f1 = f1_ref[...]
    f3 = f3_ref[...]
    jgrp = min(HY_JG, n2c)
    for order in range(2):
        def stage1(jg, carry):
            js = [jg * jgrp + jj for jj in range(jgrp)]
            xs = [seq_ref[pl.ds(seq_start(j), n1h, stride=seq_stride), :] for j in js]
            outs = [_mm(f1, x) for x in xs]
            for j, a in zip(js, outs):
                a_re[pl.ds(j, FFT_KH, stride=pitch), :] = a[:FFT_KH]
                a_im[pl.ds(j, FFT_KH, stride=pitch), :] = a[FFT_KH:]
            return carry
        lax.fori_loop(0, n2c // jgrp, stage1, 0)

        def mid(kg, carry):
            k1s = [kg * HY_KG + kk for kk in range(HY_KG)]
            offs = [pl.multiple_of(k1 * pitch, 8) for k1 in k1s]
            xin = [jnp.concatenate([a_re[pl.ds(o, n2c), :], a_im[pl.ds(o, n2c), :]], axis=0) for o in offs]
            xf = [_mm(w2_ref[k1], a) for k1, a in zip(k1s, xin)]
            ys = []
            for k1, x in zip(k1s, xf):
                xr, xi = x[:n2c], x[n2c:]
                gr, gi = g_ref[order, k1, 0], g_ref[order, k1, 1]
                ys.append(jnp.concatenate([xr * gr - xi * gi, xr * gi + xi * gr], axis=0))
            bm = [_mm(v2_ref[k1], y) for k1, y in zip(k1s, ys)]
            for o, b in zip(offs, bm):
                a_re[pl.ds(o, n2c), :] = b[:n2c]
                a_im[pl.ds(o, n2c), :] = b[n2c:]
            return carry
        lax.fori_loop(0, FFT_KH // HY_KG, mid, 0)

        def stage3(jg, carry):
            js = [jg * jgrp + jj for jj in range(jgrp)]
            bs = [jnp.concatenate([a_re[pl.ds(j, FFT_KH, stride=pitch), :],
                                   a_im[pl.ds(j, FFT_KH, stride=pitch), :]], axis=0) for j in js]
            outs = [_mm(f3, b) for b in bs]
            for j, y in zip(js, outs):
                cbuf[pl.ds(seq_start(j), n1h, stride=seq_stride), :] = y
            return carry
        lax.fori_loop(0, n2c // jgrp, stage3, 0)

        y_prev = x_next[order] * (conv_out() + y_prev * bias_ref[order:order + 1, :])
        if order == 0:
            put_seq(y_prev)
    o_ref[...] = y_prev


def _hyena_fused(lay, z, conv_w, conv_b, spec, bias, consts, l, seq, row0, s0, nslab, col_mode,
                 out_rows, out_row0, out_buf=None):
    f1, w2, v2, f3 = _half_spectrum_consts(consts)
    n2c = seq // (FFT_N1 // 2)
    assert FFT_KH % HY_KG == 0 and n2c % min(HY_JG, n2c) == 0
    pitch = _hy_pitch(n2c)
    cps = B_CH // HY_SLAB
    zspec = lambda k: pl.BlockSpec((seq, HY_SLAB), lambda j, b, k=k: (row0 + b, k * cps + s0 + j))
    once = lambda a: pl.BlockSpec(a.shape, lambda j, b: (0,) * a.ndim, pipeline_mode=pl.Buffered(1))
    mats = (jnp.asarray(f1, BF16), jnp.asarray(w2, BF16), jnp.asarray(v2, BF16), jnp.asarray(f3, BF16))
    seq_rows = (FFT_N1 // 2) * pitch
    kern = functools.partial(_hyena_lat_kernel, col_mode=col_mode, seq=seq)
    in_specs = [zspec(0), zspec(1), zspec(2),
                pl.BlockSpec((3, B_SHORT, HY_SLAB), lambda j, b: (0, 0, s0 + j)),
                pl.BlockSpec((3, 1, HY_SLAB), lambda j, b: (0, 0, s0 + j)),
                pl.BlockSpec((None, 2, FFT_KH, 2, n2c, HY_SLAB), lambda j, b: (l, 0, 0, 0, 0, s0 + j),
                             pipeline_mode=pl.Buffered(1)),
                pl.BlockSpec((2, HY_SLAB), lambda j, b: (0, s0 + j))] + [once(m) for m in mats]
    args = [z, z, z, jnp.transpose(conv_w, (1, 0, 2)), conv_b.reshape(3, 1, B_CH), spec, bias, *mats]
    aliases = {}
    if out_buf is not None:
        n_in = len(args)
        body = kern
        kern = lambda *refs: body(*refs[:n_in], *refs[n_in + 1:])
        in_specs.append(pl.BlockSpec(memory_space=pl.ANY))
        args.append(out_buf)
        aliases = {n_in: 0}
    return pl.pallas_call(
        kern,
        grid=(nslab, lay.b),
        in_specs=in_specs,
        out_specs=pl.BlockSpec((seq, HY_SLAB), lambda j, b: (out_row0 + b, s0 + j)),
        out_shape=jax.ShapeDtypeStruct((out_rows, B_CH), F32),
        input_output_aliases=aliases,
        scratch_shapes=[pltpu.VMEM((seq_rows, HY_SLAB), F32), pltpu.VMEM((seq, HY_SLAB), F32),
                        pltpu.VMEM((FFT_KH * pitch, HY_SLAB), F32), pltpu.VMEM((FFT_KH * pitch, HY_SLAB), F32),
                        pltpu.VMEM((seq if col_mode else seq_rows, HY_SLAB), F32)],
        compiler_params=_cparams(("parallel", "parallel")),
        name="hyena_col" if col_mode else "hyena_row",
    )(*args)


def _hyena(lay, z, conv_w, conv_b, spec_c, spec_x, bias, consts_c, consts_x, l, need_ctx):
    assert lay.n_ctx % lay.seq == 0
    half = B_CH // 2 // HY_SLAB
    args = (lay, z, conv_w, conv_b)
    lat0 = lay.n_ctx // lay.seq
    rows, out0, buf = (lay.n, lat0, None) if need_ctx else (lay.b * lay.seq, 0, None)
    if need_ctx:
        buf = _hyena_fused(*args, spec_c, bias, consts_c, l, lay.ctx, 0, 0, 2 * half, False, rows, 0)
    buf = _hyena_fused(*args, spec_x, bias, consts_x, l, lay.seq, lat0, 0, half, False, rows, out0, buf)
    return _hyena_fused(*args, spec_x, bias, consts_x, l, lay.seq, lat0, half, half, True, rows, out0, buf)


def _head_norm_gate(o, gate, norm_w, heads, width):
    outs = []
    for h in range(heads):
        oh = o[:, h * width:(h + 1) * width]
        y = oh * lax.rsqrt(jnp.mean(oh * oh, axis=-1, keepdims=True) + NORM_EPS) * norm_w
        outs.append(y * _silu(gate[:, h * width:(h + 1) * width]))
    return jnp.concatenate(outs, axis=1)


def _merge_kernel(x_ref, mod_ref, nw_ref, a0_ref, a1_ref, ag_ref, yb_ref, c0_ref, c1_ref, cx_ref, cz_ref,
                  d0_ref, d1_ref, dg_ref, an_ref, cd_ref, cn_ref, dn_ref, wg_ref, wb_ref, wo_ref, o_ref):
    xv = x_ref[...]
    m = mod_ref[...]
    hb = _prenorm(xv, nw_ref[1:2, :], m[3:4, :], m[4:5, :]).astype(BF16)
    ya = _head_norm_gate(a0_ref[...] + a1_ref[...], ag_ref[...], an_ref[...], A_HEADS, A_DV)
    yd = _head_norm_gate(d0_ref[...] + d1_ref[...], dg_ref[...], dn_ref[...], D_HEADS, D_DV)
    yc = (c0_ref[...] + c1_ref[...] + cd_ref[...] * cx_ref[...]) * _silu(cz_ref[...])
    gw = C_INNER // C_GROUPS
    cn = cn_ref[...]
    yc = jnp.concatenate(
        [yc[:, g * gw:(g + 1) * gw]
         * lax.rsqrt(jnp.mean(yc[:, g * gw:(g + 1) * gw] ** 2, axis=-1, keepdims=True) + NORM_EPS)
         * cn[:, g * gw:(g + 1) * gw] for g in range(C_GROUPS)], axis=1)
    acc = jnp.zeros(xv.shape, F32)
    for k, y in enumerate((ya, yb_ref[...], yc, yd)):
        gate = jax.nn.sigmoid(jnp.dot(hb, wg_ref[:, k * D_MODEL:(k + 1) * D_MODEL], preferred_element_type=F32))
        acc = acc + gate * jnp.dot(y.astype(BF16), wb_ref[k], preferred_element_type=F32)
    o_ref[...] = xv + m[5:6, :] * jnp.dot(acc.astype(BF16), wo_ref[...], preferred_element_type=F32)


def _merge(lay, x, mods, norm_w, a0, a1, ag, yb, c0, c1, cx, cz, d0, d1, dg, an, cd, cn, dn, wg, wb, wo, l,
           latent_only=False):
    n, d = x.shape
    t0 = lay.n_ctx // TOK_TILE if latent_only else 0
    row = lambda w, off=t0: pl.BlockSpec((TOK_TILE, w), lambda i: (i + off, 0))
    branch_rows = [row(BRANCH_W)] * 3 + [row(BRANCH_W, 0 if latent_only else t0)] + [row(BRANCH_W)] * 7
    return pl.pallas_call(
        _merge_kernel,
        grid=(n // TOK_TILE - t0,),
        in_specs=[row(d), _mod_spec(lay, mods, l, TOK_TILE, t0), _of_layer(norm_w, l)]
                 + branch_rows + [_of_layer(p, l) for p in (an, cd, cn, dn)]
                 + [_of_layer(w, l, once=True) for w in (wg, wb, wo)],
        out_specs=row(d),
        out_shape=jax.ShapeDtypeStruct((n, d), F32),
        compiler_params=_cparams(("parallel",)), name="merge",
    )(x, mods, norm_w, a0, a1, ag, yb, c0, c1, cx, cz, d0, d1, dg, an, cd, cn, dn, wg, wb, wo)


def kernel(x, c, ctx, c_ctx, w_ada, b_ada, norm_w, ffn_up, ffn_down, w_in, gdn_conv, gdn_a_log, gdn_dt_bias, gdn_norm, hy_conv_w, hy_conv_b, hy_w1, hy_b1, hy_w2, hy_b2, hy_wout, hy_freq, hy_bias, ssd_conv_w, ssd_conv_b, ssd_a_log, ssd_dt_bias, ssd_d, ssd_norm, gla_gk_w, gla_gk_b, gla_norm, w_branch, w_out, final_norm):
    b, seq, d = x.shape
    ctx_len = ctx.shape[1]
    depth = w_ada.shape[0]
    lay = _Layout(b, ctx_len, seq)
    consts_c = _dft_consts(2 * ctx_len // FFT_N1)
    consts_x = _dft_consts(2 * seq // FFT_N1)

    rp = -(-(1 + b) // 8) * 8
    cond = jnp.concatenate([c_ctx[None, :], c, jnp.zeros((rp - 1 - b, d), F32)], axis=0)
    mods = _ada(cond, w_ada, b_ada).reshape(depth, rp, N_MOD, d)

    ffn_up_b, ffn_down_b = ffn_up.astype(BF16), ffn_down.astype(BF16)
    w_r, w_gate = _rearrange_w_in(w_in)
    w_branch_b, w_out_b = w_branch.astype(BF16), w_out.astype(BF16)
    ssd_conv_b3 = ssd_conv_b.reshape(depth, 1, C_XBC)
    an = gdn_norm.reshape(depth, 1, A_DV)
    cd = jnp.repeat(ssd_d, C_HEADDIM, axis=-1).reshape(depth, 1, C_INNER)
    cn = ssd_norm.reshape(depth, 1, C_INNER)
    dn = gla_norm.reshape(depth, 1, D_DV)

    fargs = (hy_w1, hy_b1, hy_w2, hy_b2, hy_wout, hy_freq)
    spec_c = _hyena_spectrum(ctx_len, consts_c, *fargs)
    spec_x = _hyena_spectrum(seq, consts_x, *fargs)

    xf = jnp.concatenate([ctx.reshape(b * ctx_len, d), x.reshape(b * seq, d)], axis=0)
    for l in range(depth):
        xf = _ffn(lay, xf, mods, norm_w, ffn_up_b, ffn_down_b, l, 0)

        qkv, a_gate, zb, c_z, xbc, d_qkv, d_gate, small = _inproj(
            lay, xf, mods, norm_w, w_r, gdn_conv, ssd_conv_w, ssd_conv_b3, l)
        (a0, a1), (c0, c1), (d0, d1) = _scan_call(lay, [
            _gdn_scan(lay, qkv, small, gdn_a_log[l], gdn_dt_bias[l]),
            _ssd_scan(lay, xbc, small, ssd_a_log[l], ssd_dt_bias[l]),
            _gla_scan(lay, d_qkv, small, gla_gk_w[l], gla_gk_b[l])])
        last = l == depth - 1
        yb = _hyena(lay, zb, hy_conv_w[l], hy_conv_b[l], spec_c, spec_x, hy_bias[l], consts_c, consts_x, l,
                    need_ctx=not last)

        xf = _merge(lay, xf, mods, norm_w, a0, a1, a_gate, yb, c0, c1, xbc, c_z, d0, d1, d_gate,
                    an, cd, cn, dn, w_gate, w_branch_b, w_out_b, l, latent_only=last)
        xf = _ffn(lay, xf, mods, norm_w, ffn_up_b, ffn_down_b, l, 2, final_w=final_norm if last else None)
    return xf.reshape(b, seq, d)
```

```python
import functools
import math

import numpy as np
import jax
import jax.numpy as jnp
from jax import lax
from jax.experimental import pallas as pl
from jax.experimental.pallas import tpu as pltpu

F32 = jnp.float32
BF16 = jnp.bfloat16
HI = lax.Precision.HIGHEST

D_MODEL = 1024
GRID_W = 64
CHUNK = 64
NORM_EPS = 1e-6
N_MOD = 9
D_FF = 2816
SHORT_CONV = 5

A_HEADS, A_DK, A_DV = 4, 128, 128
A_QKV = A_HEADS * (2 * A_DK + A_DV)
B_CH, B_SHORT, B_EMB, B_FFN, B_INNER_MLPS = 512, 3, 33, 64, 2
B_WINDOW_SHIFT, B_DECAY_SHORT_PCT, B_DECAY_LONG_PCT, B_DECAY_TARGET = 0.05, 0.3, 1.5, 1e-2
C_HEADS, C_HEADDIM, C_GROUPS, C_STATE = 8, 64, 2, 64
C_INNER = C_HEADS * C_HEADDIM
C_XBC = C_INNER + 2 * C_GROUPS * C_STATE
D_HEADS, D_DK, D_DV, D_RANK = 4, 64, 128, 16
D_GATE_NORM = 16.0
N_BRANCH, BRANCH_W = 4, 512

IN_SIZES = (A_QKV, A_HEADS * A_DV, 2 * A_HEADS, 2 * A_HEADS, 3 * B_CH, C_INNER, C_XBC, 2 * C_HEADS,
            D_HEADS * (2 * D_DK + D_DV), D_HEADS * D_DV, 2 * D_RANK, N_BRANCH * D_MODEL)
IN_OFFS = tuple(int(v) for v in np.cumsum((0,) + IN_SIZES))

ROW_TILE = 256
TOK_TILE = 512
SMALL_W = 128
VMEM_LIMIT = 56 * 1024 * 1024


def _cparams(sem):
    return pltpu.CompilerParams(dimension_semantics=sem, vmem_limit_bytes=VMEM_LIMIT)


def _mm(a, b):
    return jnp.dot(a.astype(BF16), b.astype(BF16), preferred_element_type=F32)


def _mm_nt(a, b):
    return lax.dot_general(a.astype(BF16), b.astype(BF16), (((1,), (1,)), ((), ())),
                           preferred_element_type=F32)


def _mm_hi(a, b):
    return jnp.dot(a, b, precision=HI, preferred_element_type=F32)


def _bf16_parts(v, parts):
    out, rest = [], v
    for _ in range(parts):
        hi = rest.astype(BF16)
        out.append(hi)
        rest = rest - hi.astype(F32)
    return out


def _mm_3x(a, b):
    ah, al = _bf16_parts(a, 2)
    bh, bl = _bf16_parts(b, 2)
    dot = functools.partial(jnp.dot, preferred_element_type=F32)
    return dot(ah, bh) + (dot(ah, bl) + dot(al, bh))


def _mm_sel_l(sel, v, parts=3):
    sb = sel.astype(BF16)
    return sum(jnp.dot(sb, p, preferred_element_type=F32) for p in _bf16_parts(v, parts))


def _mm_sel_r(v, sel, parts=3):
    sb = sel.astype(BF16)
    return sum(jnp.dot(p, sb, preferred_element_type=F32) for p in _bf16_parts(v, parts))


def _silu(v):
    return v * jax.nn.sigmoid(v)


def _softplus(v):
    return jnp.maximum(v, 0.0) + jnp.log1p(jnp.exp(-jnp.abs(v)))


def _log_sigmoid(v):
    return jnp.minimum(v, 0.0) - jnp.log1p(jnp.exp(-jnp.abs(v)))


def _prenorm(xv, gain, shift, scale):
    ms = jnp.mean(xv * xv, axis=-1, keepdims=True)
    return (xv * lax.rsqrt(ms + NORM_EPS) * gain) * (1.0 + scale) + shift


def _full(shape):
    nd = len(shape)
    return pl.BlockSpec(shape, lambda *_: (0,) * nd)


class _Layout:
    def __init__(self, batch, ctx_len, seq):
        assert ctx_len % ROW_TILE == 0 and seq % ROW_TILE == 0
        assert (batch * ctx_len) % TOK_TILE == 0 and seq % TOK_TILE == 0
        self.b, self.ctx, self.seq = batch, ctx_len, seq
        self.n_ctx = batch * ctx_len
        self.n = batch * (ctx_len + seq)
        self.tc, self.tl = ctx_len // ROW_TILE, seq // ROW_TILE
        self.nct = batch * self.tc
        self.tiles = self.n // ROW_TILE

    def mod_index(self, i, tile=TOK_TILE):
        nct = self.n_ctx // tile
        return jnp.where(i < nct, 0, 1 + (i - nct) // (self.seq // tile))

    def seg_first(self, i):
        return jnp.where(i < self.nct, lax.rem(i, self.tc) == 0, lax.rem(i - self.nct, self.tl) == 0)

    def seg_last(self, i):
        return jnp.where(i < self.nct, lax.rem(i, self.tc) == self.tc - 1,
                         lax.rem(i - self.nct, self.tl) == self.tl - 1)

    def fwd_tile(self, b, s):
        return jnp.where(s < self.tc, b * self.tc + s, self.nct + b * self.tl + (s - self.tc))

    def rev_tile(self, b, s):
        return jnp.where(s < self.tc, b * self.tc + (self.tc - 1 - s),
                         self.nct + b * self.tl + (self.tl - 1 - (s - self.tc)))


def _ada_kernel(c_ref, w_ref, b_ref, o_ref):
    o_ref[...] = _mm_hi(_silu(c_ref[...]), w_ref[...]) + b_ref[...]


def _ada(cond, w_ada, b_ada):
    depth, d, nm = w_ada.shape
    rp = cond.shape[0]
    tn = 1152
    return pl.pallas_call(
        _ada_kernel,
        grid=(depth, nm // tn),
        in_specs=[_full((rp, d)),
                  pl.BlockSpec((None, d, tn), lambda l, j: (l, 0, j)),
                  pl.BlockSpec((None, 1, tn), lambda l, j: (l, 0, j))],
        out_specs=pl.BlockSpec((None, rp, tn), lambda l, j: (l, 0, j)),
        out_shape=jax.ShapeDtypeStruct((depth, rp, nm), F32),
        compiler_params=_cparams(("parallel", "parallel")), name="ada",
    )(cond, w_ada, b_ada.reshape(depth, 1, nm))


FF_CHUNK = 256


def _ffn_kernel(x_ref, mod_ref, nw_ref, wup_ref, wdn_ref, *rest, sub):
    o_ref = rest[-1]
    xv = x_ref[...]
    m = mod_ref[...]
    h = _prenorm(xv, nw_ref[sub:sub + 1, :], m[3 * sub:3 * sub + 1, :], m[3 * sub + 1:3 * sub + 2, :])
    hb = h.astype(BF16)
    acc = jnp.zeros(xv.shape, F32)
    for c in range(D_FF // FF_CHUNK):
        lo = c * FF_CHUNK
        a = jnp.dot(hb, wup_ref[:, lo:lo + FF_CHUNK], preferred_element_type=F32)
        g = jnp.dot(hb, wup_ref[:, D_FF + lo:D_FF + lo + FF_CHUNK], preferred_element_type=F32)
        acc = acc + jnp.dot((_silu(a) * g).astype(BF16), wdn_ref[lo:lo + FF_CHUNK, :],
                            preferred_element_type=F32)
    out = xv + 0.5 * m[3 * sub + 2:3 * sub + 3, :] * acc
    if len(rest) == 2:
        out = out * lax.rsqrt(jnp.mean(out * out, axis=-1, keepdims=True) + NORM_EPS) * rest[0][...]
    o_ref[...] = out


def _of_layer(a, l, *lead, once=False):
    idx = (l,) + lead
    block = (None,) * len(idx) + a.shape[len(idx):]
    mode = dict(pipeline_mode=pl.Buffered(1)) if once else {}
    return pl.BlockSpec(block, lambda *_: idx + (0,) * (a.ndim - len(idx)), **mode)


def _mod_spec(lay, mods, l, tile, t0=0):
    return pl.BlockSpec((None, None, N_MOD, mods.shape[-1]), lambda i: (l, lay.mod_index(i + t0, tile), 0, 0))


def _ffn(lay, x, mods, norm_w, wup, wdn, l, sub, final_w=None):
    n, d = x.shape
    t0 = 0 if final_w is None else lay.n_ctx // TOK_TILE
    extra = [] if final_w is None else [final_w.reshape(1, d)]
    return pl.pallas_call(
        functools.partial(_ffn_kernel, sub=sub),
        grid=(n // TOK_TILE - t0,),
        in_specs=[pl.BlockSpec((TOK_TILE, d), lambda i: (i + t0, 0)),
                  _mod_spec(lay, mods, l, TOK_TILE, t0),
                  _of_layer(norm_w, l),
                  _of_layer(wup, l, sub // 2, once=True),
                  _of_layer(wdn, l, sub // 2, once=True)]
                 + [_full(e.shape) for e in extra],
        out_specs=pl.BlockSpec((TOK_TILE, d), lambda i: (i, 0)),
        out_shape=jax.ShapeDtypeStruct((n - t0 * TOK_TILE, d), F32),
        compiler_params=_cparams(("parallel",)), name="ffn",
    )(x, mods, norm_w, wup, wdn, *extra)


PROJ_W = (A_QKV, A_HEADS * A_DV, 3 * B_CH, C_INNER, C_XBC, D_HEADS * (2 * D_DK + D_DV), D_HEADS * D_DV, SMALL_W)
PROJ_O = tuple(int(v) for v in np.cumsum((0,) + PROJ_W))


def _rearrange_w_in(w_in):
    o = IN_OFFS
    col = lambda a, b: w_in[..., o[a]:o[b]]
    small = jnp.concatenate([col(2, 4), col(7, 8), col(10, 11),
                             jnp.zeros(w_in.shape[:-1] + (SMALL_W - 64,), w_in.dtype)], axis=-1)
    w_r = jnp.concatenate([col(0, 1), col(1, 2), col(4, 5), col(5, 6), col(6, 7), col(8, 9), col(9, 10), small],
                          axis=-1)
    return w_r.astype(BF16), col(11, 12).astype(BF16)


PROJ_HALO = 16
PROJ_CONV = (0, 4)


def _inproj_kernel(xp_ref, x_ref, xn_ref, mod_ref, nw_ref, w_ref, cwa_ref, cwc_ref, cbc_ref, *rest, lay):
    o_refs, zbuf = rest[:-1], rest[-1]
    i = pl.program_id(0)
    m = mod_ref[...]
    norm = lambda r: _prenorm(r[...], nw_ref[1:2, :], m[3:4, :], m[4:5, :]).astype(BF16)
    hb = norm(x_ref)
    hb_all = jnp.concatenate([norm(xp_ref), hb, norm(xn_ref)], axis=0)
    keep_p = jnp.where(lay.seg_first(i), 0.0, 1.0)
    keep_n = jnp.where(lay.seg_last(i), 0.0, 1.0)
    convs = {PROJ_CONV[0]: (cwa_ref, None), PROJ_CONV[1]: (cwc_ref, cbc_ref)}
    for k, o_ref in enumerate(o_refs):
        wk = w_ref[:, PROJ_O[k]:PROJ_O[k + 1]]
        if k not in convs:
            o_ref[...] = jnp.dot(hb, wk, preferred_element_type=F32)
            continue
        cw_ref, cb_ref = convs[k]
        width = PROJ_W[k]
        z = jnp.dot(hb_all, wk, preferred_element_type=F32)
        zbuf[0:PROJ_HALO, 0:width] = z[:PROJ_HALO] * keep_p
        zbuf[PROJ_HALO:PROJ_HALO + ROW_TILE, 0:width] = z[PROJ_HALO:PROJ_HALO + ROW_TILE]
        zbuf[PROJ_HALO + ROW_TILE:, 0:width] = z[PROJ_HALO + ROW_TILE:] * keep_n
        taps = cw_ref.shape[0]
        left = (taps - 1) // 2
        acc = None
        for t in range(taps):
            off = t - left
            term = zbuf[PROJ_HALO + off:PROJ_HALO + off + ROW_TILE, 0:width] * cw_ref[t:t + 1, :]
            acc = term if acc is None else acc + term
        if cb_ref is not None:
            acc = acc + cb_ref[...]
        o_ref[...] = _silu(acc)


def _inproj(lay, x, mods, norm_w, w_r, conv_a, conv_c, conv_c_bias, l):
    n, d = x.shape
    rh = ROW_TILE // PROJ_HALO
    lasth = n // PROJ_HALO - 1
    return pl.pallas_call(
        functools.partial(_inproj_kernel, lay=lay),
        grid=(n // ROW_TILE,),
        in_specs=[pl.BlockSpec((PROJ_HALO, d), lambda i: (jnp.maximum(i * rh - 1, 0), 0)),
                  pl.BlockSpec((ROW_TILE, d), lambda i: (i, 0)),
                  pl.BlockSpec((PROJ_HALO, d), lambda i: (jnp.minimum((i + 1) * rh, lasth), 0)),
                  _mod_spec(lay, mods, l, ROW_TILE),
                  _of_layer(norm_w, l),
                  _of_layer(w_r, l, once=True),
                  _of_layer(conv_a, l), _of_layer(conv_c, l), _of_layer(conv_c_bias, l)],
        out_specs=[pl.BlockSpec((ROW_TILE, w), lambda i: (i, 0)) for w in PROJ_W],
        out_shape=[jax.ShapeDtypeStruct((n, w), F32) for w in PROJ_W],
        scratch_shapes=[pltpu.VMEM((ROW_TILE + 2 * PROJ_HALO, max(PROJ_W[k] for k in PROJ_CONV)), F32)],
        compiler_params=_cparams(("parallel",)), name="inproj",
    )(x, x, x, mods, norm_w, w_r, conv_a, conv_c, conv_c_bias)


def _tri_masks(rev):
    ii = lax.broadcasted_iota(jnp.int32, (CHUNK, CHUNK), 0)
    jj = lax.broadcasted_iota(jnp.int32, (CHUNK, CHUNK), 1)
    return (ii <= jj, ii < jj) if rev else (ii >= jj, ii > jj)


def _masked_decay(col, row, incl):
    return jnp.where(incl, jnp.exp(jnp.where(incl, col - row, 0.0)), 0.0)


def _scan_call(lay, parts):
    steps = lay.tc + lay.tl
    specs, args, out_specs, out_shape, scratch, counts = [], [], [], [], [], []
    for _, ins_tiled, ins_full, out_w, scr in parts:
        n0 = len(specs)
        for walk in (lay.fwd_tile, lay.rev_tile):
            for a in ins_tiled:
                specs.append(pl.BlockSpec((ROW_TILE, a.shape[1]), lambda b, s, walk=walk: (walk(b, s), 0)))
                args.append(a)
            out_specs.append(pl.BlockSpec((ROW_TILE, out_w), lambda b, s, walk=walk: (walk(b, s), 0)))
            out_shape.append(jax.ShapeDtypeStruct((lay.n, out_w), F32))
        for a in ins_full:
            specs.append(_full(a.shape))
            args.append(a)
        counts.append((len(specs) - n0, len(scr)))
        scratch += list(scr)

    def body(*refs):
        n_in, n_out = len(specs), len(out_specs)
        i, s = 0, n_in + n_out
        for p, (kern, *_) in enumerate(parts):
            ci, cs = counts[p]
            kern(*refs[i:i + ci], *refs[n_in + 2 * p:n_in + 2 * p + 2], *refs[s:s + cs])
            i, s = i + ci, s + cs

    res = pl.pallas_call(
        body,
        grid=(lay.b, steps),
        in_specs=specs,
        out_specs=out_specs,
        out_shape=out_shape,
        scratch_shapes=scratch,
        compiler_params=_cparams(("arbitrary", "arbitrary")), name="scans",
    )(*args)
    return [tuple(res[2 * p:2 * p + 2]) for p in range(len(parts))]


def _ssd_kernel(xf_ref, sf_ref, xr_ref, sr_ref, alog_ref, dtb_ref, alog_t_ref, dtb_t_ref, exp_ref,
                of_ref, or_ref, st_ref):
    @pl.when(pl.program_id(1) == 0)
    def _():
        st_ref[...] = jnp.zeros(st_ref.shape, F32)

    hpg = C_HEADS // C_GROUPS
    gw = hpg * C_HEADDIM
    nch = ROW_TILE // CHUNK
    refs = ((xf_ref, sf_ref, of_ref), (xr_ref, sr_ref, or_ref))
    units, pre = [], {}
    for d, (x_ref, s_ref, _) in enumerate(refs):
        incl, _ = _tri_masks(d == 1)
        m_col = incl.astype(F32)
        m_row = m_col.T
        sm = s_ref[...]
        sm_t = sm.T
        dt_all = _softplus(sm[:, 16:32] + dtb_ref[...])
        a_all = -jnp.exp(alog_ref[...]) * dt_all
        a_t_all = -jnp.exp(alog_t_ref[...]) * _softplus(sm_t[16:32, :] + dtb_t_ref[...])
        bm_t = x_ref[:, C_INNER:C_INNER + C_GROUPS * C_STATE].T
        for c in range(nch):
            rows = slice(c * CHUNK, (c + 1) * CHUNK)
            units.append((d, c))
            pre[d, c] = dict(incl=incl, m_col=m_col, m_row=m_row, a_ch=a_all[rows], a_t=a_t_all[:, rows],
                             dt=dt_all[rows], bm_t=bm_t[:, rows], e=exp_ref[d], xs=x_ref[rows, 0:C_INNER],
                             bm=x_ref[rows, C_INNER:C_INNER + C_GROUPS * C_STATE],
                             cm=x_ref[rows, C_INNER + C_GROUPS * C_STATE:C_XBC])
    for u in units:
        w = pre[u]
        w["ac"] = _mm_sel_l(w["m_col"], w["a_ch"])
        w["ac_t"] = _mm_sel_r(w["a_t"], w["m_row"])
        w["alast"] = jnp.sum(w["a_ch"], axis=0, keepdims=True)
    for u in units:
        w = pre[u]
        w["xdt"] = w["xs"] * _mm_sel_r(w["dt"], w["e"])
        w["eac_e"] = _mm_sel_r(jnp.exp(w["ac"]), w["e"])
        w["xd"] = w["xdt"] * _mm_sel_r(jnp.exp(w["alast"] - w["ac"]), w["e"])
        w["dle"] = _mm_sel_r(jnp.broadcast_to(jnp.exp(w["alast"]), (8, 2 * C_HEADS)), w["e"])[0:1]
    grp = lambda a, g: a[:, g * C_STATE:(g + 1) * C_STATE]
    cb = {(u, g): _mm_nt(grp(pre[u]["cm"], g), grp(pre[u]["bm"], g)) for u in units for g in range(C_GROUPS)}
    upd = {(u, g): _mm(pre[u]["bm_t"][g * C_STATE:(g + 1) * C_STATE], pre[u]["xd"][:, g * gw:(g + 1) * gw])
           for u in units for g in range(C_GROUPS)}
    y_diag = {}
    for u in units:
        w = pre[u]
        for h in range(C_HEADS):
            col = u[0] * C_HEADS + h
            seg = _masked_decay(w["ac"][:, col:col + 1], w["ac_t"][col:col + 1, :], w["incl"])
            y_diag[u, h] = _mm(cb[u, h // hpg] * seg, w["xdt"][:, h * C_HEADDIM:(h + 1) * C_HEADDIM])
    st = {(d, g): st_ref[d, g] for d in range(2) for g in range(C_GROUPS)}
    entry = {}
    for p in range(nch):
        for d in range(2):
            u = (d, (nch - 1 - p) if d == 1 else p)
            for g in range(C_GROUPS):
                entry[u, g] = st[d, g]
                st[d, g] = pre[u]["dle"][:, g * gw:(g + 1) * gw] * st[d, g] + upd[u, g]
    for d in range(2):
        for g in range(C_GROUPS):
            st_ref[d, g] = st[d, g]
    for u in units:
        y_off = [_mm(grp(pre[u]["cm"], g), entry[u, g]) for g in range(C_GROUPS)]
        refs[u[0]][2][u[1] * CHUNK:(u[1] + 1) * CHUNK, :] = (
            jnp.concatenate([y_diag[u, h] for h in range(C_HEADS)], axis=1)
            + pre[u]["eac_e"] * jnp.concatenate(y_off, axis=1))


def _ssd_scan(lay, xbc, small, a_log, dt_bias):
    expand = np.zeros((2, 2 * C_HEADS, C_INNER), np.float32)
    for d in range(2):
        for h in range(C_HEADS):
            expand[d, d * C_HEADS + h, h * C_HEADDIM:(h + 1) * C_HEADDIM] = 1.0
    al = a_log.reshape(1, 2 * C_HEADS)
    db = dt_bias.reshape(1, 2 * C_HEADS)
    return (_ssd_kernel, (xbc, small), (al, db, al.T, db.T, jnp.asarray(expand)), C_INNER,
            [pltpu.VMEM((2, C_GROUPS, C_STATE, C_INNER // C_GROUPS), F32)])


def _gla_kernel(xf_ref, sf_ref, xr_ref, sr_ref, gkw_ref, gkb_ref, of_ref, or_ref, st_ref):
    @pl.when(pl.program_id(1) == 0)
    def _():
        st_ref[...] = jnp.zeros(st_ref.shape, F32)

    nk = D_HEADS * D_DK
    nch = ROW_TILE // CHUNK
    refs = ((xf_ref, sf_ref, of_ref), (xr_ref, sr_ref, or_ref))
    units, pre = [], {}
    for d, (x_ref, s_ref, _) in enumerate(refs):
        incl, _ = _tri_masks(d == 1)
        m_col = incl.astype(F32)
        lr = s_ref[:, 32 + D_RANK * d:32 + D_RANK * (d + 1)]
        g_all = _log_sigmoid(_mm_3x(lr, gkw_ref[d]) + gkb_ref[d:d + 1, :]) / D_GATE_NORM
        v_t = x_ref[:, 2 * nk:].T
        for c in range(nch):
            rows = slice(c * CHUNK, (c + 1) * CHUNK)
            g_ch = g_all[rows]
            gc = _mm_sel_l(m_col, g_ch)
            glast = jnp.sum(g_ch, axis=0, keepdims=True)
            q = x_ref[rows, 0:nk] * (D_DK ** -0.5)
            k = x_ref[rows, nk:2 * nk]
            v = x_ref[rows, 2 * nk:]
            kd = k * jnp.exp(glast - gc)
            qd = q * jnp.exp(gc)
            qr = q * jnp.exp(gc - glast)
            dlast = jnp.exp(glast)
            for h in range(D_HEADS):
                ks = slice(h * D_DK, (h + 1) * D_DK)
                vs = slice(h * D_DV, (h + 1) * D_DV)
                units.append((d, c, h))
                pre[d, c, h] = dict(incl=incl, qr=qr[:, ks], kd=kd[:, ks], qd=qd[:, ks], v=v[:, vs],
                                    v_t=v_t[vs, rows], dl=dlast[:, ks])
    aqk = [jnp.where(pre[u]["incl"], _mm_nt(pre[u]["qr"], pre[u]["kd"]), 0.0) for u in units]
    upd = {u: _mm(pre[u]["v_t"], pre[u]["kd"]) for u in units}
    intra = {u: _mm(a, pre[u]["v"]) for u, a in zip(units, aqk)}
    heads = [(d, h) for d in range(2) for h in range(D_HEADS)]
    st = {dh: st_ref[dh[0], dh[1]] for dh in heads}
    entry = {}
    for p in range(nch):
        for d, h in heads:
            u = (d, (nch - 1 - p) if d == 1 else p, h)
            entry[u] = st[d, h]
            st[d, h] = st[d, h] * pre[u]["dl"] + upd[u]
    for dh in heads:
        st_ref[dh[0], dh[1]] = st[dh]
    inter = {u: _mm_nt(pre[u]["qd"], entry[u]) for u in units}
    for d, (_, _, o_ref) in enumerate(refs):
        for c in range(nch):
            o_ref[c * CHUNK:(c + 1) * CHUNK, :] = jnp.concatenate(
                [intra[d, c, h] + inter[d, c, h] for h in range(D_HEADS)], axis=1)


def _gla_scan(lay, qkv, small, gk_w, gk_b):
    return (_gla_kernel, (qkv, small), (gk_w, gk_b), D_HEADS * D_DV, [pltpu.VMEM((2, D_HEADS, D_DV, D_DK), F32)])


TRI_BLOCK = 16
GDN_GROUP = 32


def _unit_lower_inverse(nms, eye, blk):
    dg = [jnp.where(blk, nm, 0.0) for nm in nms]
    off = [nm - d for nm, d in zip(nms, dg)]
    t0 = [eye - d for d in dg]
    p = [_mm(d, d) for d in dg]
    for it in range(3):
        t0 = [t + _mm(t, q) for t, q in zip(t0, p)]
        if it < 2:
            p = [_mm(q, q) for q in p]
    m = [_mm(t, o) for t, o in zip(t0, off)]
    m2 = [_mm(a, a) for a in m]
    r = [eye - a for a in m]
    r = [a + _mm(a, b) for a, b in zip(r, m2)]
    return [_mm(a, t) for a, t in zip(r, t0)]


def _gdn_kernel(xf_ref, sf_ref, xr_ref, sr_ref, alog_ref, dtb_ref, alog_t_ref, dtb_t_ref,
                of_ref, or_ref, st_ref):
    @pl.when(pl.program_id(1) == 0)
    def _():
        st_ref[...] = jnp.zeros(st_ref.shape, F32)

    nk = A_HEADS * A_DK
    nch = ROW_TILE // CHUNK
    ii = lax.broadcasted_iota(jnp.int32, (CHUNK, CHUNK), 0)
    jj = lax.broadcasted_iota(jnp.int32, (CHUNK, CHUNK), 1)
    eye = (ii == jj).astype(F32)
    blk = (ii // TRI_BLOCK) == (jj // TRI_BLOCK)
    refs = ((xf_ref, sf_ref, of_ref), (xr_ref, sr_ref, or_ref))

    units, pre = [], {}
    for d, (x_ref, s_ref, _) in enumerate(refs):
        incl, strict = _tri_masks(d == 1)
        m_col = incl.astype(F32)
        m_row = m_col.T
        sm = s_ref[...]
        sm_t = sm.T
        beta_all = jax.nn.sigmoid(sm[:, 0:8])
        g_all = -jnp.exp(alog_ref[...]) * _softplus(sm[:, 8:16] + dtb_ref[...])
        g_t_all = -jnp.exp(alog_t_ref[...]) * _softplus(sm_t[8:16, :] + dtb_t_ref[...])
        qn, kn = [], []
        for h in range(A_HEADS):
            qh = x_ref[:, h * A_DK:(h + 1) * A_DK]
            kh = x_ref[:, nk + h * A_DK:nk + (h + 1) * A_DK]
            qn.append(qh * lax.rsqrt(jnp.sum(qh * qh, axis=-1, keepdims=True) + NORM_EPS) * (A_DK ** -0.5))
            kn.append(kh * lax.rsqrt(jnp.sum(kh * kh, axis=-1, keepdims=True) + NORM_EPS))
        kn_t = [kh.T for kh in kn]
        for c in range(nch):
            rows = slice(c * CHUNK, (c + 1) * CHUNK)
            g_ch = g_all[rows]
            gc = _mm_sel_l(m_col, g_ch)
            gc_t = _mm_sel_r(g_t_all[:, rows], m_row)
            glast = jnp.sum(g_ch, axis=0, keepdims=True)
            for h in range(A_HEADS):
                col = d * A_HEADS + h
                gcc, gcr, gl = gc[:, col:col + 1], gc_t[col:col + 1, :], glast[:, col:col + 1]
                bh = beta_all[rows, col:col + 1]
                kh = kn[h][rows]
                units.append((d, c, h))
                pre[d, c, h] = dict(
                    dmask=_masked_decay(gcc, gcr, incl), strict=strict, egc=jnp.exp(gcc), kh=kh, kb=kh * bh,
                    qh=qn[h][rows], vb=x_ref[rows, 2 * nk + h * A_DV:2 * nk + (h + 1) * A_DV] * bh,
                    kg_t=kn_t[h][:, rows] * jnp.exp(gl - gcr), dl=jnp.exp(gl))
    for g0 in range(0, len(units), GDN_GROUP):
        us = units[g0:g0 + GDN_GROUP]
        kk = [_mm_nt(pre[u]["kb"], pre[u]["kh"]) for u in us]
        qk = [_mm_nt(pre[u]["qh"], pre[u]["kh"]) for u in us]
        tinv = _unit_lower_inverse(
            [jnp.where(pre[u]["strict"], a * pre[u]["dmask"], 0.0) for u, a in zip(us, kk)], eye, blk)
        uw = [_mm(t, jnp.concatenate([pre[u]["vb"], pre[u]["kb"] * pre[u]["egc"]], axis=1))
              for u, t in zip(us, tinv)]
        for u, a, b in zip(us, uw, qk):
            pre[u]["u"] = a[:, :A_DV]
            pre[u]["wq"] = jnp.concatenate([a[:, A_DV:], pre[u]["qh"] * pre[u]["egc"]], axis=0)
            pre[u]["aqk"] = b * pre[u]["dmask"]

    heads = [(d, h) for d in range(2) for h in range(A_HEADS)]
    st = {dh: st_ref[dh[0], dh[1]] for dh in heads}
    for p in range(nch):
        us = [(d, (nch - 1 - p) if d == 1 else p, h) for d, h in heads]
        ws = [_mm(pre[u]["wq"], st[u[0], u[2]]) for u in us]
        v_new = [pre[u]["u"] - w[:CHUNK] for u, w in zip(us, ws)]
        outs = [w[CHUNK:] + _mm(pre[u]["aqk"], v) for u, w, v in zip(us, ws, v_new)]
        for u, v in zip(us, v_new):
            st[u[0], u[2]] = pre[u]["dl"] * st[u[0], u[2]] + _mm(pre[u]["kg_t"], v)
        for d, (_, _, o_ref) in enumerate(refs):
            c = us[d * A_HEADS][1]
            o_ref[c * CHUNK:(c + 1) * CHUNK, :] = jnp.concatenate(outs[d * A_HEADS:(d + 1) * A_HEADS], axis=1)
    for dh in heads:
        st_ref[dh[0], dh[1]] = st[dh]


def _gdn_scan(lay, qkv, small, a_log, dt_bias):
    al = a_log.reshape(1, 2 * A_HEADS)
    db = dt_bias.reshape(1, 2 * A_HEADS)
    return (_gdn_kernel, (qkv, small), (al, db, al.T, db.T), A_HEADS * A_DV,
            [pltpu.VMEM((2, A_HEADS, A_DK, A_DV), F32)])


FFT_N1 = 64
FFT_KB = 8


def _dft_consts(n2):
    n1 = FFT_N1
    m = n1 * n2
    k1 = np.arange(n1, dtype=np.float64)
    ang1 = 2.0 * np.pi * np.outer(k1, k1) / n1
    f1_full = np.concatenate([np.cos(ang1), -np.sin(ang1)], axis=0)
    f1 = f1_full[:, :n1 // 2]
    f3 = np.concatenate([np.cos(ang1[:, :n1 // 2]).T, -np.sin(ang1[:, :n1 // 2]).T], axis=1) / m
    j = np.arange(n2, dtype=np.float64)
    theta = 2.0 * np.pi * (np.outer(j, j)[None] / n2 + (k1[:, None, None] * j[None, None, :]) / m)
    wr, wi = np.cos(theta), -np.sin(theta)
    w2 = np.concatenate([np.concatenate([wr, -wi], axis=2), np.concatenate([wi, wr], axis=2)], axis=1)
    phi = 2.0 * np.pi * (np.outer(j, j)[None] / n2 + (k1[:, None, None] * j[None, :, None]) / m)
    vr, vi = np.cos(phi), np.sin(phi)
    v2 = np.concatenate([np.concatenate([vr, -vi], axis=2), np.concatenate([vi, vr], axis=2)], axis=1)
    return f1_full, f1, w2, v2, f3


FFT_KV = FFT_N1 // 2 + 1
FFT_KH = 40


def _half_spectrum_consts(consts):
    _, f1, w2, v2, f3 = consts
    n1 = FFT_N1
    f1h = np.zeros((2 * FFT_KH, n1 // 2))
    f1h[:FFT_KV] = f1[:FFT_KV]
    f1h[FFT_KH:FFT_KH + FFT_KV] = f1[n1:n1 + FFT_KV]
    w2h = np.zeros((FFT_KH,) + w2.shape[1:])
    w2h[:FFT_KV] = w2[:FFT_KV]
    v2h = np.zeros((FFT_KH,) + v2.shape[1:])
    v2h[:FFT_KV] = v2[:FFT_KV]
    weight = np.full((FFT_KV,), 2.0)
    weight[0] = weight[-1] = 1.0
    f3h = np.zeros((n1 // 2, 2 * FFT_KH))
    f3h[:, :FFT_KV] = f3[:, :FFT_KV] * weight
    f3h[:, FFT_KH:FFT_KH + FFT_KV] = f3[:, n1:n1 + FFT_KV] * weight
    return f1h, w2h, v2h, f3h


def _left_mm_kernel(w_ref, x_ref, o_ref):
    o_ref[...] = _mm_3x(w_ref[...], x_ref[...])


def _left_mm(wm, x):
    bs, kdim, ncols = x.shape
    mdim = wm.shape[0]
    tn = min(ncols, 4096)
    xspec = lambda rows: pl.BlockSpec((None, rows, tn), lambda b, j: (b, 0, j))
    return pl.pallas_call(
        _left_mm_kernel, grid=(bs, ncols // tn), in_specs=[_full(wm.shape), xspec(kdim)], out_specs=xspec(mdim),
        out_shape=jax.ShapeDtypeStruct((bs, mdim, ncols), F32),
        compiler_params=_cparams(("parallel", "parallel")), name="filter_dft_outer",
    )(wm, x)


def _filt_mid_kernel(a_ref, w_ref, o_ref, *, n2):
    for kk in range(FFT_KB):
        a = jnp.concatenate([a_ref[0, kk], a_ref[1, kk]], axis=0)
        xf = _mm_3x(w_ref[kk], a)
        o_ref[kk, 0] = xf[:n2]
        o_ref[kk, 1] = xf[n2:]


def _filt_mid_stage(a, w2):
    bs, _, n1, n2, ch = a.shape
    return pl.pallas_call(
        functools.partial(_filt_mid_kernel, n2=n2),
        grid=(bs, n1 // FFT_KB),
        in_specs=[pl.BlockSpec((None, 2, FFT_KB, n2, ch), lambda b, j: (b, 0, j, 0, 0)),
                  pl.BlockSpec((FFT_KB, 2 * n2, 2 * n2), lambda b, j: (j, 0, 0))],
        out_specs=pl.BlockSpec((None, FFT_KB, 2, n2, ch), lambda b, j: (b, j, 0, 0, 0)),
        out_shape=jax.ShapeDtypeStruct((bs, n1, 2, n2, ch), F32),
        compiler_params=_cparams(("parallel", "parallel")), name="filter_dft_mid",
    )(a, w2)


FILT_ROWS = 512


def _filter_kernel(z_ref, win_ref, sel_ref, w1_ref, b1_ref, w2_ref, b2_ref, wo_ref, fr_ref, o_ref):
    fr = fr_ref[...]
    h = jnp.sin(fr * (_mm_3x(z_ref[...], w1_ref[...]) + b1_ref[...]))
    for i in range(B_INNER_MLPS):
        h = jnp.sin(fr * (_mm_3x(h, w2_ref[i]) + b2_ref[i]))
    ho = _mm_3x(h, wo_ref[...])
    fwd = sel_ref[...] > 0.5
    win = win_ref[...]
    for o in range(2):
        base = o * 2 * B_CH
        o_ref[o] = jnp.where(fwd, ho[:, base:base + B_CH], ho[:, base + B_CH:base + 2 * B_CH]) * win


def _hyena_filter_time(l, w1, b1, w2, b2, w_out, freq):
    bands = (B_EMB - 1) // 2
    circ = jnp.arange(2 * l)
    pos = jnp.where(circ < l, circ, 2 * l - circ)
    pos = jnp.where(circ == l, 0, pos)
    t = (jnp.linspace(0.0, 1.0, l, dtype=F32)[pos])[:, None]
    ang = 2.0 * math.pi * pos.astype(F32)[:, None] / l
    fr = jnp.linspace(1e-4, bands - 1, bands, dtype=F32)[None, :]
    z = jnp.concatenate([t, jnp.cos(fr * ang), -jnp.sin(fr * ang)], axis=-1)
    z = jnp.pad(z, ((0, 0), (0, 128 - B_EMB)))
    max_decay = math.log(B_DECAY_TARGET) / B_DECAY_SHORT_PCT
    min_decay = math.log(B_DECAY_TARGET) / B_DECAY_LONG_PCT
    deltas = jnp.abs(jnp.linspace(min_decay, max_decay, B_CH, dtype=F32))
    win = (jnp.exp(-t * deltas) + B_WINDOW_SHIFT) * (circ != l).astype(F32)[:, None]
    sel = (circ < l).astype(F32)[:, None]
    depth = w1.shape[0]
    w1p = jnp.pad(w1, ((0, 0), (0, 128 - B_EMB), (0, 0)))
    rows = min(FILT_ROWS, 2 * l)
    per_layer = lambda a: pl.BlockSpec((None,) + a.shape[1:], lambda li, i: (li,) + (0,) * (a.ndim - 1))
    params = (w1p, b1.reshape(depth, 1, B_FFN), w2, b2.reshape(depth, B_INNER_MLPS, 1, B_FFN), w_out,
              freq.reshape(depth, 1, B_FFN))
    return pl.pallas_call(
        _filter_kernel,
        grid=(depth, 2 * l // rows),
        in_specs=[pl.BlockSpec((rows, 128), lambda li, i: (i, 0)),
                  pl.BlockSpec((rows, B_CH), lambda li, i: (i, 0)),
                  pl.BlockSpec((rows, 1), lambda li, i: (i, 0))] + [per_layer(p) for p in params],
        out_specs=pl.BlockSpec((None, 2, rows, B_CH), lambda li, i: (li, 0, i, 0)),
        out_shape=jax.ShapeDtypeStruct((depth, 2, 2 * l, B_CH), F32),
        compiler_params=_cparams(("parallel", "parallel")), name="hyena_filter",
    )(z, win, sel, *params)


def _hyena_spectrum(l, consts, w1, b1, w2m, b2, w_out, freq):
    f1_full, _, w2, _, _ = consts
    n2 = 2 * l // FFT_N1
    depth = w1.shape[0]
    g_time = _hyena_filter_time(l, w1, b1, w2m, b2, w_out, freq)
    f1_rows = np.concatenate([f1_full[:FFT_KH], f1_full[FFT_N1:FFT_N1 + FFT_KH]], axis=0)
    a = _left_mm(jnp.asarray(f1_rows, F32), g_time.reshape(depth * 2, FFT_N1, n2 * B_CH))
    g = _filt_mid_stage(a.reshape(depth * 2, 2, FFT_KH, n2, B_CH), jnp.asarray(w2[:FFT_KH], F32))
    return g.reshape(depth, 2, FFT_KH, 2, n2, B_CH)


HY_SLAB = 128


def _hy_pitch(n2c):
    return n2c + 8 if (n2c // 8) % 2 == 0 else n2c + 16


HY_KG = 20
HY_JG = 16


def _hyena_lat_kernel(zv_ref, z1_ref, z2_ref, cw_ref, cb_ref, g_ref, bias_ref, f1_ref, w2_ref, v2_ref, f3_ref,
                      o_ref, useq, ur, a_re, a_im, cbuf, *, col_mode, seq):
    n1h = FFT_N1 // 2
    n2c = seq // n1h
    pitch = _hy_pitch(n2c)
    grows = seq // GRID_W
    row_id = lax.broadcasted_iota(jnp.int32, (seq, HY_SLAB), 0)

    def short_conv(z_ref, k):
        z = z_ref[...]
        w = cw_ref[k]
        if col_mode:
            g0, gl = z[:GRID_W], z[seq - GRID_W:]
            cc = row_id[:GRID_W]
            wrap_p = jnp.where(cc == 0, 0.0, pltpu.roll(gl, 1, axis=0))
            wrap_n = jnp.where(cc == GRID_W - 1, 0.0, pltpu.roll(g0, GRID_W - 1, axis=0))
            prev = jnp.concatenate([wrap_p, z[:seq - GRID_W]], axis=0)
            nxt = jnp.concatenate([z[GRID_W:], wrap_n], axis=0)
        else:
            prev = jnp.where(row_id == 0, 0.0, pltpu.roll(z, 1, axis=0))
            nxt = jnp.where(row_id == seq - 1, 0.0, pltpu.roll(z, seq - 1, axis=0))
        return prev * w[0:1] + z * w[1:2] + nxt * w[2:3] + cb_ref[k]

    def seq_start(j):
        return (GRID_W * lax.rem(j, grows) + j // grows) if col_mode else j

    seq_stride = 2 if col_mode else pitch
    seq_ref = ur if col_mode else useq

    def put_seq(val):
        if col_mode:
            ur[...] = val
        else:
            for n1 in range(n1h):
                useq[n1 * pitch:n1 * pitch + n2c, :] = val[n1 * n2c:(n1 + 1) * n2c]

    def conv_out():
        if col_mode:
            return cbuf[...]
        return jnp.concatenate([cbuf[n1 * pitch:n1 * pitch + n2c, :] for n1 in range(n1h)], axis=0)

    y_prev = short_conv(zv_ref, 0)
    x_next = (short_conv(z1_ref, 1), short_conv(z2_ref, 2))
    put_seq(y_prev)
    f1 = f1_ref[...]
    f3 = f3_ref[...]
    jgrp = min(HY_JG, n2c)
    for order in range(2):
        def stage1(jg, carry):
            js = [jg * jgrp + jj for jj in range(jgrp)]
            xs = [seq_ref[pl.ds(seq_start(j), n1h, stride=seq_stride), :] for j in js]
            outs = [_mm(f1, x) for x in xs]
            for j, a in zip(js, outs):
                a_re[pl.ds(j, FFT_KH, stride=pitch), :] = a[:FFT_KH]
                a_im[pl.ds(j, FFT_KH, stride=pitch), :] = a[FFT_KH:]
            return carry
        lax.fori_loop(0, n2c // jgrp, stage1, 0)

        def mid(kg, carry):
            k1s = [kg * HY_KG + kk for kk in range(HY_KG)]
            offs = [pl.multiple_of(k1 * pitch, 8) for k1 in k1s]
            xin = [jnp.concatenate([a_re[pl.ds(o, n2c), :], a_im[pl.ds(o, n2c), :]], axis=0) for o in offs]
            xf = [_mm(w2_ref[k1], a) for k1, a in zip(k1s, xin)]
            ys = []
            for k1, x in zip(k1s, xf):
                xr, xi = x[:n2c], x[n2c:]
                gr, gi = g_ref[order, k1, 0], g_ref[order, k1, 1]
                ys.append(jnp.concatenate([xr * gr - xi * gi, xr * gi + xi * gr], axis=0))
            bm = [_mm(v2_ref[k1], y) for k1, y in zip(k1s, ys)]
            for o, b in zip(offs, bm):
                a_re[pl.ds(o, n2c), :] = b[:n2c]
                a_im[pl.ds(o, n2c), :] = b[n2c:]
            return carry
        lax.fori_loop(0, FFT_KH // HY_KG, mid, 0)

        def stage3(jg, carry):
            js = [jg * jgrp + jj for jj in range(jgrp)]
            bs = [jnp.concatenate([a_re[pl.ds(j, FFT_KH, stride=pitch), :],
                                   a_im[pl.ds(j, FFT_KH, stride=pitch), :]], axis=0) for j in js]
            outs = [_mm(f3, b) for b in bs]
            for j, y in zip(js, outs):
                cbuf[pl.ds(seq_start(j), n1h, stride=seq_stride), :] = y
            return carry
        lax.fori_loop(0, n2c // jgrp, stage3, 0)

        y_prev = x_next[order] * (conv_out() + y_prev * bias_ref[order:order + 1, :])
        if order == 0:
            put_seq(y_prev)
    o_ref[...] = y_prev


def _hyena_fused(lay, z, conv_w, conv_b, spec, bias, consts, l, seq, row0, s0, nslab, col_mode):
    f1, w2, v2, f3 = _half_spectrum_consts(consts)
    n2c = seq // (FFT_N1 // 2)
    assert FFT_KH % HY_KG == 0 and n2c % min(HY_JG, n2c) == 0
    pitch = _hy_pitch(n2c)
    cps = B_CH // HY_SLAB
    zspec = lambda k: pl.BlockSpec((seq, HY_SLAB), lambda j, b, k=k: (row0 + b, k * cps + s0 + j))
    once = lambda a: pl.BlockSpec(a.shape, lambda j, b: (0,) * a.ndim, pipeline_mode=pl.Buffered(1))
    mats = (jnp.asarray(f1, BF16), jnp.asarray(w2, BF16), jnp.asarray(v2, BF16), jnp.asarray(f3, BF16))
    seq_rows = (FFT_N1 // 2) * pitch
    kern = functools.partial(_hyena_lat_kernel, col_mode=col_mode, seq=seq)
    in_specs = [zspec(0), zspec(1), zspec(2),
                pl.BlockSpec((3, B_SHORT, HY_SLAB), lambda j, b: (0, 0, s0 + j)),
                pl.BlockSpec((3, 1, HY_SLAB), lambda j, b: (0, 0, s0 + j)),
                pl.BlockSpec((None, 2, FFT_KH, 2, n2c, HY_SLAB), lambda j, b: (l, 0, 0, 0, 0, s0 + j),
                             pipeline_mode=pl.Buffered(1)),
                pl.BlockSpec((2, HY_SLAB), lambda j, b: (0, s0 + j))] + [once(m) for m in mats]
    args = [z, z, z, jnp.transpose(conv_w, (1, 0, 2)), conv_b.reshape(3, 1, B_CH), spec, bias, *mats]
    return pl.pallas_call(
        kern,
        grid=(nslab, lay.b),
        in_specs=in_specs,
        out_specs=pl.BlockSpec((seq, HY_SLAB), lambda j, b: (b, j)),
        out_shape=jax.ShapeDtypeStruct((lay.b * seq, nslab * HY_SLAB), F32),
        scratch_shapes=[pltpu.VMEM((seq_rows, HY_SLAB), F32), pltpu.VMEM((seq, HY_SLAB), F32),
                        pltpu.VMEM((FFT_KH * pitch, HY_SLAB), F32), pltpu.VMEM((FFT_KH * pitch, HY_SLAB), F32),
                        pltpu.VMEM((seq if col_mode else seq_rows, HY_SLAB), F32)],
        compiler_params=_cparams(("parallel", "parallel")),
        name="hyena_col" if col_mode else "hyena_row",
    )(*args)


def _hyena(lay, z, conv_w, conv_b, spec_c, spec_x, bias, consts_c, consts_x, l, need_ctx):
    assert lay.n_ctx % lay.seq == 0
    half = B_CH // 2 // HY_SLAB
    args = (lay, z, conv_w, conv_b)
    lat0 = lay.n_ctx // lay.seq
    yc = _hyena_fused(*args, spec_c, bias, consts_c, l, lay.ctx, 0, 0, 2 * half, False) if need_ctx else None
    yr = _hyena_fused(*args, spec_x, bias, consts_x, l, lay.seq, lat0, 0, half, False)
    ycol = _hyena_fused(*args, spec_x, bias, consts_x, l, lay.seq, lat0, half, half, True)
    return yc, yr, ycol


def _head_norm_gate(o, gate, norm_w, heads, width):
    outs = []
    for h in range(heads):
        oh = o[:, h * width:(h + 1) * width]
        y = oh * lax.rsqrt(jnp.mean(oh * oh, axis=-1, keepdims=True) + NORM_EPS) * norm_w
        outs.append(y * _silu(gate[:, h * width:(h + 1) * width]))
    return jnp.concatenate(outs, axis=1)


def _merge_kernel(x_ref, mod_ref, nw_ref, a0_ref, a1_ref, ag_ref, c0_ref, c1_ref, cx_ref, cz_ref,
                  d0_ref, d1_ref, dg_ref, an_ref, cd_ref, cn_ref, dn_ref, wg_ref, wb_ref, wo_ref,
                  br_ref, bcol_ref, *rest, ctx_tiles, t0):
    o_ref = rest[-1]
    yb = jnp.concatenate([br_ref[...], bcol_ref[...]], axis=1)
    if len(rest) == 2:
        yb = jnp.where(pl.program_id(0) + t0 < ctx_tiles, rest[0][...], yb)
    xv = x_ref[...]
    m = mod_ref[...]
    hb = _prenorm(xv, nw_ref[1:2, :], m[3:4, :], m[4:5, :]).astype(BF16)
    ya = _head_norm_gate(a0_ref[...] + a1_ref[...], ag_ref[...], an_ref[...], A_HEADS, A_DV)
    yd = _head_norm_gate(d0_ref[...] + d1_ref[...], dg_ref[...], dn_ref[...], D_HEADS, D_DV)
    yc = (c0_ref[...] + c1_ref[...] + cd_ref[...] * cx_ref[...]) * _silu(cz_ref[...])
    gw = C_INNER // C_GROUPS
    cn = cn_ref[...]
    yc = jnp.concatenate(
        [yc[:, g * gw:(g + 1) * gw]
         * lax.rsqrt(jnp.mean(yc[:, g * gw:(g + 1) * gw] ** 2, axis=-1, keepdims=True) + NORM_EPS)
         * cn[:, g * gw:(g + 1) * gw] for g in range(C_GROUPS)], axis=1)
    acc = jnp.zeros(xv.shape, F32)
    for k, y in enumerate((ya, yb, yc, yd)):
        gate = jax.nn.sigmoid(jnp.dot(hb, wg_ref[:, k * D_MODEL:(k + 1) * D_MODEL], preferred_element_type=F32))
        acc = acc + gate * jnp.dot(y.astype(BF16), wb_ref[k], preferred_element_type=F32)
    o_ref[...] = xv + m[5:6, :] * jnp.dot(acc.astype(BF16), wo_ref[...], preferred_element_type=F32)


def _merge(lay, x, mods, norm_w, a0, a1, ag, yb, c0, c1, cx, cz, d0, d1, dg, an, cd, cn, dn, wg, wb, wo, l):
    n, d = x.shape
    yb_ctx, yb_row, yb_col = yb
    ctx_tiles = lay.n_ctx // TOK_TILE
    t0 = ctx_tiles if yb_ctx is None else 0
    row = lambda w: pl.BlockSpec((TOK_TILE, w), lambda i: (i + t0, 0))
    lat = lambda w: pl.BlockSpec((TOK_TILE, w), lambda i: (jnp.maximum(i + t0 - ctx_tiles, 0), 0))
    extra_specs, extra = [], []
    if yb_ctx is not None:
        extra_specs = [pl.BlockSpec((TOK_TILE, BRANCH_W), lambda i: (jnp.minimum(i, ctx_tiles - 1), 0))]
        extra = [yb_ctx]
    return pl.pallas_call(
        functools.partial(_merge_kernel, ctx_tiles=ctx_tiles, t0=t0),
        grid=(n // TOK_TILE - t0,),
        in_specs=[row(d), _mod_spec(lay, mods, l, TOK_TILE, t0), _of_layer(norm_w, l)]
                 + [row(BRANCH_W)] * 10 + [_of_layer(p, l) for p in (an, cd, cn, dn)]
                 + [_of_layer(w, l, once=True) for w in (wg, wb, wo)]
                 + [lat(BRANCH_W // 2), lat(BRANCH_W // 2)] + extra_specs,
        out_specs=row(d),
        out_shape=jax.ShapeDtypeStruct((n, d), F32),
        compiler_params=_cparams(("parallel",)), name="merge",
    )(x, mods, norm_w, a0, a1, ag, c0, c1, cx, cz, d0, d1, dg, an, cd, cn, dn, wg, wb, wo, yb_row, yb_col, *extra)


def kernel(x, c, ctx, c_ctx, w_ada, b_ada, norm_w, ffn_up, ffn_down, w_in, gdn_conv, gdn_a_log, gdn_dt_bias, gdn_norm, hy_conv_w, hy_conv_b, hy_w1, hy_b1, hy_w2, hy_b2, hy_wout, hy_freq, hy_bias, ssd_conv_w, ssd_conv_b, ssd_a_log, ssd_dt_bias, ssd_d, ssd_norm, gla_gk_w, gla_gk_b, gla_norm, w_branch, w_out, final_norm):
    b, seq, d = x.shape
    ctx_len = ctx.shape[1]
    depth = w_ada.shape[0]
    lay = _Layout(b, ctx_len, seq)
    consts_c = _dft_consts(2 * ctx_len // FFT_N1)
    consts_x = _dft_consts(2 * seq // FFT_N1)

    rp = -(-(1 + b) // 8) * 8
    cond = jnp.concatenate([c_ctx[None, :], c, jnp.zeros((rp - 1 - b, d), F32)], axis=0)
    mods = _ada(cond, w_ada, b_ada).reshape(depth, rp, N_MOD, d)

    ffn_up_b, ffn_down_b = ffn_up.astype(BF16), ffn_down.astype(BF16)
    w_r, w_gate = _rearrange_w_in(w_in)
    w_branch_b, w_out_b = w_branch.astype(BF16), w_out.astype(BF16)
    ssd_conv_b3 = ssd_conv_b.reshape(depth, 1, C_XBC)
    an = gdn_norm.reshape(depth, 1, A_DV)
    cd = jnp.repeat(ssd_d, C_HEADDIM, axis=-1).reshape(depth, 1, C_INNER)
    cn = ssd_norm.reshape(depth, 1, C_INNER)
    dn = gla_norm.reshape(depth, 1, D_DV)

    fargs = (hy_w1, hy_b1, hy_w2, hy_b2, hy_wout, hy_freq)
    spec_c = _hyena_spectrum(ctx_len, consts_c, *fargs)
    spec_x = _hyena_spectrum(seq, consts_x, *fargs)

    xf = jnp.concatenate([ctx.reshape(b * ctx_len, d), x.reshape(b * seq, d)], axis=0)
    for l in range(depth):
        xf = _ffn(lay, xf, mods, norm_w, ffn_up_b, ffn_down_b, l, 0)

        qkv, a_gate, zb, c_z, xbc, d_qkv, d_gate, small = _inproj(
            lay, xf, mods, norm_w, w_r, gdn_conv, ssd_conv_w, ssd_conv_b3, l)
        (a0, a1), (c0, c1), (d0, d1) = _scan_call(lay, [
            _gdn_scan(lay, qkv, small, gdn_a_log[l], gdn_dt_bias[l]),
            _ssd_scan(lay, xbc, small, ssd_a_log[l], ssd_dt_bias[l]),
            _gla_scan(lay, d_qkv, small, gla_gk_w[l], gla_gk_b[l])])
        last = l == depth - 1
        yb = _hyena(lay, zb, hy_conv_w[l], hy_conv_b[l], spec_c, spec_x, hy_bias[l], consts_c, consts_x, l,
                    need_ctx=not last)

        xf = _merge(lay, xf, mods, norm_w, a0, a1, a_gate, yb, c0, c1, xbc, c_z, d0, d1, d_gate,
                    an, cd, cn, dn, w_gate, w_branch_b, w_out_b, l)
        xf = _ffn(lay, xf, mods, norm_w, ffn_up_b, ffn_down_b, l, 2, final_w=final_norm if last else None)
    return xf.reshape(b, seq, d)
```

```python
import functools
import itertools
import math

import numpy as np
import jax
import jax.numpy as jnp
from jax import lax
from jax.experimental import pallas as pl
from jax.experimental.pallas import tpu as pltpu

F32 = jnp.float32
BF16 = jnp.bfloat16
HI = lax.Precision.HIGHEST

D_MODEL = 1024
GRID_W = 64
CHUNK = 64
NORM_EPS = 1e-6
N_MOD = 9
D_FF = 2816
SHORT_CONV = 5

A_HEADS, A_DK, A_DV = 4, 128, 128
A_QKV = A_HEADS * (2 * A_DK + A_DV)
B_CH, B_SHORT, B_EMB, B_FFN, B_INNER_MLPS = 512, 3, 33, 64, 2
B_WINDOW_SHIFT, B_DECAY_SHORT_PCT, B_DECAY_LONG_PCT, B_DECAY_TARGET = 0.05, 0.3, 1.5, 1e-2
C_HEADS, C_HEADDIM, C_GROUPS, C_STATE = 8, 64, 2, 64
C_INNER = C_HEADS * C_HEADDIM
C_XBC = C_INNER + 2 * C_GROUPS * C_STATE
D_HEADS, D_DK, D_DV, D_RANK = 4, 64, 128, 16
D_GATE_NORM = 16.0
N_BRANCH, BRANCH_W = 4, 512

IN_SIZES = (A_QKV, A_HEADS * A_DV, 2 * A_HEADS, 2 * A_HEADS, 3 * B_CH, C_INNER, C_XBC, 2 * C_HEADS,
            D_HEADS * (2 * D_DK + D_DV), D_HEADS * D_DV, 2 * D_RANK, N_BRANCH * D_MODEL)
IN_OFFS = tuple(int(v) for v in np.cumsum((0,) + IN_SIZES))

ROW_TILE = 256
TOK_TILE = 512
SMALL_W = 128
VMEM_LIMIT = 56 * 1024 * 1024


def _cparams(sem):
    return pltpu.CompilerParams(dimension_semantics=sem, vmem_limit_bytes=VMEM_LIMIT)


def _mm(a, b):
    return jnp.dot(a.astype(BF16), b.astype(BF16), preferred_element_type=F32)


def _mm_nt(a, b):
    return lax.dot_general(a.astype(BF16), b.astype(BF16), (((1,), (1,)), ((), ())),
                           preferred_element_type=F32)


def _mm_hi(a, b):
    return jnp.dot(a, b, precision=HI, preferred_element_type=F32)


def _bf16_parts(v, parts):
    out, rest = [], v
    for _ in range(parts):
        hi = rest.astype(BF16)
        out.append(hi)
        rest = rest - hi.astype(F32)
    return out


def _mm_3x(a, b):
    ah, al = _bf16_parts(a, 2)
    bh, bl = _bf16_parts(b, 2)
    dot = functools.partial(jnp.dot, preferred_element_type=F32)
    return dot(ah, bh) + (dot(ah, bl) + dot(al, bh))


def _mm_sel_l(sel, v, parts=3):
    sb = sel.astype(BF16)
    return sum(jnp.dot(sb, p, preferred_element_type=F32) for p in _bf16_parts(v, parts))


def _mm_sel_r(v, sel, parts=3):
    sb = sel.astype(BF16)
    return sum(jnp.dot(p, sb, preferred_element_type=F32) for p in _bf16_parts(v, parts))


def _silu(v):
    return v * jax.nn.sigmoid(v)


def _softplus(v):
    return jnp.maximum(v, 0.0) + jnp.log1p(jnp.exp(-jnp.abs(v)))


def _log_sigmoid(v):
    return jnp.minimum(v, 0.0) - jnp.log1p(jnp.exp(-jnp.abs(v)))


def _prenorm(xv, gain, shift, scale):
    ms = jnp.mean(xv * xv, axis=-1, keepdims=True)
    return (xv * lax.rsqrt(ms + NORM_EPS) * gain) * (1.0 + scale) + shift


def _full(shape):
    nd = len(shape)
    return pl.BlockSpec(shape, lambda *_: (0,) * nd)


class _Layout:
    def __init__(self, batch, ctx_len, seq):
        assert ctx_len % ROW_TILE == 0 and seq % ROW_TILE == 0
        assert (batch * ctx_len) % TOK_TILE == 0 and seq % TOK_TILE == 0
        self.b, self.ctx, self.seq = batch, ctx_len, seq
        self.n_ctx = batch * ctx_len
        self.n = batch * (ctx_len + seq)
        self.tc, self.tl = ctx_len // ROW_TILE, seq // ROW_TILE
        self.nct = batch * self.tc
        self.tiles = self.n // ROW_TILE

    def mod_index(self, i, tile=TOK_TILE):
        nct = self.n_ctx // tile
        return jnp.where(i < nct, 0, 1 + (i - nct) // (self.seq // tile))

    def seg_first(self, i):
        return jnp.where(i < self.nct, lax.rem(i, self.tc) == 0, lax.rem(i - self.nct, self.tl) == 0)

    def seg_last(self, i):
        return jnp.where(i < self.nct, lax.rem(i, self.tc) == self.tc - 1,
                         lax.rem(i - self.nct, self.tl) == self.tl - 1)

    def fwd_tile(self, b, s):
        return jnp.where(s < self.tc, b * self.tc + s, self.nct + b * self.tl + (s - self.tc))

    def rev_tile(self, b, s):
        return jnp.where(s < self.tc, b * self.tc + (self.tc - 1 - s),
                         self.nct + b * self.tl + (self.tl - 1 - (s - self.tc)))


def _ada_kernel(c_ref, w_ref, b_ref, o_ref):
    o_ref[...] = _mm_hi(_silu(c_ref[...]), w_ref[...]) + b_ref[...]


def _ada(cond, w_ada, b_ada):
    depth, d, nm = w_ada.shape
    rp = cond.shape[0]
    tn = 1152
    return pl.pallas_call(
        _ada_kernel,
        grid=(depth, nm // tn),
        in_specs=[_full((rp, d)),
                  pl.BlockSpec((None, d, tn), lambda l, j: (l, 0, j)),
                  pl.BlockSpec((None, 1, tn), lambda l, j: (l, 0, j))],
        out_specs=pl.BlockSpec((None, rp, tn), lambda l, j: (l, 0, j)),
        out_shape=jax.ShapeDtypeStruct((depth, rp, nm), F32),
        compiler_params=_cparams(("parallel", "parallel")), name="ada",
    )(cond, w_ada, b_ada.reshape(depth, 1, nm))


FF_CHUNK = 256


def _ffn_kernel(x_ref, mod_ref, nw_ref, wup_ref, wdn_ref, *rest, sub):
    o_ref = rest[-1]
    xv = x_ref[...]
    m = mod_ref[...]
    h = _prenorm(xv, nw_ref[sub:sub + 1, :], m[3 * sub:3 * sub + 1, :], m[3 * sub + 1:3 * sub + 2, :])
    hb = h.astype(BF16)
    acc = jnp.zeros(xv.shape, F32)
    for c in range(D_FF // FF_CHUNK):
        lo = c * FF_CHUNK
        a = jnp.dot(hb, wup_ref[:, lo:lo + FF_CHUNK], preferred_element_type=F32)
        g = jnp.dot(hb, wup_ref[:, D_FF + lo:D_FF + lo + FF_CHUNK], preferred_element_type=F32)
        acc = acc + jnp.dot((_silu(a) * g).astype(BF16), wdn_ref[lo:lo + FF_CHUNK, :],
                            preferred_element_type=F32)
    out = xv + 0.5 * m[3 * sub + 2:3 * sub + 3, :] * acc
    if len(rest) == 2:
        out = out * lax.rsqrt(jnp.mean(out * out, axis=-1, keepdims=True) + NORM_EPS) * rest[0][...]
    o_ref[...] = out


def _of_layer(a, l, *lead, once=False):
    idx = (l,) + lead
    block = (None,) * len(idx) + a.shape[len(idx):]
    mode = dict(pipeline_mode=pl.Buffered(1)) if once else {}
    return pl.BlockSpec(block, lambda *_: idx + (0,) * (a.ndim - len(idx)), **mode)


def _mod_spec(lay, mods, l, tile, t0=0):
    return pl.BlockSpec((None, None, N_MOD, mods.shape[-1]), lambda i: (l, lay.mod_index(i + t0, tile), 0, 0))


def _ffn(lay, x, mods, norm_w, wup, wdn, l, sub, final_w=None):
    n, d = x.shape
    t0 = 0 if final_w is None else lay.n_ctx // TOK_TILE
    extra = [] if final_w is None else [final_w.reshape(1, d)]
    return pl.pallas_call(
        functools.partial(_ffn_kernel, sub=sub),
        grid=(n // TOK_TILE - t0,),
        in_specs=[pl.BlockSpec((TOK_TILE, d), lambda i: (i + t0, 0)),
                  _mod_spec(lay, mods, l, TOK_TILE, t0),
                  _of_layer(norm_w, l),
                  _of_layer(wup, l, sub // 2, once=True),
                  _of_layer(wdn, l, sub // 2, once=True)]
                 + [_full(e.shape) for e in extra],
        out_specs=pl.BlockSpec((TOK_TILE, d), lambda i: (i, 0)),
        out_shape=jax.ShapeDtypeStruct((n - t0 * TOK_TILE, d), F32),
        compiler_params=_cparams(("parallel",)), name="ffn",
    )(x, mods, norm_w, wup, wdn, *extra)


PROJ_W = (A_QKV, A_HEADS * A_DV, 3 * B_CH, C_INNER, C_XBC, D_HEADS * (2 * D_DK + D_DV), D_HEADS * D_DV, SMALL_W)
PROJ_O = tuple(int(v) for v in np.cumsum((0,) + PROJ_W))


def _rearrange_w_in(w_in):
    o = IN_OFFS
    col = lambda a, b: w_in[..., o[a]:o[b]]
    small = jnp.concatenate([col(2, 4), col(7, 8), col(10, 11),
                             jnp.zeros(w_in.shape[:-1] + (SMALL_W - 64,), w_in.dtype)], axis=-1)
    w_r = jnp.concatenate([col(0, 1), col(1, 2), col(4, 5), col(5, 6), col(6, 7), col(8, 9), col(9, 10), small],
                          axis=-1)
    return w_r.astype(BF16), col(11, 12).astype(BF16)


PROJ_HALO = 16
PROJ_CONV = (0, 4)


def _inproj_kernel(xp_ref, x_ref, xn_ref, mod_ref, nw_ref, w_ref, cwa_ref, cwc_ref, cbc_ref, *rest, lay):
    o_refs, zbuf = rest[:-1], rest[-1]
    i = pl.program_id(0)
    m = mod_ref[...]
    norm = lambda r: _prenorm(r[...], nw_ref[1:2, :], m[3:4, :], m[4:5, :]).astype(BF16)
    hb = norm(x_ref)
    hb_all = jnp.concatenate([norm(xp_ref), hb, norm(xn_ref)], axis=0)
    keep_p = jnp.where(lay.seg_first(i), 0.0, 1.0)
    keep_n = jnp.where(lay.seg_last(i), 0.0, 1.0)
    convs = {PROJ_CONV[0]: (cwa_ref, None), PROJ_CONV[1]: (cwc_ref, cbc_ref)}
    for k, o_ref in enumerate(o_refs):
        wk = w_ref[:, PROJ_O[k]:PROJ_O[k + 1]]
        if k not in convs:
            o_ref[...] = jnp.dot(hb, wk, preferred_element_type=F32)
            continue
        cw_ref, cb_ref = convs[k]
        width = PROJ_W[k]
        z = jnp.dot(hb_all, wk, preferred_element_type=F32)
        zbuf[0:PROJ_HALO, 0:width] = z[:PROJ_HALO] * keep_p
        zbuf[PROJ_HALO:PROJ_HALO + ROW_TILE, 0:width] = z[PROJ_HALO:PROJ_HALO + ROW_TILE]
        zbuf[PROJ_HALO + ROW_TILE:, 0:width] = z[PROJ_HALO + ROW_TILE:] * keep_n
        taps = cw_ref.shape[0]
        left = (taps - 1) // 2
        acc = None
        for t in range(taps):
            off = t - left
            term = zbuf[PROJ_HALO + off:PROJ_HALO + off + ROW_TILE, 0:width] * cw_ref[t:t + 1, :]
            acc = term if acc is None else acc + term
        if cb_ref is not None:
            acc = acc + cb_ref[...]
        o_ref[...] = _silu(acc)


def _inproj(lay, x, mods, norm_w, w_r, conv_a, conv_c, conv_c_bias, l):
    n, d = x.shape
    rh = ROW_TILE // PROJ_HALO
    lasth = n // PROJ_HALO - 1
    return pl.pallas_call(
        functools.partial(_inproj_kernel, lay=lay),
        grid=(n // ROW_TILE,),
        in_specs=[pl.BlockSpec((PROJ_HALO, d), lambda i: (jnp.maximum(i * rh - 1, 0), 0)),
                  pl.BlockSpec((ROW_TILE, d), lambda i: (i, 0)),
                  pl.BlockSpec((PROJ_HALO, d), lambda i: (jnp.minimum((i + 1) * rh, lasth), 0)),
                  _mod_spec(lay, mods, l, ROW_TILE),
                  _of_layer(norm_w, l),
                  _of_layer(w_r, l, once=True),
                  _of_layer(conv_a, l), _of_layer(conv_c, l), _of_layer(conv_c_bias, l)],
        out_specs=[pl.BlockSpec((ROW_TILE, w), lambda i: (i, 0)) for w in PROJ_W],
        out_shape=[jax.ShapeDtypeStruct((n, w), F32) for w in PROJ_W],
        scratch_shapes=[pltpu.VMEM((ROW_TILE + 2 * PROJ_HALO, max(PROJ_W[k] for k in PROJ_CONV)), F32)],
        compiler_params=_cparams(("parallel",)), name="inproj",
    )(x, x, x, mods, norm_w, w_r, conv_a, conv_c, conv_c_bias)


def _tri_masks(rev):
    ii = lax.broadcasted_iota(jnp.int32, (CHUNK, CHUNK), 0)
    jj = lax.broadcasted_iota(jnp.int32, (CHUNK, CHUNK), 1)
    return (ii <= jj, ii < jj) if rev else (ii >= jj, ii > jj)


def _masked_decay(col, row, incl):
    return jnp.where(incl, jnp.exp(jnp.where(incl, col - row, 0.0)), 0.0)


def _scan_call(lay, parts):
    steps = lay.tc + lay.tl
    specs, args, out_specs, out_shape, scratch, counts = [], [], [], [], [], []
    for _, ins_tiled, ins_full, out_w, scr in parts:
        n0 = len(specs)
        for walk in (lay.fwd_tile, lay.rev_tile):
            for a in ins_tiled:
                specs.append(pl.BlockSpec((ROW_TILE, a.shape[1]), lambda b, s, walk=walk: (walk(b, s), 0)))
                args.append(a)
            out_specs.append(pl.BlockSpec((ROW_TILE, out_w), lambda b, s, walk=walk: (walk(b, s), 0)))
            out_shape.append(jax.ShapeDtypeStruct((lay.n, out_w), F32))
        for a in ins_full:
            specs.append(_full(a.shape))
            args.append(a)
        counts.append((len(specs) - n0, len(scr)))
        scratch += list(scr)

    def body(*refs):
        n_in, n_out = len(specs), len(out_specs)
        i, s = 0, n_in + n_out
        gens = []
        for p, (kern, *_) in enumerate(parts):
            ci, cs = counts[p]
            gens.append(kern(*refs[i:i + ci], *refs[n_in + 2 * p:n_in + 2 * p + 2], *refs[s:s + cs]))
            i, s = i + ci, s + cs
        lead, rest = gens[0], itertools.chain(*gens[1:])
        live = [lead, rest]
        while live:
            live = [g for g in live if next(g, StopIteration) is not StopIteration]

    res = pl.pallas_call(
        body,
        grid=(lay.b, steps),
        in_specs=specs,
        out_specs=out_specs,
        out_shape=out_shape,
        scratch_shapes=scratch,
        compiler_params=_cparams(("arbitrary", "arbitrary")), name="scans",
    )(*args)
    return [tuple(res[2 * p:2 * p + 2]) for p in range(len(parts))]


def _ssd_kernel(xf_ref, sf_ref, xr_ref, sr_ref, alog_ref, dtb_ref, alog_t_ref, dtb_t_ref, exp_ref,
                of_ref, or_ref, st_ref):
    @pl.when(pl.program_id(1) == 0)
    def _():
        st_ref[...] = jnp.zeros(st_ref.shape, F32)

    hpg = C_HEADS // C_GROUPS
    gw = hpg * C_HEADDIM
    nch = ROW_TILE // CHUNK
    refs = ((xf_ref, sf_ref, of_ref), (xr_ref, sr_ref, or_ref))
    units, pre = [], {}
    for d, (x_ref, s_ref, _) in enumerate(refs):
        incl, _ = _tri_masks(d == 1)
        m_col = incl.astype(F32)
        m_row = m_col.T
        sm = s_ref[...]
        sm_t = sm.T
        dt_all = _softplus(sm[:, 16:32] + dtb_ref[...])
        a_all = -jnp.exp(alog_ref[...]) * dt_all
        a_t_all = -jnp.exp(alog_t_ref[...]) * _softplus(sm_t[16:32, :] + dtb_t_ref[...])
        bm_t = x_ref[:, C_INNER:C_INNER + C_GROUPS * C_STATE].T
        for c in range(nch):
            rows = slice(c * CHUNK, (c + 1) * CHUNK)
            units.append((d, c))
            pre[d, c] = dict(incl=incl, m_col=m_col, m_row=m_row, a_ch=a_all[rows], a_t=a_t_all[:, rows],
                             dt=dt_all[rows], bm_t=bm_t[:, rows], e=exp_ref[d], xs=x_ref[rows, 0:C_INNER],
                             bm=x_ref[rows, C_INNER:C_INNER + C_GROUPS * C_STATE],
                             cm=x_ref[rows, C_INNER + C_GROUPS * C_STATE:C_XBC])
    for u in units:
        w = pre[u]
        w["ac"] = _mm_sel_l(w["m_col"], w["a_ch"])
        w["ac_t"] = _mm_sel_r(w["a_t"], w["m_row"])
        w["alast"] = jnp.sum(w["a_ch"], axis=0, keepdims=True)
    yield
    for u in units:
        w = pre[u]
        w["xdt"] = w["xs"] * _mm_sel_r(w["dt"], w["e"])
        w["eac_e"] = _mm_sel_r(jnp.exp(w["ac"]), w["e"])
        w["xd"] = w["xdt"] * _mm_sel_r(jnp.exp(w["alast"] - w["ac"]), w["e"])
        w["dle"] = _mm_sel_r(jnp.broadcast_to(jnp.exp(w["alast"]), (8, 2 * C_HEADS)), w["e"])[0:1]
    yield
    grp = lambda a, g: a[:, g * C_STATE:(g + 1) * C_STATE]
    cb = {(u, g): _mm_nt(grp(pre[u]["cm"], g), grp(pre[u]["bm"], g)) for u in units for g in range(C_GROUPS)}
    upd = {(u, g): _mm(pre[u]["bm_t"][g * C_STATE:(g + 1) * C_STATE], pre[u]["xd"][:, g * gw:(g + 1) * gw])
           for u in units for g in range(C_GROUPS)}
    yield
    y_diag = {}
    for u in units:
        w = pre[u]
        for h in range(C_HEADS):
            col = u[0] * C_HEADS + h
            seg = _masked_decay(w["ac"][:, col:col + 1], w["ac_t"][col:col + 1, :], w["incl"])
            y_diag[u, h] = _mm(cb[u, h // hpg] * seg, w["xdt"][:, h * C_HEADDIM:(h + 1) * C_HEADDIM])
    st = {(d, g): st_ref[d, g] for d in range(2) for g in range(C_GROUPS)}
    entry = {}
    for p in range(nch):
        for d in range(2):
            u = (d, (nch - 1 - p) if d == 1 else p)
            for g in range(C_GROUPS):
                entry[u, g] = st[d, g]
                st[d, g] = pre[u]["dle"][:, g * gw:(g + 1) * gw] * st[d, g] + upd[u, g]
    for d in range(2):
        for g in range(C_GROUPS):
            st_ref[d, g] = st[d, g]
    yield
    for u in units:
        y_off = [_mm(grp(pre[u]["cm"], g), entry[u, g]) for g in range(C_GROUPS)]
        refs[u[0]][2][u[1] * CHUNK:(u[1] + 1) * CHUNK, :] = (
            jnp.concatenate([y_diag[u, h] for h in range(C_HEADS)], axis=1)
            + pre[u]["eac_e"] * jnp.concatenate(y_off, axis=1))


def _ssd_scan(lay, xbc, small, a_log, dt_bias):
    expand = np.zeros((2, 2 * C_HEADS, C_INNER), np.float32)
    for d in range(2):
        for h in range(C_HEADS):
            expand[d, d * C_HEADS + h, h * C_HEADDIM:(h + 1) * C_HEADDIM] = 1.0
    al = a_log.reshape(1, 2 * C_HEADS)
    db = dt_bias.reshape(1, 2 * C_HEADS)
    return (_ssd_kernel, (xbc, small), (al, db, al.T, db.T, jnp.asarray(expand)), C_INNER,
            [pltpu.VMEM((2, C_GROUPS, C_STATE, C_INNER // C_GROUPS), F32)])


def _gla_kernel(xf_ref, sf_ref, xr_ref, sr_ref, gkw_ref, gkb_ref, of_ref, or_ref, st_ref):
    @pl.when(pl.program_id(1) == 0)
    def _():
        st_ref[...] = jnp.zeros(st_ref.shape, F32)

    nk = D_HEADS * D_DK
    nch = ROW_TILE // CHUNK
    refs = ((xf_ref, sf_ref, of_ref), (xr_ref, sr_ref, or_ref))
    units, pre = [], {}
    for d, (x_ref, s_ref, _) in enumerate(refs):
        incl, _ = _tri_masks(d == 1)
        m_col = incl.astype(F32)
        lr = s_ref[:, 32 + D_RANK * d:32 + D_RANK * (d + 1)]
        g_all = _log_sigmoid(_mm_3x(lr, gkw_ref[d]) + gkb_ref[d:d + 1, :]) / D_GATE_NORM
        v_t = x_ref[:, 2 * nk:].T
        for c in range(nch):
            rows = slice(c * CHUNK, (c + 1) * CHUNK)
            g_ch = g_all[rows]
            gc = _mm_sel_l(m_col, g_ch)
            glast = jnp.sum(g_ch, axis=0, keepdims=True)
            q = x_ref[rows, 0:nk] * (D_DK ** -0.5)
            k = x_ref[rows, nk:2 * nk]
            v = x_ref[rows, 2 * nk:]
            kd = k * jnp.exp(glast - gc)
            qd = q * jnp.exp(gc)
            qr = q * jnp.exp(gc - glast)
            dlast = jnp.exp(glast)
            for h in range(D_HEADS):
                ks = slice(h * D_DK, (h + 1) * D_DK)
                vs = slice(h * D_DV, (h + 1) * D_DV)
                units.append((d, c, h))
                pre[d, c, h] = dict(incl=incl, qr=qr[:, ks], kd=kd[:, ks], qd=qd[:, ks], v=v[:, vs],
                                    v_t=v_t[vs, rows], dl=dlast[:, ks])
    yield
    aqk = [jnp.where(pre[u]["incl"], _mm_nt(pre[u]["qr"], pre[u]["kd"]), 0.0) for u in units]
    yield
    upd = {u: _mm(pre[u]["v_t"], pre[u]["kd"]) for u in units}
    yield
    intra = {u: _mm(a, pre[u]["v"]) for u, a in zip(units, aqk)}
    heads = [(d, h) for d in range(2) for h in range(D_HEADS)]
    st = {dh: st_ref[dh[0], dh[1]] for dh in heads}
    entry = {}
    for p in range(nch):
        for d, h in heads:
            u = (d, (nch - 1 - p) if d == 1 else p, h)
            entry[u] = st[d, h]
            st[d, h] = st[d, h] * pre[u]["dl"] + upd[u]
    for dh in heads:
        st_ref[dh[0], dh[1]] = st[dh]
    yield
    inter = {u: _mm_nt(pre[u]["qd"], entry[u]) for u in units}
    for d, (_, _, o_ref) in enumerate(refs):
        for c in range(nch):
            o_ref[c * CHUNK:(c + 1) * CHUNK, :] = jnp.concatenate(
                [intra[d, c, h] + inter[d, c, h] for h in range(D_HEADS)], axis=1)


def _gla_scan(lay, qkv, small, gk_w, gk_b):
    return (_gla_kernel, (qkv, small), (gk_w, gk_b), D_HEADS * D_DV, [pltpu.VMEM((2, D_HEADS, D_DV, D_DK), F32)])


TRI_BLOCK = 16
GDN_GROUP = 32


def _unit_lower_inverse(nms, eye, blk):
    dg = [jnp.where(blk, nm, 0.0) for nm in nms]
    off = [nm - d for nm, d in zip(nms, dg)]
    t0 = [eye - d for d in dg]
    p = [_mm(d, d) for d in dg]
    for it in range(3):
        t0 = [t + _mm(t, q) for t, q in zip(t0, p)]
        if it < 2:
            p = [_mm(q, q) for q in p]
    m = [_mm(t, o) for t, o in zip(t0, off)]
    m2 = [_mm(a, a) for a in m]
    r = [eye - a for a in m]
    r = [a + _mm(a, b) for a, b in zip(r, m2)]
    return [_mm(a, t) for a, t in zip(r, t0)]


def _gdn_kernel(xf_ref, sf_ref, xr_ref, sr_ref, alog_ref, dtb_ref, alog_t_ref, dtb_t_ref,
                of_ref, or_ref, st_ref):
    @pl.when(pl.program_id(1) == 0)
    def _():
        st_ref[...] = jnp.zeros(st_ref.shape, F32)

    nk = A_HEADS * A_DK
    nch = ROW_TILE // CHUNK
    ii = lax.broadcasted_iota(jnp.int32, (CHUNK, CHUNK), 0)
    jj = lax.broadcasted_iota(jnp.int32, (CHUNK, CHUNK), 1)
    eye = (ii == jj).astype(F32)
    blk = (ii // TRI_BLOCK) == (jj // TRI_BLOCK)
    refs = ((xf_ref, sf_ref, of_ref), (xr_ref, sr_ref, or_ref))

    units, pre = [], {}
    for d, (x_ref, s_ref, _) in enumerate(refs):
        incl, strict = _tri_masks(d == 1)
        m_col = incl.astype(F32)
        m_row = m_col.T
        sm = s_ref[...]
        sm_t = sm.T
        beta_all = jax.nn.sigmoid(sm[:, 0:8])
        g_all = -jnp.exp(alog_ref[...]) * _softplus(sm[:, 8:16] + dtb_ref[...])
        g_t_all = -jnp.exp(alog_t_ref[...]) * _softplus(sm_t[8:16, :] + dtb_t_ref[...])
        qn, kn = [], []
        for h in range(A_HEADS):
            qh = x_ref[:, h * A_DK:(h + 1) * A_DK]
            kh = x_ref[:, nk + h * A_DK:nk + (h + 1) * A_DK]
            qn.append(qh * lax.rsqrt(jnp.sum(qh * qh, axis=-1, keepdims=True) + NORM_EPS) * (A_DK ** -0.5))
            kn.append(kh * lax.rsqrt(jnp.sum(kh * kh, axis=-1, keepdims=True) + NORM_EPS))
        kn_t = [kh.T for kh in kn]
        for c in range(nch):
            rows = slice(c * CHUNK, (c + 1) * CHUNK)
            g_ch = g_all[rows]
            gc = _mm_sel_l(m_col, g_ch)
            gc_t = _mm_sel_r(g_t_all[:, rows], m_row)
            glast = jnp.sum(g_ch, axis=0, keepdims=True)
            for h in range(A_HEADS):
                col = d * A_HEADS + h
                gcc, gcr, gl = gc[:, col:col + 1], gc_t[col:col + 1, :], glast[:, col:col + 1]
                bh = beta_all[rows, col:col + 1]
                kh = kn[h][rows]
                units.append((d, c, h))
                pre[d, c, h] = dict(
                    dmask=_masked_decay(gcc, gcr, incl), strict=strict, egc=jnp.exp(gcc), kh=kh, kb=kh * bh,
                    qh=qn[h][rows], vb=x_ref[rows, 2 * nk + h * A_DV:2 * nk + (h + 1) * A_DV] * bh,
                    kg_t=kn_t[h][:, rows] * jnp.exp(gl - gcr), dl=jnp.exp(gl))
    for g0 in range(0, len(units), GDN_GROUP):
        us = units[g0:g0 + GDN_GROUP]
        kk = [_mm_nt(pre[u]["kb"], pre[u]["kh"]) for u in us]
        qk = [_mm_nt(pre[u]["qh"], pre[u]["kh"]) for u in us]
        tinv = _unit_lower_inverse(
            [jnp.where(pre[u]["strict"], a * pre[u]["dmask"], 0.0) for u, a in zip(us, kk)], eye, blk)
        uw = [_mm(t, jnp.concatenate([pre[u]["vb"], pre[u]["kb"] * pre[u]["egc"]], axis=1))
              for u, t in zip(us, tinv)]
        for u, a, b in zip(us, uw, qk):
            pre[u]["u"] = a[:, :A_DV]
            pre[u]["wq"] = jnp.concatenate([a[:, A_DV:], pre[u]["qh"] * pre[u]["egc"]], axis=0)
            pre[u]["aqk"] = b * pre[u]["dmask"]

    heads = [(d, h) for d in range(2) for h in range(A_HEADS)]
    st = {dh: st_ref[dh[0], dh[1]] for dh in heads}
    for p in range(nch):
        yield
        us = [(d, (nch - 1 - p) if d == 1 else p, h) for d, h in heads]
        ws = [_mm(pre[u]["wq"], st[u[0], u[2]]) for u in us]
        yield
        v_new = [pre[u]["u"] - w[:CHUNK] for u, w in zip(us, ws)]
        outs = [w[CHUNK:] + _mm(pre[u]["aqk"], v) for u, w, v in zip(us, ws, v_new)]
        for u, v in zip(us, v_new):
            st[u[0], u[2]] = pre[u]["dl"] * st[u[0], u[2]] + _mm(pre[u]["kg_t"], v)
        for d, (_, _, o_ref) in enumerate(refs):
            c = us[d * A_HEADS][1]
            o_ref[c * CHUNK:(c + 1) * CHUNK, :] = jnp.concatenate(outs[d * A_HEADS:(d + 1) * A_HEADS], axis=1)
    for dh in heads:
        st_ref[dh[0], dh[1]] = st[dh]


def _gdn_scan(lay, qkv, small, a_log, dt_bias):
    al = a_log.reshape(1, 2 * A_HEADS)
    db = dt_bias.reshape(1, 2 * A_HEADS)
    return (_gdn_kernel, (qkv, small), (al, db, al.T, db.T), A_HEADS * A_DV,
            [pltpu.VMEM((2, A_HEADS, A_DK, A_DV), F32)])


FFT_N1 = 64
FFT_KB = 8


def _dft_consts(n2):
    n1 = FFT_N1
    m = n1 * n2
    k1 = np.arange(n1, dtype=np.float64)
    ang1 = 2.0 * np.pi * np.outer(k1, k1) / n1
    f1_full = np.concatenate([np.cos(ang1), -np.sin(ang1)], axis=0)
    f1 = f1_full[:, :n1 // 2]
    f3 = np.concatenate([np.cos(ang1[:, :n1 // 2]).T, -np.sin(ang1[:, :n1 // 2]).T], axis=1) / m
    j = np.arange(n2, dtype=np.float64)
    theta = 2.0 * np.pi * (np.outer(j, j)[None] / n2 + (k1[:, None, None] * j[None, None, :]) / m)
    wr, wi = np.cos(theta), -np.sin(theta)
    w2 = np.concatenate([np.concatenate([wr, -wi], axis=2), np.concatenate([wi, wr], axis=2)], axis=1)
    phi = 2.0 * np.pi * (np.outer(j, j)[None] / n2 + (k1[:, None, None] * j[None, :, None]) / m)
    vr, vi = np.cos(phi), np.sin(phi)
    v2 = np.concatenate([np.concatenate([vr, -vi], axis=2), np.concatenate([vi, vr], axis=2)], axis=1)
    return f1_full, f1, w2, v2, f3


FFT_KV = FFT_N1 // 2 + 1
FFT_KH = 40


def _half_spectrum_consts(consts):
    _, f1, w2, v2, f3 = consts
    n1 = FFT_N1
    f1h = np.zeros((2 * FFT_KH, n1 // 2))
    f1h[:FFT_KV] = f1[:FFT_KV]
    f1h[FFT_KH:FFT_KH + FFT_KV] = f1[n1:n1 + FFT_KV]
    w2h = np.zeros((FFT_KH,) + w2.shape[1:])
    w2h[:FFT_KV] = w2[:FFT_KV]
    v2h = np.zeros((FFT_KH,) + v2.shape[1:])
    v2h[:FFT_KV] = v2[:FFT_KV]
    weight = np.full((FFT_KV,), 2.0)
    weight[0] = weight[-1] = 1.0
    f3h = np.zeros((n1 // 2, 2 * FFT_KH))
    f3h[:, :FFT_KV] = f3[:, :FFT_KV] * weight
    f3h[:, FFT_KH:FFT_KH + FFT_KV] = f3[:, n1:n1 + FFT_KV] * weight
    return f1h, w2h, v2h, f3h


def _left_mm_kernel(w_ref, x_ref, o_ref):
    o_ref[...] = _mm_3x(w_ref[...], x_ref[...])


def _left_mm(wm, x):
    bs, kdim, ncols = x.shape
    mdim = wm.shape[0]
    tn = min(ncols, 4096)
    xspec = lambda rows: pl.BlockSpec((None, rows, tn), lambda b, j: (b, 0, j))
    return pl.pallas_call(
        _left_mm_kernel, grid=(bs, ncols // tn), in_specs=[_full(wm.shape), xspec(kdim)], out_specs=xspec(mdim),
        out_shape=jax.ShapeDtypeStruct((bs, mdim, ncols), F32),
        compiler_params=_cparams(("parallel", "parallel")), name="filter_dft_outer",
    )(wm, x)


def _filt_mid_kernel(a_ref, w_ref, o_ref, *, n2):
    for kk in range(FFT_KB):
        a = jnp.concatenate([a_ref[0, kk], a_ref[1, kk]], axis=0)
        xf = _mm_3x(w_ref[kk], a)
        o_ref[kk, 0] = xf[:n2]
        o_ref[kk, 1] = xf[n2:]


def _filt_mid_stage(a, w2):
    bs, _, n1, n2, ch = a.shape
    return pl.pallas_call(
        functools.partial(_filt_mid_kernel, n2=n2),
        grid=(bs, n1 // FFT_KB),
        in_specs=[pl.BlockSpec((None, 2, FFT_KB, n2, ch), lambda b, j: (b, 0, j, 0, 0)),
                  pl.BlockSpec((FFT_KB, 2 * n2, 2 * n2), lambda b, j: (j, 0, 0))],
        out_specs=pl.BlockSpec((None, FFT_KB, 2, n2, ch), lambda b, j: (b, j, 0, 0, 0)),
        out_shape=jax.ShapeDtypeStruct((bs, n1, 2, n2, ch), F32),
        compiler_params=_cparams(("parallel", "parallel")), name="filter_dft_mid",
    )(a, w2)


FILT_ROWS = 512


def _filter_kernel(z_ref, win_ref, sel_ref, w1_ref, b1_ref, w2_ref, b2_ref, wo_ref, fr_ref, o_ref):
    fr = fr_ref[...]
    h = jnp.sin(fr * (_mm_3x(z_ref[...], w1_ref[...]) + b1_ref[...]))
    for i in range(B_INNER_MLPS):
        h = jnp.sin(fr * (_mm_3x(h, w2_ref[i]) + b2_ref[i]))
    ho = _mm_3x(h, wo_ref[...])
    fwd = sel_ref[...] > 0.5
    win = win_ref[...]
    for o in range(2):
        base = o * 2 * B_CH
        o_ref[o] = jnp.where(fwd, ho[:, base:base + B_CH], ho[:, base + B_CH:base + 2 * B_CH]) * win


def _hyena_filter_time(l, w1, b1, w2, b2, w_out, freq):
    bands = (B_EMB - 1) // 2
    circ = jnp.arange(2 * l)
    pos = jnp.where(circ < l, circ, 2 * l - circ)
    pos = jnp.where(circ == l, 0, pos)
    t = (jnp.linspace(0.0, 1.0, l, dtype=F32)[pos])[:, None]
    ang = 2.0 * math.pi * pos.astype(F32)[:, None] / l
    fr = jnp.linspace(1e-4, bands - 1, bands, dtype=F32)[None, :]
    z = jnp.concatenate([t, jnp.cos(fr * ang), -jnp.sin(fr * ang)], axis=-1)
    z = jnp.pad(z, ((0, 0), (0, 128 - B_EMB)))
    max_decay = math.log(B_DECAY_TARGET) / B_DECAY_SHORT_PCT
    min_decay = math.log(B_DECAY_TARGET) / B_DECAY_LONG_PCT
    deltas = jnp.abs(jnp.linspace(min_decay, max_decay, B_CH, dtype=F32))
    win = (jnp.exp(-t * deltas) + B_WINDOW_SHIFT) * (circ != l).astype(F32)[:, None]
    sel = (circ < l).astype(F32)[:, None]
    depth = w1.shape[0]
    w1p = jnp.pad(w1, ((0, 0), (0, 128 - B_EMB), (0, 0)))
    rows = min(FILT_ROWS, 2 * l)
    per_layer = lambda a: pl.BlockSpec((None,) + a.shape[1:], lambda li, i: (li,) + (0,) * (a.ndim - 1))
    params = (w1p, b1.reshape(depth, 1, B_FFN), w2, b2.reshape(depth, B_INNER_MLPS, 1, B_FFN), w_out,
              freq.reshape(depth, 1, B_FFN))
    return pl.pallas_call(
        _filter_kernel,
        grid=(depth, 2 * l // rows),
        in_specs=[pl.BlockSpec((rows, 128), lambda li, i: (i, 0)),
                  pl.BlockSpec((rows, B_CH), lambda li, i: (i, 0)),
                  pl.BlockSpec((rows, 1), lambda li, i: (i, 0))] + [per_layer(p) for p in params],
        out_specs=pl.BlockSpec((None, 2, rows, B_CH), lambda li, i: (li, 0, i, 0)),
        out_shape=jax.ShapeDtypeStruct((depth, 2, 2 * l, B_CH), F32),
        compiler_params=_cparams(("parallel", "parallel")), name="hyena_filter",
    )(z, win, sel, *params)


def _hyena_spectrum(l, consts, w1, b1, w2m, b2, w_out, freq):
    f1_full, _, w2, _, _ = consts
    n2 = 2 * l // FFT_N1
    depth = w1.shape[0]
    g_time = _hyena_filter_time(l, w1, b1, w2m, b2, w_out, freq)
    f1_rows = np.concatenate([f1_full[:FFT_KH], f1_full[FFT_N1:FFT_N1 + FFT_KH]], axis=0)
    a = _left_mm(jnp.asarray(f1_rows, F32), g_time.reshape(depth * 2, FFT_N1, n2 * B_CH))
    g = _filt_mid_stage(a.reshape(depth * 2, 2, FFT_KH, n2, B_CH), jnp.asarray(w2[:FFT_KH], F32))
    return g.reshape(depth, 2, FFT_KH, 2, n2, B_CH)


HY_SLAB = 128


def _hy_pitch(n2c):
    return n2c + 8 if (n2c // 8) % 2 == 0 else n2c + 16


HY_KG = 20
HY_JG = 16


def _hyena_lat_kernel(zv_ref, z1_ref, z2_ref, cw_ref, cb_ref, g_ref, bias_ref, f1_ref, w2_ref, v2_ref, f3_ref,
                      o_ref, useq, ur, a_re, a_im, cbuf, *, col_mode, seq):
    n1h = FFT_N1 // 2
    n2c = seq // n1h
    pitch = _hy_pitch(n2c)
    grows = seq // GRID_W
    row_id = lax.broadcasted_iota(jnp.int32, (seq, HY_SLAB), 0)

    def short_conv(z_ref, k):
        z = z_ref[...]
        w = cw_ref[k]
        if col_mode:
            g0, gl = z[:GRID_W], z[seq - GRID_W:]
            cc = row_id[:GRID_W]
            wrap_p = jnp.where(cc == 0, 0.0, pltpu.roll(gl, 1, axis=0))
            wrap_n = jnp.where(cc == GRID_W - 1, 0.0, pltpu.roll(g0, GRID_W - 1, axis=0))
            prev = jnp.concatenate([wrap_p, z[:seq - GRID_W]], axis=0)
            nxt = jnp.concatenate([z[GRID_W:], wrap_n], axis=0)
        else:
            prev = jnp.where(row_id == 0, 0.0, pltpu.roll(z, 1, axis=0))
            nxt = jnp.where(row_id == seq - 1, 0.0, pltpu.roll(z, seq - 1, axis=0))
        return prev * w[0:1] + z * w[1:2] + nxt * w[2:3] + cb_ref[k]

    def seq_start(j):
        return (GRID_W * lax.rem(j, grows) + j // grows) if col_mode else j

    seq_stride = 2 if col_mode else pitch
    seq_ref = ur if col_mode else useq

    def put_seq(val):
        if col_mode:
            ur[...] = val
        else:
            for n1 in range(n1h):
                useq[n1 * pitch:n1 * pitch + n2c, :] = val[n1 * n2c:(n1 + 1) * n2c]

    def conv_out():
        if col_mode:
            return cbuf[...]
        return jnp.concatenate([cbuf[n1 * pitch:n1 * pitch + n2c, :] for n1 in range(n1h)], axis=0)

    y_prev = short_conv(zv_ref, 0)
    x_next = (short_conv(z1_ref, 1), short_conv(z2_ref, 2))
    put_seq(y_prev)
    f1 = f1_ref[...]
    f3 = f3_ref[...]
    jgrp = min(HY_JG, n2c)
    for order in range(2):
        def stage1(jg, carry):
            js = [jg * jgrp + jj for jj in range(jgrp)]
            xs = [seq_ref[pl.ds(seq_start(j), n1h, stride=seq_stride), :] for j in js]
            outs = [_mm(f1, x) for x in xs]
            for j, a in zip(js, outs):
                a_re[pl.ds(j, FFT_KH, stride=pitch), :] = a[:FFT_KH]
                a_im[pl.ds(j, FFT_KH, stride=pitch), :] = a[FFT_KH:]
            return carry
        lax.fori_loop(0, n2c // jgrp, stage1, 0)

        def mid(kg, carry):
            k1s = [kg * HY_KG + kk for kk in range(HY_KG)]
            offs = [pl.multiple_of(k1 * pitch, 8) for k1 in k1s]
            xin = [jnp.concatenate([a_re[pl.ds(o, n2c), :], a_im[pl.ds(o, n2c), :]], axis=0) for o in offs]
            xf = [_mm(w2_ref[k1], a) for k1, a in zip(k1s, xin)]
            ys = []
            for k1, x in zip(k1s, xf):
                xr, xi = x[:n2c], x[n2c:]
                gr, gi = g_ref[order, k1, 0], g_ref[order, k1, 1]
                ys.append(jnp.concatenate([xr * gr - xi * gi, xr * gi + xi * gr], axis=0))
            bm = [_mm(v2_ref[k1], y) for k1, y in zip(k1s, ys)]
            for o, b in zip(offs, bm):
                a_re[pl.ds(o, n2c), :] = b[:n2c]
                a_im[pl.ds(o, n2c), :] = b[n2c:]
            return carry
        lax.fori_loop(0, FFT_KH // HY_KG, mid, 0)

        def stage3(jg, carry):
            js = [jg * jgrp + jj for jj in range(jgrp)]
            bs = [jnp.concatenate([a_re[pl.ds(j, FFT_KH, stride=pitch), :],
                                   a_im[pl.ds(j, FFT_KH, stride=pitch), :]], axis=0) for j in js]
            outs = [_mm(f3, b) for b in bs]
            for j, y in zip(js, outs):
                cbuf[pl.ds(seq_start(j), n1h, stride=seq_stride), :] = y
            return carry
        lax.fori_loop(0, n2c // jgrp, stage3, 0)

        y_prev = x_next[order] * (conv_out() + y_prev * bias_ref[order:order + 1, :])
        if order == 0:
            put_seq(y_prev)
    o_ref[...] = y_prev


def _hyena_fused(lay, z, conv_w, conv_b, spec, bias, consts, l, seq, row0, s0, nslab, col_mode):
    f1, w2, v2, f3 = _half_spectrum_consts(consts)
    n2c = seq // (FFT_N1 // 2)
    assert FFT_KH % HY_KG == 0 and n2c % min(HY_JG, n2c) == 0
    pitch = _hy_pitch(n2c)
    cps = B_CH // HY_SLAB
    zspec = lambda k: pl.BlockSpec((seq, HY_SLAB), lambda j, b, k=k: (row0 + b, k * cps + s0 + j))
    once = lambda a: pl.BlockSpec(a.shape, lambda j, b: (0,) * a.ndim, pipeline_mode=pl.Buffered(1))
    mats = tuple(jnp.asarray(m, F32) for m in (f1, w2, v2, f3))
    seq_rows = (FFT_N1 // 2) * pitch
    kern = functools.partial(_hyena_lat_kernel, col_mode=col_mode, seq=seq)
    in_specs = [zspec(0), zspec(1), zspec(2),
                pl.BlockSpec((3, B_SHORT, HY_SLAB), lambda j, b: (0, 0, s0 + j)),
                pl.BlockSpec((3, 1, HY_SLAB), lambda j, b: (0, 0, s0 + j)),
                pl.BlockSpec((None, 2, FFT_KH, 2, n2c, HY_SLAB), lambda j, b: (l, 0, 0, 0, 0, s0 + j),
                             pipeline_mode=pl.Buffered(1)),
                pl.BlockSpec((2, HY_SLAB), lambda j, b: (0, s0 + j))] + [once(m) for m in mats]
    args = [z, z, z, jnp.transpose(conv_w, (1, 0, 2)), conv_b.reshape(3, 1, B_CH), spec, bias, *mats]
    return pl.pallas_call(
        kern,
        grid=(nslab, lay.b),
        in_specs=in_specs,
        out_specs=pl.BlockSpec((seq, HY_SLAB), lambda j, b: (b, j)),
        out_shape=jax.ShapeDtypeStruct((lay.b * seq, nslab * HY_SLAB), F32),
        scratch_shapes=[pltpu.VMEM((seq_rows, HY_SLAB), F32), pltpu.VMEM((seq, HY_SLAB), F32),
                        pltpu.VMEM((FFT_KH * pitch, HY_SLAB), F32), pltpu.VMEM((FFT_KH * pitch, HY_SLAB), F32),
                        pltpu.VMEM((seq if col_mode else seq_rows, HY_SLAB), F32)],
        compiler_params=_cparams(("parallel", "parallel")),
        name="hyena_col" if col_mode else "hyena_row",
    )(*args)


def _hyena(lay, z, conv_w, conv_b, spec_c, spec_x, bias, consts_c, consts_x, l, need_ctx):
    assert lay.n_ctx % lay.seq == 0
    half = B_CH // 2 // HY_SLAB
    args = (lay, z, conv_w, conv_b)
    lat0 = lay.n_ctx // lay.seq
    yc = _hyena_fused(*args, spec_c, bias, consts_c, l, lay.ctx, 0, 0, 2 * half, False) if need_ctx else None
    yr = _hyena_fused(*args, spec_x, bias, consts_x, l, lay.seq, lat0, 0, half, False)
    ycol = _hyena_fused(*args, spec_x, bias, consts_x, l, lay.seq, lat0, half, half, True)
    return yc, yr, ycol


def _head_norm_gate(o, gate, norm_w, heads, width):
    outs = []
    for h in range(heads):
        oh = o[:, h * width:(h + 1) * width]
        y = oh * lax.rsqrt(jnp.mean(oh * oh, axis=-1, keepdims=True) + NORM_EPS) * norm_w
        outs.append(y * _silu(gate[:, h * width:(h + 1) * width]))
    return jnp.concatenate(outs, axis=1)


def _merge_kernel(x_ref, mod_ref, nw_ref, a0_ref, a1_ref, ag_ref, c0_ref, c1_ref, cx_ref, cz_ref,
                  d0_ref, d1_ref, dg_ref, an_ref, cd_ref, cn_ref, dn_ref, wg_ref, wb_ref, wo_ref,
                  br_ref, bcol_ref, *rest, ctx_tiles, t0):
    o_ref = rest[-1]
    yb = jnp.concatenate([br_ref[...], bcol_ref[...]], axis=1)
    if len(rest) == 2:
        yb = jnp.where(pl.program_id(0) + t0 < ctx_tiles, rest[0][...], yb)
    xv = x_ref[...]
    m = mod_ref[...]
    hb = _prenorm(xv, nw_ref[1:2, :], m[3:4, :], m[4:5, :]).astype(BF16)
    ya = _head_norm_gate(a0_ref[...] + a1_ref[...], ag_ref[...], an_ref[...], A_HEADS, A_DV)
    yd = _head_norm_gate(d0_ref[...] + d1_ref[...], dg_ref[...], dn_ref[...], D_HEADS, D_DV)
    yc = (c0_ref[...] + c1_ref[...] + cd_ref[...] * cx_ref[...]) * _silu(cz_ref[...])
    gw = C_INNER // C_GROUPS
    cn = cn_ref[...]
    yc = jnp.concatenate(
        [yc[:, g * gw:(g + 1) * gw]
         * lax.rsqrt(jnp.mean(yc[:, g * gw:(g + 1) * gw] ** 2, axis=-1, keepdims=True) + NORM_EPS)
         * cn[:, g * gw:(g + 1) * gw] for g in range(C_GROUPS)], axis=1)
    acc = jnp.zeros(xv.shape, F32)
    for k, y in enumerate((ya, yb, yc, yd)):
        gate = jax.nn.sigmoid(jnp.dot(hb, wg_ref[:, k * D_MODEL:(k + 1) * D_MODEL], preferred_element_type=F32))
        acc = acc + gate * jnp.dot(y.astype(BF16), wb_ref[k], preferred_element_type=F32)
    o_ref[...] = xv + m[5:6, :] * jnp.dot(acc.astype(BF16), wo_ref[...], preferred_element_type=F32)


def _merge(lay, x, mods, norm_w, a0, a1, ag, yb, c0, c1, cx, cz, d0, d1, dg, an, cd, cn, dn, wg, wb, wo, l):
    n, d = x.shape
    yb_ctx, yb_row, yb_col = yb
    ctx_tiles = lay.n_ctx // TOK_TILE
    t0 = ctx_tiles if yb_ctx is None else 0
    row = lambda w: pl.BlockSpec((TOK_TILE, w), lambda i: (i + t0, 0))
    lat = lambda w: pl.BlockSpec((TOK_TILE, w), lambda i: (jnp.maximum(i + t0 - ctx_tiles, 0), 0))
    extra_specs, extra = [], []
    if yb_ctx is not None:
        extra_specs = [pl.BlockSpec((TOK_TILE, BRANCH_W), lambda i: (jnp.minimum(i, ctx_tiles - 1), 0))]
        extra = [yb_ctx]
    return pl.pallas_call(
        functools.partial(_merge_kernel, ctx_tiles=ctx_tiles, t0=t0),
        grid=(n // TOK_TILE - t0,),
        in_specs=[row(d), _mod_spec(lay, mods, l, TOK_TILE, t0), _of_layer(norm_w, l)]
                 + [row(BRANCH_W)] * 10 + [_of_layer(p, l) for p in (an, cd, cn, dn)]
                 + [_of_layer(w, l, once=True) for w in (wg, wb, wo)]
                 + [lat(BRANCH_W // 2), lat(BRANCH_W // 2)] + extra_specs,
        out_specs=row(d),
        out_shape=jax.ShapeDtypeStruct((n, d), F32),
        compiler_params=_cparams(("parallel",)), name="merge",
    )(x, mods, norm_w, a0, a1, ag, c0, c1, cx, cz, d0, d1, dg, an, cd, cn, dn, wg, wb, wo, yb_row, yb_col, *extra)


def kernel(x, c, ctx, c_ctx, w_ada, b_ada, norm_w, ffn_up, ffn_down, w_in, gdn_conv, gdn_a_log, gdn_dt_bias, gdn_norm, hy_conv_w, hy_conv_b, hy_w1, hy_b1, hy_w2, hy_b2, hy_wout, hy_freq, hy_bias, ssd_conv_w, ssd_conv_b, ssd_a_log, ssd_dt_bias, ssd_d, ssd_norm, gla_gk_w, gla_gk_b, gla_norm, w_branch, w_out, final_norm):
    b, seq, d = x.shape
    ctx_len = ctx.shape[1]
    depth = w_ada.shape[0]
    lay = _Layout(b, ctx_len, seq)
    consts_c = _dft_consts(2 * ctx_len // FFT_N1)
    consts_x = _dft_consts(2 * seq // FFT_N1)

    rp = -(-(1 + b) // 8) * 8
    cond = jnp.concatenate([c_ctx[None, :], c, jnp.zeros((rp - 1 - b, d), F32)], axis=0)
    mods = _ada(cond, w_ada, b_ada).reshape(depth, rp, N_MOD, d)

    ffn_up_b, ffn_down_b = ffn_up.astype(BF16), ffn_down.astype(BF16)
    w_r, w_gate = _rearrange_w_in(w_in)
    w_branch_b, w_out_b = w_branch.astype(BF16), w_out.astype(BF16)
    ssd_conv_b3 = ssd_conv_b.reshape(depth, 1, C_XBC)
    an = gdn_norm.reshape(depth, 1, A_DV)
    cd = jnp.repeat(ssd_d, C_HEADDIM, axis=-1).reshape(depth, 1, C_INNER)
    cn = ssd_norm.reshape(depth, 1, C_INNER)
    dn = gla_norm.reshape(depth, 1, D_DV)

    fargs = (hy_w1, hy_b1, hy_w2, hy_b2, hy_wout, hy_freq)
    spec_c = _hyena_spectrum(ctx_len, consts_c, *fargs)
    spec_x = _hyena_spectrum(seq, consts_x, *fargs)

    xf = jnp.concatenate([ctx.reshape(b * ctx_len, d), x.reshape(b * seq, d)], axis=0)
    for l in range(depth):
        xf = _ffn(lay, xf, mods, norm_w, ffn_up_b, ffn_down_b, l, 0)

        qkv, a_gate, zb, c_z, xbc, d_qkv, d_gate, small = _inproj(
            lay, xf, mods, norm_w, w_r, gdn_conv, ssd_conv_w, ssd_conv_b3, l)
        (a0, a1), (c0, c1), (d0, d1) = _scan_call(lay, [
            _gdn_scan(lay, qkv, small, gdn_a_log[l], gdn_dt_bias[l]),
            _ssd_scan(lay, xbc, small, ssd_a_log[l], ssd_dt_bias[l]),
            _gla_scan(lay, d_qkv, small, gla_gk_w[l], gla_gk_b[l])])
        last = l == depth - 1
        yb = _hyena(lay, zb, hy_conv_w[l], hy_conv_b[l], spec_c, spec_x, hy_bias[l], consts_c, consts_x, l,
                    need_ctx=not last)

        xf = _merge(lay, xf, mods, norm_w, a0, a1, a_gate, yb, c0, c1, xbc, c_z, d0, d1, d_gate,
                    an, cd, cn, dn, w_gate, w_branch_b, w_out_b, l)
        xf = _ffn(lay, xf, mods, norm_w, ffn_up_b, ffn_down_b, l, 2, final_w=final_norm if last else None)
    return xf.reshape(b, seq, d)
```

```python
import functools
import itertools
import math

import numpy as np
import jax
import jax.numpy as jnp
from jax import lax
from jax.experimental import pallas as pl
from jax.experimental.pallas import tpu as pltpu

F32 = jnp.float32
BF16 = jnp.bfloat16
HI = lax.Precision.HIGHEST

D_MODEL = 1024
GRID_W = 64
CHUNK = 64
NORM_EPS = 1e-6
N_MOD = 9
D_FF = 2816
SHORT_CONV = 5

A_HEADS, A_DK, A_DV = 4, 128, 128
A_QKV = A_HEADS * (2 * A_DK + A_DV)
B_CH, B_SHORT, B_EMB, B_FFN, B_INNER_MLPS = 512, 3, 33, 64, 2
B_WINDOW_SHIFT, B_DECAY_SHORT_PCT, B_DECAY_LONG_PCT, B_DECAY_TARGET = 0.05, 0.3, 1.5, 1e-2
C_HEADS, C_HEADDIM, C_GROUPS, C_STATE = 8, 64, 2, 64
C_INNER = C_HEADS * C_HEADDIM
C_XBC = C_INNER + 2 * C_GROUPS * C_STATE
D_HEADS, D_DK, D_DV, D_RANK = 4, 64, 128, 16
D_GATE_NORM = 16.0
N_BRANCH, BRANCH_W = 4, 512

IN_SIZES = (A_QKV, A_HEADS * A_DV, 2 * A_HEADS, 2 * A_HEADS, 3 * B_CH, C_INNER, C_XBC, 2 * C_HEADS,
            D_HEADS * (2 * D_DK + D_DV), D_HEADS * D_DV, 2 * D_RANK, N_BRANCH * D_MODEL)
IN_OFFS = tuple(int(v) for v in np.cumsum((0,) + IN_SIZES))

ROW_TILE = 256
TOK_TILE = 512
SMALL_W = 128
VMEM_LIMIT = 56 * 1024 * 1024


def _cparams(sem):
    return pltpu.CompilerParams(dimension_semantics=sem, vmem_limit_bytes=VMEM_LIMIT)


def _mm(a, b):
    return jnp.dot(a.astype(BF16), b.astype(BF16), preferred_element_type=F32)


def _mm_nt(a, b):
    return lax.dot_general(a.astype(BF16), b.astype(BF16), (((1,), (1,)), ((), ())),
                           preferred_element_type=F32)


def _mm_hi(a, b):
    return jnp.dot(a, b, precision=HI, preferred_element_type=F32)


def _bf16_parts(v, parts):
    out, rest = [], v
    for _ in range(parts):
        hi = rest.astype(BF16)
        out.append(hi)
        rest = rest - hi.astype(F32)
    return out


def _mm_3x(a, b):
    ah, al = _bf16_parts(a, 2)
    bh, bl = _bf16_parts(b, 2)
    dot = functools.partial(jnp.dot, preferred_element_type=F32)
    return dot(ah, bh) + (dot(ah, bl) + dot(al, bh))


def _mm_sel_l(sel, v, parts=3):
    sb = sel.astype(BF16)
    return sum(jnp.dot(sb, p, preferred_element_type=F32) for p in _bf16_parts(v, parts))


def _mm_sel_r(v, sel, parts=3):
    sb = sel.astype(BF16)
    return sum(jnp.dot(p, sb, preferred_element_type=F32) for p in _bf16_parts(v, parts))


def _silu(v):
    return v * jax.nn.sigmoid(v)


def _softplus(v):
    return jnp.maximum(v, 0.0) + jnp.log1p(jnp.exp(-jnp.abs(v)))


def _log_sigmoid(v):
    return jnp.minimum(v, 0.0) - jnp.log1p(jnp.exp(-jnp.abs(v)))


def _prenorm(xv, gain, shift, scale):
    ms = jnp.mean(xv * xv, axis=-1, keepdims=True)
    return (xv * lax.rsqrt(ms + NORM_EPS) * gain) * (1.0 + scale) + shift


def _full(shape):
    nd = len(shape)
    return pl.BlockSpec(shape, lambda *_: (0,) * nd)


class _Layout:
    def __init__(self, batch, ctx_len, seq):
        assert ctx_len % ROW_TILE == 0 and seq % ROW_TILE == 0
        assert (batch * ctx_len) % TOK_TILE == 0 and seq % TOK_TILE == 0
        self.b, self.ctx, self.seq = batch, ctx_len, seq
        self.n_ctx = batch * ctx_len
        self.n = batch * (ctx_len + seq)
        self.tc, self.tl = ctx_len // ROW_TILE, seq // ROW_TILE
        self.nct = batch * self.tc
        self.tiles = self.n // ROW_TILE

    def mod_index(self, i, tile=TOK_TILE):
        nct = self.n_ctx // tile
        return jnp.where(i < nct, 0, 1 + (i - nct) // (self.seq // tile))

    def seg_first(self, i):
        return jnp.where(i < self.nct, lax.rem(i, self.tc) == 0, lax.rem(i - self.nct, self.tl) == 0)

    def seg_last(self, i):
        return jnp.where(i < self.nct, lax.rem(i, self.tc) == self.tc - 1,
                         lax.rem(i - self.nct, self.tl) == self.tl - 1)

    def fwd_tile(self, b, s):
        return jnp.where(s < self.tc, b * self.tc + s, self.nct + b * self.tl + (s - self.tc))

    def rev_tile(self, b, s):
        return jnp.where(s < self.tc, b * self.tc + (self.tc - 1 - s),
                         self.nct + b * self.tl + (self.tl - 1 - (s - self.tc)))


def _ada_kernel(c_ref, w_ref, b_ref, o_ref):
    o_ref[...] = _mm_hi(_silu(c_ref[...]), w_ref[...]) + b_ref[...]


def _ada(cond, w_ada, b_ada):
    depth, d, nm = w_ada.shape
    rp = cond.shape[0]
    tn = 1152
    return pl.pallas_call(
        _ada_kernel,
        grid=(depth, nm // tn),
        in_specs=[_full((rp, d)),
                  pl.BlockSpec((None, d, tn), lambda l, j: (l, 0, j)),
                  pl.BlockSpec((None, 1, tn), lambda l, j: (l, 0, j))],
        out_specs=pl.BlockSpec((None, rp, tn), lambda l, j: (l, 0, j)),
        out_shape=jax.ShapeDtypeStruct((depth, rp, nm), F32),
        compiler_params=_cparams(("parallel", "parallel")), name="ada",
    )(cond, w_ada, b_ada.reshape(depth, 1, nm))


FF_CHUNK = 256


def _ffn_kernel(x_ref, mod_ref, nw_ref, wup_ref, wdn_ref, *rest, sub):
    o_ref = rest[-1]
    xv = x_ref[...]
    m = mod_ref[...]
    h = _prenorm(xv, nw_ref[sub:sub + 1, :], m[3 * sub:3 * sub + 1, :], m[3 * sub + 1:3 * sub + 2, :])
    hb = h.astype(BF16)
    acc = jnp.zeros(xv.shape, F32)
    for c in range(D_FF // FF_CHUNK):
        lo = c * FF_CHUNK
        a = jnp.dot(hb, wup_ref[:, lo:lo + FF_CHUNK], preferred_element_type=F32)
        g = jnp.dot(hb, wup_ref[:, D_FF + lo:D_FF + lo + FF_CHUNK], preferred_element_type=F32)
        acc = acc + jnp.dot((_silu(a) * g).astype(BF16), wdn_ref[lo:lo + FF_CHUNK, :],
                            preferred_element_type=F32)
    out = xv + 0.5 * m[3 * sub + 2:3 * sub + 3, :] * acc
    if len(rest) == 2:
        out = out * lax.rsqrt(jnp.mean(out * out, axis=-1, keepdims=True) + NORM_EPS) * rest[0][...]
    o_ref[...] = out


def _of_layer(a, l, *lead, once=False):
    idx = (l,) + lead
    block = (None,) * len(idx) + a.shape[len(idx):]
    mode = dict(pipeline_mode=pl.Buffered(1)) if once else {}
    return pl.BlockSpec(block, lambda *_: idx + (0,) * (a.ndim - len(idx)), **mode)


def _mod_spec(lay, mods, l, tile, t0=0):
    return pl.BlockSpec((None, None, N_MOD, mods.shape[-1]), lambda i: (l, lay.mod_index(i + t0, tile), 0, 0))


def _ffn(lay, x, mods, norm_w, wup, wdn, l, sub, final_w=None):
    n, d = x.shape
    t0 = 0 if final_w is None else lay.n_ctx // TOK_TILE
    extra = [] if final_w is None else [final_w.reshape(1, d)]
    return pl.pallas_call(
        functools.partial(_ffn_kernel, sub=sub),
        grid=(n // TOK_TILE - t0,),
        in_specs=[pl.BlockSpec((TOK_TILE, d), lambda i: (i + t0, 0)),
                  _mod_spec(lay, mods, l, TOK_TILE, t0),
                  _of_layer(norm_w, l),
                  _of_layer(wup, l, sub // 2, once=True),
                  _of_layer(wdn, l, sub // 2, once=True)]
                 + [_full(e.shape) for e in extra],
        out_specs=pl.BlockSpec((TOK_TILE, d), lambda i: (i, 0)),
        out_shape=jax.ShapeDtypeStruct((n - t0 * TOK_TILE, d), F32),
        compiler_params=_cparams(("parallel",)), name="ffn",
    )(x, mods, norm_w, wup, wdn, *extra)


PROJ_W = (A_QKV, A_HEADS * A_DV, 3 * B_CH, C_INNER, C_XBC, D_HEADS * (2 * D_DK + D_DV), D_HEADS * D_DV, SMALL_W)
PROJ_O = tuple(int(v) for v in np.cumsum((0,) + PROJ_W))


def _rearrange_w_in(w_in):
    o = IN_OFFS
    col = lambda a, b: w_in[..., o[a]:o[b]]
    small = jnp.concatenate([col(2, 4), col(7, 8), col(10, 11),
                             jnp.zeros(w_in.shape[:-1] + (SMALL_W - 64,), w_in.dtype)], axis=-1)
    w_r = jnp.concatenate([col(0, 1), col(1, 2), col(4, 5), col(5, 6), col(6, 7), col(8, 9), col(9, 10), small],
                          axis=-1)
    return w_r.astype(BF16), col(11, 12).astype(BF16)


PROJ_HALO = 16
PROJ_CONV = (0, 4)


def _inproj_kernel(xp_ref, x_ref, xn_ref, mod_ref, nw_ref, w_ref, cwa_ref, cwc_ref, cbc_ref, *rest, lay):
    o_refs, zbuf = rest[:-1], rest[-1]
    i = pl.program_id(0)
    m = mod_ref[...]
    norm = lambda r: _prenorm(r[...], nw_ref[1:2, :], m[3:4, :], m[4:5, :]).astype(BF16)
    hb = norm(x_ref)
    hb_all = jnp.concatenate([norm(xp_ref), hb, norm(xn_ref)], axis=0)
    keep_p = jnp.where(lay.seg_first(i), 0.0, 1.0)
    keep_n = jnp.where(lay.seg_last(i), 0.0, 1.0)
    convs = {PROJ_CONV[0]: (cwa_ref, None), PROJ_CONV[1]: (cwc_ref, cbc_ref)}
    for k, o_ref in enumerate(o_refs):
        wk = w_ref[:, PROJ_O[k]:PROJ_O[k + 1]]
        if k not in convs:
            o_ref[...] = jnp.dot(hb, wk, preferred_element_type=F32)
            continue
        cw_ref, cb_ref = convs[k]
        width = PROJ_W[k]
        z = jnp.dot(hb_all, wk, preferred_element_type=F32)
        zbuf[0:PROJ_HALO, 0:width] = z[:PROJ_HALO] * keep_p
        zbuf[PROJ_HALO:PROJ_HALO + ROW_TILE, 0:width] = z[PROJ_HALO:PROJ_HALO + ROW_TILE]
        zbuf[PROJ_HALO + ROW_TILE:, 0:width] = z[PROJ_HALO + ROW_TILE:] * keep_n
        taps = cw_ref.shape[0]
        left = (taps - 1) // 2
        acc = None
        for t in range(taps):
            off = t - left
            term = zbuf[PROJ_HALO + off:PROJ_HALO + off + ROW_TILE, 0:width] * cw_ref[t:t + 1, :]
            acc = term if acc is None else acc + term
        if cb_ref is not None:
            acc = acc + cb_ref[...]
        o_ref[...] = _silu(acc)


def _inproj(lay, x, mods, norm_w, w_r, conv_a, conv_c, conv_c_bias, l):
    n, d = x.shape
    rh = ROW_TILE // PROJ_HALO
    lasth = n // PROJ_HALO - 1
    return pl.pallas_call(
        functools.partial(_inproj_kernel, lay=lay),
        grid=(n // ROW_TILE,),
        in_specs=[pl.BlockSpec((PROJ_HALO, d), lambda i: (jnp.maximum(i * rh - 1, 0), 0)),
                  pl.BlockSpec((ROW_TILE, d), lambda i: (i, 0)),
                  pl.BlockSpec((PROJ_HALO, d), lambda i: (jnp.minimum((i + 1) * rh, lasth), 0)),
                  _mod_spec(lay, mods, l, ROW_TILE),
                  _of_layer(norm_w, l),
                  _of_layer(w_r, l, once=True),
                  _of_layer(conv_a, l), _of_layer(conv_c, l), _of_layer(conv_c_bias, l)],
        out_specs=[pl.BlockSpec((ROW_TILE, w), lambda i: (i, 0)) for w in PROJ_W],
        out_shape=[jax.ShapeDtypeStruct((n, w), F32) for w in PROJ_W],
        scratch_shapes=[pltpu.VMEM((ROW_TILE + 2 * PROJ_HALO, max(PROJ_W[k] for k in PROJ_CONV)), F32)],
        compiler_params=_cparams(("parallel",)), name="inproj",
    )(x, x, x, mods, norm_w, w_r, conv_a, conv_c, conv_c_bias)


def _tri_masks(rev):
    ii = lax.broadcasted_iota(jnp.int32, (CHUNK, CHUNK), 0)
    jj = lax.broadcasted_iota(jnp.int32, (CHUNK, CHUNK), 1)
    return (ii <= jj, ii < jj) if rev else (ii >= jj, ii > jj)


def _masked_decay(col, row, incl):
    return jnp.where(incl, jnp.exp(jnp.where(incl, col - row, 0.0)), 0.0)


def _scan_call(lay, parts):
    steps = lay.tc + lay.tl
    specs, args, out_specs, out_shape, scratch, counts = [], [], [], [], [], []
    for _, ins_tiled, ins_full, out_w, scr in parts:
        n0 = len(specs)
        for walk in (lay.fwd_tile, lay.rev_tile):
            for a in ins_tiled:
                specs.append(pl.BlockSpec((ROW_TILE, a.shape[1]), lambda b, s, walk=walk: (walk(b, s), 0)))
                args.append(a)
            out_specs.append(pl.BlockSpec((ROW_TILE, out_w), lambda b, s, walk=walk: (walk(b, s), 0)))
            out_shape.append(jax.ShapeDtypeStruct((lay.n, out_w), F32))
        for a in ins_full:
            specs.append(_full(a.shape))
            args.append(a)
        counts.append((len(specs) - n0, len(scr)))
        scratch += list(scr)

    def body(*refs):
        n_in, n_out = len(specs), len(out_specs)
        i, s = 0, n_in + n_out
        gens = []
        for p, (kern, *_) in enumerate(parts):
            ci, cs = counts[p]
            gens.append(kern(*refs[i:i + ci], *refs[n_in + 2 * p:n_in + 2 * p + 2], *refs[s:s + cs]))
            i, s = i + ci, s + cs
        lead, rest = gens[0], itertools.chain(*gens[1:])
        live = [lead, rest]
        while live:
            live = [g for g in live if next(g, StopIteration) is not StopIteration]

    res = pl.pallas_call(
        body,
        grid=(lay.b, steps),
        in_specs=specs,
        out_specs=out_specs,
        out_shape=out_shape,
        scratch_shapes=scratch,
        compiler_params=_cparams(("arbitrary", "arbitrary")), name="scans",
    )(*args)
    return [tuple(res[2 * p:2 * p + 2]) for p in range(len(parts))]


def _ssd_kernel(xf_ref, sf_ref, xr_ref, sr_ref, alog_ref, dtb_ref, alog_t_ref, dtb_t_ref, exp_ref,
                of_ref, or_ref, st_ref):
    @pl.when(pl.program_id(1) == 0)
    def _():
        st_ref[...] = jnp.zeros(st_ref.shape, F32)

    hpg = C_HEADS // C_GROUPS
    gw = hpg * C_HEADDIM
    nch = ROW_TILE // CHUNK
    refs = ((xf_ref, sf_ref, of_ref), (xr_ref, sr_ref, or_ref))
    units, pre = [], {}
    for d, (x_ref, s_ref, _) in enumerate(refs):
        incl, _ = _tri_masks(d == 1)
        m_col = incl.astype(F32)
        m_row = m_col.T
        sm = s_ref[...]
        sm_t = sm.T
        dt_all = _softplus(sm[:, 16:32] + dtb_ref[...])
        a_all = -jnp.exp(alog_ref[...]) * dt_all
        a_t_all = -jnp.exp(alog_t_ref[...]) * _softplus(sm_t[16:32, :] + dtb_t_ref[...])
        bm_t = x_ref[:, C_INNER:C_INNER + C_GROUPS * C_STATE].T
        for c in range(nch):
            rows = slice(c * CHUNK, (c + 1) * CHUNK)
            units.append((d, c))
            pre[d, c] = dict(incl=incl, m_col=m_col, m_row=m_row, a_ch=a_all[rows], a_t=a_t_all[:, rows],
                             dt=dt_all[rows], bm_t=bm_t[:, rows], e=exp_ref[d], xs=x_ref[rows, 0:C_INNER],
                             bm=x_ref[rows, C_INNER:C_INNER + C_GROUPS * C_STATE],
                             cm=x_ref[rows, C_INNER + C_GROUPS * C_STATE:C_XBC])
    for u in units:
        w = pre[u]
        w["ac"] = _mm_sel_l(w["m_col"], w["a_ch"])
        w["ac_t"] = _mm_sel_r(w["a_t"], w["m_row"])
        w["alast"] = jnp.sum(w["a_ch"], axis=0, keepdims=True)
    yield
    for u in units:
        w = pre[u]
        w["xdt"] = w["xs"] * _mm_sel_r(w["dt"], w["e"])
        w["eac_e"] = _mm_sel_r(jnp.exp(w["ac"]), w["e"])
        w["xd"] = w["xdt"] * _mm_sel_r(jnp.exp(w["alast"] - w["ac"]), w["e"])
        w["dle"] = _mm_sel_r(jnp.broadcast_to(jnp.exp(w["alast"]), (8, 2 * C_HEADS)), w["e"])[0:1]
    yield
    grp = lambda a, g: a[:, g * C_STATE:(g + 1) * C_STATE]
    cb = {(u, g): _mm_nt(grp(pre[u]["cm"], g), grp(pre[u]["bm"], g)) for u in units for g in range(C_GROUPS)}
    upd = {(u, g): _mm(pre[u]["bm_t"][g * C_STATE:(g + 1) * C_STATE], pre[u]["xd"][:, g * gw:(g + 1) * gw])
           for u in units for g in range(C_GROUPS)}
    yield
    y_diag = {}
    for u in units:
        w = pre[u]
        for h in range(C_HEADS):
            col = u[0] * C_HEADS + h
            seg = _masked_decay(w["ac"][:, col:col + 1], w["ac_t"][col:col + 1, :], w["incl"])
            y_diag[u, h] = _mm(cb[u, h // hpg] * seg, w["xdt"][:, h * C_HEADDIM:(h + 1) * C_HEADDIM])
    st = {(d, g): st_ref[d, g] for d in range(2) for g in range(C_GROUPS)}
    entry = {}
    for p in range(nch):
        for d in range(2):
            u = (d, (nch - 1 - p) if d == 1 else p)
            for g in range(C_GROUPS):
                entry[u, g] = st[d, g]
                st[d, g] = pre[u]["dle"][:, g * gw:(g + 1) * gw] * st[d, g] + upd[u, g]
    for d in range(2):
        for g in range(C_GROUPS):
            st_ref[d, g] = st[d, g]
    yield
    for u in units:
        y_off = [_mm(grp(pre[u]["cm"], g), entry[u, g]) for g in range(C_GROUPS)]
        refs[u[0]][2][u[1] * CHUNK:(u[1] + 1) * CHUNK, :] = (
            jnp.concatenate([y_diag[u, h] for h in range(C_HEADS)], axis=1)
            + pre[u]["eac_e"] * jnp.concatenate(y_off, axis=1))


def _ssd_scan(lay, xbc, small, a_log, dt_bias):
    expand = np.zeros((2, 2 * C_HEADS, C_INNER), np.float32)
    for d in range(2):
        for h in range(C_HEADS):
            expand[d, d * C_HEADS + h, h * C_HEADDIM:(h + 1) * C_HEADDIM] = 1.0
    al = a_log.reshape(1, 2 * C_HEADS)
    db = dt_bias.reshape(1, 2 * C_HEADS)
    return (_ssd_kernel, (xbc, small), (al, db, al.T, db.T, jnp.asarray(expand)), C_INNER,
            [pltpu.VMEM((2, C_GROUPS, C_STATE, C_INNER // C_GROUPS), F32)])


def _gla_kernel(xf_ref, sf_ref, xr_ref, sr_ref, gkw_ref, gkb_ref, of_ref, or_ref, st_ref):
    @pl.when(pl.program_id(1) == 0)
    def _():
        st_ref[...] = jnp.zeros(st_ref.shape, F32)

    nk = D_HEADS * D_DK
    nch = ROW_TILE // CHUNK
    refs = ((xf_ref, sf_ref, of_ref), (xr_ref, sr_ref, or_ref))
    units, pre = [], {}
    for d, (x_ref, s_ref, _) in enumerate(refs):
        incl, _ = _tri_masks(d == 1)
        m_col = incl.astype(F32)
        lr = s_ref[:, 32 + D_RANK * d:32 + D_RANK * (d + 1)]
        g_all = _log_sigmoid(_mm_3x(lr, gkw_ref[d]) + gkb_ref[d:d + 1, :]) / D_GATE_NORM
        v_t = x_ref[:, 2 * nk:].T
        for c in range(nch):
            rows = slice(c * CHUNK, (c + 1) * CHUNK)
            g_ch = g_all[rows]
            gc = _mm_sel_l(m_col, g_ch)
            glast = jnp.sum(g_ch, axis=0, keepdims=True)
            q = x_ref[rows, 0:nk] * (D_DK ** -0.5)
            k = x_ref[rows, nk:2 * nk]
            v = x_ref[rows, 2 * nk:]
            kd = k * jnp.exp(glast - gc)
            qd = q * jnp.exp(gc)
            qr = q * jnp.exp(gc - glast)
            dlast = jnp.exp(glast)
            for h in range(D_HEADS):
                ks = slice(h * D_DK, (h + 1) * D_DK)
                vs = slice(h * D_DV, (h + 1) * D_DV)
                units.append((d, c, h))
                pre[d, c, h] = dict(incl=incl, qr=qr[:, ks], kd=kd[:, ks], qd=qd[:, ks], v=v[:, vs],
                                    v_t=v_t[vs, rows], dl=dlast[:, ks])
    yield
    aqk = [jnp.where(pre[u]["incl"], _mm_nt(pre[u]["qr"], pre[u]["kd"]), 0.0) for u in units]
    yield
    upd = {u: _mm(pre[u]["v_t"], pre[u]["kd"]) for u in units}
    yield
    intra = {u: _mm(a, pre[u]["v"]) for u, a in zip(units, aqk)}
    heads = [(d, h) for d in range(2) for h in range(D_HEADS)]
    st = {dh: st_ref[dh[0], dh[1]] for dh in heads}
    entry = {}
    for p in range(nch):
        for d, h in heads:
            u = (d, (nch - 1 - p) if d == 1 else p, h)
            entry[u] = st[d, h]
            st[d, h] = st[d, h] * pre[u]["dl"] + upd[u]
    for dh in heads:
        st_ref[dh[0], dh[1]] = st[dh]
    yield
    inter = {u: _mm_nt(pre[u]["qd"], entry[u]) for u in units}
    for d, (_, _, o_ref) in enumerate(refs):
        for c in range(nch):
            o_ref[c * CHUNK:(c + 1) * CHUNK, :] = jnp.concatenate(
                [intra[d, c, h] + inter[d, c, h] for h in range(D_HEADS)], axis=1)


def _gla_scan(lay, qkv, small, gk_w, gk_b):
    return (_gla_kernel, (qkv, small), (gk_w, gk_b), D_HEADS * D_DV, [pltpu.VMEM((2, D_HEADS, D_DV, D_DK), F32)])


TRI_BLOCK = 16
GDN_GROUP = 32


def _unit_lower_inverse(nms, eye, blk):
    dg = [jnp.where(blk, nm, 0.0) for nm in nms]
    off = [nm - d for nm, d in zip(nms, dg)]
    t0 = [eye - d for d in dg]
    p = [_mm(d, d) for d in dg]
    for it in range(3):
        t0 = [t + _mm(t, q) for t, q in zip(t0, p)]
        if it < 2:
            p = [_mm(q, q) for q in p]
    m = [_mm(t, o) for t, o in zip(t0, off)]
    m2 = [_mm(a, a) for a in m]
    r = [eye - a for a in m]
    r = [a + _mm(a, b) for a, b in zip(r, m2)]
    return [_mm(a, t) for a, t in zip(r, t0)]


def _gdn_kernel(xf_ref, sf_ref, xr_ref, sr_ref, alog_ref, dtb_ref, alog_t_ref, dtb_t_ref,
                of_ref, or_ref, st_ref):
    @pl.when(pl.program_id(1) == 0)
    def _():
        st_ref[...] = jnp.zeros(st_ref.shape, F32)

    nk = A_HEADS * A_DK
    nch = ROW_TILE // CHUNK
    ii = lax.broadcasted_iota(jnp.int32, (CHUNK, CHUNK), 0)
    jj = lax.broadcasted_iota(jnp.int32, (CHUNK, CHUNK), 1)
    eye = (ii == jj).astype(F32)
    blk = (ii // TRI_BLOCK) == (jj // TRI_BLOCK)
    refs = ((xf_ref, sf_ref, of_ref), (xr_ref, sr_ref, or_ref))

    units, pre = [], {}
    for d, (x_ref, s_ref, _) in enumerate(refs):
        incl, strict = _tri_masks(d == 1)
        m_col = incl.astype(F32)
        m_row = m_col.T
        sm = s_ref[...]
        sm_t = sm.T
        beta_all = jax.nn.sigmoid(sm[:, 0:8])
        g_all = -jnp.exp(alog_ref[...]) * _softplus(sm[:, 8:16] + dtb_ref[...])
        g_t_all = -jnp.exp(alog_t_ref[...]) * _softplus(sm_t[8:16, :] + dtb_t_ref[...])
        qn, kn = [], []
        for h in range(A_HEADS):
            qh = x_ref[:, h * A_DK:(h + 1) * A_DK]
            kh = x_ref[:, nk + h * A_DK:nk + (h + 1) * A_DK]
            qn.append(qh * lax.rsqrt(jnp.sum(qh * qh, axis=-1, keepdims=True) + NORM_EPS) * (A_DK ** -0.5))
            kn.append(kh * lax.rsqrt(jnp.sum(kh * kh, axis=-1, keepdims=True) + NORM_EPS))
        kn_t = [kh.T for kh in kn]
        for c in range(nch):
            rows = slice(c * CHUNK, (c + 1) * CHUNK)
            g_ch = g_all[rows]
            gc = _mm_sel_l(m_col, g_ch)
            gc_t = _mm_sel_r(g_t_all[:, rows], m_row)
            glast = jnp.sum(g_ch, axis=0, keepdims=True)
            for h in range(A_HEADS):
                col = d * A_HEADS + h
                gcc, gcr, gl = gc[:, col:col + 1], gc_t[col:col + 1, :], glast[:, col:col + 1]
                bh = beta_all[rows, col:col + 1]
                kh = kn[h][rows]
                units.append((d, c, h))
                pre[d, c, h] = dict(
                    dmask=_masked_decay(gcc, gcr, incl), strict=strict, egc=jnp.exp(gcc), kh=kh, kb=kh * bh,
                    qh=qn[h][rows], vb=x_ref[rows, 2 * nk + h * A_DV:2 * nk + (h + 1) * A_DV] * bh,
                    kg_t=kn_t[h][:, rows] * jnp.exp(gl - gcr), dl=jnp.exp(gl))
    for g0 in range(0, len(units), GDN_GROUP):
        us = units[g0:g0 + GDN_GROUP]
        kk = [_mm_nt(pre[u]["kb"], pre[u]["kh"]) for u in us]
        qk = [_mm_nt(pre[u]["qh"], pre[u]["kh"]) for u in us]
        tinv = _unit_lower_inverse(
            [jnp.where(pre[u]["strict"], a * pre[u]["dmask"], 0.0) for u, a in zip(us, kk)], eye, blk)
        uw = [_mm(t, jnp.concatenate([pre[u]["vb"], pre[u]["kb"] * pre[u]["egc"]], axis=1))
              for u, t in zip(us, tinv)]
        for u, a, b in zip(us, uw, qk):
            pre[u]["u"] = a[:, :A_DV]
            pre[u]["wq"] = jnp.concatenate([a[:, A_DV:], pre[u]["qh"] * pre[u]["egc"]], axis=0)
            pre[u]["aqk"] = b * pre[u]["dmask"]

    heads = [(d, h) for d in range(2) for h in range(A_HEADS)]
    st = {dh: st_ref[dh[0], dh[1]] for dh in heads}
    for p in range(nch):
        yield
        us = [(d, (nch - 1 - p) if d == 1 else p, h) for d, h in heads]
        ws = [_mm(pre[u]["wq"], st[u[0], u[2]]) for u in us]
        yield
        v_new = [pre[u]["u"] - w[:CHUNK] for u, w in zip(us, ws)]
        outs = [w[CHUNK:] + _mm(pre[u]["aqk"], v) for u, w, v in zip(us, ws, v_new)]
        for u, v in zip(us, v_new):
            st[u[0], u[2]] = pre[u]["dl"] * st[u[0], u[2]] + _mm(pre[u]["kg_t"], v)
        for d, (_, _, o_ref) in enumerate(refs):
            c = us[d * A_HEADS][1]
            o_ref[c * CHUNK:(c + 1) * CHUNK, :] = jnp.concatenate(outs[d * A_HEADS:(d + 1) * A_HEADS], axis=1)
    for dh in heads:
        st_ref[dh[0], dh[1]] = st[dh]


def _gdn_scan(lay, qkv, small, a_log, dt_bias):
    al = a_log.reshape(1, 2 * A_HEADS)
    db = dt_bias.reshape(1, 2 * A_HEADS)
    return (_gdn_kernel, (qkv, small), (al, db, al.T, db.T), A_HEADS * A_DV,
            [pltpu.VMEM((2, A_HEADS, A_DK, A_DV), F32)])


FFT_N1 = 64
FFT_KB = 8


def _dft_consts(n2):
    n1 = FFT_N1
    m = n1 * n2
    k1 = np.arange(n1, dtype=np.float64)
    ang1 = 2.0 * np.pi * np.outer(k1, k1) / n1
    f1_full = np.concatenate([np.cos(ang1), -np.sin(ang1)], axis=0)
    f1 = f1_full[:, :n1 // 2]
    f3 = np.concatenate([np.cos(ang1[:, :n1 // 2]).T, -np.sin(ang1[:, :n1 // 2]).T], axis=1) / m
    j = np.arange(n2, dtype=np.float64)
    theta = 2.0 * np.pi * (np.outer(j, j)[None] / n2 + (k1[:, None, None] * j[None, None, :]) / m)
    wr, wi = np.cos(theta), -np.sin(theta)
    w2 = np.concatenate([np.concatenate([wr, -wi], axis=2), np.concatenate([wi, wr], axis=2)], axis=1)
    phi = 2.0 * np.pi * (np.outer(j, j)[None] / n2 + (k1[:, None, None] * j[None, :, None]) / m)
    vr, vi = np.cos(phi), np.sin(phi)
    v2 = np.concatenate([np.concatenate([vr, -vi], axis=2), np.concatenate([vi, vr], axis=2)], axis=1)
    return f1_full, f1, w2, v2, f3


FFT_KV = FFT_N1 // 2 + 1
FFT_KH = 40


def _half_spectrum_consts(consts):
    _, f1, w2, v2, f3 = consts
    n1 = FFT_N1
    f1h = np.zeros((2 * FFT_KH, n1 // 2))
    f1h[:FFT_KV] = f1[:FFT_KV]
    f1h[FFT_KH:FFT_KH + FFT_KV] = f1[n1:n1 + FFT_KV]
    w2h = np.zeros((FFT_KH,) + w2.shape[1:])
    w2h[:FFT_KV] = w2[:FFT_KV]
    v2h = np.zeros((FFT_KH,) + v2.shape[1:])
    v2h[:FFT_KV] = v2[:FFT_KV]
    weight = np.full((FFT_KV,), 2.0)
    weight[0] = weight[-1] = 1.0
    f3h = np.zeros((n1 // 2, 2 * FFT_KH))
    f3h[:, :FFT_KV] = f3[:, :FFT_KV] * weight
    f3h[:, FFT_KH:FFT_KH + FFT_KV] = f3[:, n1:n1 + FFT_KV] * weight
    return f1h, w2h, v2h, f3h


def _left_mm_kernel(w_ref, x_ref, o_ref):
    o_ref[...] = _mm_3x(w_ref[...], x_ref[...])


def _left_mm(wm, x):
    bs, kdim, ncols = x.shape
    mdim = wm.shape[0]
    tn = min(ncols, 4096)
    xspec = lambda rows: pl.BlockSpec((None, rows, tn), lambda b, j: (b, 0, j))
    return pl.pallas_call(
        _left_mm_kernel, grid=(bs, ncols // tn), in_specs=[_full(wm.shape), xspec(kdim)], out_specs=xspec(mdim),
        out_shape=jax.ShapeDtypeStruct((bs, mdim, ncols), F32),
        compiler_params=_cparams(("parallel", "parallel")), name="filter_dft_outer",
    )(wm, x)


def _filt_mid_kernel(a_ref, w_ref, o_ref, *, n2):
    for kk in range(FFT_KB):
        a = jnp.concatenate([a_ref[0, kk], a_ref[1, kk]], axis=0)
        xf = _mm_3x(w_ref[kk], a)
        o_ref[kk, 0] = xf[:n2]
        o_ref[kk, 1] = xf[n2:]


def _filt_mid_stage(a, w2):
    bs, _, n1, n2, ch = a.shape
    return pl.pallas_call(
        functools.partial(_filt_mid_kernel, n2=n2),
        grid=(bs, n1 // FFT_KB),
        in_specs=[pl.BlockSpec((None, 2, FFT_KB, n2, ch), lambda b, j: (b, 0, j, 0, 0)),
                  pl.BlockSpec((FFT_KB, 2 * n2, 2 * n2), lambda b, j: (j, 0, 0))],
        out_specs=pl.BlockSpec((None, FFT_KB, 2, n2, ch), lambda b, j: (b, j, 0, 0, 0)),
        out_shape=jax.ShapeDtypeStruct((bs, n1, 2, n2, ch), F32),
        compiler_params=_cparams(("parallel", "parallel")), name="filter_dft_mid",
    )(a, w2)


FILT_ROWS = 512


def _filter_kernel(z_ref, win_ref, sel_ref, w1_ref, b1_ref, w2_ref, b2_ref, wo_ref, fr_ref, o_ref):
    fr = fr_ref[...]
    h = jnp.sin(fr * (_mm_3x(z_ref[...], w1_ref[...]) + b1_ref[...]))
    for i in range(B_INNER_MLPS):
        h = jnp.sin(fr * (_mm_3x(h, w2_ref[i]) + b2_ref[i]))
    ho = _mm_3x(h, wo_ref[...])
    fwd = sel_ref[...] > 0.5
    win = win_ref[...]
    for o in range(2):
        base = o * 2 * B_CH
        o_ref[o] = jnp.where(fwd, ho[:, base:base + B_CH], ho[:, base + B_CH:base + 2 * B_CH]) * win


def _hyena_filter_time(l, w1, b1, w2, b2, w_out, freq):
    bands = (B_EMB - 1) // 2
    circ = jnp.arange(2 * l)
    pos = jnp.where(circ < l, circ, 2 * l - circ)
    pos = jnp.where(circ == l, 0, pos)
    t = (jnp.linspace(0.0, 1.0, l, dtype=F32)[pos])[:, None]
    ang = 2.0 * math.pi * pos.astype(F32)[:, None] / l
    fr = jnp.linspace(1e-4, bands - 1, bands, dtype=F32)[None, :]
    z = jnp.concatenate([t, jnp.cos(fr * ang), -jnp.sin(fr * ang)], axis=-1)
    z = jnp.pad(z, ((0, 0), (0, 128 - B_EMB)))
    max_decay = math.log(B_DECAY_TARGET) / B_DECAY_SHORT_PCT
    min_decay = math.log(B_DECAY_TARGET) / B_DECAY_LONG_PCT
    deltas = jnp.abs(jnp.linspace(min_decay, max_decay, B_CH, dtype=F32))
    win = (jnp.exp(-t * deltas) + B_WINDOW_SHIFT) * (circ != l).astype(F32)[:, None]
    sel = (circ < l).astype(F32)[:, None]
    depth = w1.shape[0]
    w1p = jnp.pad(w1, ((0, 0), (0, 128 - B_EMB), (0, 0)))
    rows = min(FILT_ROWS, 2 * l)
    per_layer = lambda a: pl.BlockSpec((None,) + a.shape[1:], lambda li, i: (li,) + (0,) * (a.ndim - 1))
    params = (w1p, b1.reshape(depth, 1, B_FFN), w2, b2.reshape(depth, B_INNER_MLPS, 1, B_FFN), w_out,
              freq.reshape(depth, 1, B_FFN))
    return pl.pallas_call(
        _filter_kernel,
        grid=(depth, 2 * l // rows),
        in_specs=[pl.BlockSpec((rows, 128), lambda li, i: (i, 0)),
                  pl.BlockSpec((rows, B_CH), lambda li, i: (i, 0)),
                  pl.BlockSpec((rows, 1), lambda li, i: (i, 0))] + [per_layer(p) for p in params],
        out_specs=pl.BlockSpec((None, 2, rows, B_CH), lambda li, i: (li, 0, i, 0)),
        out_shape=jax.ShapeDtypeStruct((depth, 2, 2 * l, B_CH), F32),
        compiler_params=_cparams(("parallel", "parallel")), name="hyena_filter",
    )(z, win, sel, *params)


def _hyena_spectrum(l, consts, w1, b1, w2m, b2, w_out, freq):
    f1_full, _, w2, _, _ = consts
    n2 = 2 * l // FFT_N1
    depth = w1.shape[0]
    g_time = _hyena_filter_time(l, w1, b1, w2m, b2, w_out, freq)
    f1_rows = np.concatenate([f1_full[:FFT_KH], f1_full[FFT_N1:FFT_N1 + FFT_KH]], axis=0)
    a = _left_mm(jnp.asarray(f1_rows, F32), g_time.reshape(depth * 2, FFT_N1, n2 * B_CH))
    g = _filt_mid_stage(a.reshape(depth * 2, 2, FFT_KH, n2, B_CH), jnp.asarray(w2[:FFT_KH], F32))
    return g.reshape(depth, 2, FFT_KH, 2, n2, B_CH)


HY_SLAB = 128


def _hy_pitch(n2c):
    return n2c + 8 if (n2c // 8) % 2 == 0 else n2c + 16


HY_KG = 20
HY_JG = 16


def _hyena_lat_kernel(zv_ref, z1_ref, z2_ref, cw_ref, cb_ref, g_ref, bias_ref, f1_ref, w2_ref, v2_ref, f3_ref,
                      o_ref, useq, ur, a_re, a_im, cbuf, *, col_mode, seq):
    n1h = FFT_N1 // 2
    n2c = seq // n1h
    pitch = _hy_pitch(n2c)
    grows = seq // GRID_W
    row_id = lax.broadcasted_iota(jnp.int32, (seq, HY_SLAB), 0)

    def short_conv(z_ref, k):
        z = z_ref[...]
        w = cw_ref[k]
        if col_mode:
            g0, gl = z[:GRID_W], z[seq - GRID_W:]
            cc = row_id[:GRID_W]
            wrap_p = jnp.where(cc == 0, 0.0, pltpu.roll(gl, 1, axis=0))
            wrap_n = jnp.where(cc == GRID_W - 1, 0.0, pltpu.roll(g0, GRID_W - 1, axis=0))
            prev = jnp.concatenate([wrap_p, z[:seq - GRID_W]], axis=0)
            nxt = jnp.concatenate([z[GRID_W:], wrap_n], axis=0)
        else:
            prev = jnp.where(row_id == 0, 0.0, pltpu.roll(z, 1, axis=0))
            nxt = jnp.where(row_id == seq - 1, 0.0, pltpu.roll(z, seq - 1, axis=0))
        return prev * w[0:1] + z * w[1:2] + nxt * w[2:3] + cb_ref[k]

    def seq_start(j):
        return (GRID_W * lax.rem(j, grows) + j // grows) if col_mode else j

    seq_stride = 2 if col_mode else pitch
    seq_ref = ur if col_mode else useq

    def put_seq(val):
        if col_mode:
            ur[...] = val
        else:
            for n1 in range(n1h):
                useq[n1 * pitch:n1 * pitch + n2c, :] = val[n1 * n2c:(n1 + 1) * n2c]

    def conv_out():
        if col_mode:
            return cbuf[...]
        return jnp.concatenate([cbuf[n1 * pitch:n1 * pitch + n2c, :] for n1 in range(n1h)], axis=0)

    y_prev = short_conv(zv_ref, 0)
    x_next = (short_conv(z1_ref, 1), short_conv(z2_ref, 2))
    put_seq(y_prev)
    f1 = f1_ref[...]
    f3 = f3_ref[...]
    jgrp = min(HY_JG, n2c)
    for order in range(2):
        def stage1(jg, carry):
            js = [jg * jgrp + jj for jj in range(jgrp)]
            xs = [seq_ref[pl.ds(seq_start(j), n1h, stride=seq_stride), :] for j in js]
            outs = [_mm(f1, x) for x in xs]
            for j, a in zip(js, outs):
                a_re[pl.ds(j, FFT_KH, stride=pitch), :] = a[:FFT_KH]
                a_im[pl.ds(j, FFT_KH, stride=pitch), :] = a[FFT_KH:]
            return carry
        lax.fori_loop(0, n2c // jgrp, stage1, 0)

        def mid(kg, carry):
            k1s = [kg * HY_KG + kk for kk in range(HY_KG)]
            offs = [pl.multiple_of(k1 * pitch, 8) for k1 in k1s]
            xin = [jnp.concatenate([a_re[pl.ds(o, n2c), :], a_im[pl.ds(o, n2c), :]], axis=0) for o in offs]
            xf = [_mm(w2_ref[k1], a) for k1, a in zip(k1s, xin)]
            ys = []
            for k1, x in zip(k1s, xf):
                xr, xi = x[:n2c], x[n2c:]
                gr, gi = g_ref[order, k1, 0], g_ref[order, k1, 1]
                ys.append(jnp.concatenate([xr * gr - xi * gi, xr * gi + xi * gr], axis=0))
            bm = [_mm(v2_ref[k1], y) for k1, y in zip(k1s, ys)]
            for o, b in zip(offs, bm):
                a_re[pl.ds(o, n2c), :] = b[:n2c]
                a_im[pl.ds(o, n2c), :] = b[n2c:]
            return carry
        lax.fori_loop(0, FFT_KH // HY_KG, mid, 0)

        def stage3(jg, carry):
            js = [jg * jgrp + jj for jj in range(jgrp)]
            bs = [jnp.concatenate([a_re[pl.ds(j, FFT_KH, stride=pitch), :],
                                   a_im[pl.ds(j, FFT_KH, stride=pitch), :]], axis=0) for j in js]
            outs = [_mm(f3, b) for b in bs]
            for j, y in zip(js, outs):
                cbuf[pl.ds(seq_start(j), n1h, stride=seq_stride), :] = y
            return carry
        lax.fori_loop(0, n2c // jgrp, stage3, 0)

        y_prev = x_next[order] * (conv_out() + y_prev * bias_ref[order:order + 1, :])
        if order == 0:
            put_seq(y_prev)
    o_ref[...] = y_prev


def _hyena_fused(lay, z, conv_w, conv_b, spec, bias, consts, l, seq, row0, s0, nslab, col_mode):
    f1, w2, v2, f3 = _half_spectrum_consts(consts)
    n2c = seq // (FFT_N1 // 2)
    assert FFT_KH % HY_KG == 0 and n2c % min(HY_JG, n2c) == 0
    pitch = _hy_pitch(n2c)
    cps = B_CH // HY_SLAB
    zspec = lambda k: pl.BlockSpec((seq, HY_SLAB), lambda j, b, k=k: (row0 + b, k * cps + s0 + j))
    once = lambda a: pl.BlockSpec(a.shape, lambda j, b: (0,) * a.ndim, pipeline_mode=pl.Buffered(1))
    mats = tuple(jnp.asarray(m, F32).astype(BF16) for m in (f1, w2, v2, f3))
    seq_rows = (FFT_N1 // 2) * pitch
    kern = functools.partial(_hyena_lat_kernel, col_mode=col_mode, seq=seq)
    in_specs = [zspec(0), zspec(1), zspec(2),
                pl.BlockSpec((3, B_SHORT, HY_SLAB), lambda j, b: (0, 0, s0 + j)),
                pl.BlockSpec((3, 1, HY_SLAB), lambda j, b: (0, 0, s0 + j)),
                pl.BlockSpec((None, 2, FFT_KH, 2, n2c, HY_SLAB), lambda j, b: (l, 0, 0, 0, 0, s0 + j),
                             pipeline_mode=pl.Buffered(1)),
                pl.BlockSpec((2, HY_SLAB), lambda j, b: (0, s0 + j))] + [once(m) for m in mats]
    args = [z, z, z, jnp.transpose(conv_w, (1, 0, 2)), conv_b.reshape(3, 1, B_CH), spec, bias, *mats]
    return pl.pallas_call(
        kern,
        grid=(nslab, lay.b),
        in_specs=in_specs,
        out_specs=pl.BlockSpec((seq, HY_SLAB), lambda j, b: (b, j)),
        out_shape=jax.ShapeDtypeStruct((lay.b * seq, nslab * HY_SLAB), F32),
        scratch_shapes=[pltpu.VMEM((seq_rows, HY_SLAB), F32), pltpu.VMEM((seq, HY_SLAB), F32),
                        pltpu.VMEM((FFT_KH * pitch, HY_SLAB), F32), pltpu.VMEM((FFT_KH * pitch, HY_SLAB), F32),
                        pltpu.VMEM((seq if col_mode else seq_rows, HY_SLAB), F32)],
        compiler_params=_cparams(("parallel", "parallel")),
        name="hyena_col" if col_mode else "hyena_row",
    )(*args)


def _hyena(lay, z, conv_w, conv_b, spec_c, spec_x, bias, consts_c, consts_x, l, need_ctx):
    assert lay.n_ctx % lay.seq == 0
    half = B_CH // 2 // HY_SLAB
    args = (lay, z, conv_w, conv_b)
    lat0 = lay.n_ctx // lay.seq
    yc = _hyena_fused(*args, spec_c, bias, consts_c, l, lay.ctx, 0, 0, 2 * half, False) if need_ctx else None
    yr = _hyena_fused(*args, spec_x, bias, consts_x, l, lay.seq, lat0, 0, half, False)
    ycol = _hyena_fused(*args, spec_x, bias, consts_x, l, lay.seq, lat0, half, half, True)
    return yc, yr, ycol


def _head_norm_gate(o, gate, norm_w, heads, width):
    outs = []
    for h in range(heads):
        oh = o[:, h * width:(h + 1) * width]
        y = oh * lax.rsqrt(jnp.mean(oh * oh, axis=-1, keepdims=True) + NORM_EPS) * norm_w
        outs.append(y * _silu(gate[:, h * width:(h + 1) * width]))
    return jnp.concatenate(outs, axis=1)


def _merge_kernel(x_ref, mod_ref, nw_ref, a0_ref, a1_ref, ag_ref, c0_ref, c1_ref, cx_ref, cz_ref,
                  d0_ref, d1_ref, dg_ref, an_ref, cd_ref, cn_ref, dn_ref, wg_ref, wb_ref, wo_ref,
                  br_ref, bcol_ref, *rest, ctx_tiles, t0):
    o_ref = rest[-1]
    yb = jnp.concatenate([br_ref[...], bcol_ref[...]], axis=1)
    if len(rest) == 2:
        yb = jnp.where(pl.program_id(0) + t0 < ctx_tiles, rest[0][...], yb)
    xv = x_ref[...]
    m = mod_ref[...]
    hb = _prenorm(xv, nw_ref[1:2, :], m[3:4, :], m[4:5, :]).astype(BF16)
    ya = _head_norm_gate(a0_ref[...] + a1_ref[...], ag_ref[...], an_ref[...], A_HEADS, A_DV)
    yd = _head_norm_gate(d0_ref[...] + d1_ref[...], dg_ref[...], dn_ref[...], D_HEADS, D_DV)
    yc = (c0_ref[...] + c1_ref[...] + cd_ref[...] * cx_ref[...]) * _silu(cz_ref[...])
    gw = C_INNER // C_GROUPS
    cn = cn_ref[...]
    yc = jnp.concatenate(
        [yc[:, g * gw:(g + 1) * gw]
         * lax.rsqrt(jnp.mean(yc[:, g * gw:(g + 1) * gw] ** 2, axis=-1, keepdims=True) + NORM_EPS)
         * cn[:, g * gw:(g + 1) * gw] for g in range(C_GROUPS)], axis=1)
    acc = jnp.zeros(xv.shape, F32)
    for k, y in enumerate((ya, yb, yc, yd)):
        gate = jax.nn.sigmoid(jnp.dot(hb, wg_ref[:, k * D_MODEL:(k + 1) * D_MODEL], preferred_element_type=F32))
        acc = acc + gate * jnp.dot(y.astype(BF16), wb_ref[k], preferred_element_type=F32)
    o_ref[...] = xv + m[5:6, :] * jnp.dot(acc.astype(BF16), wo_ref[...], preferred_element_type=F32)


def _merge(lay, x, mods, norm_w, a0, a1, ag, yb, c0, c1, cx, cz, d0, d1, dg, an, cd, cn, dn, wg, wb, wo, l):
    n, d = x.shape
    yb_ctx, yb_row, yb_col = yb
    ctx_tiles = lay.n_ctx // TOK_TILE
    t0 = ctx_tiles if yb_ctx is None else 0
    row = lambda w: pl.BlockSpec((TOK_TILE, w), lambda i: (i + t0, 0))
    lat = lambda w: pl.BlockSpec((TOK_TILE, w), lambda i: (jnp.maximum(i + t0 - ctx_tiles, 0), 0))
    extra_specs, extra = [], []
    if yb_ctx is not None:
        extra_specs = [pl.BlockSpec((TOK_TILE, BRANCH_W), lambda i: (jnp.minimum(i, ctx_tiles - 1), 0))]
        extra = [yb_ctx]
    return pl.pallas_call(
        functools.partial(_merge_kernel, ctx_tiles=ctx_tiles, t0=t0),
        grid=(n // TOK_TILE - t0,),
        in_specs=[row(d), _mod_spec(lay, mods, l, TOK_TILE, t0), _of_layer(norm_w, l)]
                 + [row(BRANCH_W)] * 10 + [_of_layer(p, l) for p in (an, cd, cn, dn)]
                 + [_of_layer(w, l, once=True) for w in (wg, wb, wo)]
                 + [lat(BRANCH_W // 2), lat(BRANCH_W // 2)] + extra_specs,
        out_specs=row(d),
        out_shape=jax.ShapeDtypeStruct((n, d), F32),
        compiler_params=_cparams(("parallel",)), name="merge",
    )(x, mods, norm_w, a0, a1, ag, c0, c1, cx, cz, d0, d1, dg, an, cd, cn, dn, wg, wb, wo, yb_row, yb_col, *extra)


def kernel(x, c, ctx, c_ctx, w_ada, b_ada, norm_w, ffn_up, ffn_down, w_in, gdn_conv, gdn_a_log, gdn_dt_bias, gdn_norm, hy_conv_w, hy_conv_b, hy_w1, hy_b1, hy_w2, hy_b2, hy_wout, hy_freq, hy_bias, ssd_conv_w, ssd_conv_b, ssd_a_log, ssd_dt_bias, ssd_d, ssd_norm, gla_gk_w, gla_gk_b, gla_norm, w_branch, w_out, final_norm):
    b, seq, d = x.shape
    ctx_len = ctx.shape[1]
    depth = w_ada.shape[0]
    lay = _Layout(b, ctx_len, seq)
    consts_c = _dft_consts(2 * ctx_len // FFT_N1)
    consts_x = _dft_consts(2 * seq // FFT_N1)

    rp = -(-(1 + b) // 8) * 8
    cond = jnp.concatenate([c_ctx[None, :], c, jnp.zeros((rp - 1 - b, d), F32)], axis=0)
    mods = _ada(cond, w_ada, b_ada).reshape(depth, rp, N_MOD, d)

    ffn_up_b, ffn_down_b = ffn_up.astype(BF16), ffn_down.astype(BF16)
    w_r, w_gate = _rearrange_w_in(w_in)
    w_branch_b, w_out_b = w_branch.astype(BF16), w_out.astype(BF16)
    ssd_conv_b3 = ssd_conv_b.reshape(depth, 1, C_XBC)
    an = gdn_norm.reshape(depth, 1, A_DV)
    cd = jnp.repeat(ssd_d, C_HEADDIM, axis=-1).reshape(depth, 1, C_INNER)
    cn = ssd_norm.reshape(depth, 1, C_INNER)
    dn = gla_norm.reshape(depth, 1, D_DV)

    fargs = (hy_w1, hy_b1, hy_w2, hy_b2, hy_wout, hy_freq)
    spec_c = _hyena_spectrum(ctx_len, consts_c, *fargs)
    spec_x = _hyena_spectrum(seq, consts_x, *fargs)

    xf = jnp.concatenate([ctx.reshape(b * ctx_len, d), x.reshape(b * seq, d)], axis=0)
    for l in range(depth):
        xf = _ffn(lay, xf, mods, norm_w, ffn_up_b, ffn_down_b, l, 0)

        qkv, a_gate, zb, c_z, xbc, d_qkv, d_gate, small = _inproj(
            lay, xf, mods, norm_w, w_r, gdn_conv, ssd_conv_w, ssd_conv_b3, l)
        (a0, a1), (c0, c1), (d0, d1) = _scan_call(lay, [
            _gdn_scan(lay, qkv, small, gdn_a_log[l], gdn_dt_bias[l]),
            _ssd_scan(lay, xbc, small, ssd_a_log[l], ssd_dt_bias[l]),
            _gla_scan(lay, d_qkv, small, gla_gk_w[l], gla_gk_b[l])])
        last = l == depth - 1
        yb = _hyena(lay, zb, hy_conv_w[l], hy_conv_b[l], spec_c, spec_x, hy_bias[l], consts_c, consts_x, l,
                    need_ctx=not last)

        xf = _merge(lay, xf, mods, norm_w, a0, a1, a_gate, yb, c0, c1, xbc, c_z, d0, d1, d_gate,
                    an, cd, cn, dn, w_gate, w_branch_b, w_out_b, l)
        xf = _ffn(lay, xf, mods, norm_w, ffn_up_b, ffn_down_b, l, 2, final_w=final_norm if last else None)
    return xf.reshape(b, seq, d)
```
